```python
import functools
import math
import jax
import jax.numpy as jnp
from jax import lax
import numpy as np

D_MODEL = 1024
BATCH = 4
SEQ = 4096
DEPTH = 2
DEC_BATCH = 32
DEC_SEQ = 64
PAST_LEN = 1024

CHUNK = 64
HEAD_DIM = 64
A_HEADS = 8
A_PAST_CHUNKS = 8
A_REL_CLIP = 128
B_HEADS = 8
C_HEADS = 4
C_VDIM = 2 * HEAD_DIM
T5_BUCKETS = 32
T5_MAX_DIST = 128
D_FF = 2816
CONV_W = 3
N_BRANCH = 3
A_W = A_HEADS * HEAD_DIM
B_W = B_HEADS * HEAD_DIM
C_QK_W = C_HEADS * 2 * HEAD_DIM
C_V_W = C_HEADS * C_VDIM
BRANCH_W = A_W
IN_SPLITS = (A_W, A_W, A_W, B_W, B_W, B_W, C_QK_W, C_QK_W, C_V_W, N_BRANCH * D_MODEL)
IN_COLS = sum(IN_SPLITS)
QBLK = 128
EPS = 1e-6
NEG_INF = -1e30

kernel_name = 'streaming_hybrid_gated_trunk'


def rmsnorm(x, g):
    xf = x.astype(jnp.float32)
    y = xf * lax.rsqrt(jnp.mean(xf * xf, axis=-1, keepdims=True) + EPS)
    return (y * g.astype(jnp.float32)).astype(x.dtype)


def t5_bucket(rel):
    half = T5_BUCKETS // 2
    max_exact = half // 2
    ret = jnp.where(rel > 0, half, 0)
    n = jnp.abs(rel)
    nf = jnp.maximum(n, 1).astype(jnp.float32)
    large = max_exact + (jnp.log(nf / max_exact) / math.log(T5_MAX_DIST / max_exact) * (half - max_exact)).astype(jnp.int32)
    large = jnp.minimum(large, half - 1)
    return ret + jnp.where(n < max_exact, n, large)


def t5_bucket_bias(q_pos, k_pos, table):
    b = table[t5_bucket(k_pos[None, :] - q_pos[:, None])]
    return jnp.moveaxis(b, -1, 0).astype(jnp.float32)


def clipped_rel_bias(rel, table):
    b = table[jnp.clip(rel, -A_REL_CLIP, A_REL_CLIP) + A_REL_CLIP]
    return jnp.moveaxis(b, -1, 0).astype(jnp.float32)


def mixer_inputs(xn, w_in):
    b, t = xn.shape[0], xn.shape[1]
    p = jnp.einsum('btd,dc->btc', xn, w_in)
    offs = np.cumsum((0,) + IN_SPLITS)
    parts = [p[..., int(offs[i]):int(offs[i + 1])] for i in range(len(IN_SPLITS))]
    qa, ka, va, qb, kb, vb, qc, kc, vc, gate_pre = parts
    return (qa.reshape(b, t, A_HEADS, HEAD_DIM), ka.reshape(b, t, A_HEADS, HEAD_DIM),
            va.reshape(b, t, A_HEADS, HEAD_DIM),
            qb.reshape(b, t, B_HEADS, HEAD_DIM), kb.reshape(b, t, B_HEADS, HEAD_DIM),
            vb.reshape(b, t, B_HEADS, HEAD_DIM),
            qc.reshape(b, t, C_HEADS, 2, HEAD_DIM), kc.reshape(b, t, C_HEADS, 2, HEAD_DIM),
            vc.reshape(b, t, C_HEADS, C_VDIM), gate_pre)


def band_attention_prompt(q, k, v, rel_table):
    b, s, h, d = q.shape
    nc = s // CHUNK
    nb = A_PAST_CHUNKS + 1
    qc = q.reshape(b, nc, CHUNK, h, d)
    pad = ((0, 0), (A_PAST_CHUNKS, 0), (0, 0), (0, 0), (0, 0))
    kp = jnp.pad(k.reshape(b, nc, CHUNK, h, d), pad)
    vp = jnp.pad(v.reshape(b, nc, CHUNK, h, d), pad)
    idx = jnp.arange(nc)[:, None] + jnp.arange(nb)[None, :]
    kband = kp[:, idx].reshape(b, nc, nb * CHUNK, h, d)
    vband = vp[:, idx].reshape(b, nc, nb * CHUNK, h, d)
    i = jnp.arange(CHUNK)
    j = jnp.arange(nb)
    k_off = ((j[:, None] - A_PAST_CHUNKS) * CHUNK + i[None, :]).reshape(-1)
    bias = clipped_rel_bias(i[:, None] - k_off[None, :], rel_table)
    valid = jnp.repeat(idx >= A_PAST_CHUNKS, CHUNK, axis=1)
    logits = jnp.einsum('bnqhd,bnkhd->bhnqk', qc, kband).astype(jnp.float32) * (HEAD_DIM ** -0.5)
    logits = logits + bias[None, :, None]
    logits = jnp.where(valid[None, None, :, None, :], logits, NEG_INF)
    p = jax.nn.softmax(logits, axis=-1)
    o = jnp.einsum('bhnqk,bnkhd->bnqhd', p.astype(v.dtype), vband)
    return o.reshape(b, s, h * d)


def band_attention_sample(q, k_new, v_new, k_cache, v_cache, rel_table, past):
    b, t = q.shape[0], q.shape[1]
    keep = k_cache.shape[1]
    keys = jnp.concatenate([k_cache, k_new], axis=1)
    vals = jnp.concatenate([v_cache, v_new], axis=1)
    q_pos = past + jnp.arange(t, dtype=jnp.int32)
    k_pos = past - keep + jnp.arange(keep + t, dtype=jnp.int32)
    bias = clipped_rel_bias(q_pos[:, None] - k_pos[None, :], rel_table)
    logits = jnp.einsum('bqhd,bkhd->bhqk', q, keys).astype(jnp.float32) * (HEAD_DIM ** -0.5) + bias[None]
    p = jax.nn.softmax(logits, axis=-1)
    o = jnp.einsum('bhqk,bkhd->bqhd', p.astype(vals.dtype), vals).reshape(b, t, -1)
    return o, keys[:, t:], vals[:, t:]


def stick_breaking_block(q, k, v, q_pos, k_pos):
    z = jnp.einsum('bqhd,bkhd->bhqk', q, k).astype(jnp.float32) * (HEAD_DIM ** -0.5)
    mask = (k_pos[None, :] < q_pos[:, None])[None, None]
    log_keep = jnp.where(mask, jax.nn.log_sigmoid(-z), 0.0)
    rc = lax.cumsum(log_keep, axis=3, reverse=True)
    after = jnp.concatenate([rc[..., 1:], jnp.zeros_like(rc[..., :1])], axis=-1)
    w = jnp.where(mask, jnp.exp(jax.nn.log_sigmoid(z) + after), 0.0)
    return jnp.einsum('bhqk,bkhd->bqhd', w.astype(v.dtype), v)


def diff_lambda(lam_params, layer_idx):
    lam_init = 0.8 - 0.6 * math.exp(-0.3 * layer_idx)
    lp = lam_params.astype(jnp.float32)
    lam = jnp.exp(jnp.sum(lp[0] * lp[1])) - jnp.exp(jnp.sum(lp[2] * lp[3])) + lam_init
    return lam, lam_init


def diff_attention_block(q, k, v, q_pos, k_pos, t5_table, lam, sub_gain, sub_scale):
    logits = jnp.einsum('bqhmd,bkhmd->bhmqk', q, k).astype(jnp.float32) * (HEAD_DIM ** -0.5)
    logits = logits + t5_bucket_bias(q_pos, k_pos, t5_table)[None, :, None]
    mask = (k_pos[None, :] // CHUNK) <= (q_pos[:, None] // CHUNK)
    logits = jnp.where(mask[None, None, None], logits, NEG_INF)
    p = jax.nn.softmax(logits, axis=-1)
    w = p[:, :, 0] - lam * p[:, :, 1]
    o = jnp.einsum('bhqk,bkhe->bqhe', w.astype(v.dtype), v)
    return rmsnorm(o, sub_gain) * sub_scale


def sweep_query_blocks(block_fn, q, k, v):
    b, s = q.shape[0], q.shape[1]
    nblk = s // QBLK
    q_blocks = jnp.moveaxis(q.reshape((b, nblk, QBLK) + q.shape[2:]), 1, 0)
    q_pos = jnp.arange(s, dtype=jnp.int32).reshape(nblk, QBLK)
    k_pos = jnp.arange(s, dtype=jnp.int32)
    out = lax.map(lambda qp: block_fn(qp[0], k, v, qp[1], k_pos), (q_blocks, q_pos))
    return jnp.moveaxis(out, 0, 1).reshape(b, s, -1)


def merge_branches(o_a, o_b, o_c, gate_pre, b_gate, w_branch, w_out):
    g = jax.nn.sigmoid(gate_pre.reshape(gate_pre.shape[:-1] + (N_BRANCH, D_MODEL)) + b_gate)
    h = (g[..., 0, :] * (o_a @ w_branch[0]) + g[..., 1, :] * (o_b @ w_branch[1])
         + g[..., 2, :] * (o_c @ w_branch[2]))
    return h @ w_out


def conv_ffn(xn, w_up, conv_w, conv_b, w_down, conv_state):
    gu = xn @ w_up
    g, u = gu[..., :D_FF], gu[..., D_FF:]
    t = g.shape[1]
    gp = jnp.concatenate([conv_state.astype(g.dtype), g], axis=1)
    c = conv_b + conv_w[0] * gp[:, 0:t]
    for kk in range(1, CONV_W):
        c = c + conv_w[kk] * gp[:, kk:kk + t]
    h = jax.nn.gelu(c) * u
    return h @ w_down, gp[:, gp.shape[1] - (CONV_W - 1):]


def trunk(x, caches, norm_mix, w_in, b_gate, a_rel_bias, t5_bias, c_lambda, c_subln,
          w_branch, w_out, norm_ffn, w_up, conv_w, conv_b, w_down, norm_final):
    b, t = x.shape[0], x.shape[1]
    new_states = [[] for _ in range(7)]
    for l in range(DEPTH):
        xn = rmsnorm(x, norm_mix[l])
        qa, ka, va, qb, kb, vb, qc, kc, vc, gate_pre = mixer_inputs(xn, w_in[l])
        lam, lam_init = diff_lambda(c_lambda[l], l)
        diff_fn = functools.partial(diff_attention_block, t5_table=t5_bias, lam=lam,
                                    sub_gain=c_subln[l], sub_scale=1.0 - lam_init)
        if caches is None:
            o_a = band_attention_prompt(qa, ka, va, a_rel_bias[l])
            o_b = sweep_query_blocks(stick_breaking_block, qb, kb, vb)
            o_c = sweep_query_blocks(diff_fn, qc, kc, vc)
            a_keep = min(A_PAST_CHUNKS * CHUNK, t)
            st_a_k, st_a_v = ka[:, t - a_keep:], va[:, t - a_keep:]
            conv_state = jnp.zeros((b, CONV_W - 1, D_FF), x.dtype)
        else:
            cache_a_k, cache_a_v, cache_b_k, cache_b_v, cache_c_k, cache_c_v, state_conv = caches
            past = cache_b_k.shape[2]
            q_pos = past + jnp.arange(t, dtype=jnp.int32)
            k_pos = jnp.arange(past + t, dtype=jnp.int32)
            o_a, st_a_k, st_a_v = band_attention_sample(qa, ka, va, cache_a_k[l], cache_a_v[l],
                                                        a_rel_bias[l], past)
            o_b = stick_breaking_block(qb, jnp.concatenate([cache_b_k[l], kb], axis=1),
                                       jnp.concatenate([cache_b_v[l], vb], axis=1),
                                       q_pos, k_pos).reshape(b, t, -1)
            o_c = diff_fn(qc, jnp.concatenate([cache_c_k[l], kc], axis=1),
                          jnp.concatenate([cache_c_v[l], vc], axis=1), q_pos, k_pos).reshape(b, t, -1)
            conv_state = state_conv[l]
        x = x + merge_branches(o_a, o_b, o_c, gate_pre, b_gate[l], w_branch[l], w_out[l])
        f, st_conv = conv_ffn(rmsnorm(x, norm_ffn[l]), w_up[l], conv_w[l], conv_b[l], w_down[l], conv_state)
        x = x + f
        for lst, s in zip(new_states, (st_a_k, st_a_v, kb, vb, kc, vc, st_conv)):
            lst.append(s)
    y = rmsnorm(x, norm_final)
    return y, tuple(jnp.stack(s, axis=0) for s in new_states)


def setup_inputs(seed: int = 0) -> dict:
    key = jax.random.key(seed)
    ks = jax.random.split(key, 24)
    a_len = min(A_PAST_CHUNKS * CHUNK, PAST_LEN)

    def nrm(k, shape, scale):
        return jax.random.normal(k, shape, jnp.float32) * scale

    return {
        'x_prompt': nrm(ks[0], (BATCH, SEQ, D_MODEL), 1.0),
        'x_sample': nrm(ks[1], (DEC_BATCH, DEC_SEQ, D_MODEL), 1.0),
        'cache_a_k': nrm(ks[2], (DEPTH, DEC_BATCH, a_len, A_HEADS, HEAD_DIM), 1.0),
        'cache_a_v': nrm(ks[3], (DEPTH, DEC_BATCH, a_len, A_HEADS, HEAD_DIM), 1.0),
        'cache_b_k': nrm(ks[4], (DEPTH, DEC_BATCH, PAST_LEN, B_HEADS, HEAD_DIM), 1.0),
        'cache_b_v': nrm(ks[5], (DEPTH, DEC_BATCH, PAST_LEN, B_HEADS, HEAD_DIM), 1.0),
        'cache_c_k': nrm(ks[6], (DEPTH, DEC_BATCH, PAST_LEN, C_HEADS, 2, HEAD_DIM), 1.0),
        'cache_c_v': nrm(ks[7], (DEPTH, DEC_BATCH, PAST_LEN, C_HEADS, C_VDIM), 1.0),
        'state_ffn_conv': nrm(ks[8], (DEPTH, DEC_BATCH, CONV_W - 1, D_FF), 1.0),
        'norm_mix': 1.0 + nrm(ks[9], (DEPTH, D_MODEL), 0.01),
        'w_in': nrm(ks[10], (DEPTH, D_MODEL, IN_COLS), D_MODEL ** -0.5),
        'b_gate': nrm(ks[11], (DEPTH, N_BRANCH, D_MODEL), 0.1),
        'a_rel_bias': nrm(ks[12], (DEPTH, 2 * A_REL_CLIP + 1, A_HEADS), 0.5),
        't5_bias': nrm(ks[13], (T5_BUCKETS, C_HEADS), 0.5),
        'c_lambda': nrm(ks[14], (DEPTH, 4, HEAD_DIM), 0.1),
        'c_subln': 1.0 + nrm(ks[15], (DEPTH, C_VDIM), 0.01),
        'w_branch': nrm(ks[16], (DEPTH, N_BRANCH, BRANCH_W, D_MODEL), BRANCH_W ** -0.5),
        'w_out': nrm(ks[17], (DEPTH, D_MODEL, D_MODEL), D_MODEL ** -0.5),
        'norm_ffn': 1.0 + nrm(ks[18], (DEPTH, D_MODEL), 0.01),
        'w_up': nrm(ks[19], (DEPTH, D_MODEL, 2 * D_FF), D_MODEL ** -0.5),
        'conv_w': nrm(ks[20], (DEPTH, CONV_W, D_FF), CONV_W ** -0.5),
        'conv_b': nrm(ks[21], (DEPTH, D_FF), 0.01),
        'w_down': nrm(ks[22], (DEPTH, D_FF, D_MODEL), D_FF ** -0.5),
        'norm_final': 1.0 + nrm(ks[23], (D_MODEL,), 0.01),
    }


def reference(x_prompt, x_sample, cache_a_k, cache_a_v, cache_b_k, cache_b_v, cache_c_k, cache_c_v,
              state_ffn_conv, norm_mix, w_in, b_gate, a_rel_bias, t5_bias, c_lambda, c_subln,
              w_branch, w_out, norm_ffn, w_up, conv_w, conv_b, w_down, norm_final):
    y_prompt, p_states = trunk(x_prompt, None, norm_mix, w_in, b_gate, a_rel_bias, t5_bias, c_lambda,
                               c_subln, w_branch, w_out, norm_ffn, w_up, conv_w, conv_b, w_down, norm_final)
    caches = (cache_a_k, cache_a_v, cache_b_k, cache_b_v, cache_c_k, cache_c_v, state_ffn_conv)
    y_sample, s_states = trunk(x_sample, caches, norm_mix, w_in, b_gate, a_rel_bias, t5_bias, c_lambda,
                               c_subln, w_branch, w_out, norm_ffn, w_up, conv_w, conv_b, w_down, norm_final)
    pa_k, pa_v, pb_k, pb_v, pc_k, pc_v, p_conv = p_states
    sa_k, sa_v, sb_k, sb_v, sc_k, sc_v, s_conv = s_states
    return (y_prompt, y_sample, pa_k, pa_v, pb_k, pb_v, pc_k, pc_v, p_conv,
            sa_k, sa_v, sb_k, sb_v, sc_k, sc_v, s_conv)
```

```python
import functools
import math

import jax
import jax.numpy as jnp
from jax import lax
from jax.experimental import pallas as pl
from jax.experimental.pallas import tpu as pltpu

F32 = jnp.float32
BF16 = jnp.bfloat16

D_MODEL = 1024
CHUNK = 64
HEAD_DIM = 64
A_HEADS = 8
A_PAST_CHUNKS = 8
A_REL_CLIP = 128
B_HEADS = 8
C_HEADS = 4
C_VDIM = 128
T5_BUCKETS = 32
D_FF = 2816
N_BRANCH = 3
BRANCH_W = 512
GATE_W = N_BRANCH * D_MODEL
IN_COLS = 9 * BRANCH_W + GATE_W
EPS = 1e-6
NEG_INF = -1e30
SCALE = HEAD_DIM ** -0.5

LANES = 128
_CB = {name: (GATE_W + i * BRANCH_W) // LANES
       for i, name in enumerate(("qa", "ka", "va", "qb", "kb", "vb", "qc", "kc", "vc"))}

ATT_BLK = 256
SB_LOG_CUT = -104.0
T5_FAR_BUCKET = 15
V7X_VMEM_BYTES = 64 * 1024 * 1024
VMEM_LIMIT = V7X_VMEM_BYTES - 8 * 1024 * 1024


def _cparams(*sem):
    return pltpu.CompilerParams(dimension_semantics=sem, vmem_limit_bytes=VMEM_LIMIT)


def _rms(x, g):
    return x * lax.rsqrt(jnp.mean(x * x, axis=-1, keepdims=True) + EPS) * g


def _dot(a, b):
    return jnp.dot(a, b, preferred_element_type=F32)


def _dot_nt(a, b):
    return lax.dot_general(a, b, (((1,), (1,)), ((), ())), preferred_element_type=F32)


def _norm_proj_kernel(x_ref, g_ref, w_ref, o_ref, xn_ref):
    @pl.when(pl.program_id(1) == 0)
    def _():
        xn_ref[...] = _rms(x_ref[...], g_ref[...]).astype(BF16)

    o_ref[...] = _dot(xn_ref[...], w_ref[...])


def _norm_proj(x, g, w, tm, tn):
    t, d = x.shape
    n = w.shape[1]
    return pl.pallas_call(
        _norm_proj_kernel,
        grid=(t // tm, n // tn),
        in_specs=[pl.BlockSpec((tm, d), lambda i, j: (i, 0)),
                  pl.BlockSpec((1, d), lambda i, j: (0, 0)),
                  pl.BlockSpec((d, tn), lambda i, j: (0, j))],
        out_specs=pl.BlockSpec((tm, tn), lambda i, j: (i, j)),
        out_shape=jax.ShapeDtypeStruct((t, n), F32),
        scratch_shapes=[pltpu.VMEM((tm, d), BF16)],
        compiler_params=_cparams("parallel", "arbitrary"),
        name="norm_proj",
    )(x, g.reshape(1, d), w)


def _abias_kernel(tbl_ref, o_ref):
    h = pl.program_id(0)
    rows = 32
    for seg, off in ((0, 2 * ATT_BLK), (1, ATT_BLK), (2, 0)):
        for rc in range(ATT_BLK // rows):
            ql = lax.broadcasted_iota(jnp.int32, (rows, ATT_BLK), 0) + rc * rows
            kl = lax.broadcasted_iota(jnp.int32, (rows, ATT_BLK), 1)
            if seg == 0:
                tile = jnp.full((rows, ATT_BLK), tbl_ref[2 * A_REL_CLIP * A_HEADS + h], F32)
                tile = jnp.where((kl >> 6) >= (ql >> 6), tile, NEG_INF)
            else:
                idx = jnp.clip(ql - kl + off, -A_REL_CLIP, A_REL_CLIP) + A_REL_CLIP
                lo = A_REL_CLIP + 1 if seg == 1 else 0

                def body(r, acc, idx=idx):
                    return jnp.where(idx == r, tbl_ref[r * A_HEADS + h], acc)

                tile = lax.fori_loop(lo, 2 * A_REL_CLIP + 1, body, jnp.zeros((rows, ATT_BLK), F32))
                if seg == 2:
                    tile = jnp.where((kl >> 6) <= (ql >> 6), tile, NEG_INF)
            o_ref[0, seg, rc * rows:(rc + 1) * rows, :] = tile


def _build_abias(table):
    return pl.pallas_call(
        _abias_kernel,
        grid=(A_HEADS,),
        in_specs=[pl.BlockSpec(memory_space=pltpu.SMEM)],
        out_specs=pl.BlockSpec((1, 3, ATT_BLK, ATT_BLK), lambda h: (h, 0, 0, 0)),
        out_shape=jax.ShapeDtypeStruct((A_HEADS, 3, ATT_BLK, ATT_BLK), F32),
        compiler_params=_cparams("parallel"),
        name="band_bias_tiles",
    )(table.reshape(-1))


_T5_LARGE_STEPS = (12, 16, 23, 32, 46, 64, 91)


def _t5bias_kernel(tbl_ref, o_ref):
    h = pl.program_id(0)
    rows = 64
    for seg, off in ((0, 0), (1, -ATT_BLK)):
        for rc in range(ATT_BLK // rows):
            ql = lax.broadcasted_iota(jnp.int32, (rows, ATT_BLK), 0) + rc * rows
            kl = lax.broadcasted_iota(jnp.int32, (rows, ATT_BLK), 1)
            rel = kl - ql + off
            n = jnp.abs(rel)
            large = jnp.full((rows, ATT_BLK), T5_BUCKETS // 4, jnp.int32)
            for th in _T5_LARGE_STEPS:
                large = large + jnp.where(n >= th, 1, 0)
            bucket = jnp.where(rel > 0, T5_BUCKETS // 2, 0) + jnp.where(n < T5_BUCKETS // 4, n, large)
            tile = jnp.zeros((rows, ATT_BLK), F32)
            for r in range(T5_BUCKETS):
                tile = jnp.where(bucket == r, tbl_ref[r * C_HEADS + h], tile)
            if seg == 0:
                tile = jnp.where((kl >> 6) <= (ql >> 6), tile, NEG_INF)
            o_ref[0, seg, rc * rows:(rc + 1) * rows, :] = tile


def _build_t5bias(table):
    return pl.pallas_call(
        _t5bias_kernel,
        grid=(C_HEADS,),
        in_specs=[pl.BlockSpec(memory_space=pltpu.SMEM)],
        out_specs=pl.BlockSpec((1, 2, ATT_BLK, ATT_BLK), lambda h: (h, 0, 0, 0)),
        out_shape=jax.ShapeDtypeStruct((C_HEADS, 2, ATT_BLK, ATT_BLK), F32),
        compiler_params=_cparams("parallel"),
        name="t5_bias_tiles",
    )(table.reshape(-1))


def _band_kernel(nseg, min_qi, q_ref, *refs):
    k_refs, v_refs, b_refs = refs[:nseg], refs[nseg:2 * nseg], refs[2 * nseg:3 * nseg]
    o_ref = refs[3 * nseg]
    qi = pl.program_id(2)
    q = q_ref[...] * SCALE
    lane = lax.broadcasted_iota(jnp.int32, q.shape, 1)
    kb = [k_refs[j][...].astype(BF16) for j in range(nseg)]
    vb = [v_refs[j][...].astype(BF16) for j in range(nseg)]
    outs = []
    for h in range(2):
        qh = jnp.where((lane >= HEAD_DIM) == bool(h), q, 0.0).astype(BF16)
        s = []
        for j in range(nseg):
            sj = _dot_nt(qh, kb[j]) + b_refs[j][h]
            if min_qi[j] > 0:
                sj = jnp.where(qi >= min_qi[j], sj, NEG_INF)
            s.append(sj)
        m = functools.reduce(jnp.maximum, [jnp.max(sj, axis=1, keepdims=True) for sj in s])
        p = [jnp.exp(sj - m) for sj in s]
        l = functools.reduce(jnp.add, [jnp.sum(pj, axis=1, keepdims=True) for pj in p])
        acc = functools.reduce(jnp.add, [_dot(p[j].astype(BF16), vb[j]) for j in range(nseg)])
        outs.append(acc / l)
    o_ref[...] = jnp.where(lane < HEAD_DIM, outs[0], outs[1]).astype(o_ref.dtype)


def _band_attn_prompt(p3, abias):
    bn, s, _ = p3.shape
    nq = s // ATT_BLK
    blk = (None, ATT_BLK, LANES)

    def kv_spec(name, back):
        return pl.BlockSpec(blk, lambda b, hp, qi: (b, jnp.maximum(qi - back, 0), _CB[name] + hp))

    def bias_spec(seg):
        return pl.BlockSpec((2, None, ATT_BLK, ATT_BLK), lambda b, hp, qi: (hp, seg, 0, 0))

    return pl.pallas_call(
        functools.partial(_band_kernel, 3, (2, 1, 0)),
        grid=(bn, A_HEADS // 2, nq),
        in_specs=([pl.BlockSpec(blk, lambda b, hp, qi: (b, qi, _CB["qa"] + hp))]
                  + [kv_spec("ka", back) for back in (2, 1, 0)]
                  + [kv_spec("va", back) for back in (2, 1, 0)]
                  + [bias_spec(seg) for seg in range(3)]),
        out_specs=pl.BlockSpec(blk, lambda b, hp, qi: (b, qi, hp)),
        out_shape=jax.ShapeDtypeStruct((bn, s, BRANCH_W), BF16),
        compiler_params=_cparams("parallel", "parallel", "arbitrary"),
        name="band_attn_prompt",
    )(p3, p3, p3, p3, p3, p3, p3, abias, abias, abias)


def _band_attn_sample(p3, k_cache, v_cache, bias_cache, bias_new):
    bn, t, _ = p3.shape
    keep = k_cache.shape[1]
    new = (None, t, LANES)
    old = (None, keep, LANES)
    return pl.pallas_call(
        functools.partial(_band_kernel, 2, (0, 0)),
        grid=(bn, A_HEADS // 2, 1),
        in_specs=[pl.BlockSpec(new, lambda b, hp, qi: (b, 0, _CB["qa"] + hp)),
                  pl.BlockSpec(old, lambda b, hp, qi: (b, 0, hp)),
                  pl.BlockSpec(new, lambda b, hp, qi: (b, 0, _CB["ka"] + hp)),
                  pl.BlockSpec(old, lambda b, hp, qi: (b, 0, hp)),
                  pl.BlockSpec(new, lambda b, hp, qi: (b, 0, _CB["va"] + hp)),
                  pl.BlockSpec((2, t, keep), lambda b, hp, qi: (hp, 0, 0)),
                  pl.BlockSpec((2, t, t), lambda b, hp, qi: (hp, 0, 0))],
        out_specs=pl.BlockSpec(new, lambda b, hp, qi: (b, 0, hp)),
        out_shape=jax.ShapeDtypeStruct((bn, t, BRANCH_W), BF16),
        compiler_params=_cparams("parallel", "parallel", "arbitrary"),
        name="band_attn_sample",
    )(p3, k_cache, p3, v_cache, p3, bias_cache, bias_new)


def _strict_upper(n):
    r = lax.broadcasted_iota(jnp.int32, (n, n), 0)
    c = lax.broadcasted_iota(jnp.int32, (n, n), 1)
    return jnp.where(r > c, 1.0, 0.0).astype(BF16)


def _sb_block(qh, k, v, upper, carry, acc, mask):
    z = _dot_nt(qh, k.astype(BF16))
    soft = jnp.log1p(jnp.exp(-jnp.abs(z)))
    log_keep = -(jnp.maximum(z, 0.0) + soft)
    log_take = jnp.minimum(z, 0.0) - soft
    if mask is not None:
        log_keep = jnp.where(mask, log_keep, 0.0)
    hi = log_keep.astype(BF16)
    lo = (log_keep - hi.astype(F32)).astype(BF16)
    after = _dot(hi, upper) + _dot(lo, upper) + carry
    w = jnp.exp(log_take + after)
    if mask is not None:
        w = jnp.where(mask, w, 0.0)
    acc = acc + _dot(w.astype(BF16), v.astype(BF16))
    carry = carry + jnp.sum(log_keep, axis=1, keepdims=True)
    return carry, acc


def _sb_kernel(tk, n_past_static, q_ref, kd_ref, vd_ref, kp_ref, vp_ref, o_ref):
    qi = pl.program_id(2)
    n_past = qi if n_past_static is None else n_past_static
    q = q_ref[...] * SCALE
    tq = q.shape[0]
    lane = lax.broadcasted_iota(jnp.int32, q.shape, 1)
    rq = lax.broadcasted_iota(jnp.int32, (tq, tq), 0)
    ck = lax.broadcasted_iota(jnp.int32, (tq, tq), 1)
    causal = ck < rq
    upper_d = _strict_upper(tq)
    upper_p = upper_d if tk == tq else _strict_upper(tk)
    kd = kd_ref[...]
    vd = vd_ref[...]
    outs = []
    for h in range(2):
        qh = jnp.where((lane >= HEAD_DIM) == bool(h), q, 0.0).astype(BF16)
        carry, acc = _sb_block(qh, kd, vd, upper_d, jnp.zeros((tq, 1), F32),
                               jnp.zeros((tq, LANES), F32), causal)

        def cond(st):
            return jnp.logical_and(st[0] >= 0, st[1] > 0)

        def body(st, qh=qh):
            j, _, carry, acc = st
            start = pl.multiple_of(j * tk, tk)
            carry, acc = _sb_block(qh, kp_ref[pl.ds(start, tk), :], vp_ref[pl.ds(start, tk), :],
                                   upper_p, carry, acc, None)
            return j - 1, (jnp.max(carry) > SB_LOG_CUT).astype(jnp.int32), carry, acc

        live = (jnp.max(carry) > SB_LOG_CUT).astype(jnp.int32)
        _, _, _, acc = lax.while_loop(cond, body, (n_past - 1, live, carry, acc))
        outs.append(acc)
    o_ref[...] = jnp.where(lane < HEAD_DIM, outs[0], outs[1]).astype(o_ref.dtype)


def _sb_attn_prompt(p3):
    bn, s, _ = p3.shape
    blk = (None, ATT_BLK, LANES)
    full = (None, s, LANES)
    return pl.pallas_call(
        functools.partial(_sb_kernel, ATT_BLK, None),
        grid=(bn, B_HEADS // 2, s // ATT_BLK),
        in_specs=[pl.BlockSpec(blk, lambda b, hp, qi: (b, qi, _CB["qb"] + hp)),
                  pl.BlockSpec(blk, lambda b, hp, qi: (b, qi, _CB["kb"] + hp)),
                  pl.BlockSpec(blk, lambda b, hp, qi: (b, qi, _CB["vb"] + hp)),
                  pl.BlockSpec(full, lambda b, hp, qi: (b, 0, _CB["kb"] + hp)),
                  pl.BlockSpec(full, lambda b, hp, qi: (b, 0, _CB["vb"] + hp))],
        out_specs=pl.BlockSpec(blk, lambda b, hp, qi: (b, qi, hp)),
        out_shape=jax.ShapeDtypeStruct((bn, s, BRANCH_W), BF16),
        compiler_params=_cparams("parallel", "parallel", "arbitrary"),
        name="stick_attn_prompt",
    )(p3, p3, p3, p3, p3)


def _sb_attn_sample(p3, k_cache, v_cache):
    bn, t, _ = p3.shape
    past = k_cache.shape[1]
    new = (None, t, LANES)
    old = (None, past, LANES)
    return pl.pallas_call(
        functools.partial(_sb_kernel, ATT_BLK, past // ATT_BLK),
        grid=(bn, B_HEADS // 2, 1),
        in_specs=[pl.BlockSpec(new, lambda b, hp, qi: (b, 0, _CB["qb"] + hp)),
                  pl.BlockSpec(new, lambda b, hp, qi: (b, 0, _CB["kb"] + hp)),
                  pl.BlockSpec(new, lambda b, hp, qi: (b, 0, _CB["vb"] + hp)),
                  pl.BlockSpec(old, lambda b, hp, qi: (b, 0, hp)),
                  pl.BlockSpec(old, lambda b, hp, qi: (b, 0, hp))],
        out_specs=pl.BlockSpec(new, lambda b, hp, qi: (b, 0, hp)),
        out_shape=jax.ShapeDtypeStruct((bn, t, BRANCH_W), BF16),
        compiler_params=_cparams("parallel", "parallel", "arbitrary"),
        name="stick_attn_sample",
    )(p3, p3, p3, k_cache, v_cache)


def _diff_kernel(tk, n_past_static, lam_init, tbl_ref, lam_ref, gain_ref,
                 q_ref, kd_ref, vd_ref, kp_ref, vp_ref, bd_ref, bp_ref, o_ref):
    h = pl.program_id(1)
    qi = pl.program_id(2)
    n_past = qi if n_past_static is None else n_past_static
    q = q_ref[...] * SCALE
    tq = q.shape[0]
    lane = lax.broadcasted_iota(jnp.int32, q.shape, 1)
    qm = [jnp.where((lane >= HEAD_DIM) == bool(m), q, 0.0).astype(BF16) for m in range(2)]
    far = tbl_ref[T5_FAR_BUCKET * C_HEADS + h]

    kd = kd_ref[...].astype(BF16)
    vd = vd_ref[...].astype(BF16)
    bd = bd_ref[...]
    state = []
    for m in range(2):
        s = _dot_nt(qm[m], kd) + bd
        mx = jnp.max(s, axis=1, keepdims=True)
        p = jnp.exp(s - mx)
        state += [mx, jnp.sum(p, axis=1, keepdims=True), _dot(p.astype(BF16), vd)]

    def body(j, st):
        start = pl.multiple_of(j * tk, tk)
        k = kp_ref[pl.ds(start, tk), :].astype(BF16)
        v = vp_ref[pl.ds(start, tk), :].astype(BF16)
        bias = jnp.where(j == n_past - 1, bp_ref[...], far)
        new = []
        for m in range(2):
            mx, l, acc = st[3 * m:3 * m + 3]
            s = _dot_nt(qm[m], k) + bias
            mx_new = jnp.maximum(mx, jnp.max(s, axis=1, keepdims=True))
            alpha = jnp.exp(mx - mx_new)
            p = jnp.exp(s - mx_new)
            new += [mx_new, alpha * l + jnp.sum(p, axis=1, keepdims=True),
                    alpha * acc + _dot(p.astype(BF16), v)]
        return tuple(new)

    st = lax.fori_loop(0, n_past, body, tuple(state))
    lp = lam_ref[...]
    lam = (jnp.exp(jnp.sum(lp[0:1] * lp[1:2], axis=1, keepdims=True))
           - jnp.exp(jnp.sum(lp[2:3] * lp[3:4], axis=1, keepdims=True)) + lam_init)
    o = st[2] / st[1] - lam * (st[5] / st[4])
    o_ref[...] = (_rms(o, gain_ref[...]) * (1.0 - lam_init)).astype(o_ref.dtype)


def _diff_attn(p3, k_past, v_past, past_cb, bias_d, bias_p, tq, n_past_static, lam_init,
               t5_flat, lam_params, gain):
    bn, s, _ = p3.shape
    blk = (None, tq, LANES)
    full = (None, k_past.shape[1], LANES)
    tkd = bias_d.shape[-1]
    return pl.pallas_call(
        functools.partial(_diff_kernel, ATT_BLK, n_past_static, lam_init),
        grid=(bn, C_HEADS, s // tq),
        in_specs=[pl.BlockSpec(memory_space=pltpu.SMEM),
                  pl.BlockSpec((4, HEAD_DIM), lambda b, h, qi: (0, 0)),
                  pl.BlockSpec((1, C_VDIM), lambda b, h, qi: (0, 0)),
                  pl.BlockSpec(blk, lambda b, h, qi: (b, qi, _CB["qc"] + h)),
                  pl.BlockSpec(blk, lambda b, h, qi: (b, qi, _CB["kc"] + h)),
                  pl.BlockSpec(blk, lambda b, h, qi: (b, qi, _CB["vc"] + h)),
                  pl.BlockSpec(full, lambda b, h, qi: (b, 0, past_cb[0] + h)),
                  pl.BlockSpec(full, lambda b, h, qi: (b, 0, past_cb[1] + h)),
                  pl.BlockSpec((None, tq, tkd), lambda b, h, qi: (h, 0, 0)),
                  pl.BlockSpec((None, tq, ATT_BLK), lambda b, h, qi: (h, 0, 0))],
        out_specs=pl.BlockSpec(blk, lambda b, h, qi: (b, qi, h)),
        out_shape=jax.ShapeDtypeStruct((bn, s, BRANCH_W), BF16),
        compiler_params=_cparams("parallel", "parallel", "arbitrary"),
        name="diff_attn_prompt" if n_past_static is None else "diff_attn_sample",
    )(t5_flat, lam_params, gain.reshape(1, C_VDIM), p3, p3, p3, k_past, v_past, bias_d, bias_p)


def _merge_kernel(x_ref, oa_ref, ob_ref, oc_ref, g0_ref, g1_ref, g2_ref, bg_ref, wb_ref, wo_ref, o_ref):
    h = None
    for n, (o_r, g_r) in enumerate(((oa_ref, g0_ref), (ob_ref, g1_ref), (oc_ref, g2_ref))):
        gate = 1.0 / (1.0 + jnp.exp(-(g_r[...] + bg_ref[n:n + 1, :])))
        t = gate * _dot(o_r[...], wb_ref[n])
        h = t if h is None else h + t
    o_ref[...] = x_ref[...] + _dot(h.astype(BF16), wo_ref[...])


def _merge(x, o_a, o_b, o_c, p, b_gate, w_branch, w_out, tm):
    t, d = x.shape
    row = lambda i: (i, 0)
    fixed2 = lambda i: (0, 0)
    return pl.pallas_call(
        _merge_kernel,
        grid=(t // tm,),
        in_specs=[pl.BlockSpec((tm, d), row),
                  pl.BlockSpec((tm, BRANCH_W), row),
                  pl.BlockSpec((tm, BRANCH_W), row),
                  pl.BlockSpec((tm, BRANCH_W), row),
                  pl.BlockSpec((tm, d), lambda i: (i, 0)),
                  pl.BlockSpec((tm, d), lambda i: (i, 1)),
                  pl.BlockSpec((tm, d), lambda i: (i, 2)),
                  pl.BlockSpec((N_BRANCH, d), fixed2),
                  pl.BlockSpec((N_BRANCH, BRANCH_W, d), lambda i: (0, 0, 0)),
                  pl.BlockSpec((d, d), fixed2)],
        out_specs=pl.BlockSpec((tm, d), row),
        out_shape=jax.ShapeDtypeStruct((t, d), F32),
        compiler_params=_cparams("parallel"),
        name="gated_merge",
    )(x, o_a, o_b, o_c, p, p, p, b_gate, w_branch, w_out)


FFN_TF = D_FF // 2
FFN_PREV = 16


def _gelu_tanh(x):
    return x * (0.5 * (1.0 + jnp.tanh(math.sqrt(2.0 / math.pi) * (x + 0.044715 * (x * x * x)))))


def _ffn_kernel(tiles_per_seq, x_ref, xp_ref, gn_ref, wg_ref, wu_ref, cw_ref, cb_ref, wd_ref, st_ref,
                o_ref, so_ref, xn_ref, xpn_ref):
    i = pl.program_id(0)
    j = pl.program_id(1)
    whole_seqs = tiles_per_seq == 0

    @pl.when(j == 0)
    def _():
        xn_ref[...] = _rms(x_ref[...], gn_ref[...]).astype(BF16)
        if not whole_seqs:
            xpn_ref[...] = _rms(xp_ref[...], gn_ref[...]).astype(BF16)

    xn = xn_ref[...]
    tm = xn.shape[0]
    tf = wg_ref.shape[1]
    g = _dot(xn, wg_ref[...])
    u = _dot(xn, wu_ref[...])
    if whole_seqs:
        ns = st_ref.shape[0]
        st = st_ref[...]
        g = g.reshape(ns, tm // ns, tf)
        u = u.reshape(ns, tm // ns, tf)
        pm2, pm1 = st[:, 0:1, :], st[:, 1:2, :]
    else:
        gp = _dot(xpn_ref[...], wg_ref[...])
        first = (i % tiles_per_seq) == 0
        st = st_ref[0]
        pm2 = jnp.where(first, st[0:1, :], gp[FFN_PREV - 2:FFN_PREV - 1, :])
        pm1 = jnp.where(first, st[1:2, :], gp[FFN_PREV - 1:FFN_PREV, :])
    ax = g.ndim - 2
    row = lax.broadcasted_iota(jnp.int32, g.shape, ax)
    g1 = jnp.where(row == 0, pm1, pltpu.roll(g, 1, ax))
    g2 = jnp.where(row == 0, pm2, jnp.where(row == 1, pm1, pltpu.roll(g, 2, ax)))
    cw = cw_ref[...]
    c = cb_ref[...] + cw[0:1, :] * g2 + cw[1:2, :] * g1 + cw[2:3, :] * g
    hid = (_gelu_tanh(c) * u).reshape(tm, tf).astype(BF16)
    contrib = _dot(hid, wd_ref[...])

    @pl.when(j == 0)
    def _():
        o_ref[...] = x_ref[...] + contrib

    @pl.when(j > 0)
    def _():
        o_ref[...] += contrib

    seq_len = g.shape[ax]
    if whole_seqs:
        so_ref[...] = g[:, seq_len - 2:seq_len, :]
    else:
        so_ref[0] = g[seq_len - 2:seq_len, :]


def _conv_ffn(x, g_norm, w_gate, w_up, conv_w, conv_b, w_down, state, seq_len, tm):
    t, d = x.shape
    nseq = t // seq_len
    nf = D_FF // FFN_TF
    if seq_len >= tm:
        tiles_per_seq = seq_len // tm
        st_spec = pl.BlockSpec((1, 2, FFN_TF), lambda i, j: (i // tiles_per_seq, 0, j))
    else:
        tiles_per_seq = 0
        st_spec = pl.BlockSpec((tm // seq_len, 2, FFN_TF), lambda i, j: (i, 0, j))
    prev_blocks = tm // FFN_PREV
    return pl.pallas_call(
        functools.partial(_ffn_kernel, tiles_per_seq),
        grid=(t // tm, nf),
        in_specs=[pl.BlockSpec((tm, d), lambda i, j: (i, 0)),
                  pl.BlockSpec((FFN_PREV, d), lambda i, j: (jnp.maximum(i * prev_blocks - 1, 0), 0)),
                  pl.BlockSpec((1, d), lambda i, j: (0, 0)),
                  pl.BlockSpec((d, FFN_TF), lambda i, j: (0, j)),
                  pl.BlockSpec((d, FFN_TF), lambda i, j: (0, j)),
                  pl.BlockSpec((3, FFN_TF), lambda i, j: (0, j)),
                  pl.BlockSpec((1, FFN_TF), lambda i, j: (0, j)),
                  pl.BlockSpec((FFN_TF, d), lambda i, j: (j, 0)),
                  st_spec],
        out_specs=[pl.BlockSpec((tm, d), lambda i, j: (i, 0)), st_spec],
        out_shape=[jax.ShapeDtypeStruct((t, d), F32),
                   jax.ShapeDtypeStruct((nseq, 2, D_FF), F32)],
        scratch_shapes=[pltpu.VMEM((tm, d), BF16), pltpu.VMEM((FFN_PREV, d), BF16)],
        compiler_params=_cparams("arbitrary", "arbitrary"),
        name="conv_ffn_prompt" if tiles_per_seq else "conv_ffn_sample",
    )(x, x, g_norm.reshape(1, d), w_gate, w_up, conv_w, conv_b.reshape(1, D_FF), w_down, state)


def _final_norm_kernel(x_ref, g_ref, o_ref):
    o_ref[...] = _rms(x_ref[...], g_ref[...])


def _final_norm(x, g, tm):
    t, d = x.shape
    return pl.pallas_call(
        _final_norm_kernel,
        grid=(t // tm,),
        in_specs=[pl.BlockSpec((tm, d), lambda i: (i, 0)), pl.BlockSpec((1, d), lambda i: (0, 0))],
        out_specs=pl.BlockSpec((tm, d), lambda i: (i, 0)),
        out_shape=jax.ShapeDtypeStruct((t, d), F32),
        compiler_params=_cparams("parallel"),
        name="final_norm",
    )(x, g.reshape(1, d))


PROJ_TM, PROJ_TN = 1024, 1280
ROW_TM = 512


def _lam_init(layer):
    return 0.8 - 0.6 * math.exp(-0.3 * layer)


def _cols(p3, name):
    c0 = _CB[name] * LANES
    return p3[..., c0:c0 + BRANCH_W]


def _trunk(x, caches, prm, abias, t5tiles, t5_flat):
    bn, t, d = x.shape
    x2 = x.reshape(bn * t, d)
    states = [[] for _ in range(7)]
    depth = prm["w_in"].shape[0]
    for l in range(depth):
        p = _norm_proj(x2, prm["norm_mix"][l], prm["w_in"][l], PROJ_TM, PROJ_TN)
        p3 = p.reshape(bn, t, IN_COLS)
        ka, va, kb, vb, kc, vc = (_cols(p3, n) for n in ("ka", "va", "kb", "vb", "kc", "vc"))
        lam_init = _lam_init(l)
        if caches is None:
            o_a = _band_attn_prompt(p3, abias[l])
            o_b = _sb_attn_prompt(p3)
            o_c = _diff_attn(p3, p3, p3, (_CB["kc"], _CB["vc"]), t5tiles[:, 0], t5tiles[:, 1],
                             ATT_BLK, None, lam_init, t5_flat, prm["c_lambda"][l], prm["c_subln"][l])
            keep = min(A_PAST_CHUNKS * CHUNK, t)
            st_a_k, st_a_v = ka[:, t - keep:], va[:, t - keep:]
            conv_state = jnp.zeros((bn, 2, D_FF), F32)
        else:
            ca_k, ca_v, cb_k, cb_v, cc_k, cc_v, st_conv = (c[l] for c in caches)
            flat = lambda a: a.reshape(a.shape[0], a.shape[1], BRANCH_W)
            keep = ca_k.shape[1]
            bias_cache = jnp.concatenate([abias[l][:, 0, :t, :], abias[l][:, 1, :t, :]], axis=-1)[..., -keep:]
            bias_new = abias[l][:, 2, :t, :t]
            o_a = _band_attn_sample(p3, flat(ca_k), flat(ca_v), bias_cache, bias_new)
            o_b = _sb_attn_sample(p3, flat(cb_k), flat(cb_v))
            o_c = _diff_attn(p3, flat(cc_k), flat(cc_v), (0, 0), t5tiles[:, 0, :t, :t], t5tiles[:, 1, :t, :],
                             t, cc_k.shape[1] // ATT_BLK, lam_init, t5_flat,
                             prm["c_lambda"][l], prm["c_subln"][l])
            st_a_k = jnp.concatenate([flat(ca_k), ka], axis=1)[:, t:]
            st_a_v = jnp.concatenate([flat(ca_v), va], axis=1)[:, t:]
            conv_state = st_conv
        flat2 = lambda a: a.reshape(bn * t, BRANCH_W)
        x2 = _merge(x2, flat2(o_a), flat2(o_b), flat2(o_c), p, prm["b_gate"][l],
                    prm["w_branch"][l], prm["w_out"][l], ROW_TM)
        x2, st_conv = _conv_ffn(x2, prm["norm_ffn"][l], prm["w_gate"][l], prm["w_up"][l], prm["conv_w"][l],
                                prm["conv_b"][l], prm["w_down"][l], conv_state, t, ROW_TM)
        heads = lambda a: a.reshape(a.shape[0], a.shape[1], A_HEADS, HEAD_DIM)
        new = (heads(st_a_k), heads(st_a_v), heads(kb), heads(vb),
               kc.reshape(bn, t, C_HEADS, 2, HEAD_DIM), vc.reshape(bn, t, C_HEADS, C_VDIM), st_conv)
        for lst, s in zip(states, new):
            lst.append(s)
    y = _final_norm(x2, prm["norm_final"], ROW_TM).reshape(bn, t, d)
    return y, tuple(jnp.stack(s, axis=0) for s in states)


def kernel(x_prompt, x_sample, cache_a_k, cache_a_v, cache_b_k, cache_b_v, cache_c_k, cache_c_v,
           state_ffn_conv, norm_mix, w_in, b_gate, a_rel_bias, t5_bias, c_lambda, c_subln,
           w_branch, w_out, norm_ffn, w_up, conv_w, conv_b, w_down, norm_final):
    split = IN_COLS - GATE_W
    prm = dict(
        norm_mix=norm_mix,
        w_in=jnp.concatenate([w_in[..., split:], w_in[..., :split]], axis=-1).astype(BF16),
        b_gate=b_gate, c_lambda=c_lambda, c_subln=c_subln,
        w_branch=w_branch.astype(BF16), w_out=w_out.astype(BF16), norm_ffn=norm_ffn,
        w_gate=w_up[..., :D_FF].astype(BF16), w_up=w_up[..., D_FF:].astype(BF16),
        conv_w=conv_w, conv_b=conv_b, w_down=w_down.astype(BF16), norm_final=norm_final)
    abias = [_build_abias(a_rel_bias[l]) for l in range(a_rel_bias.shape[0])]
    t5tiles = _build_t5bias(t5_bias)
    t5_flat = t5_bias.reshape(-1)
    y_p, p_states = _trunk(x_prompt, None, prm, abias, t5tiles, t5_flat)
    caches = (cache_a_k, cache_a_v, cache_b_k, cache_b_v, cache_c_k, cache_c_v, state_ffn_conv)
    y_s, s_states = _trunk(x_sample, caches, prm, abias, t5tiles, t5_flat)
    return (y_p, y_s) + p_states + s_states
```

```python
import functools
import math

import jax
import jax.numpy as jnp
from jax import lax
from jax.experimental import pallas as pl
from jax.experimental.pallas import tpu as pltpu

F32 = jnp.float32
BF16 = jnp.bfloat16

D_MODEL = 1024
CHUNK = 64
HEAD_DIM = 64
A_HEADS = 8
A_PAST_CHUNKS = 8
A_REL_CLIP = 128
B_HEADS = 8
C_HEADS = 4
C_VDIM = 128
T5_BUCKETS = 32
D_FF = 2816
N_BRANCH = 3
BRANCH_W = 512
GATE_W = N_BRANCH * D_MODEL
IN_COLS = 9 * BRANCH_W + GATE_W
EPS = 1e-6
NEG_INF = -1e30
SCALE = HEAD_DIM ** -0.5

LANES = 128
_CB = {name: (GATE_W + i * BRANCH_W) // LANES
       for i, name in enumerate(("qa", "ka", "va", "qb", "kb", "vb", "qc", "kc", "vc"))}

ATT_BLK = 256
SB_LOG_CUT = -104.0
T5_FAR_BUCKET = 15
DIFF_FAR_GROUP = 4
V7X_VMEM_BYTES = 64 * 1024 * 1024
VMEM_LIMIT = V7X_VMEM_BYTES - 8 * 1024 * 1024


def _cparams(*sem):
    return pltpu.CompilerParams(dimension_semantics=sem, vmem_limit_bytes=VMEM_LIMIT)


def _rms(x, g):
    return x * lax.rsqrt(jnp.mean(x * x, axis=-1, keepdims=True) + EPS) * g


def _dot(a, b):
    return jnp.dot(a, b, preferred_element_type=F32)


def _dot_nt(a, b):
    return lax.dot_general(a, b, (((1,), (1,)), ((), ())), preferred_element_type=F32)


def _norm_proj_kernel(x_ref, g_ref, w_ref, o_ref, xn_ref):
    @pl.when(pl.program_id(1) == 0)
    def _():
        xn_ref[...] = _rms(x_ref[...], g_ref[...]).astype(BF16)

    o_ref[...] = _dot(xn_ref[...], w_ref[...])


def _norm_proj(x, g, w, tm, tn):
    t, d = x.shape
    n = w.shape[1]
    return pl.pallas_call(
        _norm_proj_kernel,
        grid=(t // tm, n // tn),
        in_specs=[pl.BlockSpec((tm, d), lambda i, j: (i, 0)),
                  pl.BlockSpec((1, d), lambda i, j: (0, 0)),
                  pl.BlockSpec((d, tn), lambda i, j: (0, j))],
        out_specs=pl.BlockSpec((tm, tn), lambda i, j: (i, j)),
        out_shape=jax.ShapeDtypeStruct((t, n), F32),
        scratch_shapes=[pltpu.VMEM((tm, d), BF16)],
        compiler_params=_cparams("parallel", "arbitrary"),
        name="norm_proj",
    )(x, g.reshape(1, d), w)


def _abias_kernel(tbl_ref, o_ref):
    h = pl.program_id(0)
    sub = 8
    ql = lax.broadcasted_iota(jnp.int32, (ATT_BLK, ATT_BLK), 0)
    kl = lax.broadcasted_iota(jnp.int32, (ATT_BLK, ATT_BLK), 1)
    far = jnp.full((ATT_BLK, ATT_BLK), tbl_ref[2 * A_REL_CLIP * A_HEADS + h], F32)
    o_ref[0, 0] = jnp.where((kl >> 6) >= (ql >> 6), far, NEG_INF)
    lead = ATT_BLK - sub
    b = lax.broadcasted_iota(jnp.int32, (sub, 2 * ATT_BLK), 0)
    x = lax.broadcasted_iota(jnp.int32, (sub, 2 * ATT_BLK), 1)
    for seg, off in ((1, ATT_BLK), (2, 0)):
        idx = jnp.clip(b + lead - x + off, -A_REL_CLIP, A_REL_CLIP) + A_REL_CLIP

        def body(r, acc, idx=idx):
            return jnp.where(idx == r, tbl_ref[r * A_HEADS + h], acc)

        strip = lax.fori_loop(0, 2 * A_REL_CLIP + 1, body, jnp.zeros((sub, 2 * ATT_BLK), F32))
        for a in range(ATT_BLK // sub):
            o_ref[0, seg, a * sub:(a + 1) * sub, :] = strip[:, lead - a * sub:lead - a * sub + ATT_BLK]
    o_ref[0, 2] = jnp.where((kl >> 6) <= (ql >> 6), o_ref[0, 2], NEG_INF)


def _build_abias(table):
    return pl.pallas_call(
        _abias_kernel,
        grid=(A_HEADS,),
        in_specs=[pl.BlockSpec(memory_space=pltpu.SMEM)],
        out_specs=pl.BlockSpec((1, 3, ATT_BLK, ATT_BLK), lambda h: (h, 0, 0, 0)),
        out_shape=jax.ShapeDtypeStruct((A_HEADS, 3, ATT_BLK, ATT_BLK), F32),
        compiler_params=_cparams("parallel"),
        name="band_bias_tiles",
    )(table.reshape(-1))


_T5_LARGE_STEPS = (12, 16, 23, 32, 46, 64, 91)


def _t5bias_kernel(tbl_ref, o_ref):
    h = pl.program_id(0)
    rows = 64
    for seg, off in ((0, 0), (1, -ATT_BLK)):
        for rc in range(ATT_BLK // rows):
            ql = lax.broadcasted_iota(jnp.int32, (rows, ATT_BLK), 0) + rc * rows
            kl = lax.broadcasted_iota(jnp.int32, (rows, ATT_BLK), 1)
            rel = kl - ql + off
            n = jnp.abs(rel)
            large = jnp.full((rows, ATT_BLK), T5_BUCKETS // 4, jnp.int32)
            for th in _T5_LARGE_STEPS:
                large = large + jnp.where(n >= th, 1, 0)
            bucket = jnp.where(rel > 0, T5_BUCKETS // 2, 0) + jnp.where(n < T5_BUCKETS // 4, n, large)
            tile = jnp.zeros((rows, ATT_BLK), F32)
            for r in range(T5_BUCKETS):
                tile = jnp.where(bucket == r, tbl_ref[r * C_HEADS + h], tile)
            if seg == 0:
                tile = jnp.where((kl >> 6) <= (ql >> 6), tile, NEG_INF)
            o_ref[0, seg, rc * rows:(rc + 1) * rows, :] = tile


def _build_t5bias(table):
    return pl.pallas_call(
        _t5bias_kernel,
        grid=(C_HEADS,),
        in_specs=[pl.BlockSpec(memory_space=pltpu.SMEM)],
        out_specs=pl.BlockSpec((1, 2, ATT_BLK, ATT_BLK), lambda h: (h, 0, 0, 0)),
        out_shape=jax.ShapeDtypeStruct((C_HEADS, 2, ATT_BLK, ATT_BLK), F32),
        compiler_params=_cparams("parallel"),
        name="t5_bias_tiles",
    )(table.reshape(-1))


def _half_lanes(q, upper):
    lane = lax.broadcasted_iota(jnp.int32, q.shape, 1)
    return jnp.where((lane >= HEAD_DIM) == upper, q, 0.0).astype(BF16)


def _softmax_segments(qh, kb, vb, biases):
    s = [_dot_nt(qh, k) + b for k, b in zip(kb, biases)]
    m = functools.reduce(jnp.maximum, [jnp.max(sj, axis=1, keepdims=True) for sj in s])
    p = [jnp.exp(sj - m) for sj in s]
    l = functools.reduce(jnp.add, [jnp.sum(pj, axis=1, keepdims=True) for pj in p])
    acc = functools.reduce(jnp.add, [_dot(pj.astype(BF16), v) for pj, v in zip(p, vb)])
    return acc / l


def _band_kernel(nseg, min_qi, q_ref, *refs):
    k_refs, v_refs, b_refs = refs[:nseg], refs[nseg:2 * nseg], refs[2 * nseg:3 * nseg]
    o_ref = refs[3 * nseg]
    qi = pl.program_id(2)
    q = q_ref[...] * SCALE
    kb = [k_refs[j][...].astype(BF16) for j in range(nseg)]
    vb = [v_refs[j][...].astype(BF16) for j in range(nseg)]
    outs = []
    for h in range(2):
        biases = [b_refs[j][h] if min_qi[j] == 0 else jnp.where(qi >= min_qi[j], b_refs[j][h], NEG_INF)
                  for j in range(nseg)]
        outs.append(_softmax_segments(_half_lanes(q, bool(h)), kb, vb, biases))
    lane = lax.broadcasted_iota(jnp.int32, q.shape, 1)
    o_ref[...] = jnp.where(lane < HEAD_DIM, outs[0], outs[1]).astype(o_ref.dtype)


def _band_attn_prompt(p3, abias):
    bn, s, _ = p3.shape
    nq = s // ATT_BLK
    blk = (None, ATT_BLK, LANES)

    def kv_spec(name, back):
        return pl.BlockSpec(blk, lambda b, hp, qi: (b, jnp.maximum(qi - back, 0), _CB[name] + hp))

    def bias_spec(seg):
        return pl.BlockSpec((2, None, ATT_BLK, ATT_BLK), lambda b, hp, qi: (hp, seg, 0, 0))

    return pl.pallas_call(
        functools.partial(_band_kernel, 3, (2, 1, 0)),
        grid=(bn, A_HEADS // 2, nq),
        in_specs=([pl.BlockSpec(blk, lambda b, hp, qi: (b, qi, _CB["qa"] + hp))]
                  + [kv_spec("ka", back) for back in (2, 1, 0)]
                  + [kv_spec("va", back) for back in (2, 1, 0)]
                  + [bias_spec(seg) for seg in range(3)]),
        out_specs=pl.BlockSpec(blk, lambda b, hp, qi: (b, qi, hp)),
        out_shape=jax.ShapeDtypeStruct((bn, s, BRANCH_W), BF16),
        compiler_params=_cparams("parallel", "parallel", "arbitrary"),
        name="band_attn_prompt",
    )(p3, p3, p3, p3, p3, p3, p3, abias, abias, abias)


def _band_attn_sample(p3, k_cache, v_cache, bias_cache, bias_new):
    bn, t, _ = p3.shape
    keep = k_cache.shape[1]
    new = (None, t, LANES)
    old = (None, keep, LANES)
    return pl.pallas_call(
        functools.partial(_band_kernel, 2, (0, 0)),
        grid=(bn, A_HEADS // 2, 1),
        in_specs=[pl.BlockSpec(new, lambda b, hp, qi: (b, 0, _CB["qa"] + hp)),
                  pl.BlockSpec(old, lambda b, hp, qi: (b, 0, hp)),
                  pl.BlockSpec(new, lambda b, hp, qi: (b, 0, _CB["ka"] + hp)),
                  pl.BlockSpec(old, lambda b, hp, qi: (b, 0, hp)),
                  pl.BlockSpec(new, lambda b, hp, qi: (b, 0, _CB["va"] + hp)),
                  pl.BlockSpec((2, t, keep), lambda b, hp, qi: (hp, 0, 0)),
                  pl.BlockSpec((2, t, t), lambda b, hp, qi: (hp, 0, 0))],
        out_specs=pl.BlockSpec(new, lambda b, hp, qi: (b, 0, hp)),
        out_shape=jax.ShapeDtypeStruct((bn, t, BRANCH_W), BF16),
        compiler_params=_cparams("parallel", "parallel", "arbitrary"),
        name="band_attn_sample",
    )(p3, k_cache, p3, v_cache, p3, bias_cache, bias_new)


def _strict_upper(n):
    r = lax.broadcasted_iota(jnp.int32, (n, n), 0)
    c = lax.broadcasted_iota(jnp.int32, (n, n), 1)
    return jnp.where(r > c, 1.0, 0.0).astype(BF16)


def _sb_fold(qh, k, v, upper, mask, carry_ref, acc_ref, first):
    kb = k.astype(BF16)
    vb = v.astype(BF16)
    tq, tk = qh[0].shape[0], kb.shape[0]
    top = None
    for h in range(2):
        z = _dot_nt(qh[h], kb)
        soft = jnp.log1p(jnp.exp(-jnp.abs(z)))
        log_keep = -(jnp.maximum(z, 0.0) + soft)
        log_take = jnp.minimum(z, 0.0) - soft
        if mask is not None:
            log_keep = jnp.where(mask, log_keep, 0.0)
        hi = log_keep.astype(BF16)
        lo = (log_keep - hi.astype(F32)).astype(BF16)
        after = _dot(hi, upper) + _dot(lo, upper)
        if not first:
            after = after + pltpu.repeat(carry_ref[h], tk // LANES, axis=1)
        w = jnp.exp(log_take + after)
        if mask is not None:
            w = jnp.where(mask, w, 0.0)
        pv = _dot(w.astype(BF16), vb)
        total = jnp.sum(log_keep, axis=1, keepdims=True)
        if first:
            acc_ref[h] = pv
            carry = jnp.broadcast_to(total, (tq, LANES))
        else:
            acc_ref[h] += pv
            carry = carry_ref[h] + total
        carry_ref[h] = carry
        top = jnp.max(carry) if top is None else jnp.maximum(top, jnp.max(carry))
    return (top > SB_LOG_CUT).astype(jnp.int32)


def _sb_kernel(tk, n_past_static, q_ref, kd_ref, vd_ref, kp_ref, vp_ref, o_ref, carry_ref, acc_ref):
    qi = pl.program_id(2)
    n_past = qi if n_past_static is None else n_past_static
    q = q_ref[...] * SCALE
    tq = q.shape[0]
    qh = [_half_lanes(q, False), _half_lanes(q, True)]
    rq = lax.broadcasted_iota(jnp.int32, (tq, tq), 0)
    ck = lax.broadcasted_iota(jnp.int32, (tq, tq), 1)
    live = _sb_fold(qh, kd_ref[...], vd_ref[...], _strict_upper(tq), ck < rq, carry_ref, acc_ref, True)

    def cond(st):
        return jnp.logical_and(st[0] >= 0, st[1] > 0)

    def body(st):
        start = pl.multiple_of(st[0] * tk, tk)
        live = _sb_fold(qh, kp_ref[pl.ds(start, tk), :], vp_ref[pl.ds(start, tk), :], _strict_upper(tk),
                        None, carry_ref, acc_ref, False)
        return st[0] - 1, live

    lax.while_loop(cond, body, (n_past - 1, live))
    lane = lax.broadcasted_iota(jnp.int32, q.shape, 1)
    o_ref[...] = jnp.where(lane < HEAD_DIM, acc_ref[0], acc_ref[1]).astype(o_ref.dtype)


def _sb_attn_prompt(p3):
    bn, s, _ = p3.shape
    blk = (None, ATT_BLK, LANES)
    full = (None, s, LANES)
    return pl.pallas_call(
        functools.partial(_sb_kernel, ATT_BLK, None),
        grid=(bn, B_HEADS // 2, s // ATT_BLK),
        in_specs=[pl.BlockSpec(blk, lambda b, hp, qi: (b, qi, _CB["qb"] + hp)),
                  pl.BlockSpec(blk, lambda b, hp, qi: (b, qi, _CB["kb"] + hp)),
                  pl.BlockSpec(blk, lambda b, hp, qi: (b, qi, _CB["vb"] + hp)),
                  pl.BlockSpec(full, lambda b, hp, qi: (b, 0, _CB["kb"] + hp)),
                  pl.BlockSpec(full, lambda b, hp, qi: (b, 0, _CB["vb"] + hp))],
        out_specs=pl.BlockSpec(blk, lambda b, hp, qi: (b, qi, hp)),
        out_shape=jax.ShapeDtypeStruct((bn, s, BRANCH_W), BF16),
        scratch_shapes=[pltpu.VMEM((2, ATT_BLK, LANES), F32), pltpu.VMEM((2, ATT_BLK, LANES), F32)],
        compiler_params=_cparams("parallel", "parallel", "arbitrary"),
        name="stick_attn_prompt",
    )(p3, p3, p3, p3, p3)


def _sb_attn_sample(p3, k_cache, v_cache):
    bn, t, _ = p3.shape
    past = k_cache.shape[1]
    new = (None, t, LANES)
    old = (None, past, LANES)
    return pl.pallas_call(
        functools.partial(_sb_kernel, ATT_BLK, past // ATT_BLK),
        grid=(bn, B_HEADS // 2, 1),
        in_specs=[pl.BlockSpec(new, lambda b, hp, qi: (b, 0, _CB["qb"] + hp)),
                  pl.BlockSpec(new, lambda b, hp, qi: (b, 0, _CB["kb"] + hp)),
                  pl.BlockSpec(new, lambda b, hp, qi: (b, 0, _CB["vb"] + hp)),
                  pl.BlockSpec(old, lambda b, hp, qi: (b, 0, hp)),
                  pl.BlockSpec(old, lambda b, hp, qi: (b, 0, hp))],
        out_specs=pl.BlockSpec(new, lambda b, hp, qi: (b, 0, hp)),
        out_shape=jax.ShapeDtypeStruct((bn, t, BRANCH_W), BF16),
        scratch_shapes=[pltpu.VMEM((2, t, LANES), F32), pltpu.VMEM((2, t, LANES), F32)],
        compiler_params=_cparams("parallel", "parallel", "arbitrary"),
        name="stick_attn_sample",
    )(p3, p3, p3, k_cache, v_cache)


def _diff_combine(o0, o1, lam_init, lam_ref, gain_ref):
    lp = lam_ref[...]
    lam = (jnp.exp(jnp.sum(lp[0:1] * lp[1:2], axis=1, keepdims=True))
           - jnp.exp(jnp.sum(lp[2:3] * lp[3:4], axis=1, keepdims=True)) + lam_init)
    return _rms(o0 - lam * o1, gain_ref[...]) * (1.0 - lam_init)


def _diff_fold(qm, k, v, bias, m_ref, l_ref, acc_ref, first):
    kb = k.astype(BF16)
    vb = v.astype(BF16)
    tq, tk = qm[0].shape[0], kb.shape[0]
    for m in range(2):
        s = _dot_nt(qm[m], kb) + bias
        m_cur = jnp.max(s, axis=1, keepdims=True)
        m_new = jnp.broadcast_to(m_cur, (tq, LANES)) if first else jnp.maximum(m_ref[m], m_cur)
        p = jnp.exp(s - pltpu.repeat(m_new, tk // LANES, axis=1))
        row = jnp.sum(p, axis=1, keepdims=True)
        pv = _dot(p.astype(BF16), vb)
        if first:
            l_ref[m] = jnp.broadcast_to(row, (tq, LANES))
            acc_ref[m] = pv
        else:
            alpha = jnp.exp(m_ref[m] - m_new)
            l_ref[m] = alpha * l_ref[m] + row
            acc_ref[m] = alpha * acc_ref[m] + pv
        m_ref[m] = m_new


def _diff_prompt_kernel(lam_init, tbl_ref, lam_ref, gain_ref, q_ref, kd_ref, vd_ref, kp_ref, vp_ref,
                        bd_ref, bp_ref, o_ref, m_ref, l_ref, acc_ref):
    h = pl.program_id(1)
    qi = pl.program_id(2)
    q = q_ref[...] * SCALE
    blk = q.shape[0]
    qm = [_half_lanes(q, False), _half_lanes(q, True)]
    far = tbl_ref[T5_FAR_BUCKET * C_HEADS + h]

    near = pl.multiple_of(jnp.maximum(qi - 1, 0) * blk, blk)
    k0 = jnp.concatenate([kp_ref[pl.ds(near, blk), :], kd_ref[...]], axis=0)
    v0 = jnp.concatenate([vp_ref[pl.ds(near, blk), :], vd_ref[...]], axis=0)
    b0 = jnp.concatenate([jnp.where(qi >= 1, bp_ref[...], NEG_INF), bd_ref[...]], axis=1)
    _diff_fold(qm, k0, v0, b0, m_ref, l_ref, acc_ref, True)

    n_far = jnp.maximum(qi - 1, 0)
    n_group = n_far // DIFF_FAR_GROUP
    wide = DIFF_FAR_GROUP * blk

    def body(j, c):
        start = pl.multiple_of(j * wide, wide)
        _diff_fold(qm, kp_ref[pl.ds(start, wide), :], vp_ref[pl.ds(start, wide), :], far,
                   m_ref, l_ref, acc_ref, False)
        return c

    lax.fori_loop(0, n_group, body, 0)
    rest = pl.multiple_of(n_group * wide, wide)
    for r in range(1, DIFF_FAR_GROUP):
        @pl.when(n_far - n_group * DIFF_FAR_GROUP == r)
        def _(r=r):
            _diff_fold(qm, kp_ref[pl.ds(rest, r * blk), :], vp_ref[pl.ds(rest, r * blk), :], far,
                       m_ref, l_ref, acc_ref, False)

    o = _diff_combine(acc_ref[0] / l_ref[0], acc_ref[1] / l_ref[1], lam_init, lam_ref, gain_ref)
    o_ref[...] = o.astype(o_ref.dtype)


def _diff_sample_kernel(lam_init, lam_ref, gain_ref, q_ref, kc_ref, vc_ref, kn_ref, vn_ref,
                        bc_ref, bn_ref, o_ref):
    q = q_ref[...] * SCALE
    kb = [kc_ref[...].astype(BF16), kn_ref[...].astype(BF16)]
    vb = [vc_ref[...].astype(BF16), vn_ref[...].astype(BF16)]
    biases = [bc_ref[...], bn_ref[...]]
    o0 = _softmax_segments(_half_lanes(q, False), kb, vb, biases)
    o1 = _softmax_segments(_half_lanes(q, True), kb, vb, biases)
    o_ref[...] = _diff_combine(o0, o1, lam_init, lam_ref, gain_ref).astype(o_ref.dtype)


def _diff_attn_prompt(p3, t5tiles, lam_init, t5_flat, lam_params, gain):
    bn, s, _ = p3.shape
    blk = (None, ATT_BLK, LANES)
    full = (None, s, LANES)
    state = pltpu.VMEM((2, ATT_BLK, LANES), F32)
    return pl.pallas_call(
        functools.partial(_diff_prompt_kernel, lam_init),
        grid=(bn, C_HEADS, s // ATT_BLK),
        in_specs=[pl.BlockSpec(memory_space=pltpu.SMEM),
                  pl.BlockSpec((4, HEAD_DIM), lambda b, h, qi: (0, 0)),
                  pl.BlockSpec((1, C_VDIM), lambda b, h, qi: (0, 0)),
                  pl.BlockSpec(blk, lambda b, h, qi: (b, qi, _CB["qc"] + h)),
                  pl.BlockSpec(blk, lambda b, h, qi: (b, qi, _CB["kc"] + h)),
                  pl.BlockSpec(blk, lambda b, h, qi: (b, qi, _CB["vc"] + h)),
                  pl.BlockSpec(full, lambda b, h, qi: (b, 0, _CB["kc"] + h)),
                  pl.BlockSpec(full, lambda b, h, qi: (b, 0, _CB["vc"] + h)),
                  pl.BlockSpec((None, None, ATT_BLK, ATT_BLK), lambda b, h, qi: (h, 0, 0, 0)),
                  pl.BlockSpec((None, None, ATT_BLK, ATT_BLK), lambda b, h, qi: (h, 1, 0, 0))],
        out_specs=pl.BlockSpec(blk, lambda b, h, qi: (b, qi, h)),
        out_shape=jax.ShapeDtypeStruct((bn, s, BRANCH_W), BF16),
        scratch_shapes=[state, state, state],
        compiler_params=_cparams("parallel", "parallel", "arbitrary"),
        name="diff_attn_prompt",
    )(t5_flat, lam_params, gain.reshape(1, C_VDIM), p3, p3, p3, p3, p3, t5tiles, t5tiles)


def _diff_attn_sample(p3, k_cache, v_cache, bias_cache, bias_new, lam_init, lam_params, gain):
    bn, t, _ = p3.shape
    past = k_cache.shape[1]
    new = (None, t, LANES)
    old = (None, past, LANES)
    return pl.pallas_call(
        functools.partial(_diff_sample_kernel, lam_init),
        grid=(bn, C_HEADS),
        in_specs=[pl.BlockSpec((4, HEAD_DIM), lambda b, h: (0, 0)),
                  pl.BlockSpec((1, C_VDIM), lambda b, h: (0, 0)),
                  pl.BlockSpec(new, lambda b, h: (b, 0, _CB["qc"] + h)),
                  pl.BlockSpec(old, lambda b, h: (b, 0, h)),
                  pl.BlockSpec(old, lambda b, h: (b, 0, h)),
                  pl.BlockSpec(new, lambda b, h: (b, 0, _CB["kc"] + h)),
                  pl.BlockSpec(new, lambda b, h: (b, 0, _CB["vc"] + h)),
                  pl.BlockSpec((None, t, past), lambda b, h: (h, 0, 0)),
                  pl.BlockSpec((None, t, t), lambda b, h: (h, 0, 0))],
        out_specs=pl.BlockSpec(new, lambda b, h: (b, 0, h)),
        out_shape=jax.ShapeDtypeStruct((bn, t, BRANCH_W), BF16),
        compiler_params=_cparams("parallel", "parallel"),
        name="diff_attn_sample",
    )(lam_params, gain.reshape(1, C_VDIM), p3, k_cache, v_cache, p3, p3, bias_cache, bias_new)


def _merge_kernel(x_ref, oa_ref, ob_ref, oc_ref, g0_ref, g1_ref, g2_ref, bg_ref, wb_ref, wo_ref, o_ref):
    h = None
    for n, (o_r, g_r) in enumerate(((oa_ref, g0_ref), (ob_ref, g1_ref), (oc_ref, g2_ref))):
        gate = 1.0 / (1.0 + jnp.exp(-(g_r[...] + bg_ref[n:n + 1, :])))
        t = gate * _dot(o_r[...], wb_ref[n])
        h = t if h is None else h + t
    o_ref[...] = x_ref[...] + _dot(h.astype(BF16), wo_ref[...])


def _merge(x, o_a, o_b, o_c, p, b_gate, w_branch, w_out, tm):
    t, d = x.shape
    row = lambda i: (i, 0)
    fixed2 = lambda i: (0, 0)
    return pl.pallas_call(
        _merge_kernel,
        grid=(t // tm,),
        in_specs=[pl.BlockSpec((tm, d), row),
                  pl.BlockSpec((tm, BRANCH_W), row),
                  pl.BlockSpec((tm, BRANCH_W), row),
                  pl.BlockSpec((tm, BRANCH_W), row),
                  pl.BlockSpec((tm, d), lambda i: (i, 0)),
                  pl.BlockSpec((tm, d), lambda i: (i, 1)),
                  pl.BlockSpec((tm, d), lambda i: (i, 2)),
                  pl.BlockSpec((N_BRANCH, d), fixed2),
                  pl.BlockSpec((N_BRANCH, BRANCH_W, d), lambda i: (0, 0, 0)),
                  pl.BlockSpec((d, d), fixed2)],
        out_specs=pl.BlockSpec((tm, d), row),
        out_shape=jax.ShapeDtypeStruct((t, d), F32),
        compiler_params=_cparams("parallel"),
        name="gated_merge",
    )(x, o_a, o_b, o_c, p, p, p, b_gate, w_branch, w_out)


FFN_TF = D_FF // 2
FFN_PREV = 16


def _gelu_tanh(x):
    return x * (0.5 * (1.0 + jnp.tanh(math.sqrt(2.0 / math.pi) * (x + 0.044715 * (x * x * x)))))


def _ffn_kernel(tiles_per_seq, x_ref, xp_ref, gn_ref, wg_ref, wu_ref, cw_ref, cb_ref, wd_ref, st_ref,
                o_ref, so_ref, xn_ref, xpn_ref):
    i = pl.program_id(0)
    j = pl.program_id(1)
    whole_seqs = tiles_per_seq == 0

    @pl.when(j == 0)
    def _():
        xn_ref[...] = _rms(x_ref[...], gn_ref[...]).astype(BF16)
        if not whole_seqs:
            xpn_ref[...] = _rms(xp_ref[...], gn_ref[...]).astype(BF16)

    xn = xn_ref[...]
    tm = xn.shape[0]
    tf = wg_ref.shape[1]
    g = _dot(xn, wg_ref[...])
    u = _dot(xn, wu_ref[...])
    if whole_seqs:
        ns = st_ref.shape[0]
        st = st_ref[...]
        g = g.reshape(ns, tm // ns, tf)
        u = u.reshape(ns, tm // ns, tf)
        pm2, pm1 = st[:, 0:1, :], st[:, 1:2, :]
    else:
        gp = _dot(xpn_ref[...], wg_ref[...])
        first = (i % tiles_per_seq) == 0
        st = st_ref[0]
        pm2 = jnp.where(first, st[0:1, :], gp[FFN_PREV - 2:FFN_PREV - 1, :])
        pm1 = jnp.where(first, st[1:2, :], gp[FFN_PREV - 1:FFN_PREV, :])
    ax = g.ndim - 2
    row = lax.broadcasted_iota(jnp.int32, g.shape, ax)
    g1 = jnp.where(row == 0, pm1, pltpu.roll(g, 1, ax))
    g2 = jnp.where(row == 0, pm2, jnp.where(row == 1, pm1, pltpu.roll(g, 2, ax)))
    cw = cw_ref[...]
    c = cb_ref[...] + cw[0:1, :] * g2 + cw[1:2, :] * g1 + cw[2:3, :] * g
    hid = (_gelu_tanh(c) * u).reshape(tm, tf).astype(BF16)
    contrib = _dot(hid, wd_ref[...])

    @pl.when(j == 0)
    def _():
        o_ref[...] = x_ref[...] + contrib

    @pl.when(j > 0)
    def _():
        o_ref[...] += contrib

    seq_len = g.shape[ax]
    if whole_seqs:
        so_ref[...] = g[:, seq_len - 2:seq_len, :]
    else:
        so_ref[0] = g[seq_len - 2:seq_len, :]


def _conv_ffn(x, g_norm, w_gate, w_up, conv_w, conv_b, w_down, state, seq_len, tm):
    t, d = x.shape
    nseq = t // seq_len
    nf = D_FF // FFN_TF
    if seq_len >= tm:
        tiles_per_seq = seq_len // tm
        st_spec = pl.BlockSpec((1, 2, FFN_TF), lambda i, j: (i // tiles_per_seq, 0, j))
    else:
        tiles_per_seq = 0
        st_spec = pl.BlockSpec((tm // seq_len, 2, FFN_TF), lambda i, j: (i, 0, j))
    prev_blocks = tm // FFN_PREV
    tail_spec = pl.BlockSpec(st_spec.block_shape, lambda i, j: (i, 0, j))
    n_tail = t // tm * st_spec.block_shape[0]
    y, tails = pl.pallas_call(
        functools.partial(_ffn_kernel, tiles_per_seq),
        grid=(t // tm, nf),
        in_specs=[pl.BlockSpec((tm, d), lambda i, j: (i, 0)),
                  pl.BlockSpec((FFN_PREV, d), lambda i, j: (jnp.maximum(i * prev_blocks - 1, 0), 0)),
                  pl.BlockSpec((1, d), lambda i, j: (0, 0)),
                  pl.BlockSpec((d, FFN_TF), lambda i, j: (0, j)),
                  pl.BlockSpec((d, FFN_TF), lambda i, j: (0, j)),
                  pl.BlockSpec((3, FFN_TF), lambda i, j: (0, j)),
                  pl.BlockSpec((1, FFN_TF), lambda i, j: (0, j)),
                  pl.BlockSpec((FFN_TF, d), lambda i, j: (j, 0)),
                  st_spec],
        out_specs=[pl.BlockSpec((tm, d), lambda i, j: (i, 0)), tail_spec],
        out_shape=[jax.ShapeDtypeStruct((t, d), F32),
                   jax.ShapeDtypeStruct((n_tail, 2, D_FF), F32)],
        scratch_shapes=[pltpu.VMEM((tm, d), BF16), pltpu.VMEM((FFN_PREV, d), BF16)],
        compiler_params=_cparams("arbitrary", "arbitrary"),
        name="conv_ffn_prompt" if tiles_per_seq else "conv_ffn_sample",
    )(x, x, g_norm.reshape(1, d), w_gate, w_up, conv_w, conv_b.reshape(1, D_FF), w_down, state)
    return y, tails[n_tail // nseq - 1::n_tail // nseq]


def _final_norm_kernel(x_ref, g_ref, o_ref):
    o_ref[...] = _rms(x_ref[...], g_ref[...])


def _final_norm(x, g, tm):
    t, d = x.shape
    return pl.pallas_call(
        _final_norm_kernel,
        grid=(t // tm,),
        in_specs=[pl.BlockSpec((tm, d), lambda i: (i, 0)), pl.BlockSpec((1, d), lambda i: (0, 0))],
        out_specs=pl.BlockSpec((tm, d), lambda i: (i, 0)),
        out_shape=jax.ShapeDtypeStruct((t, d), F32),
        compiler_params=_cparams("parallel"),
        name="final_norm",
    )(x, g.reshape(1, d))


PROJ_TM, PROJ_TN = 1024, 1280
ROW_TM = 512


def _lam_init(layer):
    return 0.8 - 0.6 * math.exp(-0.3 * layer)


def _cols(p3, name):
    c0 = _CB[name] * LANES
    return p3[..., c0:c0 + BRANCH_W]


def _trunk(x, caches, prm, abias, t5tiles, t5_flat):
    bn, t, d = x.shape
    x2 = x.reshape(bn * t, d)
    states = [[] for _ in range(7)]
    depth = prm["w_in"].shape[0]
    for l in range(depth):
        p = _norm_proj(x2, prm["norm_mix"][l], prm["w_in"][l], PROJ_TM, PROJ_TN)
        p3 = p.reshape(bn, t, IN_COLS)
        ka, va, kb, vb, kc, vc = (_cols(p3, n) for n in ("ka", "va", "kb", "vb", "kc", "vc"))
        lam_init = _lam_init(l)
        if caches is None:
            o_a = _band_attn_prompt(p3, abias[l])
            o_b = _sb_attn_prompt(p3)
            o_c = _diff_attn_prompt(p3, t5tiles, lam_init, t5_flat, prm["c_lambda"][l], prm["c_subln"][l])
            keep = min(A_PAST_CHUNKS * CHUNK, t)
            st_a_k, st_a_v = ka[:, t - keep:], va[:, t - keep:]
            conv_state = jnp.zeros((bn, 2, D_FF), F32)
        else:
            ca_k, ca_v, cb_k, cb_v, cc_k, cc_v, st_conv = (c[l] for c in caches)
            flat = lambda a: a.reshape(a.shape[0], a.shape[1], BRANCH_W)
            keep = ca_k.shape[1]
            bias_cache = jnp.concatenate([abias[l][:, 0, :t, :], abias[l][:, 1, :t, :]], axis=-1)[..., -keep:]
            bias_new = abias[l][:, 2, :t, :t]
            o_a = _band_attn_sample(p3, flat(ca_k), flat(ca_v), bias_cache, bias_new)
            o_b = _sb_attn_sample(p3, flat(cb_k), flat(cb_v))
            past = cc_k.shape[1]
            far = jnp.broadcast_to(prm["t5_bias"][T5_FAR_BUCKET][:, None, None], (C_HEADS, t, past - ATT_BLK))
            bias_c = jnp.concatenate([far, t5tiles[:, 1, :t, :]], axis=-1)
            o_c = _diff_attn_sample(p3, flat(cc_k), flat(cc_v), bias_c, t5tiles[:, 0, :t, :t], lam_init,
                                    prm["c_lambda"][l], prm["c_subln"][l])
            st_a_k = jnp.concatenate([flat(ca_k), ka], axis=1)[:, t:]
            st_a_v = jnp.concatenate([flat(ca_v), va], axis=1)[:, t:]
            conv_state = st_conv
        flat2 = lambda a: a.reshape(bn * t, BRANCH_W)
        x2 = _merge(x2, flat2(o_a), flat2(o_b), flat2(o_c), p, prm["b_gate"][l],
                    prm["w_branch"][l], prm["w_out"][l], ROW_TM)
        x2, st_conv = _conv_ffn(x2, prm["norm_ffn"][l], prm["w_gate"][l], prm["w_up"][l], prm["conv_w"][l],
                                prm["conv_b"][l], prm["w_down"][l], conv_state, t, ROW_TM)
        heads = lambda a: a.reshape(a.shape[0], a.shape[1], A_HEADS, HEAD_DIM)
        new = (heads(st_a_k), heads(st_a_v), heads(kb), heads(vb),
               kc.reshape(bn, t, C_HEADS, 2, HEAD_DIM), vc.reshape(bn, t, C_HEADS, C_VDIM), st_conv)
        for lst, s in zip(states, new):
            lst.append(s)
    y = _final_norm(x2, prm["norm_final"], ROW_TM).reshape(bn, t, d)
    return y, tuple(jnp.stack(s, axis=0) for s in states)


def kernel(x_prompt, x_sample, cache_a_k, cache_a_v, cache_b_k, cache_b_v, cache_c_k, cache_c_v,
           state_ffn_conv, norm_mix, w_in, b_gate, a_rel_bias, t5_bias, c_lambda, c_subln,
           w_branch, w_out, norm_ffn, w_up, conv_w, conv_b, w_down, norm_final):
    split = IN_COLS - GATE_W
    prm = dict(
        norm_mix=norm_mix,
        w_in=jnp.concatenate([w_in[..., split:], w_in[..., :split]], axis=-1).astype(BF16),
        b_gate=b_gate, c_lambda=c_lambda, c_subln=c_subln, t5_bias=t5_bias,
        w_branch=w_branch.astype(BF16), w_out=w_out.astype(BF16), norm_ffn=norm_ffn,
        w_gate=w_up[..., :D_FF].astype(BF16), w_up=w_up[..., D_FF:].astype(BF16),
        conv_w=conv_w, conv_b=conv_b, w_down=w_down.astype(BF16), norm_final=norm_final)
    abias = [_build_abias(a_rel_bias[l]) for l in range(a_rel_bias.shape[0])]
    t5tiles = _build_t5bias(t5_bias)
    t5_flat = t5_bias.reshape(-1)
    y_p, p_states = _trunk(x_prompt, None, prm, abias, t5tiles, t5_flat)
    caches = (cache_a_k, cache_a_v, cache_b_k, cache_b_v, cache_c_k, cache_c_v, state_ffn_conv)
    y_s, s_states = _trunk(x_sample, caches, prm, abias, t5tiles, t5_flat)
    return (y_p, y_s) + p_states + s_states
```

```python
import functools
import math

import jax
import jax.numpy as jnp
from jax import lax
from jax.experimental import pallas as pl
from jax.experimental.pallas import tpu as pltpu

F32 = jnp.float32
BF16 = jnp.bfloat16

D_MODEL = 1024
CHUNK = 64
HEAD_DIM = 64
A_HEADS = 8
A_PAST_CHUNKS = 8
A_REL_CLIP = 128
B_HEADS = 8
C_HEADS = 4
C_VDIM = 128
T5_BUCKETS = 32
D_FF = 2816
N_BRANCH = 3
BRANCH_W = 512
GATE_W = N_BRANCH * D_MODEL
IN_COLS = 9 * BRANCH_W + GATE_W
EPS = 1e-6
NEG_INF = -1e30
SCALE = HEAD_DIM ** -0.5
LOG2E = math.log2(math.e)
SOFTMAX_QSCALE = SCALE * LOG2E

LANES = 128
_CB = {name: (GATE_W + i * BRANCH_W) // LANES
       for i, name in enumerate(("qa", "ka", "va", "qb", "kb", "vb", "qc", "kc", "vc"))}

ATT_BLK = 256
SB_LOG_CUT = -104.0
T5_FAR_BUCKET = 15
DIFF_FAR_GROUP = 4
V7X_VMEM_BYTES = 64 * 1024 * 1024
VMEM_LIMIT = V7X_VMEM_BYTES - 8 * 1024 * 1024


def _cparams(*sem):
    return pltpu.CompilerParams(dimension_semantics=sem, vmem_limit_bytes=VMEM_LIMIT)


def _rms(x, g):
    return x * lax.rsqrt(jnp.mean(x * x, axis=-1, keepdims=True) + EPS) * g


def _dot(a, b):
    return jnp.dot(a, b, preferred_element_type=F32)


def _dot_nt(a, b):
    return lax.dot_general(a, b, (((1,), (1,)), ((), ())), preferred_element_type=F32)


def _norm_proj_kernel(x_ref, g_ref, w_ref, o_ref, xn_ref):
    @pl.when(pl.program_id(1) == 0)
    def _():
        xn_ref[...] = _rms(x_ref[...], g_ref[...]).astype(BF16)

    o_ref[...] = _dot(xn_ref[...], w_ref[...])


def _norm_proj_into_kernel(x_ref, g_ref, w_ref, stack_ref, o_ref, xn_ref):
    del stack_ref
    _norm_proj_kernel(x_ref, g_ref, w_ref, o_ref, xn_ref)


def _norm_proj(x, g, w, tm, tn, layer, depth, stacked):
    t, d = x.shape
    n = w.shape[1]
    in_specs = [pl.BlockSpec((tm, d), lambda i, j: (i, 0)),
                pl.BlockSpec((1, d), lambda i, j: (0, 0)),
                pl.BlockSpec((d, tn), lambda i, j: (0, j))]
    args = [x, g.reshape(1, d), w]
    if stacked is not None:
        in_specs.append(pl.BlockSpec(memory_space=pl.ANY))
        args.append(stacked)
    return pl.pallas_call(
        _norm_proj_kernel if stacked is None else _norm_proj_into_kernel,
        grid=(t // tm, n // tn),
        in_specs=in_specs,
        out_specs=pl.BlockSpec((None, tm, tn), lambda i, j: (layer, i, j)),
        out_shape=jax.ShapeDtypeStruct((depth, t, n), F32),
        scratch_shapes=[pltpu.VMEM((tm, d), BF16)],
        input_output_aliases={} if stacked is None else {3: 0},
        compiler_params=_cparams("parallel", "arbitrary"),
        name="norm_proj",
    )(*args)


def _abias_kernel(tbl_ref, o_ref):
    h = pl.program_id(0)
    sub = 8
    ql = lax.broadcasted_iota(jnp.int32, (ATT_BLK, ATT_BLK), 0)
    kl = lax.broadcasted_iota(jnp.int32, (ATT_BLK, ATT_BLK), 1)
    far = jnp.full((ATT_BLK, ATT_BLK), tbl_ref[2 * A_REL_CLIP * A_HEADS + h] * LOG2E, F32)
    o_ref[0, 0] = jnp.where((kl >> 6) >= (ql >> 6), far, NEG_INF)
    lead = ATT_BLK - sub
    b = lax.broadcasted_iota(jnp.int32, (sub, 2 * ATT_BLK), 0)
    x = lax.broadcasted_iota(jnp.int32, (sub, 2 * ATT_BLK), 1)
    for seg, off in ((1, ATT_BLK), (2, 0)):
        idx = jnp.clip(b + lead - x + off, -A_REL_CLIP, A_REL_CLIP) + A_REL_CLIP

        def body(r, acc, idx=idx):
            return jnp.where(idx == r, tbl_ref[r * A_HEADS + h] * LOG2E, acc)

        strip = lax.fori_loop(0, 2 * A_REL_CLIP + 1, body, jnp.zeros((sub, 2 * ATT_BLK), F32))
        for a in range(ATT_BLK // sub):
            o_ref[0, seg, a * sub:(a + 1) * sub, :] = strip[:, lead - a * sub:lead - a * sub + ATT_BLK]
    o_ref[0, 2] = jnp.where((kl >> 6) <= (ql >> 6), o_ref[0, 2], NEG_INF)


def _build_abias(table):
    return pl.pallas_call(
        _abias_kernel,
        grid=(A_HEADS,),
        in_specs=[pl.BlockSpec(memory_space=pltpu.SMEM)],
        out_specs=pl.BlockSpec((1, 3, ATT_BLK, ATT_BLK), lambda h: (h, 0, 0, 0)),
        out_shape=jax.ShapeDtypeStruct((A_HEADS, 3, ATT_BLK, ATT_BLK), F32),
        compiler_params=_cparams("parallel"),
        name="band_bias_tiles",
    )(table.reshape(-1))


_T5_LARGE_STEPS = (12, 16, 23, 32, 46, 64, 91)


def _t5bias_kernel(tbl_ref, o_ref):
    h = pl.program_id(0)
    rows = 64
    for seg, off in ((0, 0), (1, -ATT_BLK)):
        for rc in range(ATT_BLK // rows):
            ql = lax.broadcasted_iota(jnp.int32, (rows, ATT_BLK), 0) + rc * rows
            kl = lax.broadcasted_iota(jnp.int32, (rows, ATT_BLK), 1)
            rel = kl - ql + off
            n = jnp.abs(rel)
            large = jnp.full((rows, ATT_BLK), T5_BUCKETS // 4, jnp.int32)
            for th in _T5_LARGE_STEPS:
                large = large + jnp.where(n >= th, 1, 0)
            bucket = jnp.where(rel > 0, T5_BUCKETS // 2, 0) + jnp.where(n < T5_BUCKETS // 4, n, large)
            tile = jnp.zeros((rows, ATT_BLK), F32)
            for r in range(T5_BUCKETS):
                tile = jnp.where(bucket == r, tbl_ref[r * C_HEADS + h] * LOG2E, tile)
            if seg == 0:
                tile = jnp.where((kl >> 6) <= (ql >> 6), tile, NEG_INF)
            o_ref[0, seg, rc * rows:(rc + 1) * rows, :] = tile


def _build_t5bias(table):
    return pl.pallas_call(
        _t5bias_kernel,
        grid=(C_HEADS,),
        in_specs=[pl.BlockSpec(memory_space=pltpu.SMEM)],
        out_specs=pl.BlockSpec((1, 2, ATT_BLK, ATT_BLK), lambda h: (h, 0, 0, 0)),
        out_shape=jax.ShapeDtypeStruct((C_HEADS, 2, ATT_BLK, ATT_BLK), F32),
        compiler_params=_cparams("parallel"),
        name="t5_bias_tiles",
    )(table.reshape(-1))


def _pspec(block, index_map, layer):
    return pl.BlockSpec((None,) + block, lambda *g: (layer,) + tuple(index_map(*g)))


def _lane_tile(x, n):
    return x if n == 1 else jnp.concatenate([x] * n, axis=1)


def _half_lanes(q, upper):
    lane = lax.broadcasted_iota(jnp.int32, q.shape, 1)
    return jnp.where((lane >= HEAD_DIM) == upper, q, 0.0).astype(BF16)


def _softmax_segments(qh, kb, vb, biases):
    s = [_dot_nt(qh, k) + b for k, b in zip(kb, biases)]
    m = functools.reduce(jnp.maximum, [jnp.max(sj, axis=1, keepdims=True) for sj in s])
    p = [jnp.exp2(sj - m) for sj in s]
    l = functools.reduce(jnp.add, [jnp.sum(pj, axis=1, keepdims=True) for pj in p])
    acc = functools.reduce(jnp.add, [_dot(pj.astype(BF16), v) for pj, v in zip(p, vb)])
    return acc / l


def _band_kernel(nseg, min_qi, q_ref, *refs):
    k_refs, v_refs, b_refs = refs[:nseg], refs[nseg:2 * nseg], refs[2 * nseg:3 * nseg]
    o_ref = refs[3 * nseg]
    qi = pl.program_id(2)
    q = q_ref[...] * SOFTMAX_QSCALE
    kb = [k_refs[j][...].astype(BF16) for j in range(nseg)]
    vb = [v_refs[j][...].astype(BF16) for j in range(nseg)]
    outs = []
    for h in range(2):
        biases = [b_refs[j][h] if min_qi[j] == 0 else jnp.where(qi >= min_qi[j], b_refs[j][h], NEG_INF)
                  for j in range(nseg)]
        outs.append(_softmax_segments(_half_lanes(q, bool(h)), kb, vb, biases))
    lane = lax.broadcasted_iota(jnp.int32, q.shape, 1)
    o_ref[...] = jnp.where(lane < HEAD_DIM, outs[0], outs[1]).astype(o_ref.dtype)


def _band_attn_prompt(p4, layer, abias):
    _, bn, s, _ = p4.shape
    nq = s // ATT_BLK
    blk = (None, ATT_BLK, LANES)

    def kv_spec(name, back):
        return _pspec(blk, lambda b, hp, qi: (b, jnp.maximum(qi - back, 0), _CB[name] + hp), layer)

    def bias_spec(seg):
        return pl.BlockSpec((2, None, ATT_BLK, ATT_BLK), lambda b, hp, qi: (hp, seg, 0, 0))

    return pl.pallas_call(
        functools.partial(_band_kernel, 3, (2, 1, 0)),
        grid=(bn, A_HEADS // 2, nq),
        in_specs=([_pspec(blk, lambda b, hp, qi: (b, qi, _CB["qa"] + hp), layer)]
                  + [kv_spec("ka", back) for back in (2, 1, 0)]
                  + [kv_spec("va", back) for back in (2, 1, 0)]
                  + [bias_spec(seg) for seg in range(3)]),
        out_specs=pl.BlockSpec(blk, lambda b, hp, qi: (b, qi, hp)),
        out_shape=jax.ShapeDtypeStruct((bn, s, BRANCH_W), BF16),
        compiler_params=_cparams("parallel", "parallel", "arbitrary"),
        name="band_attn_prompt",
    )(p4, p4, p4, p4, p4, p4, p4, abias, abias, abias)


def _band_attn_sample(p4, layer, k_cache, v_cache, bias_cache, bias_new):
    _, bn, t, _ = p4.shape
    keep = k_cache.shape[1]
    new = (None, t, LANES)
    old = (None, keep, LANES)
    return pl.pallas_call(
        functools.partial(_band_kernel, 2, (0, 0)),
        grid=(bn, A_HEADS // 2, 1),
        in_specs=[_pspec(new, lambda b, hp, qi: (b, 0, _CB["qa"] + hp), layer),
                  pl.BlockSpec(old, lambda b, hp, qi: (b, 0, hp)),
                  _pspec(new, lambda b, hp, qi: (b, 0, _CB["ka"] + hp), layer),
                  pl.BlockSpec(old, lambda b, hp, qi: (b, 0, hp)),
                  _pspec(new, lambda b, hp, qi: (b, 0, _CB["va"] + hp), layer),
                  pl.BlockSpec((2, t, keep), lambda b, hp, qi: (hp, 0, 0)),
                  pl.BlockSpec((2, t, t), lambda b, hp, qi: (hp, 0, 0))],
        out_specs=pl.BlockSpec(new, lambda b, hp, qi: (b, 0, hp)),
        out_shape=jax.ShapeDtypeStruct((bn, t, BRANCH_W), BF16),
        compiler_params=_cparams("parallel", "parallel", "arbitrary"),
        name="band_attn_sample",
    )(p4, k_cache, p4, v_cache, p4, bias_cache, bias_new)


def _strict_upper(n):
    r = lax.broadcasted_iota(jnp.int32, (n, n), 0)
    c = lax.broadcasted_iota(jnp.int32, (n, n), 1)
    return jnp.where(r > c, 1.0, 0.0).astype(BF16)


def _sb_fold(qh, k, v, upper, mask, carry_ref, acc_ref, first):
    kb = k.astype(BF16)
    vb = v.astype(BF16)
    tq, tk = qh[0].shape[0], kb.shape[0]
    top = None
    for h in range(2):
        z = _dot_nt(qh[h], kb)
        soft = jnp.log1p(jnp.exp(-jnp.abs(z)))
        log_keep = -(jnp.maximum(z, 0.0) + soft)
        log_take = jnp.minimum(z, 0.0) - soft
        if mask is not None:
            log_keep = jnp.where(mask, log_keep, 0.0)
        hi = log_keep.astype(BF16)
        lo = (log_keep - hi.astype(F32)).astype(BF16)
        after = _dot(hi, upper) + _dot(lo, upper)
        if not first:
            after = after + _lane_tile(carry_ref[h], tk // LANES)
        w = jnp.exp(log_take + after)
        if mask is not None:
            w = jnp.where(mask, w, 0.0)
        pv = _dot(w.astype(BF16), vb)
        total = jnp.sum(log_keep, axis=1, keepdims=True)
        if first:
            acc_ref[h] = pv
            carry = jnp.broadcast_to(total, (tq, LANES))
        else:
            acc_ref[h] += pv
            carry = carry_ref[h] + total
        carry_ref[h] = carry
        top = jnp.max(carry) if top is None else jnp.maximum(top, jnp.max(carry))
    return (top > SB_LOG_CUT).astype(jnp.int32)


def _sb_kernel(tk, n_past_static, q_ref, kd_ref, vd_ref, kp_ref, vp_ref, o_ref, carry_ref, acc_ref):
    qi = pl.program_id(2)
    n_past = qi if n_past_static is None else n_past_static
    q = q_ref[...] * SCALE
    tq = q.shape[0]
    qh = [_half_lanes(q, False), _half_lanes(q, True)]
    rq = lax.broadcasted_iota(jnp.int32, (tq, tq), 0)
    ck = lax.broadcasted_iota(jnp.int32, (tq, tq), 1)
    live = _sb_fold(qh, kd_ref[...], vd_ref[...], _strict_upper(tq), ck < rq, carry_ref, acc_ref, True)

    def cond(st):
        return jnp.logical_and(st[0] >= 0, st[1] > 0)

    def body(st):
        start = pl.multiple_of(st[0] * tk, tk)
        live = _sb_fold(qh, kp_ref[pl.ds(start, tk), :], vp_ref[pl.ds(start, tk), :], _strict_upper(tk),
                        None, carry_ref, acc_ref, False)
        return st[0] - 1, live

    lax.while_loop(cond, body, (n_past - 1, live))
    lane = lax.broadcasted_iota(jnp.int32, q.shape, 1)
    o_ref[...] = jnp.where(lane < HEAD_DIM, acc_ref[0], acc_ref[1]).astype(o_ref.dtype)


def _sb_attn_prompt(p4, layer):
    _, bn, s, _ = p4.shape
    blk = (None, ATT_BLK, LANES)
    full = (None, s, LANES)
    return pl.pallas_call(
        functools.partial(_sb_kernel, ATT_BLK, None),
        grid=(bn, B_HEADS // 2, s // ATT_BLK),
        in_specs=[_pspec(blk, lambda b, hp, qi: (b, qi, _CB["qb"] + hp), layer),
                  _pspec(blk, lambda b, hp, qi: (b, qi, _CB["kb"] + hp), layer),
                  _pspec(blk, lambda b, hp, qi: (b, qi, _CB["vb"] + hp), layer),
                  _pspec(full, lambda b, hp, qi: (b, 0, _CB["kb"] + hp), layer),
                  _pspec(full, lambda b, hp, qi: (b, 0, _CB["vb"] + hp), layer)],
        out_specs=pl.BlockSpec(blk, lambda b, hp, qi: (b, qi, hp)),
        out_shape=jax.ShapeDtypeStruct((bn, s, BRANCH_W), BF16),
        scratch_shapes=[pltpu.VMEM((2, ATT_BLK, LANES), F32), pltpu.VMEM((2, ATT_BLK, LANES), F32)],
        compiler_params=_cparams("parallel", "parallel", "arbitrary"),
        name="stick_attn_prompt",
    )(p4, p4, p4, p4, p4)


def _sb_attn_sample(p4, layer, k_cache, v_cache):
    _, bn, t, _ = p4.shape
    past = k_cache.shape[1]
    new = (None, t, LANES)
    old = (None, past, LANES)
    return pl.pallas_call(
        functools.partial(_sb_kernel, ATT_BLK, past // ATT_BLK),
        grid=(bn, B_HEADS // 2, 1),
        in_specs=[_pspec(new, lambda b, hp, qi: (b, 0, _CB["qb"] + hp), layer),
                  _pspec(new, lambda b, hp, qi: (b, 0, _CB["kb"] + hp), layer),
                  _pspec(new, lambda b, hp, qi: (b, 0, _CB["vb"] + hp), layer),
                  pl.BlockSpec(old, lambda b, hp, qi: (b, 0, hp)),
                  pl.BlockSpec(old, lambda b, hp, qi: (b, 0, hp))],
        out_specs=pl.BlockSpec(new, lambda b, hp, qi: (b, 0, hp)),
        out_shape=jax.ShapeDtypeStruct((bn, t, BRANCH_W), BF16),
        scratch_shapes=[pltpu.VMEM((2, t, LANES), F32), pltpu.VMEM((2, t, LANES), F32)],
        compiler_params=_cparams("parallel", "parallel", "arbitrary"),
        name="stick_attn_sample",
    )(p4, p4, p4, k_cache, v_cache)


def _diff_combine(o0, o1, lam_init, lam_ref, gain_ref):
    lp = lam_ref[...]
    lam = (jnp.exp(jnp.sum(lp[0:1] * lp[1:2], axis=1, keepdims=True))
           - jnp.exp(jnp.sum(lp[2:3] * lp[3:4], axis=1, keepdims=True)) + lam_init)
    return _rms(o0 - lam * o1, gain_ref[...]) * (1.0 - lam_init)


def _diff_fold(qm, k, v, bias, m_ref, l_ref, acc_ref, first):
    kb = k.astype(BF16)
    vb = v.astype(BF16)
    tq, tk = qm[0].shape[0], kb.shape[0]
    for m in range(2):
        s = _dot_nt(qm[m], kb) + bias
        m_cur = jnp.max(s, axis=1, keepdims=True)
        m_new = jnp.broadcast_to(m_cur, (tq, LANES)) if first else jnp.maximum(m_ref[m], m_cur)
        p = jnp.exp2(s - _lane_tile(m_new, tk // LANES))
        row = jnp.sum(p, axis=1, keepdims=True)
        pv = _dot(p.astype(BF16), vb)
        if first:
            l_ref[m] = jnp.broadcast_to(row, (tq, LANES))
            acc_ref[m] = pv
        else:
            alpha = jnp.exp2(m_ref[m] - m_new)
            l_ref[m] = alpha * l_ref[m] + row
            acc_ref[m] = alpha * acc_ref[m] + pv
        m_ref[m] = m_new


def _diff_prompt_kernel(lam_init, tbl_ref, lam_ref, gain_ref, q_ref, kd_ref, vd_ref, kp_ref, vp_ref,
                        bd_ref, bp_ref, o_ref, m_ref, l_ref, acc_ref):
    h = pl.program_id(1)
    qi = pl.program_id(2)
    q = q_ref[...] * SOFTMAX_QSCALE
    blk = q.shape[0]
    qm = [_half_lanes(q, False), _half_lanes(q, True)]
    far = tbl_ref[T5_FAR_BUCKET * C_HEADS + h] * LOG2E

    near = pl.multiple_of(jnp.maximum(qi - 1, 0) * blk, blk)
    k0 = jnp.concatenate([kp_ref[pl.ds(near, blk), :], kd_ref[...]], axis=0)
    v0 = jnp.concatenate([vp_ref[pl.ds(near, blk), :], vd_ref[...]], axis=0)
    b0 = jnp.concatenate([jnp.where(qi >= 1, bp_ref[...], NEG_INF), bd_ref[...]], axis=1)
    _diff_fold(qm, k0, v0, b0, m_ref, l_ref, acc_ref, True)

    n_far = jnp.maximum(qi - 1, 0)
    n_group = n_far // DIFF_FAR_GROUP
    wide = DIFF_FAR_GROUP * blk

    def body(j, c):
        start = pl.multiple_of(j * wide, wide)
        _diff_fold(qm, kp_ref[pl.ds(start, wide), :], vp_ref[pl.ds(start, wide), :], far,
                   m_ref, l_ref, acc_ref, False)
        return c

    lax.fori_loop(0, n_group, body, 0)
    rest = pl.multiple_of(n_group * wide, wide)
    for r in range(1, DIFF_FAR_GROUP):
        @pl.when(n_far - n_group * DIFF_FAR_GROUP == r)
        def _(r=r):
            _diff_fold(qm, kp_ref[pl.ds(rest, r * blk), :], vp_ref[pl.ds(rest, r * blk), :], far,
                       m_ref, l_ref, acc_ref, False)

    o = _diff_combine(acc_ref[0] / l_ref[0], acc_ref[1] / l_ref[1], lam_init, lam_ref, gain_ref)
    o_ref[...] = o.astype(o_ref.dtype)


def _diff_sample_kernel(lam_init, lam_ref, gain_ref, q_ref, kc_ref, vc_ref, kn_ref, vn_ref,
                        bc_ref, bn_ref, o_ref):
    q = q_ref[...] * SOFTMAX_QSCALE
    kb = [kc_ref[...].astype(BF16), kn_ref[...].astype(BF16)]
    vb = [vc_ref[...].astype(BF16), vn_ref[...].astype(BF16)]
    biases = [bc_ref[...], bn_ref[...]]
    o0 = _softmax_segments(_half_lanes(q, False), kb, vb, biases)
    o1 = _softmax_segments(_half_lanes(q, True), kb, vb, biases)
    o_ref[...] = _diff_combine(o0, o1, lam_init, lam_ref, gain_ref).astype(o_ref.dtype)


def _diff_attn_prompt(p4, layer, t5tiles, lam_init, t5_flat, lam_params, gain):
    _, bn, s, _ = p4.shape
    blk = (None, ATT_BLK, LANES)
    full = (None, s, LANES)
    state = pltpu.VMEM((2, ATT_BLK, LANES), F32)
    return pl.pallas_call(
        functools.partial(_diff_prompt_kernel, lam_init),
        grid=(bn, C_HEADS, s // ATT_BLK),
        in_specs=[pl.BlockSpec(memory_space=pltpu.SMEM),
                  pl.BlockSpec((4, HEAD_DIM), lambda b, h, qi: (0, 0)),
                  pl.BlockSpec((1, C_VDIM), lambda b, h, qi: (0, 0)),
                  _pspec(blk, lambda b, h, qi: (b, qi, _CB["qc"] + h), layer),
                  _pspec(blk, lambda b, h, qi: (b, qi, _CB["kc"] + h), layer),
                  _pspec(blk, lambda b, h, qi: (b, qi, _CB["vc"] + h), layer),
                  _pspec(full, lambda b, h, qi: (b, 0, _CB["kc"] + h), layer),
                  _pspec(full, lambda b, h, qi: (b, 0, _CB["vc"] + h), layer),
                  pl.BlockSpec((None, None, ATT_BLK, ATT_BLK), lambda b, h, qi: (h, 0, 0, 0)),
                  pl.BlockSpec((None, None, ATT_BLK, ATT_BLK), lambda b, h, qi: (h, 1, 0, 0))],
        out_specs=pl.BlockSpec(blk, lambda b, h, qi: (b, qi, h)),
        out_shape=jax.ShapeDtypeStruct((bn, s, BRANCH_W), BF16),
        scratch_shapes=[state, state, state],
        compiler_params=_cparams("parallel", "parallel", "arbitrary"),
        name="diff_attn_prompt",
    )(t5_flat, lam_params, gain.reshape(1, C_VDIM), p4, p4, p4, p4, p4, t5tiles, t5tiles)


def _diff_attn_sample(p4, layer, k_cache, v_cache, bias_cache, bias_new, lam_init, lam_params, gain):
    _, bn, t, _ = p4.shape
    past = k_cache.shape[1]
    new = (None, t, LANES)
    old = (None, past, LANES)
    return pl.pallas_call(
        functools.partial(_diff_sample_kernel, lam_init),
        grid=(bn, C_HEADS),
        in_specs=[pl.BlockSpec((4, HEAD_DIM), lambda b, h: (0, 0)),
                  pl.BlockSpec((1, C_VDIM), lambda b, h: (0, 0)),
                  _pspec(new, lambda b, h: (b, 0, _CB["qc"] + h), layer),
                  pl.BlockSpec(old, lambda b, h: (b, 0, h)),
                  pl.BlockSpec(old, lambda b, h: (b, 0, h)),
                  _pspec(new, lambda b, h: (b, 0, _CB["kc"] + h), layer),
                  _pspec(new, lambda b, h: (b, 0, _CB["vc"] + h), layer),
                  pl.BlockSpec((None, t, past), lambda b, h: (h, 0, 0)),
                  pl.BlockSpec((None, t, t), lambda b, h: (h, 0, 0))],
        out_specs=pl.BlockSpec(new, lambda b, h: (b, 0, h)),
        out_shape=jax.ShapeDtypeStruct((bn, t, BRANCH_W), BF16),
        compiler_params=_cparams("parallel", "parallel"),
        name="diff_attn_sample",
    )(lam_params, gain.reshape(1, C_VDIM), p4, k_cache, v_cache, p4, p4, bias_cache, bias_new)


def _merge_kernel(x_ref, oa_ref, ob_ref, oc_ref, g0_ref, g1_ref, g2_ref, bg_ref, wb_ref, wo_ref, o_ref):
    h = None
    for n, (o_r, g_r) in enumerate(((oa_ref, g0_ref), (ob_ref, g1_ref), (oc_ref, g2_ref))):
        gate = 1.0 / (1.0 + jnp.exp(-(g_r[...] + bg_ref[n:n + 1, :])))
        t = gate * _dot(o_r[...], wb_ref[n])
        h = t if h is None else h + t
    o_ref[...] = x_ref[...] + _dot(h.astype(BF16), wo_ref[...])


def _merge(x, o_a, o_b, o_c, p, layer, b_gate, w_branch, w_out, tm):
    t, d = x.shape
    row = lambda i: (i, 0)
    fixed2 = lambda i: (0, 0)
    return pl.pallas_call(
        _merge_kernel,
        grid=(t // tm,),
        in_specs=[pl.BlockSpec((tm, d), row),
                  pl.BlockSpec((tm, BRANCH_W), row),
                  pl.BlockSpec((tm, BRANCH_W), row),
                  pl.BlockSpec((tm, BRANCH_W), row),
                  pl.BlockSpec((None, tm, d), lambda i: (layer, i, 0)),
                  pl.BlockSpec((None, tm, d), lambda i: (layer, i, 1)),
                  pl.BlockSpec((None, tm, d), lambda i: (layer, i, 2)),
                  pl.BlockSpec((N_BRANCH, d), fixed2),
                  pl.BlockSpec((N_BRANCH, BRANCH_W, d), lambda i: (0, 0, 0)),
                  pl.BlockSpec((d, d), fixed2)],
        out_specs=pl.BlockSpec((tm, d), row),
        out_shape=jax.ShapeDtypeStruct((t, d), F32),
        compiler_params=_cparams("parallel"),
        name="gated_merge",
    )(x, o_a, o_b, o_c, p, p, p, b_gate, w_branch, w_out)


FFN_TF = D_FF // 2
FFN_PREV = 16


def _gelu_tanh(x):
    return x * (0.5 * (1.0 + jnp.tanh(math.sqrt(2.0 / math.pi) * (x + 0.044715 * (x * x * x)))))


def _ffn_kernel(tiles_per_seq, x_ref, xp_ref, gn_ref, wg_ref, wu_ref, cw_ref, cb_ref, wd_ref, st_ref,
                o_ref, so_ref, xn_ref, xpn_ref):
    i = pl.program_id(0)
    j = pl.program_id(1)
    whole_seqs = tiles_per_seq == 0

    @pl.when(j == 0)
    def _():
        xn_ref[...] = _rms(x_ref[...], gn_ref[...]).astype(BF16)
        if not whole_seqs:
            xpn_ref[...] = _rms(xp_ref[...], gn_ref[...]).astype(BF16)

    xn = xn_ref[...]
    tm = xn.shape[0]
    tf = wg_ref.shape[1]
    g = _dot(xn, wg_ref[...])
    u = _dot(xn, wu_ref[...])
    if whole_seqs:
        ns = st_ref.shape[0]
        st = st_ref[...]
        g = g.reshape(ns, tm // ns, tf)
        u = u.reshape(ns, tm // ns, tf)
        pm2, pm1 = st[:, 0:1, :], st[:, 1:2, :]
    else:
        gp = _dot(xpn_ref[...], wg_ref[...])
        first = (i % tiles_per_seq) == 0
        st = st_ref[0]
        pm2 = jnp.where(first, st[0:1, :], gp[FFN_PREV - 2:FFN_PREV - 1, :])
        pm1 = jnp.where(first, st[1:2, :], gp[FFN_PREV - 1:FFN_PREV, :])
    ax = g.ndim - 2
    row = lax.broadcasted_iota(jnp.int32, g.shape, ax)
    g1 = jnp.where(row == 0, pm1, pltpu.roll(g, 1, ax))
    g2 = jnp.where(row == 0, pm2, jnp.where(row == 1, pm1, pltpu.roll(g, 2, ax)))
    cw = cw_ref[...]
    c = cb_ref[...] + cw[0:1, :] * g2 + cw[1:2, :] * g1 + cw[2:3, :] * g
    hid = (_gelu_tanh(c) * u).reshape(tm, tf).astype(BF16)
    contrib = _dot(hid, wd_ref[...])

    @pl.when(j == 0)
    def _():
        o_ref[...] = x_ref[...] + contrib

    @pl.when(j > 0)
    def _():
        o_ref[...] += contrib

    seq_len = g.shape[ax]
    if whole_seqs:
        so_ref[...] = g[:, seq_len - 2:seq_len, :]
    else:
        so_ref[0] = g[seq_len - 2:seq_len, :]


def _conv_ffn(x, g_norm, w_gate, w_up, conv_w, conv_b, w_down, state, seq_len, tm):
    t, d = x.shape
    nseq = t // seq_len
    nf = D_FF // FFN_TF
    if seq_len >= tm:
        tiles_per_seq = seq_len // tm
        st_spec = pl.BlockSpec((1, 2, FFN_TF), lambda i, j: (i // tiles_per_seq, 0, j))
    else:
        tiles_per_seq = 0
        st_spec = pl.BlockSpec((tm // seq_len, 2, FFN_TF), lambda i, j: (i, 0, j))
    prev_blocks = tm // FFN_PREV
    tail_spec = pl.BlockSpec(st_spec.block_shape, lambda i, j: (i, 0, j))
    n_tail = t // tm * st_spec.block_shape[0]
    y, tails = pl.pallas_call(
        functools.partial(_ffn_kernel, tiles_per_seq),
        grid=(t // tm, nf),
        in_specs=[pl.BlockSpec((tm, d), lambda i, j: (i, 0)),
                  pl.BlockSpec((FFN_PREV, d), lambda i, j: (jnp.maximum(i * prev_blocks - 1, 0), 0)),
                  pl.BlockSpec((1, d), lambda i, j: (0, 0)),
                  pl.BlockSpec((d, FFN_TF), lambda i, j: (0, j)),
                  pl.BlockSpec((d, FFN_TF), lambda i, j: (0, j)),
                  pl.BlockSpec((3, FFN_TF), lambda i, j: (0, j)),
                  pl.BlockSpec((1, FFN_TF), lambda i, j: (0, j)),
                  pl.BlockSpec((FFN_TF, d), lambda i, j: (j, 0)),
                  st_spec],
        out_specs=[pl.BlockSpec((tm, d), lambda i, j: (i, 0)), tail_spec],
        out_shape=[jax.ShapeDtypeStruct((t, d), F32),
                   jax.ShapeDtypeStruct((n_tail, 2, D_FF), F32)],
        scratch_shapes=[pltpu.VMEM((tm, d), BF16), pltpu.VMEM((FFN_PREV, d), BF16)],
        compiler_params=_cparams("arbitrary", "arbitrary"),
        name="conv_ffn_prompt" if tiles_per_seq else "conv_ffn_sample",
    )(x, x, g_norm.reshape(1, d), w_gate, w_up, conv_w, conv_b.reshape(1, D_FF), w_down, state)
    return y, tails[n_tail // nseq - 1::n_tail // nseq]


def _final_norm_kernel(x_ref, g_ref, o_ref):
    o_ref[...] = _rms(x_ref[...], g_ref[...])


def _final_norm(x, g, tm):
    t, d = x.shape
    return pl.pallas_call(
        _final_norm_kernel,
        grid=(t // tm,),
        in_specs=[pl.BlockSpec((tm, d), lambda i: (i, 0)), pl.BlockSpec((1, d), lambda i: (0, 0))],
        out_specs=pl.BlockSpec((tm, d), lambda i: (i, 0)),
        out_shape=jax.ShapeDtypeStruct((t, d), F32),
        compiler_params=_cparams("parallel"),
        name="final_norm",
    )(x, g.reshape(1, d))


PROJ_TM, PROJ_TN = 1024, 1280
ROW_TM = 512


def _lam_init(layer):
    return 0.8 - 0.6 * math.exp(-0.3 * layer)


def _cols(p4, name):
    c0 = _CB[name] * LANES
    return p4[..., c0:c0 + BRANCH_W]


def _trunk(x, caches, prm, abias, t5tiles, t5_flat):
    bn, t, d = x.shape
    x2 = x.reshape(bn * t, d)
    depth = prm["w_in"].shape[0]
    conv_states = []
    p_all = None
    for l in range(depth):
        p_all = _norm_proj(x2, prm["norm_mix"][l], prm["w_in"][l], PROJ_TM, PROJ_TN, l, depth, p_all)
        p4 = p_all.reshape(depth, bn, t, IN_COLS)
        lam_init = _lam_init(l)
        if caches is None:
            o_a = _band_attn_prompt(p4, l, abias[l])
            o_b = _sb_attn_prompt(p4, l)
            o_c = _diff_attn_prompt(p4, l, t5tiles, lam_init, t5_flat, prm["c_lambda"][l], prm["c_subln"][l])
            conv_state = jnp.zeros((bn, 2, D_FF), F32)
        else:
            ca_k, ca_v, cb_k, cb_v, cc_k, cc_v, st_conv = (c[l] for c in caches)
            flat = lambda a: a.reshape(a.shape[0], a.shape[1], BRANCH_W)
            keep = ca_k.shape[1]
            bias_cache = jnp.concatenate([abias[l][:, 0, :t, :], abias[l][:, 1, :t, :]], axis=-1)[..., -keep:]
            bias_new = abias[l][:, 2, :t, :t]
            o_a = _band_attn_sample(p4, l, flat(ca_k), flat(ca_v), bias_cache, bias_new)
            o_b = _sb_attn_sample(p4, l, flat(cb_k), flat(cb_v))
            past = cc_k.shape[1]
            far = jnp.broadcast_to(t5tiles[:, 1, :1, :1], (C_HEADS, t, past - ATT_BLK))
            bias_c = jnp.concatenate([far, t5tiles[:, 1, :t, :]], axis=-1)
            o_c = _diff_attn_sample(p4, l, flat(cc_k), flat(cc_v), bias_c, t5tiles[:, 0, :t, :t], lam_init,
                                    prm["c_lambda"][l], prm["c_subln"][l])
            conv_state = st_conv
        flat2 = lambda a: a.reshape(bn * t, BRANCH_W)
        x2 = _merge(x2, flat2(o_a), flat2(o_b), flat2(o_c), p_all, l, prm["b_gate"][l],
                    prm["w_branch"][l], prm["w_out"][l], ROW_TM)
        x2, st_conv = _conv_ffn(x2, prm["norm_ffn"][l], prm["w_gate"][l], prm["w_up"][l], prm["conv_w"][l],
                                prm["conv_b"][l], prm["w_down"][l], conv_state, t, ROW_TM)
        conv_states.append(st_conv)
    y = _final_norm(x2, prm["norm_final"], ROW_TM).reshape(bn, t, d)
    ka, va, kb, vb, kc, vc = (_cols(p4, n) for n in ("ka", "va", "kb", "vb", "kc", "vc"))
    heads = lambda a: a.reshape(a.shape[:3] + (A_HEADS, HEAD_DIM))
    if caches is None:
        keep = min(A_PAST_CHUNKS * CHUNK, t)
        st_a_k, st_a_v = heads(ka[:, :, t - keep:]), heads(va[:, :, t - keep:])
    else:
        st_a_k = jnp.concatenate([caches[0][:, :, t:], heads(ka)], axis=2)
        st_a_v = jnp.concatenate([caches[1][:, :, t:], heads(va)], axis=2)
    states = (st_a_k, st_a_v, heads(kb), heads(vb),
              kc.reshape(depth, bn, t, C_HEADS, 2, HEAD_DIM), vc.reshape(depth, bn, t, C_HEADS, C_VDIM),
              jnp.stack(conv_states, axis=0))
    return y, states


def kernel(x_prompt, x_sample, cache_a_k, cache_a_v, cache_b_k, cache_b_v, cache_c_k, cache_c_v,
           state_ffn_conv, norm_mix, w_in, b_gate, a_rel_bias, t5_bias, c_lambda, c_subln,
           w_branch, w_out, norm_ffn, w_up, conv_w, conv_b, w_down, norm_final):
    split = IN_COLS - GATE_W
    prm = dict(
        norm_mix=norm_mix,
        w_in=jnp.concatenate([w_in[..., split:], w_in[..., :split]], axis=-1).astype(BF16),
        b_gate=b_gate, c_lambda=c_lambda, c_subln=c_subln,
        w_branch=w_branch.astype(BF16), w_out=w_out.astype(BF16), norm_ffn=norm_ffn,
        w_gate=w_up[..., :D_FF].astype(BF16), w_up=w_up[..., D_FF:].astype(BF16),
        conv_w=conv_w, conv_b=conv_b, w_down=w_down.astype(BF16), norm_final=norm_final)
    abias = [_build_abias(a_rel_bias[l]) for l in range(a_rel_bias.shape[0])]
    t5tiles = _build_t5bias(t5_bias)
    t5_flat = t5_bias.reshape(-1)
    y_p, p_states = _trunk(x_prompt, None, prm, abias, t5tiles, t5_flat)
    caches = (cache_a_k, cache_a_v, cache_b_k, cache_b_v, cache_c_k, cache_c_v, state_ffn_conv)
    y_s, s_states = _trunk(x_sample, caches, prm, abias, t5tiles, t5_flat)
    return (y_p, y_s) + p_states + s_states
```

```python
import functools
import math

import jax
import jax.numpy as jnp
from jax import lax
from jax.experimental import pallas as pl
from jax.experimental.pallas import tpu as pltpu

F32 = jnp.float32
BF16 = jnp.bfloat16

D_MODEL = 1024
CHUNK = 64
HEAD_DIM = 64
A_HEADS = 8
A_PAST_CHUNKS = 8
A_REL_CLIP = 128
B_HEADS = 8
C_HEADS = 4
C_VDIM = 128
T5_BUCKETS = 32
D_FF = 2816
N_BRANCH = 3
BRANCH_W = 512
GATE_W = N_BRANCH * D_MODEL
IN_COLS = 9 * BRANCH_W + GATE_W
EPS = 1e-6
NEG_INF = -1e30
SCALE = HEAD_DIM ** -0.5
LOG2E = math.log2(math.e)
SOFTMAX_QSCALE = SCALE * LOG2E

LANES = 128
_GROUPS = ("qa", "ka", "va", "qb", "kb", "vb", "qc", "kc", "vc")
_CB = {name: (GATE_W + i * BRANCH_W) // LANES for i, name in enumerate(_GROUPS)}
_ROW_GROUPS = ("qa", "qb", "qc", "vc")
_T_GROUPS = ("ka", "va", "kb", "vb", "kc")
_CBP = {name: (GATE_W + i * BRANCH_W) // LANES for i, name in enumerate(_ROW_GROUPS)}
ROW_COLS = GATE_W + len(_ROW_GROUPS) * BRANCH_W

ATT_BLK = 256
SB_LOG_CUT = -104.0
T5_FAR_BUCKET = 15
DIFF_FAR_GROUP = 4
V7X_VMEM_BYTES = 64 * 1024 * 1024
VMEM_LIMIT = V7X_VMEM_BYTES - 8 * 1024 * 1024


def _cparams(*sem):
    return pltpu.CompilerParams(dimension_semantics=sem, vmem_limit_bytes=VMEM_LIMIT)


def _rms(x, g):
    return x * lax.rsqrt(jnp.mean(x * x, axis=-1, keepdims=True) + EPS) * g


def _dot(a, b):
    return jnp.dot(a, b, preferred_element_type=F32)


def _dot_nt(a, b):
    return lax.dot_general(a, b, (((1,), (1,)), ((), ())), preferred_element_type=F32)


def _norm_proj_kernel(x_ref, g_ref, w_ref, o_ref, xn_ref):
    @pl.when(pl.program_id(1) == 0)
    def _():
        xn_ref[...] = _rms(x_ref[...], g_ref[...]).astype(BF16)

    o_ref[...] = _dot(xn_ref[...], w_ref[...])


def _norm_proj_into_kernel(x_ref, g_ref, w_ref, stack_ref, o_ref, xn_ref):
    del stack_ref
    _norm_proj_kernel(x_ref, g_ref, w_ref, o_ref, xn_ref)


def _norm_proj(x, g, w, tm, tn, layer, depth, stacked):
    t, d = x.shape
    n = w.shape[1]
    in_specs = [pl.BlockSpec((tm, d), lambda i, j: (i, 0)),
                pl.BlockSpec((1, d), lambda i, j: (0, 0)),
                pl.BlockSpec((d, tn), lambda i, j: (0, j))]
    args = [x, g.reshape(1, d), w]
    if stacked is not None:
        in_specs.append(pl.BlockSpec(memory_space=pl.ANY))
        args.append(stacked)
    return pl.pallas_call(
        _norm_proj_kernel if stacked is None else _norm_proj_into_kernel,
        grid=(t // tm, n // tn),
        in_specs=in_specs,
        out_specs=pl.BlockSpec((None, tm, tn), lambda i, j: (layer, i, j)),
        out_shape=jax.ShapeDtypeStruct((depth, t, n), F32),
        scratch_shapes=[pltpu.VMEM((tm, d), BF16)],
        input_output_aliases={} if stacked is None else {3: 0},
        compiler_params=_cparams("parallel", "arbitrary"),
        name="norm_proj",
    )(*args)


def _norm_proj_t_kernel(ngroups, x_ref, g_ref, wt_ref, *refs):
    o_refs, xn_ref = refs[-1 - ngroups:-1], refs[-1]
    j = pl.program_id(1)

    @pl.when(j == 0)
    def _():
        xn_ref[...] = _rms(x_ref[...], g_ref[...]).astype(BF16)

    for g in range(ngroups):
        @pl.when(j == g)
        def _(g=g):
            o_refs[g][...] = _dot_nt(wt_ref[...], xn_ref[...])


def _norm_proj_t(x, g, wt, tm, layer, depth, bn, stacked):
    t, d = x.shape
    s = t // bn
    ngroups = wt.shape[0] // BRANCH_W
    tiles = s // tm
    in_specs = [pl.BlockSpec((tm, d), lambda i, j: (i, 0)),
                pl.BlockSpec((1, d), lambda i, j: (0, 0)),
                pl.BlockSpec((BRANCH_W, d), lambda i, j: (j, 0))]
    args = [x, g.reshape(1, d), wt]
    aliases = {}
    if stacked is not None:
        in_specs += [pl.BlockSpec(memory_space=pl.ANY)] * ngroups
        args += list(stacked)
        aliases = {3 + n: n for n in range(ngroups)}
    out_spec = pl.BlockSpec((None, None, BRANCH_W, tm), lambda i, j: (layer, i // tiles, 0, i % tiles))
    return pl.pallas_call(
        functools.partial(_norm_proj_t_kernel, ngroups),
        grid=(t // tm, ngroups),
        in_specs=in_specs,
        out_specs=[out_spec] * ngroups,
        out_shape=[jax.ShapeDtypeStruct((depth, bn, BRANCH_W, s), F32)] * ngroups,
        scratch_shapes=[pltpu.VMEM((tm, d), BF16)],
        input_output_aliases=aliases,
        compiler_params=_cparams("arbitrary", "arbitrary"),
        name="norm_proj_t",
    )(*args)


def _abias_kernel(tbl_ref, o_ref):
    h = pl.program_id(0)
    sub = 8
    ql = lax.broadcasted_iota(jnp.int32, (ATT_BLK, ATT_BLK), 0)
    kl = lax.broadcasted_iota(jnp.int32, (ATT_BLK, ATT_BLK), 1)
    far = jnp.full((ATT_BLK, ATT_BLK), tbl_ref[2 * A_REL_CLIP * A_HEADS + h] * LOG2E, F32)
    o_ref[0, 0] = jnp.where((kl >> 6) >= (ql >> 6), far, NEG_INF)
    lead = ATT_BLK - sub
    b = lax.broadcasted_iota(jnp.int32, (sub, 2 * ATT_BLK), 0)
    x = lax.broadcasted_iota(jnp.int32, (sub, 2 * ATT_BLK), 1)
    for seg, off in ((1, ATT_BLK), (2, 0)):
        idx = jnp.clip(b + lead - x + off, -A_REL_CLIP, A_REL_CLIP) + A_REL_CLIP

        def body(r, acc, idx=idx):
            return jnp.where(idx == r, tbl_ref[r * A_HEADS + h] * LOG2E, acc)

        strip = lax.fori_loop(0, 2 * A_REL_CLIP + 1, body, jnp.zeros((sub, 2 * ATT_BLK), F32))
        for a in range(ATT_BLK // sub):
            o_ref[0, seg, a * sub:(a + 1) * sub, :] = strip[:, lead - a * sub:lead - a * sub + ATT_BLK]
    o_ref[0, 2] = jnp.where((kl >> 6) <= (ql >> 6), o_ref[0, 2], NEG_INF)


def _build_abias(table):
    return pl.pallas_call(
        _abias_kernel,
        grid=(A_HEADS,),
        in_specs=[pl.BlockSpec(memory_space=pltpu.SMEM)],
        out_specs=pl.BlockSpec((1, 3, ATT_BLK, ATT_BLK), lambda h: (h, 0, 0, 0)),
        out_shape=jax.ShapeDtypeStruct((A_HEADS, 3, ATT_BLK, ATT_BLK), F32),
        compiler_params=_cparams("parallel"),
        name="band_bias_tiles",
    )(table.reshape(-1))


_T5_LARGE_STEPS = (12, 16, 23, 32, 46, 64, 91)


def _t5bias_kernel(tbl_ref, o_ref):
    h = pl.program_id(0)
    rows = 64
    for seg, off in ((0, 0), (1, -ATT_BLK)):
        for rc in range(ATT_BLK // rows):
            ql = lax.broadcasted_iota(jnp.int32, (rows, ATT_BLK), 0) + rc * rows
            kl = lax.broadcasted_iota(jnp.int32, (rows, ATT_BLK), 1)
            rel = kl - ql + off
            n = jnp.abs(rel)
            large = jnp.full((rows, ATT_BLK), T5_BUCKETS // 4, jnp.int32)
            for th in _T5_LARGE_STEPS:
                large = large + jnp.where(n >= th, 1, 0)
            bucket = jnp.where(rel > 0, T5_BUCKETS // 2, 0) + jnp.where(n < T5_BUCKETS // 4, n, large)
            tile = jnp.zeros((rows, ATT_BLK), F32)
            for r in range(T5_BUCKETS):
                tile = jnp.where(bucket == r, tbl_ref[r * C_HEADS + h] * LOG2E, tile)
            if seg == 0:
                tile = jnp.where((kl >> 6) <= (ql >> 6), tile, NEG_INF)
            o_ref[0, seg, rc * rows:(rc + 1) * rows, :] = tile


def _build_t5bias(table):
    return pl.pallas_call(
        _t5bias_kernel,
        grid=(C_HEADS,),
        in_specs=[pl.BlockSpec(memory_space=pltpu.SMEM)],
        out_specs=pl.BlockSpec((1, 2, ATT_BLK, ATT_BLK), lambda h: (h, 0, 0, 0)),
        out_shape=jax.ShapeDtypeStruct((C_HEADS, 2, ATT_BLK, ATT_BLK), F32),
        compiler_params=_cparams("parallel"),
        name="t5_bias_tiles",
    )(table.reshape(-1))


def _pspec(block, index_map, layer):
    return pl.BlockSpec((None,) + block, lambda *g: (layer,) + tuple(index_map(*g)))


def _lane_tile(x, n):
    return x if n == 1 else jnp.concatenate([x] * n, axis=1)


def _half_lanes(q, upper):
    lane = lax.broadcasted_iota(jnp.int32, q.shape, 1)
    return jnp.where((lane >= HEAD_DIM) == upper, q, 0.0).astype(BF16)


def _qk(q, k, k_t):
    return _dot(q, k) if k_t else _dot_nt(q, k)


def _pv(p, v, v_t):
    return _dot_nt(p, v) if v_t else _dot(p, v)


def _softmax_segments(qh, kb, vb, biases, k_t, v_t):
    s = [_qk(qh, k, t) + b for k, b, t in zip(kb, biases, k_t)]
    m = functools.reduce(jnp.maximum, [jnp.max(sj, axis=1, keepdims=True) for sj in s])
    p = [jnp.exp2(sj - m) for sj in s]
    l = functools.reduce(jnp.add, [jnp.sum(pj, axis=1, keepdims=True) for pj in p])
    acc = functools.reduce(jnp.add, [_pv(pj.astype(BF16), v, t) for pj, v, t in zip(p, vb, v_t)])
    return acc / l


def _band_kernel(nseg, min_qi, kv_t, q_ref, *refs):
    k_refs, v_refs, b_refs = refs[:nseg], refs[nseg:2 * nseg], refs[2 * nseg:3 * nseg]
    o_ref = refs[3 * nseg]
    qi = pl.program_id(2)
    q = q_ref[...] * SOFTMAX_QSCALE
    kb = [k_refs[j][...].astype(BF16) for j in range(nseg)]
    vb = [v_refs[j][...].astype(BF16) for j in range(nseg)]
    outs = []
    for h in range(2):
        biases = [b_refs[j][h] if min_qi[j] == 0 else jnp.where(qi >= min_qi[j], b_refs[j][h], NEG_INF)
                  for j in range(nseg)]
        outs.append(_softmax_segments(_half_lanes(q, bool(h)), kb, vb, biases, kv_t, kv_t))
    lane = lax.broadcasted_iota(jnp.int32, q.shape, 1)
    o_ref[...] = jnp.where(lane < HEAD_DIM, outs[0], outs[1]).astype(o_ref.dtype)


def _band_attn_prompt(pn4, layer, k_t, v_t, abias):
    _, bn, s, _ = pn4.shape
    nq = s // ATT_BLK
    blk = (None, ATT_BLK, LANES)

    def kv_spec(back):
        return pl.BlockSpec((None, None, LANES, ATT_BLK),
                            lambda b, hp, qi: (layer, b, hp, jnp.maximum(qi - back, 0)))

    def bias_spec(seg):
        return pl.BlockSpec((2, None, ATT_BLK, ATT_BLK), lambda b, hp, qi: (hp, seg, 0, 0))

    return pl.pallas_call(
        functools.partial(_band_kernel, 3, (2, 1, 0), (True, True, True)),
        grid=(bn, A_HEADS // 2, nq),
        in_specs=([_pspec(blk, lambda b, hp, qi: (b, qi, _CBP["qa"] + hp), layer)]
                  + [kv_spec(back) for back in (2, 1, 0)]
                  + [kv_spec(back) for back in (2, 1, 0)]
                  + [bias_spec(seg) for seg in range(3)]),
        out_specs=pl.BlockSpec(blk, lambda b, hp, qi: (b, qi, hp)),
        out_shape=jax.ShapeDtypeStruct((bn, s, BRANCH_W), BF16),
        compiler_params=_cparams("parallel", "parallel", "arbitrary"),
        name="band_attn_prompt",
    )(pn4, k_t, k_t, k_t, v_t, v_t, v_t, abias, abias, abias)


def _band_attn_sample(p4, layer, k_cache_t, v_cache_t, bias_cache, bias_new):
    _, bn, t, _ = p4.shape
    keep = k_cache_t.shape[-1]
    new = (None, t, LANES)
    old = pl.BlockSpec((None, None, LANES, keep), lambda b, hp, qi: (layer, b, hp, 0))
    return pl.pallas_call(
        functools.partial(_band_kernel, 2, (0, 0), (True, False)),
        grid=(bn, A_HEADS // 2, 1),
        in_specs=[_pspec(new, lambda b, hp, qi: (b, 0, _CB["qa"] + hp), layer),
                  old,
                  _pspec(new, lambda b, hp, qi: (b, 0, _CB["ka"] + hp), layer),
                  old,
                  _pspec(new, lambda b, hp, qi: (b, 0, _CB["va"] + hp), layer),
                  pl.BlockSpec((2, t, keep), lambda b, hp, qi: (hp, 0, 0)),
                  pl.BlockSpec((2, t, t), lambda b, hp, qi: (hp, 0, 0))],
        out_specs=pl.BlockSpec(new, lambda b, hp, qi: (b, 0, hp)),
        out_shape=jax.ShapeDtypeStruct((bn, t, BRANCH_W), BF16),
        compiler_params=_cparams("parallel", "parallel", "arbitrary"),
        name="band_attn_sample",
    )(p4, k_cache_t, p4, v_cache_t, p4, bias_cache, bias_new)


def _strict_upper(n):
    r = lax.broadcasted_iota(jnp.int32, (n, n), 0)
    c = lax.broadcasted_iota(jnp.int32, (n, n), 1)
    return jnp.where(r > c, 1.0, 0.0).astype(BF16)


def _sb_fold(qh, k, v, kv_t, upper, mask, carry_ref, acc_ref, first):
    kb = k.astype(BF16)
    vb = v.astype(BF16)
    tq, tk = qh[0].shape[0], upper.shape[0]
    top = None
    for h in range(2):
        z = _qk(qh[h], kb, kv_t)
        soft = jnp.log1p(jnp.exp(-jnp.abs(z)))
        log_keep = -(jnp.maximum(z, 0.0) + soft)
        log_take = jnp.minimum(z, 0.0) - soft
        if mask is not None:
            log_keep = jnp.where(mask, log_keep, 0.0)
        hi = log_keep.astype(BF16)
        lo = (log_keep - hi.astype(F32)).astype(BF16)
        after = _dot(hi, upper) + _dot(lo, upper)
        if not first:
            after = after + _lane_tile(carry_ref[h], tk // LANES)
        w = jnp.exp(log_take + after)
        if mask is not None:
            w = jnp.where(mask, w, 0.0)
        pv = _pv(w.astype(BF16), vb, kv_t)
        total = jnp.sum(log_keep, axis=1, keepdims=True)
        if first:
            acc_ref[h] = pv
            carry = jnp.broadcast_to(total, (tq, LANES))
        else:
            acc_ref[h] += pv
            carry = carry_ref[h] + total
        carry_ref[h] = carry
        top = jnp.max(carry) if top is None else jnp.maximum(top, jnp.max(carry))
    return (top > SB_LOG_CUT).astype(jnp.int32)


def _sb_kernel(tk, n_past_static, diag_t, q_ref, kd_ref, vd_ref, kp_ref, vp_ref, o_ref, carry_ref, acc_ref):
    qi = pl.program_id(2)
    n_past = qi if n_past_static is None else n_past_static
    q = q_ref[...] * SCALE
    tq = q.shape[0]
    qh = [_half_lanes(q, False), _half_lanes(q, True)]
    rq = lax.broadcasted_iota(jnp.int32, (tq, tq), 0)
    ck = lax.broadcasted_iota(jnp.int32, (tq, tq), 1)
    live = _sb_fold(qh, kd_ref[...], vd_ref[...], diag_t, _strict_upper(tq), ck < rq, carry_ref, acc_ref, True)

    def cond(st):
        return jnp.logical_and(st[0] >= 0, st[1] > 0)

    def body(st):
        start = pl.multiple_of(st[0] * tk, tk)
        live = _sb_fold(qh, kp_ref[:, pl.ds(start, tk)], vp_ref[:, pl.ds(start, tk)], True, _strict_upper(tk),
                        None, carry_ref, acc_ref, False)
        return st[0] - 1, live

    lax.while_loop(cond, body, (n_past - 1, live))
    lane = lax.broadcasted_iota(jnp.int32, q.shape, 1)
    o_ref[...] = jnp.where(lane < HEAD_DIM, acc_ref[0], acc_ref[1]).astype(o_ref.dtype)


def _sb_attn_prompt(pn4, layer, k_t, v_t):
    _, bn, s, _ = pn4.shape
    blk = (None, ATT_BLK, LANES)
    own = pl.BlockSpec((None, None, LANES, ATT_BLK), lambda b, hp, qi: (layer, b, hp, qi))
    older = pl.BlockSpec((None, None, LANES, s), lambda b, hp, qi: (layer, b, hp, 0))
    return pl.pallas_call(
        functools.partial(_sb_kernel, ATT_BLK, None, True),
        grid=(bn, B_HEADS // 2, s // ATT_BLK),
        in_specs=[_pspec(blk, lambda b, hp, qi: (b, qi, _CBP["qb"] + hp), layer), own, own, older, older],
        out_specs=pl.BlockSpec(blk, lambda b, hp, qi: (b, qi, hp)),
        out_shape=jax.ShapeDtypeStruct((bn, s, BRANCH_W), BF16),
        scratch_shapes=[pltpu.VMEM((2, ATT_BLK, LANES), F32), pltpu.VMEM((2, ATT_BLK, LANES), F32)],
        compiler_params=_cparams("parallel", "parallel", "arbitrary"),
        name="stick_attn_prompt",
    )(pn4, k_t, v_t, k_t, v_t)


def _sb_attn_sample(p4, layer, k_cache_t, v_cache_t):
    _, bn, t, _ = p4.shape
    past = k_cache_t.shape[-1]
    new = (None, t, LANES)
    older = pl.BlockSpec((None, None, LANES, past), lambda b, hp, qi: (layer, b, hp, 0))
    return pl.pallas_call(
        functools.partial(_sb_kernel, ATT_BLK, past // ATT_BLK, False),
        grid=(bn, B_HEADS // 2, 1),
        in_specs=[_pspec(new, lambda b, hp, qi: (b, 0, _CB["qb"] + hp), layer),
                  _pspec(new, lambda b, hp, qi: (b, 0, _CB["kb"] + hp), layer),
                  _pspec(new, lambda b, hp, qi: (b, 0, _CB["vb"] + hp), layer),
                  older, older],
        out_specs=pl.BlockSpec(new, lambda b, hp, qi: (b, 0, hp)),
        out_shape=jax.ShapeDtypeStruct((bn, t, BRANCH_W), BF16),
        scratch_shapes=[pltpu.VMEM((2, t, LANES), F32), pltpu.VMEM((2, t, LANES), F32)],
        compiler_params=_cparams("parallel", "parallel", "arbitrary"),
        name="stick_attn_sample",
    )(p4, p4, p4, k_cache_t, v_cache_t)


def _diff_combine(o0, o1, lam_init, lam_ref, gain_ref):
    lp = lam_ref[...]
    lam = (jnp.exp(jnp.sum(lp[0:1] * lp[1:2], axis=1, keepdims=True))
           - jnp.exp(jnp.sum(lp[2:3] * lp[3:4], axis=1, keepdims=True)) + lam_init)
    return _rms(o0 - lam * o1, gain_ref[...]) * (1.0 - lam_init)


def _diff_fold(qm, k_t, v, bias, m_ref, l_ref, acc_ref, first):
    kb = k_t.astype(BF16)
    vb = v.astype(BF16)
    tq, tk = qm[0].shape[0], vb.shape[0]
    for m in range(2):
        s = _dot(qm[m], kb) + bias
        m_cur = jnp.max(s, axis=1, keepdims=True)
        m_new = jnp.broadcast_to(m_cur, (tq, LANES)) if first else jnp.maximum(m_ref[m], m_cur)
        p = jnp.exp2(s - _lane_tile(m_new, tk // LANES))
        row = jnp.sum(p, axis=1, keepdims=True)
        pv = _dot(p.astype(BF16), vb)
        if first:
            l_ref[m] = jnp.broadcast_to(row, (tq, LANES))
            acc_ref[m] = pv
        else:
            alpha = jnp.exp2(m_ref[m] - m_new)
            l_ref[m] = alpha * l_ref[m] + row
            acc_ref[m] = alpha * acc_ref[m] + pv
        m_ref[m] = m_new


def _diff_prompt_kernel(lam_init, tbl_ref, lam_ref, gain_ref, q_ref, kd_ref, vd_ref, kp_ref, vp_ref,
                        bd_ref, bp_ref, o_ref, m_ref, l_ref, acc_ref):
    h = pl.program_id(1)
    qi = pl.program_id(2)
    q = q_ref[...] * SOFTMAX_QSCALE
    blk = q.shape[0]
    qm = [_half_lanes(q, False), _half_lanes(q, True)]
    far = tbl_ref[T5_FAR_BUCKET * C_HEADS + h] * LOG2E

    near = pl.multiple_of(jnp.maximum(qi - 1, 0) * blk, blk)
    k0 = jnp.concatenate([kp_ref[:, pl.ds(near, blk)], kd_ref[...]], axis=1)
    v0 = jnp.concatenate([vp_ref[pl.ds(near, blk), :], vd_ref[...]], axis=0)
    b0 = jnp.concatenate([jnp.where(qi >= 1, bp_ref[...], NEG_INF), bd_ref[...]], axis=1)
    _diff_fold(qm, k0, v0, b0, m_ref, l_ref, acc_ref, True)

    n_far = jnp.maximum(qi - 1, 0)
    n_group = n_far // DIFF_FAR_GROUP
    wide = DIFF_FAR_GROUP * blk

    def body(j, c):
        start = pl.multiple_of(j * wide, wide)
        _diff_fold(qm, kp_ref[:, pl.ds(start, wide)], vp_ref[pl.ds(start, wide), :], far,
                   m_ref, l_ref, acc_ref, False)
        return c

    lax.fori_loop(0, n_group, body, 0)
    rest = pl.multiple_of(n_group * wide, wide)
    for r in range(1, DIFF_FAR_GROUP):
        @pl.when(n_far - n_group * DIFF_FAR_GROUP == r)
        def _(r=r):
            _diff_fold(qm, kp_ref[:, pl.ds(rest, r * blk)], vp_ref[pl.ds(rest, r * blk), :], far,
                       m_ref, l_ref, acc_ref, False)

    o = _diff_combine(acc_ref[0] / l_ref[0], acc_ref[1] / l_ref[1], lam_init, lam_ref, gain_ref)
    o_ref[...] = o.astype(o_ref.dtype)


def _diff_sample_kernel(lam_init, lam_ref, gain_ref, q_ref, kc_ref, vc_ref, kn_ref, vn_ref,
                        bc_ref, bn_ref, o_ref):
    q = q_ref[...] * SOFTMAX_QSCALE
    kb = [kc_ref[...].astype(BF16), kn_ref[...].astype(BF16)]
    vb = [vc_ref[...].astype(BF16), vn_ref[...].astype(BF16)]
    biases = [bc_ref[...], bn_ref[...]]
    k_t, v_t = (True, False), (False, False)
    o0 = _softmax_segments(_half_lanes(q, False), kb, vb, biases, k_t, v_t)
    o1 = _softmax_segments(_half_lanes(q, True), kb, vb, biases, k_t, v_t)
    o_ref[...] = _diff_combine(o0, o1, lam_init, lam_ref, gain_ref).astype(o_ref.dtype)


def _diff_attn_prompt(pn4, layer, k_t, t5tiles, lam_init, t5_flat, lam_params, gain):
    _, bn, s, _ = pn4.shape
    blk = (None, ATT_BLK, LANES)
    full = (None, s, LANES)
    state = pltpu.VMEM((2, ATT_BLK, LANES), F32)
    return pl.pallas_call(
        functools.partial(_diff_prompt_kernel, lam_init),
        grid=(bn, C_HEADS, s // ATT_BLK),
        in_specs=[pl.BlockSpec(memory_space=pltpu.SMEM),
                  pl.BlockSpec((4, HEAD_DIM), lambda b, h, qi: (0, 0)),
                  pl.BlockSpec((1, C_VDIM), lambda b, h, qi: (0, 0)),
                  _pspec(blk, lambda b, h, qi: (b, qi, _CBP["qc"] + h), layer),
                  pl.BlockSpec((None, None, LANES, ATT_BLK), lambda b, h, qi: (layer, b, h, qi)),
                  _pspec(blk, lambda b, h, qi: (b, qi, _CBP["vc"] + h), layer),
                  pl.BlockSpec((None, None, LANES, s), lambda b, h, qi: (layer, b, h, 0)),
                  _pspec(full, lambda b, h, qi: (b, 0, _CBP["vc"] + h), layer),
                  pl.BlockSpec((None, None, ATT_BLK, ATT_BLK), lambda b, h, qi: (h, 0, 0, 0)),
                  pl.BlockSpec((None, None, ATT_BLK, ATT_BLK), lambda b, h, qi: (h, 1, 0, 0))],
        out_specs=pl.BlockSpec(blk, lambda b, h, qi: (b, qi, h)),
        out_shape=jax.ShapeDtypeStruct((bn, s, BRANCH_W), BF16),
        scratch_shapes=[state, state, state],
        compiler_params=_cparams("parallel", "parallel", "arbitrary"),
        name="diff_attn_prompt",
    )(t5_flat, lam_params, gain.reshape(1, C_VDIM), pn4, k_t, pn4, k_t, pn4, t5tiles, t5tiles)


def _diff_attn_sample(p4, layer, k_cache_t, v_cache, bias_cache, bias_new, lam_init, lam_params, gain):
    _, bn, t, _ = p4.shape
    past = v_cache.shape[1]
    new = (None, t, LANES)
    return pl.pallas_call(
        functools.partial(_diff_sample_kernel, lam_init),
        grid=(bn, C_HEADS),
        in_specs=[pl.BlockSpec((4, HEAD_DIM), lambda b, h: (0, 0)),
                  pl.BlockSpec((1, C_VDIM), lambda b, h: (0, 0)),
                  _pspec(new, lambda b, h: (b, 0, _CB["qc"] + h), layer),
                  pl.BlockSpec((None, None, LANES, past), lambda b, h: (layer, b, h, 0)),
                  pl.BlockSpec((None, past, LANES), lambda b, h: (b, 0, h)),
                  _pspec(new, lambda b, h: (b, 0, _CB["kc"] + h), layer),
                  _pspec(new, lambda b, h: (b, 0, _CB["vc"] + h), layer),
                  pl.BlockSpec((None, t, past), lambda b, h: (h, 0, 0)),
                  pl.BlockSpec((None, t, t), lambda b, h: (h, 0, 0))],
        out_specs=pl.BlockSpec(new, lambda b, h: (b, 0, h)),
        out_shape=jax.ShapeDtypeStruct((bn, t, BRANCH_W), BF16),
        compiler_params=_cparams("parallel", "parallel"),
        name="diff_attn_sample",
    )(lam_params, gain.reshape(1, C_VDIM), p4, k_cache_t, v_cache, p4, p4, bias_cache, bias_new)


def _merge_kernel(x_ref, oa_ref, ob_ref, oc_ref, g0_ref, g1_ref, g2_ref, bg_ref, wb_ref, wo_ref, o_ref):
    h = None
    for n, (o_r, g_r) in enumerate(((oa_ref, g0_ref), (ob_ref, g1_ref), (oc_ref, g2_ref))):
        gate = 1.0 / (1.0 + jnp.exp(-(g_r[...] + bg_ref[n:n + 1, :])))
        t = gate * _dot(o_r[...], wb_ref[n])
        h = t if h is None else h + t
    o_ref[...] = x_ref[...] + _dot(h.astype(BF16), wo_ref[...])


def _merge(x, o_a, o_b, o_c, p, layer, b_gate, w_branch, w_out, tm):
    t, d = x.shape
    row = lambda i: (i, 0)
    fixed2 = lambda i: (0, 0)
    return pl.pallas_call(
        _merge_kernel,
        grid=(t // tm,),
        in_specs=[pl.BlockSpec((tm, d), row),
                  pl.BlockSpec((tm, BRANCH_W), row),
                  pl.BlockSpec((tm, BRANCH_W), row),
                  pl.BlockSpec((tm, BRANCH_W), row),
                  pl.BlockSpec((None, tm, d), lambda i: (layer, i, 0)),
                  pl.BlockSpec((None, tm, d), lambda i: (layer, i, 1)),
                  pl.BlockSpec((None, tm, d), lambda i: (layer, i, 2)),
                  pl.BlockSpec((N_BRANCH, d), fixed2),
                  pl.BlockSpec((N_BRANCH, BRANCH_W, d), lambda i: (0, 0, 0)),
                  pl.BlockSpec((d, d), fixed2)],
        out_specs=pl.BlockSpec((tm, d), row),
        out_shape=jax.ShapeDtypeStruct((t, d), F32),
        compiler_params=_cparams("parallel"),
        name="gated_merge",
    )(x, o_a, o_b, o_c, p, p, p, b_gate, w_branch, w_out)


FFN_TF = D_FF // 2
FFN_PREV = 16


def _gelu_tanh(x):
    return x * (0.5 * (1.0 + jnp.tanh(math.sqrt(2.0 / math.pi) * (x + 0.044715 * (x * x * x)))))


def _ffn_kernel(tiles_per_seq, x_ref, xp_ref, gn_ref, wg_ref, wu_ref, cw_ref, cb_ref, wd_ref, st_ref,
                o_ref, so_ref, xn_ref, xpn_ref):
    i = pl.program_id(0)
    j = pl.program_id(1)
    whole_seqs = tiles_per_seq == 0

    @pl.when(j == 0)
    def _():
        xn_ref[...] = _rms(x_ref[...], gn_ref[...]).astype(BF16)
        if not whole_seqs:
            xpn_ref[...] = _rms(xp_ref[...], gn_ref[...]).astype(BF16)

    xn = xn_ref[...]
    tm = xn.shape[0]
    tf = wg_ref.shape[1]
    g = _dot(xn, wg_ref[...])
    u = _dot(xn, wu_ref[...])
    if whole_seqs:
        ns = st_ref.shape[0]
        st = st_ref[...]
        g = g.reshape(ns, tm // ns, tf)
        u = u.reshape(ns, tm // ns, tf)
        pm2, pm1 = st[:, 0:1, :], st[:, 1:2, :]
    else:
        gp = _dot(xpn_ref[...], wg_ref[...])
        first = (i % tiles_per_seq) == 0
        st = st_ref[0]
        pm2 = jnp.where(first, st[0:1, :], gp[FFN_PREV - 2:FFN_PREV - 1, :])
        pm1 = jnp.where(first, st[1:2, :], gp[FFN_PREV - 1:FFN_PREV, :])
    ax = g.ndim - 2
    row = lax.broadcasted_iota(jnp.int32, g.shape, ax)
    g1 = jnp.where(row == 0, pm1, pltpu.roll(g, 1, ax))
    g2 = jnp.where(row == 0, pm2, jnp.where(row == 1, pm1, pltpu.roll(g, 2, ax)))
    cw = cw_ref[...]
    c = cb_ref[...] + cw[0:1, :] * g2 + cw[1:2, :] * g1 + cw[2:3, :] * g
    hid = (_gelu_tanh(c) * u).reshape(tm, tf).astype(BF16)
    contrib = _dot(hid, wd_ref[...])

    @pl.when(j == 0)
    def _():
        o_ref[...] = x_ref[...] + contrib

    @pl.when(j > 0)
    def _():
        o_ref[...] += contrib

    seq_len = g.shape[ax]
    if whole_seqs:
        so_ref[...] = g[:, seq_len - 2:seq_len, :]
    else:
        so_ref[0] = g[seq_len - 2:seq_len, :]


def _conv_ffn(x, g_norm, w_gate, w_up, conv_w, conv_b, w_down, state, seq_len, tm):
    t, d = x.shape
    nseq = t // seq_len
    nf = D_FF // FFN_TF
    if seq_len >= tm:
        tiles_per_seq = seq_len // tm
        st_spec = pl.BlockSpec((1, 2, FFN_TF), lambda i, j: (i // tiles_per_seq, 0, j))
    else:
        tiles_per_seq = 0
        st_spec = pl.BlockSpec((tm // seq_len, 2, FFN_TF), lambda i, j: (i, 0, j))
    prev_blocks = tm // FFN_PREV
    tail_spec = pl.BlockSpec(st_spec.block_shape, lambda i, j: (i, 0, j))
    n_tail = t // tm * st_spec.block_shape[0]
    y, tails = pl.pallas_call(
        functools.partial(_ffn_kernel, tiles_per_seq),
        grid=(t // tm, nf),
        in_specs=[pl.BlockSpec((tm, d), lambda i, j: (i, 0)),
                  pl.BlockSpec((FFN_PREV, d), lambda i, j: (jnp.maximum(i * prev_blocks - 1, 0), 0)),
                  pl.BlockSpec((1, d), lambda i, j: (0, 0)),
                  pl.BlockSpec((d, FFN_TF), lambda i, j: (0, j)),
                  pl.BlockSpec((d, FFN_TF), lambda i, j: (0, j)),
                  pl.BlockSpec((3, FFN_TF), lambda i, j: (0, j)),
                  pl.BlockSpec((1, FFN_TF), lambda i, j: (0, j)),
                  pl.BlockSpec((FFN_TF, d), lambda i, j: (j, 0)),
                  st_spec],
        out_specs=[pl.BlockSpec((tm, d), lambda i, j: (i, 0)), tail_spec],
        out_shape=[jax.ShapeDtypeStruct((t, d), F32),
                   jax.ShapeDtypeStruct((n_tail, 2, D_FF), F32)],
        scratch_shapes=[pltpu.VMEM((tm, d), BF16), pltpu.VMEM((FFN_PREV, d), BF16)],
        compiler_params=_cparams("arbitrary", "arbitrary"),
        name="conv_ffn_prompt" if tiles_per_seq else "conv_ffn_sample",
    )(x, x, g_norm.reshape(1, d), w_gate, w_up, conv_w, conv_b.reshape(1, D_FF), w_down, state)
    return y, tails[n_tail // nseq - 1::n_tail // nseq]


def _final_norm_kernel(x_ref, g_ref, o_ref):
    o_ref[...] = _rms(x_ref[...], g_ref[...])


def _final_norm(x, g, tm):
    t, d = x.shape
    return pl.pallas_call(
        _final_norm_kernel,
        grid=(t // tm,),
        in_specs=[pl.BlockSpec((tm, d), lambda i: (i, 0)), pl.BlockSpec((1, d), lambda i: (0, 0))],
        out_specs=pl.BlockSpec((tm, d), lambda i: (i, 0)),
        out_shape=jax.ShapeDtypeStruct((t, d), F32),
        compiler_params=_cparams("parallel"),
        name="final_norm",
    )(x, g.reshape(1, d))


PROJ_TM, PROJ_TN = 1024, 1280
ROW_TM = 512


def _lam_init(layer):
    return 0.8 - 0.6 * math.exp(-0.3 * layer)


def _cols(p4, name):
    c0 = _CB[name] * LANES
    return p4[..., c0:c0 + BRANCH_W]


def _layer_tail(x2, o_a, o_b, o_c, p_rows, layer, prm, conv_state, seq_len):
    flat = lambda a: a.reshape(x2.shape[0], BRANCH_W)
    x2 = _merge(x2, flat(o_a), flat(o_b), flat(o_c), p_rows, layer, prm["b_gate"][layer],
                prm["w_branch"][layer], prm["w_out"][layer], ROW_TM)
    return _conv_ffn(x2, prm["norm_ffn"][layer], prm["w_gate"][layer], prm["w_up"][layer],
                     prm["conv_w"][layer], prm["conv_b"][layer], prm["w_down"][layer], conv_state, seq_len, ROW_TM)


def _trunk_prompt(x, prm, abias, t5tiles, t5_flat):
    bn, s, d = x.shape
    x2 = x.reshape(bn * s, d)
    depth = prm["w_rows"].shape[0]
    conv_states = []
    p_rows, kv_t = None, None
    for l in range(depth):
        p_rows = _norm_proj(x2, prm["norm_mix"][l], prm["w_rows"][l], PROJ_TM, PROJ_TN, l, depth, p_rows)
        kv_t = _norm_proj_t(x2, prm["norm_mix"][l], prm["w_feat_t"][l], PROJ_TM, l, depth, bn, kv_t)
        ka_t, va_t, kb_t, vb_t, kc_t = kv_t
        pn4 = p_rows.reshape(depth, bn, s, ROW_COLS)
        o_a = _band_attn_prompt(pn4, l, ka_t, va_t, abias[l])
        o_b = _sb_attn_prompt(pn4, l, kb_t, vb_t)
        o_c = _diff_attn_prompt(pn4, l, kc_t, t5tiles, _lam_init(l), t5_flat, prm["c_lambda"][l], prm["c_subln"][l])
        x2, st_conv = _layer_tail(x2, o_a, o_b, o_c, p_rows, l, prm, jnp.zeros((bn, 2, D_FF), F32), s)
        conv_states.append(st_conv)
    y = _final_norm(x2, prm["norm_final"], ROW_TM).reshape(bn, s, d)
    heads_t = lambda a: jnp.transpose(a.reshape(depth, bn, A_HEADS, HEAD_DIM, a.shape[-1]), (0, 1, 4, 2, 3))
    keep = min(A_PAST_CHUNKS * CHUNK, s)
    c0 = _CBP["vc"] * LANES
    states = (heads_t(ka_t[..., s - keep:]), heads_t(va_t[..., s - keep:]), heads_t(kb_t), heads_t(vb_t),
              jnp.transpose(kc_t.reshape(depth, bn, C_HEADS, 2, HEAD_DIM, s), (0, 1, 5, 2, 3, 4)),
              pn4[..., c0:c0 + BRANCH_W].reshape(depth, bn, s, C_HEADS, C_VDIM),
              jnp.stack(conv_states, axis=0))
    return y, states


def _feature_major(cache):
    nd = cache.ndim
    t = jnp.transpose(cache, (0, 1) + tuple(range(3, nd)) + (2,))
    return t.reshape(cache.shape[0], cache.shape[1], BRANCH_W, cache.shape[2])


def _trunk_sample(x, caches, prm, abias, t5tiles):
    bn, t, d = x.shape
    x2 = x.reshape(bn * t, d)
    depth = prm["w_in"].shape[0]
    ca_k, ca_v, cb_k, cb_v, cc_k, cc_v, st_conv_in = caches
    ca_kt, ca_vt, cb_kt, cb_vt, cc_kt = (_feature_major(c) for c in (ca_k, ca_v, cb_k, cb_v, cc_k))
    keep, past = ca_k.shape[2], cc_k.shape[2]
    conv_states = []
    p_all = None
    for l in range(depth):
        p_all = _norm_proj(x2, prm["norm_mix"][l], prm["w_in"][l], PROJ_TM, PROJ_TN, l, depth, p_all)
        p4 = p_all.reshape(depth, bn, t, IN_COLS)
        bias_cache = jnp.concatenate([abias[l][:, 0, :t, :], abias[l][:, 1, :t, :]], axis=-1)[..., -keep:]
        o_a = _band_attn_sample(p4, l, ca_kt, ca_vt, bias_cache, abias[l][:, 2, :t, :t])
        o_b = _sb_attn_sample(p4, l, cb_kt, cb_vt)
        far = jnp.broadcast_to(t5tiles[:, 1, :1, :1], (C_HEADS, t, past - ATT_BLK))
        bias_c = jnp.concatenate([far, t5tiles[:, 1, :t, :]], axis=-1)
        o_c = _diff_attn_sample(p4, l, cc_kt, cc_v[l].reshape(bn, past, BRANCH_W), bias_c, t5tiles[:, 0, :t, :t],
                                _lam_init(l), prm["c_lambda"][l], prm["c_subln"][l])
        x2, st_conv = _layer_tail(x2, o_a, o_b, o_c, p_all, l, prm, st_conv_in[l], t)
        conv_states.append(st_conv)
    y = _final_norm(x2, prm["norm_final"], ROW_TM).reshape(bn, t, d)
    ka, va, kb, vb, kc, vc = (_cols(p4, n) for n in ("ka", "va", "kb", "vb", "kc", "vc"))
    heads = lambda a: a.reshape(a.shape[:3] + (A_HEADS, HEAD_DIM))
    states = (jnp.concatenate([ca_k[:, :, t:], heads(ka)], axis=2),
              jnp.concatenate([ca_v[:, :, t:], heads(va)], axis=2), heads(kb), heads(vb),
              kc.reshape(depth, bn, t, C_HEADS, 2, HEAD_DIM), vc.reshape(depth, bn, t, C_HEADS, C_VDIM),
              jnp.stack(conv_states, axis=0))
    return y, states


def kernel(x_prompt, x_sample, cache_a_k, cache_a_v, cache_b_k, cache_b_v, cache_c_k, cache_c_v,
           state_ffn_conv, norm_mix, w_in, b_gate, a_rel_bias, t5_bias, c_lambda, c_subln,
           w_branch, w_out, norm_ffn, w_up, conv_w, conv_b, w_down, norm_final):
    group = {name: w_in[..., i * BRANCH_W:(i + 1) * BRANCH_W] for i, name in enumerate(_GROUPS)}
    gate = w_in[..., len(_GROUPS) * BRANCH_W:]
    prm = dict(
        norm_mix=norm_mix,
        w_in=jnp.concatenate([gate] + [group[n] for n in _GROUPS], axis=-1).astype(BF16),
        w_rows=jnp.concatenate([gate] + [group[n] for n in _ROW_GROUPS], axis=-1).astype(BF16),
        w_feat_t=jnp.swapaxes(jnp.concatenate([group[n] for n in _T_GROUPS], axis=-1), 1, 2).astype(BF16),
        b_gate=b_gate, c_lambda=c_lambda, c_subln=c_subln,
        w_branch=w_branch.astype(BF16), w_out=w_out.astype(BF16), norm_ffn=norm_ffn,
        w_gate=w_up[..., :D_FF].astype(BF16), w_up=w_up[..., D_FF:].astype(BF16),
        conv_w=conv_w, conv_b=conv_b, w_down=w_down.astype(BF16), norm_final=norm_final)
    abias = [_build_abias(a_rel_bias[l]) for l in range(a_rel_bias.shape[0])]
    t5tiles = _build_t5bias(t5_bias)
    y_p, p_states = _trunk_prompt(x_prompt, prm, abias, t5tiles, t5_bias.reshape(-1))
    caches = (cache_a_k, cache_a_v, cache_b_k, cache_b_v, cache_c_k, cache_c_v, state_ffn_conv)
    y_s, s_states = _trunk_sample(x_sample, caches, prm, abias, t5tiles)
    return (y_p, y_s) + p_states + s_states
```

```python
import functools
import math

import jax
import jax.numpy as jnp
from jax import lax
from jax.experimental import pallas as pl
from jax.experimental.pallas import tpu as pltpu

F32 = jnp.float32
BF16 = jnp.bfloat16

D_MODEL = 1024
CHUNK = 64
HEAD_DIM = 64
A_HEADS = 8
A_PAST_CHUNKS = 8
A_REL_CLIP = 128
B_HEADS = 8
C_HEADS = 4
C_VDIM = 128
T5_BUCKETS = 32
D_FF = 2816
N_BRANCH = 3
BRANCH_W = 512
GATE_W = N_BRANCH * D_MODEL
IN_COLS = 9 * BRANCH_W + GATE_W
EPS = 1e-6
NEG_INF = -1e30
SCALE = HEAD_DIM ** -0.5
LOG2E = math.log2(math.e)
SOFTMAX_QSCALE = SCALE * LOG2E

LANES = 128
_GROUPS = ("qa", "ka", "va", "qb", "kb", "vb", "qc", "kc", "vc")
_CB = {name: (GATE_W + i * BRANCH_W) // LANES for i, name in enumerate(_GROUPS)}
_ROW_GROUPS = ("qa", "qb", "qc", "vc")
_T_GROUPS = ("ka", "va", "kb", "vb", "kc")
_CBP = {name: (GATE_W + i * BRANCH_W) // LANES for i, name in enumerate(_ROW_GROUPS)}
ROW_COLS = GATE_W + len(_ROW_GROUPS) * BRANCH_W

ATT_BLK = 256
SB_LOG_CUT = -104.0
T5_FAR_BUCKET = 15
DIFF_FAR_GROUP = 4
V7X_VMEM_BYTES = 64 * 1024 * 1024
VMEM_LIMIT = V7X_VMEM_BYTES - 8 * 1024 * 1024


def _cparams(*sem):
    return pltpu.CompilerParams(dimension_semantics=sem, vmem_limit_bytes=VMEM_LIMIT)


def _rms(x, g):
    return x * lax.rsqrt(jnp.mean(x * x, axis=-1, keepdims=True) + EPS) * g


def _dot(a, b):
    return jnp.dot(a, b, preferred_element_type=F32)


def _dot_nt(a, b):
    return lax.dot_general(a, b, (((1,), (1,)), ((), ())), preferred_element_type=F32)


def _norm_proj_kernel(n_stack, heads_cols, x_ref, g_ref, w_ref, *refs):
    outs, xn_ref = refs[n_stack:-1], refs[-1]
    j = pl.program_id(1)

    @pl.when(j == 0)
    def _():
        xn_ref[...] = _rms(x_ref[...], g_ref[...]).astype(BF16)

    res = _dot(xn_ref[...], w_ref[...])
    outs[0][...] = res
    if heads_cols is not None:
        tile, first, heads = heads_cols

        @pl.when(j == tile)
        def _():
            for h in range(heads):
                lo = first + h * LANES
                outs[1][pl.ds(h, res.shape[0], stride=heads), :] = res[:, lo:lo + LANES]


def _norm_proj(x, g, w, tm, tn, layer, depth, stacked, heads_group=None):
    t, d = x.shape
    n = w.shape[1]
    in_specs = [pl.BlockSpec((tm, d), lambda i, j: (i, 0)),
                pl.BlockSpec((1, d), lambda i, j: (0, 0)),
                pl.BlockSpec((d, tn), lambda i, j: (0, j))]
    args = [x, g.reshape(1, d), w]
    out_specs = [pl.BlockSpec((None, tm, tn), lambda i, j: (layer, i, j))]
    out_shape = [jax.ShapeDtypeStruct((depth, t, n), F32)]
    heads_cols = None
    if heads_group is not None:
        first, heads = heads_group
        heads_cols = (first // tn, first % tn, heads)
        out_specs.append(pl.BlockSpec((None, tm * heads, LANES), lambda i, j: (layer, i, 0)))
        out_shape.append(jax.ShapeDtypeStruct((depth, t * heads, LANES), F32))
    n_stack = 0 if stacked is None else len(out_shape)
    if stacked is not None:
        in_specs += [pl.BlockSpec(memory_space=pl.ANY)] * n_stack
        args += list(stacked)
    return pl.pallas_call(
        functools.partial(_norm_proj_kernel, n_stack, heads_cols),
        grid=(t // tm, n // tn),
        in_specs=in_specs,
        out_specs=out_specs,
        out_shape=out_shape,
        scratch_shapes=[pltpu.VMEM((tm, d), BF16)],
        input_output_aliases={3 + k: k for k in range(n_stack)},
        compiler_params=_cparams("arbitrary", "arbitrary"),
        name="norm_proj",
    )(*args)


def _norm_proj_t_kernel(ngroups, x_ref, g_ref, wt_ref, *refs):
    o_refs, xn_ref = refs[-1 - ngroups:-1], refs[-1]
    j = pl.program_id(1)

    @pl.when(j == 0)
    def _():
        xn_ref[...] = _rms(x_ref[...], g_ref[...]).astype(BF16)

    for g in range(ngroups):
        @pl.when(j == g)
        def _(g=g):
            o_refs[g][...] = _dot_nt(wt_ref[...], xn_ref[...])


def _norm_proj_t(x, g, wt, tm, layer, depth, bn, stacked):
    t, d = x.shape
    s = t // bn
    ngroups = wt.shape[0] // BRANCH_W
    tiles = s // tm
    in_specs = [pl.BlockSpec((tm, d), lambda i, j: (i, 0)),
                pl.BlockSpec((1, d), lambda i, j: (0, 0)),
                pl.BlockSpec((BRANCH_W, d), lambda i, j: (j, 0))]
    args = [x, g.reshape(1, d), wt]
    aliases = {}
    if stacked is not None:
        in_specs += [pl.BlockSpec(memory_space=pl.ANY)] * ngroups
        args += list(stacked)
        aliases = {3 + n: n for n in range(ngroups)}
    out_spec = pl.BlockSpec((None, None, BRANCH_W, tm), lambda i, j: (layer, i // tiles, 0, i % tiles))
    return pl.pallas_call(
        functools.partial(_norm_proj_t_kernel, ngroups),
        grid=(t // tm, ngroups),
        in_specs=in_specs,
        out_specs=[out_spec] * ngroups,
        out_shape=[jax.ShapeDtypeStruct((depth, bn, BRANCH_W, s), F32)] * ngroups,
        scratch_shapes=[pltpu.VMEM((tm, d), BF16)],
        input_output_aliases=aliases,
        compiler_params=_cparams("arbitrary", "arbitrary"),
        name="norm_proj_t",
    )(*args)


def _abias_kernel(tbl_ref, o_ref):
    h = pl.program_id(0)
    sub = 8
    ql = lax.broadcasted_iota(jnp.int32, (ATT_BLK, ATT_BLK), 0)
    kl = lax.broadcasted_iota(jnp.int32, (ATT_BLK, ATT_BLK), 1)
    far = jnp.full((ATT_BLK, ATT_BLK), tbl_ref[2 * A_REL_CLIP * A_HEADS + h] * LOG2E, F32)
    o_ref[0, 0] = jnp.where((kl >> 6) >= (ql >> 6), far, NEG_INF)
    lead = ATT_BLK - sub
    b = lax.broadcasted_iota(jnp.int32, (sub, 2 * ATT_BLK), 0)
    x = lax.broadcasted_iota(jnp.int32, (sub, 2 * ATT_BLK), 1)
    for seg, off in ((1, ATT_BLK), (2, 0)):
        idx = jnp.clip(b + lead - x + off, -A_REL_CLIP, A_REL_CLIP) + A_REL_CLIP

        def body(r, acc, idx=idx):
            return jnp.where(idx == r, tbl_ref[r * A_HEADS + h] * LOG2E, acc)

        strip = lax.fori_loop(0, 2 * A_REL_CLIP + 1, body, jnp.zeros((sub, 2 * ATT_BLK), F32))
        for a in range(ATT_BLK // sub):
            o_ref[0, seg, a * sub:(a + 1) * sub, :] = strip[:, lead - a * sub:lead - a * sub + ATT_BLK]
    o_ref[0, 2] = jnp.where((kl >> 6) <= (ql >> 6), o_ref[0, 2], NEG_INF)


def _build_abias(table):
    return pl.pallas_call(
        _abias_kernel,
        grid=(A_HEADS,),
        in_specs=[pl.BlockSpec(memory_space=pltpu.SMEM)],
        out_specs=pl.BlockSpec((1, 3, ATT_BLK, ATT_BLK), lambda h: (h, 0, 0, 0)),
        out_shape=jax.ShapeDtypeStruct((A_HEADS, 3, ATT_BLK, ATT_BLK), F32),
        compiler_params=_cparams("parallel"),
        name="band_bias_tiles",
    )(table.reshape(-1))


_T5_LARGE_STEPS = (12, 16, 23, 32, 46, 64, 91)


def _t5bias_kernel(tbl_ref, o_ref):
    h = pl.program_id(0)
    rows = 64
    for seg, off in ((0, 0), (1, -ATT_BLK)):
        for rc in range(ATT_BLK // rows):
            ql = lax.broadcasted_iota(jnp.int32, (rows, ATT_BLK), 0) + rc * rows
            kl = lax.broadcasted_iota(jnp.int32, (rows, ATT_BLK), 1)
            rel = kl - ql + off
            n = jnp.abs(rel)
            large = jnp.full((rows, ATT_BLK), T5_BUCKETS // 4, jnp.int32)
            for th in _T5_LARGE_STEPS:
                large = large + jnp.where(n >= th, 1, 0)
            bucket = jnp.where(rel > 0, T5_BUCKETS // 2, 0) + jnp.where(n < T5_BUCKETS // 4, n, large)
            tile = jnp.zeros((rows, ATT_BLK), F32)
            for r in range(T5_BUCKETS):
                tile = jnp.where(bucket == r, tbl_ref[r * C_HEADS + h] * LOG2E, tile)
            if seg == 0:
                tile = jnp.where((kl >> 6) <= (ql >> 6), tile, NEG_INF)
            o_ref[0, seg, rc * rows:(rc + 1) * rows, :] = tile


def _build_t5bias(table):
    return pl.pallas_call(
        _t5bias_kernel,
        grid=(C_HEADS,),
        in_specs=[pl.BlockSpec(memory_space=pltpu.SMEM)],
        out_specs=pl.BlockSpec((1, 2, ATT_BLK, ATT_BLK), lambda h: (h, 0, 0, 0)),
        out_shape=jax.ShapeDtypeStruct((C_HEADS, 2, ATT_BLK, ATT_BLK), F32),
        compiler_params=_cparams("parallel"),
        name="t5_bias_tiles",
    )(table.reshape(-1))


def _pspec(block, index_map, layer):
    return pl.BlockSpec((None,) + block, lambda *g: (layer,) + tuple(index_map(*g)))


def _lane_tile(x, n):
    return x if n == 1 else jnp.concatenate([x] * n, axis=1)


def _half_lanes(q, upper):
    lane = lax.broadcasted_iota(jnp.int32, q.shape, 1)
    return jnp.where((lane >= HEAD_DIM) == upper, q, 0.0).astype(BF16)


def _pair(ref, pr, feature_major):
    lo = pr * LANES
    return ref[lo:lo + LANES, :] if feature_major else ref[:, lo:lo + LANES]


def _qk(q, k, k_t):
    return _dot(q, k) if k_t else _dot_nt(q, k)


def _pv(p, v, v_t):
    return _dot_nt(p, v) if v_t else _dot(p, v)


def _softmax_segments(qh, kb, vb, biases, k_t, v_t):
    s = [_qk(qh, k, t) + b for k, b, t in zip(kb, biases, k_t)]
    m = functools.reduce(jnp.maximum, [jnp.max(sj, axis=1, keepdims=True) for sj in s])
    p = [jnp.exp2(sj - m) for sj in s]
    l = functools.reduce(jnp.add, [jnp.sum(pj, axis=1, keepdims=True) for pj in p])
    acc = functools.reduce(jnp.add, [_pv(pj.astype(BF16), v, t) for pj, v, t in zip(p, vb, v_t)])
    return acc / l


def _band_kernel(nseg, min_qi, kv_t, pairs, q_ref, *refs):
    k_refs, v_refs, b_refs = refs[:nseg], refs[nseg:2 * nseg], refs[2 * nseg:3 * nseg]
    o_ref = refs[3 * nseg]
    qi = pl.program_id(2) if any(min_qi) else 0
    q_all = q_ref[...] * SOFTMAX_QSCALE
    for pr in range(pairs):
        q = q_all[:, pr * LANES:(pr + 1) * LANES]
        kb = [_pair(k_refs[j], pr, kv_t[j]).astype(BF16) for j in range(nseg)]
        vb = [_pair(v_refs[j], pr, kv_t[j]).astype(BF16) for j in range(nseg)]
        outs = []
        for h in range(2):
            tiles = [b_refs[j][2 * pr + h] for j in range(nseg)]
            biases = [t if min_qi[j] == 0 else jnp.where(qi >= min_qi[j], t, NEG_INF) for j, t in enumerate(tiles)]
            outs.append(_softmax_segments(_half_lanes(q, bool(h)), kb, vb, biases, kv_t, kv_t))
        lane = lax.broadcasted_iota(jnp.int32, q.shape, 1)
        o_ref[:, pr * LANES:(pr + 1) * LANES] = jnp.where(lane < HEAD_DIM, outs[0], outs[1]).astype(o_ref.dtype)


def _band_attn_prompt(pn4, layer, k_t, v_t, abias):
    _, bn, s, _ = pn4.shape
    nq = s // ATT_BLK
    blk = (None, ATT_BLK, LANES)

    def kv_spec(back):
        return pl.BlockSpec((None, None, LANES, ATT_BLK),
                            lambda b, hp, qi: (layer, b, hp, jnp.maximum(qi - back, 0)))

    def bias_spec(seg):
        return pl.BlockSpec((2, None, ATT_BLK, ATT_BLK), lambda b, hp, qi: (hp, seg, 0, 0))

    return pl.pallas_call(
        functools.partial(_band_kernel, 3, (2, 1, 0), (True, True, True), 1),
        grid=(bn, A_HEADS // 2, nq),
        in_specs=([_pspec(blk, lambda b, hp, qi: (b, qi, _CBP["qa"] + hp), layer)]
                  + [kv_spec(back) for back in (2, 1, 0)]
                  + [kv_spec(back) for back in (2, 1, 0)]
                  + [bias_spec(seg) for seg in range(3)]),
        out_specs=pl.BlockSpec(blk, lambda b, hp, qi: (b, qi, hp)),
        out_shape=jax.ShapeDtypeStruct((bn, s, BRANCH_W), BF16),
        compiler_params=_cparams("parallel", "parallel", "arbitrary"),
        name="band_attn_prompt",
    )(pn4, k_t, k_t, k_t, v_t, v_t, v_t, abias, abias, abias)


def _band_attn_sample(p4, layer, k_cache_t, v_cache_t, bias_cache, bias_new):
    _, bn, t, _ = p4.shape
    keep = k_cache_t.shape[-1]
    pairs = A_HEADS // 2
    new = lambda name: _pspec((None, t, BRANCH_W), lambda b: (b, 0, _CB[name] * LANES // BRANCH_W), layer)
    old = pl.BlockSpec((None, None, BRANCH_W, keep), lambda b: (layer, b, 0, 0))
    return pl.pallas_call(
        functools.partial(_band_kernel, 2, (0, 0), (True, False), pairs),
        grid=(bn,),
        in_specs=[new("qa"), old, new("ka"), old, new("va"),
                  pl.BlockSpec((A_HEADS, t, keep), lambda b: (0, 0, 0)),
                  pl.BlockSpec((A_HEADS, t, t), lambda b: (0, 0, 0))],
        out_specs=pl.BlockSpec((None, t, BRANCH_W), lambda b: (b, 0, 0)),
        out_shape=jax.ShapeDtypeStruct((bn, t, BRANCH_W), BF16),
        compiler_params=_cparams("parallel"),
        name="band_attn_sample",
    )(p4, k_cache_t, p4, v_cache_t, p4, bias_cache, bias_new)


def _strict_upper(n):
    r = lax.broadcasted_iota(jnp.int32, (n, n), 0)
    c = lax.broadcasted_iota(jnp.int32, (n, n), 1)
    return jnp.where(r > c, 1.0, 0.0).astype(BF16)


def _sb_fold(qh, kb, vb, kv_t, upper, mask, carry_ref, acc_ref, first):
    tq, tk = qh[0].shape[0], upper.shape[0]
    top = None
    for h in range(len(qh)):
        z = _qk(qh[h], kb[h // 2], kv_t)
        soft = jnp.log1p(jnp.exp(-jnp.abs(z)))
        log_keep = -(jnp.maximum(z, 0.0) + soft)
        log_take = jnp.minimum(z, 0.0) - soft
        if mask is not None:
            log_keep = jnp.where(mask, log_keep, 0.0)
        hi = log_keep.astype(BF16)
        lo = (log_keep - hi.astype(F32)).astype(BF16)
        after = _dot(hi, upper) + _dot(lo, upper)
        if not first:
            after = after + _lane_tile(carry_ref[h], tk // LANES)
        w = jnp.exp(log_take + after)
        if mask is not None:
            w = jnp.where(mask, w, 0.0)
        pv = _pv(w.astype(BF16), vb[h // 2], kv_t)
        total = jnp.sum(log_keep, axis=1, keepdims=True)
        if first:
            acc_ref[h] = pv
            carry = jnp.broadcast_to(total, (tq, LANES))
        else:
            acc_ref[h] += pv
            carry = carry_ref[h] + total
        carry_ref[h] = carry
        top = jnp.max(carry) if top is None else jnp.maximum(top, jnp.max(carry))
    return (top > SB_LOG_CUT).astype(jnp.int32)


def _sb_kernel(tk, n_past_static, diag_t, pairs, q_ref, kd_ref, vd_ref, kp_ref, vp_ref, o_ref, carry_ref, acc_ref):
    n_past = pl.program_id(2) if n_past_static is None else n_past_static
    q_all = q_ref[...] * SCALE
    tq = q_all.shape[0]
    qh = []
    for pr in range(pairs):
        q = q_all[:, pr * LANES:(pr + 1) * LANES]
        qh += [_half_lanes(q, False), _half_lanes(q, True)]
    rq = lax.broadcasted_iota(jnp.int32, (tq, tq), 0)
    ck = lax.broadcasted_iota(jnp.int32, (tq, tq), 1)
    kd = [_pair(kd_ref, pr, diag_t).astype(BF16) for pr in range(pairs)]
    vd = [_pair(vd_ref, pr, diag_t).astype(BF16) for pr in range(pairs)]
    live = _sb_fold(qh, kd, vd, diag_t, _strict_upper(tq), ck < rq, carry_ref, acc_ref, True)

    def cond(st):
        return jnp.logical_and(st[0] >= 0, st[1] > 0)

    def body(st):
        start = pl.multiple_of(st[0] * tk, tk)
        kp = [kp_ref[pr * LANES:(pr + 1) * LANES, pl.ds(start, tk)].astype(BF16) for pr in range(pairs)]
        vp = [vp_ref[pr * LANES:(pr + 1) * LANES, pl.ds(start, tk)].astype(BF16) for pr in range(pairs)]
        live = _sb_fold(qh, kp, vp, True, _strict_upper(tk), None, carry_ref, acc_ref, False)
        return st[0] - 1, live

    lax.while_loop(cond, body, (n_past - 1, live))
    lane = lax.broadcasted_iota(jnp.int32, (tq, LANES), 1)
    for pr in range(pairs):
        o_ref[:, pr * LANES:(pr + 1) * LANES] = jnp.where(lane < HEAD_DIM, acc_ref[2 * pr],
                                                          acc_ref[2 * pr + 1]).astype(o_ref.dtype)


def _sb_attn_prompt(pn4, layer, k_t, v_t):
    _, bn, s, _ = pn4.shape
    blk = (None, ATT_BLK, LANES)
    own = pl.BlockSpec((None, None, LANES, ATT_BLK), lambda b, hp, qi: (layer, b, hp, qi))
    older = pl.BlockSpec((None, None, LANES, s), lambda b, hp, qi: (layer, b, hp, 0))
    return pl.pallas_call(
        functools.partial(_sb_kernel, ATT_BLK, None, True, 1),
        grid=(bn, B_HEADS // 2, s // ATT_BLK),
        in_specs=[_pspec(blk, lambda b, hp, qi: (b, qi, _CBP["qb"] + hp), layer), own, own, older, older],
        out_specs=pl.BlockSpec(blk, lambda b, hp, qi: (b, qi, hp)),
        out_shape=jax.ShapeDtypeStruct((bn, s, BRANCH_W), BF16),
        scratch_shapes=[pltpu.VMEM((2, ATT_BLK, LANES), F32), pltpu.VMEM((2, ATT_BLK, LANES), F32)],
        compiler_params=_cparams("parallel", "parallel", "arbitrary"),
        name="stick_attn_prompt",
    )(pn4, k_t, v_t, k_t, v_t)


def _sb_attn_sample(p4, layer, k_cache_t, v_cache_t):
    _, bn, t, _ = p4.shape
    past = k_cache_t.shape[-1]
    new = lambda name: _pspec((None, t, BRANCH_W), lambda b: (b, 0, _CB[name] * LANES // BRANCH_W), layer)
    older = pl.BlockSpec((None, None, BRANCH_W, past), lambda b: (layer, b, 0, 0))
    state = pltpu.VMEM((B_HEADS, t, LANES), F32)
    return pl.pallas_call(
        functools.partial(_sb_kernel, ATT_BLK, past // ATT_BLK, False, B_HEADS // 2),
        grid=(bn,),
        in_specs=[new("qb"), new("kb"), new("vb"), older, older],
        out_specs=pl.BlockSpec((None, t, BRANCH_W), lambda b: (b, 0, 0)),
        out_shape=jax.ShapeDtypeStruct((bn, t, BRANCH_W), BF16),
        scratch_shapes=[state, state],
        compiler_params=_cparams("parallel"),
        name="stick_attn_sample",
    )(p4, p4, p4, k_cache_t, v_cache_t)


def _diff_combine(o0, o1, lam_init, lam_ref, gain_ref):
    lp = lam_ref[...]
    lam = (jnp.exp(jnp.sum(lp[0:1] * lp[1:2], axis=1, keepdims=True))
           - jnp.exp(jnp.sum(lp[2:3] * lp[3:4], axis=1, keepdims=True)) + lam_init)
    return _rms(o0 - lam * o1, gain_ref[...]) * (1.0 - lam_init)


def _diff_fold(qm, k_t, v, bias, m_ref, l_ref, acc_ref, first):
    kb = k_t.astype(BF16)
    vb = v.astype(BF16)
    tq, tk = qm[0].shape[0], vb.shape[0]
    for m in range(2):
        s = _dot(qm[m], kb) + bias
        m_cur = jnp.max(s, axis=1, keepdims=True)
        m_new = jnp.broadcast_to(m_cur, (tq, LANES)) if first else jnp.maximum(m_ref[m], m_cur)
        p = jnp.exp2(s - _lane_tile(m_new, tk // LANES))
        row = jnp.sum(p, axis=1, keepdims=True)
        pv = _dot(p.astype(BF16), vb)
        if first:
            l_ref[m] = jnp.broadcast_to(row, (tq, LANES))
            acc_ref[m] = pv
        else:
            alpha = jnp.exp2(m_ref[m] - m_new)
            l_ref[m] = alpha * l_ref[m] + row
            acc_ref[m] = alpha * acc_ref[m] + pv
        m_ref[m] = m_new


def _diff_prompt_kernel(lam_init, tbl_ref, lam_ref, gain_ref, q_ref, kd_ref, vd_ref, kp_ref, vp_ref,
                        bd_ref, bp_ref, o_ref, m_ref, l_ref, acc_ref):
    h = pl.program_id(1)
    qi = pl.program_id(2)
    q = q_ref[...] * SOFTMAX_QSCALE
    blk = q.shape[0]
    qm = [_half_lanes(q, False), _half_lanes(q, True)]
    far = tbl_ref[T5_FAR_BUCKET * C_HEADS + h] * LOG2E

    near = pl.multiple_of(jnp.maximum(qi - 1, 0) * blk, blk)
    k0 = jnp.concatenate([kp_ref[:, pl.ds(near, blk)], kd_ref[...]], axis=1)
    v0 = jnp.concatenate([vp_ref[pl.ds(near, blk), :], vd_ref[...]], axis=0)
    b0 = jnp.concatenate([jnp.where(qi >= 1, bp_ref[...], NEG_INF), bd_ref[...]], axis=1)
    _diff_fold(qm, k0, v0, b0, m_ref, l_ref, acc_ref, True)

    n_far = jnp.maximum(qi - 1, 0)
    n_group = n_far // DIFF_FAR_GROUP
    wide = DIFF_FAR_GROUP * blk

    def body(j, c):
        start = pl.multiple_of(j * wide, wide)
        _diff_fold(qm, kp_ref[:, pl.ds(start, wide)], vp_ref[pl.ds(start, wide), :], far,
                   m_ref, l_ref, acc_ref, False)
        return c

    lax.fori_loop(0, n_group, body, 0)
    rest = pl.multiple_of(n_group * wide, wide)
    for r in range(1, DIFF_FAR_GROUP):
        @pl.when(n_far - n_group * DIFF_FAR_GROUP == r)
        def _(r=r):
            _diff_fold(qm, kp_ref[:, pl.ds(rest, r * blk)], vp_ref[pl.ds(rest, r * blk), :], far,
                       m_ref, l_ref, acc_ref, False)

    o = _diff_combine(acc_ref[0] / l_ref[0], acc_ref[1] / l_ref[1], lam_init, lam_ref, gain_ref)
    o_ref[...] = o.astype(o_ref.dtype)


def _diff_sample_kernel(lam_init, lam_ref, gain_ref, q_ref, kc_ref, vc_ref, kn_ref, vn_ref,
                        bc_ref, bn_ref, o_ref):
    q_all = q_ref[...] * SOFTMAX_QSCALE
    past = kc_ref.shape[1]
    k_t, v_t = (True, False), (False, False)
    for h in range(C_HEADS):
        cols = slice(h * LANES, (h + 1) * LANES)
        q = q_all[:, cols]
        kb = [kc_ref[cols, :].astype(BF16), kn_ref[:, cols].astype(BF16)]
        vb = [vc_ref[pl.ds(h, past, stride=C_HEADS), :].astype(BF16), vn_ref[:, cols].astype(BF16)]
        biases = [bc_ref[h], bn_ref[h]]
        o0 = _softmax_segments(_half_lanes(q, False), kb, vb, biases, k_t, v_t)
        o1 = _softmax_segments(_half_lanes(q, True), kb, vb, biases, k_t, v_t)
        o_ref[:, cols] = _diff_combine(o0, o1, lam_init, lam_ref, gain_ref).astype(o_ref.dtype)


def _diff_attn_prompt(pn4, layer, k_t, t5tiles, lam_init, t5_flat, lam_params, gain):
    _, bn, s, _ = pn4.shape
    blk = (None, ATT_BLK, LANES)
    full = (None, s, LANES)
    state = pltpu.VMEM((2, ATT_BLK, LANES), F32)
    return pl.pallas_call(
        functools.partial(_diff_prompt_kernel, lam_init),
        grid=(bn, C_HEADS, s // ATT_BLK),
        in_specs=[pl.BlockSpec(memory_space=pltpu.SMEM),
                  pl.BlockSpec((4, HEAD_DIM), lambda b, h, qi: (0, 0)),
                  pl.BlockSpec((1, C_VDIM), lambda b, h, qi: (0, 0)),
                  _pspec(blk, lambda b, h, qi: (b, qi, _CBP["qc"] + h), layer),
                  pl.BlockSpec((None, None, LANES, ATT_BLK), lambda b, h, qi: (layer, b, h, qi)),
                  _pspec(blk, lambda b, h, qi: (b, qi, _CBP["vc"] + h), layer),
                  pl.BlockSpec((None, None, LANES, s), lambda b, h, qi: (layer, b, h, 0)),
                  _pspec(full, lambda b, h, qi: (b, 0, _CBP["vc"] + h), layer),
                  pl.BlockSpec((None, None, ATT_BLK, ATT_BLK), lambda b, h, qi: (h, 0, 0, 0)),
                  pl.BlockSpec((None, None, ATT_BLK, ATT_BLK), lambda b, h, qi: (h, 1, 0, 0))],
        out_specs=pl.BlockSpec(blk, lambda b, h, qi: (b, qi, h)),
        out_shape=jax.ShapeDtypeStruct((bn, s, BRANCH_W), BF16),
        scratch_shapes=[state, state, state],
        compiler_params=_cparams("parallel", "parallel", "arbitrary"),
        name="diff_attn_prompt",
    )(t5_flat, lam_params, gain.reshape(1, C_VDIM), pn4, k_t, pn4, k_t, pn4, t5tiles, t5tiles)


def _diff_attn_sample(p4, layer, k_cache_t, v_cache, bias_cache, bias_new, lam_init, lam_params, gain):
    _, bn, t, _ = p4.shape
    past = k_cache_t.shape[-1]
    new = lambda name: _pspec((None, t, BRANCH_W), lambda b: (b, 0, _CB[name] * LANES // BRANCH_W), layer)
    return pl.pallas_call(
        functools.partial(_diff_sample_kernel, lam_init),
        grid=(bn,),
        in_specs=[pl.BlockSpec((4, HEAD_DIM), lambda b: (0, 0)),
                  pl.BlockSpec((1, C_VDIM), lambda b: (0, 0)),
                  new("qc"),
                  pl.BlockSpec((None, None, BRANCH_W, past), lambda b: (layer, b, 0, 0)),
                  pl.BlockSpec((None, None, past * C_HEADS, C_VDIM), lambda b: (layer, b, 0, 0)),
                  new("kc"), new("vc"),
                  pl.BlockSpec((C_HEADS, t, past), lambda b: (0, 0, 0)),
                  pl.BlockSpec((C_HEADS, t, t), lambda b: (0, 0, 0))],
        out_specs=pl.BlockSpec((None, t, BRANCH_W), lambda b: (b, 0, 0)),
        out_shape=jax.ShapeDtypeStruct((bn, t, BRANCH_W), BF16),
        compiler_params=_cparams("parallel"),
        name="diff_attn_sample",
    )(lam_params, gain.reshape(1, C_VDIM), p4, k_cache_t, v_cache, p4, p4, bias_cache, bias_new)


def _merge_kernel(x_ref, oa_ref, ob_ref, oc_ref, g0_ref, g1_ref, g2_ref, bg_ref, wb_ref, wo_ref, o_ref):
    h = None
    for n, (o_r, g_r) in enumerate(((oa_ref, g0_ref), (ob_ref, g1_ref), (oc_ref, g2_ref))):
        gate = 1.0 / (1.0 + jnp.exp(-(g_r[...] + bg_ref[n:n + 1, :])))
        t = gate * _dot(o_r[...], wb_ref[n])
        h = t if h is None else h + t
    o_ref[...] = x_ref[...] + _dot(h.astype(BF16), wo_ref[...])


def _merge(x, o_a, o_b, o_c, p, layer, b_gate, w_branch, w_out, tm):
    t, d = x.shape
    row = lambda i: (i, 0)
    fixed2 = lambda i: (0, 0)
    return pl.pallas_call(
        _merge_kernel,
        grid=(t // tm,),
        in_specs=[pl.BlockSpec((tm, d), row),
                  pl.BlockSpec((tm, BRANCH_W), row),
                  pl.BlockSpec((tm, BRANCH_W), row),
                  pl.BlockSpec((tm, BRANCH_W), row),
                  pl.BlockSpec((None, tm, d), lambda i: (layer, i, 0)),
                  pl.BlockSpec((None, tm, d), lambda i: (layer, i, 1)),
                  pl.BlockSpec((None, tm, d), lambda i: (layer, i, 2)),
                  pl.BlockSpec((N_BRANCH, d), fixed2),
                  pl.BlockSpec((N_BRANCH, BRANCH_W, d), lambda i: (0, 0, 0)),
                  pl.BlockSpec((d, d), fixed2)],
        out_specs=pl.BlockSpec((tm, d), row),
        out_shape=jax.ShapeDtypeStruct((t, d), F32),
        compiler_params=_cparams("parallel"),
        name="gated_merge",
    )(x, o_a, o_b, o_c, p, p, p, b_gate, w_branch, w_out)


FFN_TF = D_FF // 2
FFN_PREV = 16


def _gelu_tanh(x):
    return x * (0.5 * (1.0 + jnp.tanh(math.sqrt(2.0 / math.pi) * (x + 0.044715 * (x * x * x)))))


def _ffn_kernel(tiles_per_seq, x_ref, xp_ref, gn_ref, wg_ref, wu_ref, cw_ref, cb_ref, wd_ref, st_ref,
                o_ref, so_ref, xn_ref, xpn_ref):
    i = pl.program_id(0)
    j = pl.program_id(1)
    whole_seqs = tiles_per_seq == 0

    @pl.when(j == 0)
    def _():
        xn_ref[...] = _rms(x_ref[...], gn_ref[...]).astype(BF16)
        if not whole_seqs:
            xpn_ref[...] = _rms(xp_ref[...], gn_ref[...]).astype(BF16)

    xn = xn_ref[...]
    tm = xn.shape[0]
    tf = wg_ref.shape[1]
    g = _dot(xn, wg_ref[...])
    u = _dot(xn, wu_ref[...])
    if whole_seqs:
        ns = st_ref.shape[0]
        st = st_ref[...]
        g = g.reshape(ns, tm // ns, tf)
        u = u.reshape(ns, tm // ns, tf)
        pm2, pm1 = st[:, 0:1, :], st[:, 1:2, :]
    else:
        gp = _dot(xpn_ref[...], wg_ref[...])
        first = (i % tiles_per_seq) == 0
        st = st_ref[0]
        pm2 = jnp.where(first, st[0:1, :], gp[FFN_PREV - 2:FFN_PREV - 1, :])
        pm1 = jnp.where(first, st[1:2, :], gp[FFN_PREV - 1:FFN_PREV, :])
    ax = g.ndim - 2
    row = lax.broadcasted_iota(jnp.int32, g.shape, ax)
    g1 = jnp.where(row == 0, pm1, pltpu.roll(g, 1, ax))
    g2 = jnp.where(row == 0, pm2, jnp.where(row == 1, pm1, pltpu.roll(g, 2, ax)))
    cw = cw_ref[...]
    c = cb_ref[...] + cw[0:1, :] * g2 + cw[1:2, :] * g1 + cw[2:3, :] * g
    hid = (_gelu_tanh(c) * u).reshape(tm, tf).astype(BF16)
    contrib = _dot(hid, wd_ref[...])

    @pl.when(j == 0)
    def _():
        o_ref[...] = x_ref[...] + contrib

    @pl.when(j > 0)
    def _():
        o_ref[...] += contrib

    seq_len = g.shape[ax]
    if whole_seqs:
        so_ref[...] = g[:, seq_len - 2:seq_len, :]
    else:
        so_ref[0] = g[seq_len - 2:seq_len, :]


def _conv_ffn(x, g_norm, w_gate, w_up, conv_w, conv_b, w_down, state, seq_len, tm):
    t, d = x.shape
    nseq = t // seq_len
    nf = D_FF // FFN_TF
    if seq_len >= tm:
        tiles_per_seq = seq_len // tm
        st_spec = pl.BlockSpec((1, 2, FFN_TF), lambda i, j: (i // tiles_per_seq, 0, j))
    else:
        tiles_per_seq = 0
        st_spec = pl.BlockSpec((tm // seq_len, 2, FFN_TF), lambda i, j: (i, 0, j))
    prev_blocks = tm // FFN_PREV
    tail_spec = pl.BlockSpec(st_spec.block_shape, lambda i, j: (i, 0, j))
    n_tail = t // tm * st_spec.block_shape[0]
    y, tails = pl.pallas_call(
        functools.partial(_ffn_kernel, tiles_per_seq),
        grid=(t // tm, nf),
        in_specs=[pl.BlockSpec((tm, d), lambda i, j: (i, 0)),
                  pl.BlockSpec((FFN_PREV, d), lambda i, j: (jnp.maximum(i * prev_blocks - 1, 0), 0)),
                  pl.BlockSpec((1, d), lambda i, j: (0, 0)),
                  pl.BlockSpec((d, FFN_TF), lambda i, j: (0, j)),
                  pl.BlockSpec((d, FFN_TF), lambda i, j: (0, j)),
                  pl.BlockSpec((3, FFN_TF), lambda i, j: (0, j)),
                  pl.BlockSpec((1, FFN_TF), lambda i, j: (0, j)),
                  pl.BlockSpec((FFN_TF, d), lambda i, j: (j, 0)),
                  st_spec],
        out_specs=[pl.BlockSpec((tm, d), lambda i, j: (i, 0)), tail_spec],
        out_shape=[jax.ShapeDtypeStruct((t, d), F32),
                   jax.ShapeDtypeStruct((n_tail, 2, D_FF), F32)],
        scratch_shapes=[pltpu.VMEM((tm, d), BF16), pltpu.VMEM((FFN_PREV, d), BF16)],
        compiler_params=_cparams("arbitrary", "arbitrary"),
        name="conv_ffn_prompt" if tiles_per_seq else "conv_ffn_sample",
    )(x, x, g_norm.reshape(1, d), w_gate, w_up, conv_w, conv_b.reshape(1, D_FF), w_down, state)
    return y, tails[n_tail // nseq - 1::n_tail // nseq]


def _final_norm_kernel(x_ref, g_ref, o_ref):
    o_ref[...] = _rms(x_ref[...], g_ref[...])


def _final_norm(x, g, tm):
    t, d = x.shape
    return pl.pallas_call(
        _final_norm_kernel,
        grid=(t // tm,),
        in_specs=[pl.BlockSpec((tm, d), lambda i: (i, 0)), pl.BlockSpec((1, d), lambda i: (0, 0))],
        out_specs=pl.BlockSpec((tm, d), lambda i: (i, 0)),
        out_shape=jax.ShapeDtypeStruct((t, d), F32),
        compiler_params=_cparams("parallel"),
        name="final_norm",
    )(x, g.reshape(1, d))


PROJ_TM, PROJ_TN = 1024, 1280
ROW_TM = 512


def _lam_init(layer):
    return 0.8 - 0.6 * math.exp(-0.3 * layer)


def _cols(p4, name):
    c0 = _CB[name] * LANES
    return p4[..., c0:c0 + BRANCH_W]


def _layer_tail(x2, o_a, o_b, o_c, p_rows, layer, prm, conv_state, seq_len):
    flat = lambda a: a.reshape(x2.shape[0], BRANCH_W)
    x2 = _merge(x2, flat(o_a), flat(o_b), flat(o_c), p_rows, layer, prm["b_gate"][layer],
                prm["w_branch"][layer], prm["w_out"][layer], ROW_TM)
    return _conv_ffn(x2, prm["norm_ffn"][layer], prm["w_gate"][layer], prm["w_up"][layer],
                     prm["conv_w"][layer], prm["conv_b"][layer], prm["w_down"][layer], conv_state, seq_len, ROW_TM)


def _trunk_prompt(x, prm, abias, t5tiles, t5_flat):
    bn, s, d = x.shape
    x2 = x.reshape(bn * s, d)
    depth = prm["w_rows"].shape[0]
    conv_states = []
    rows, kv_t = None, None
    for l in range(depth):
        rows = _norm_proj(x2, prm["norm_mix"][l], prm["w_rows"][l], PROJ_TM, PROJ_TN, l, depth, rows,
                          heads_group=(_CBP["vc"] * LANES, C_HEADS))
        p_rows, vc_heads = rows
        kv_t = _norm_proj_t(x2, prm["norm_mix"][l], prm["w_feat_t"][l], PROJ_TM, l, depth, bn, kv_t)
        ka_t, va_t, kb_t, vb_t, kc_t = kv_t
        pn4 = p_rows.reshape(depth, bn, s, ROW_COLS)
        o_a = _band_attn_prompt(pn4, l, ka_t, va_t, abias[l])
        o_b = _sb_attn_prompt(pn4, l, kb_t, vb_t)
        o_c = _diff_attn_prompt(pn4, l, kc_t, t5tiles, _lam_init(l), t5_flat, prm["c_lambda"][l], prm["c_subln"][l])
        x2, st_conv = _layer_tail(x2, o_a, o_b, o_c, p_rows, l, prm, jnp.zeros((bn, 2, D_FF), F32), s)
        conv_states.append(st_conv)
    y = _final_norm(x2, prm["norm_final"], ROW_TM).reshape(bn, s, d)
    heads_t = lambda a: jnp.transpose(a.reshape(depth, bn, A_HEADS, HEAD_DIM, a.shape[-1]), (0, 1, 4, 2, 3))
    keep = min(A_PAST_CHUNKS * CHUNK, s)
    states = (heads_t(ka_t[..., s - keep:]), heads_t(va_t[..., s - keep:]), heads_t(kb_t), heads_t(vb_t),
              jnp.transpose(kc_t.reshape(depth, bn, C_HEADS, 2, HEAD_DIM, s), (0, 1, 5, 2, 3, 4)),
              vc_heads.reshape(depth, bn, s, C_HEADS, C_VDIM),
              jnp.stack(conv_states, axis=0))
    return y, states


def _feature_major(cache):
    nd = cache.ndim
    t = jnp.transpose(cache, (0, 1) + tuple(range(3, nd)) + (2,))
    return t.reshape(cache.shape[0], cache.shape[1], BRANCH_W, cache.shape[2])


def _trunk_sample(x, caches, prm, abias, t5tiles):
    bn, t, d = x.shape
    x2 = x.reshape(bn * t, d)
    depth = prm["w_in"].shape[0]
    ca_k, ca_v, cb_k, cb_v, cc_k, cc_v, st_conv_in = caches
    ca_kt, ca_vt, cb_kt, cb_vt, cc_kt = (_feature_major(c) for c in (ca_k, ca_v, cb_k, cb_v, cc_k))
    keep, past = ca_k.shape[2], cc_k.shape[2]
    cc_vr = cc_v.reshape(depth, bn, past * C_HEADS, C_VDIM)
    conv_states = []
    p_all = None
    for l in range(depth):
        p_all, = _norm_proj(x2, prm["norm_mix"][l], prm["w_in"][l], PROJ_TM, PROJ_TN, l, depth,
                            None if p_all is None else [p_all])
        p4 = p_all.reshape(depth, bn, t, IN_COLS)
        bias_cache = jnp.concatenate([abias[l][:, 0, :t, :], abias[l][:, 1, :t, :]], axis=-1)[..., -keep:]
        o_a = _band_attn_sample(p4, l, ca_kt, ca_vt, bias_cache, abias[l][:, 2, :t, :t])
        o_b = _sb_attn_sample(p4, l, cb_kt, cb_vt)
        far = jnp.broadcast_to(t5tiles[:, 1, :1, :1], (C_HEADS, t, past - ATT_BLK))
        bias_c = jnp.concatenate([far, t5tiles[:, 1, :t, :]], axis=-1)
        o_c = _diff_attn_sample(p4, l, cc_kt, cc_vr, bias_c, t5tiles[:, 0, :t, :t],
                                _lam_init(l), prm["c_lambda"][l], prm["c_subln"][l])
        x2, st_conv = _layer_tail(x2, o_a, o_b, o_c, p_all, l, prm, st_conv_in[l], t)
        conv_states.append(st_conv)
    y = _final_norm(x2, prm["norm_final"], ROW_TM).reshape(bn, t, d)
    ka, va, kb, vb, kc, vc = (_cols(p4, n) for n in ("ka", "va", "kb", "vb", "kc", "vc"))
    heads = lambda a: a.reshape(a.shape[:3] + (A_HEADS, HEAD_DIM))
    states = (jnp.concatenate([ca_k[:, :, t:], heads(ka)], axis=2),
              jnp.concatenate([ca_v[:, :, t:], heads(va)], axis=2), heads(kb), heads(vb),
              kc.reshape(depth, bn, t, C_HEADS, 2, HEAD_DIM), vc.reshape(depth, bn, t, C_HEADS, C_VDIM),
              jnp.stack(conv_states, axis=0))
    return y, states


def kernel(x_prompt, x_sample, cache_a_k, cache_a_v, cache_b_k, cache_b_v, cache_c_k, cache_c_v,
           state_ffn_conv, norm_mix, w_in, b_gate, a_rel_bias, t5_bias, c_lambda, c_subln,
           w_branch, w_out, norm_ffn, w_up, conv_w, conv_b, w_down, norm_final):
    group = {name: w_in[..., i * BRANCH_W:(i + 1) * BRANCH_W] for i, name in enumerate(_GROUPS)}
    gate = w_in[..., len(_GROUPS) * BRANCH_W:]
    prm = dict(
        norm_mix=norm_mix,
        w_in=jnp.concatenate([gate] + [group[n] for n in _GROUPS], axis=-1).astype(BF16),
        w_rows=jnp.concatenate([gate] + [group[n] for n in _ROW_GROUPS], axis=-1).astype(BF16),
        w_feat_t=jnp.swapaxes(jnp.concatenate([group[n] for n in _T_GROUPS], axis=-1), 1, 2).astype(BF16),
        b_gate=b_gate, c_lambda=c_lambda, c_subln=c_subln,
        w_branch=w_branch.astype(BF16), w_out=w_out.astype(BF16), norm_ffn=norm_ffn,
        w_gate=w_up[..., :D_FF].astype(BF16), w_up=w_up[..., D_FF:].astype(BF16),
        conv_w=conv_w, conv_b=conv_b, w_down=w_down.astype(BF16), norm_final=norm_final)
    abias = [_build_abias(a_rel_bias[l]) for l in range(a_rel_bias.shape[0])]
    t5tiles = _build_t5bias(t5_bias)
    y_p, p_states = _trunk_prompt(x_prompt, prm, abias, t5tiles, t5_bias.reshape(-1))
    caches = (cache_a_k, cache_a_v, cache_b_k, cache_b_v, cache_c_k, cache_c_v, state_ffn_conv)
    y_s, s_states = _trunk_sample(x_sample, caches, prm, abias, t5tiles)
    return (y_p, y_s) + p_states + s_states
```

```python
import functools
import math

import jax
import jax.numpy as jnp
from jax import lax
from jax.experimental import pallas as pl
from jax.experimental.pallas import tpu as pltpu

F32 = jnp.float32
BF16 = jnp.bfloat16

D_MODEL = 1024
CHUNK = 64
HEAD_DIM = 64
A_HEADS = 8
A_PAST_CHUNKS = 8
A_REL_CLIP = 128
B_HEADS = 8
C_HEADS = 4
C_VDIM = 128
T5_BUCKETS = 32
D_FF = 2816
N_BRANCH = 3
BRANCH_W = 512
GATE_W = N_BRANCH * D_MODEL
IN_COLS = 9 * BRANCH_W + GATE_W
EPS = 1e-6
NEG_INF = -1e30
SCALE = HEAD_DIM ** -0.5
LOG2E = math.log2(math.e)
SOFTMAX_QSCALE = SCALE * LOG2E

LANES = 128
_GROUPS = ("qa", "ka", "va", "qb", "kb", "vb", "qc", "kc", "vc")
_CB = {name: (GATE_W + i * BRANCH_W) // LANES for i, name in enumerate(_GROUPS)}
_ROW_GROUPS = ("qa", "qb", "qc", "vc")
_T_GROUPS = ("ka", "va", "kb", "vb", "kc")
_CBP = {name: (GATE_W + i * BRANCH_W) // LANES for i, name in enumerate(_ROW_GROUPS)}
ROW_COLS = GATE_W + len(_ROW_GROUPS) * BRANCH_W

ATT_BLK = 256
SB_LOG_CUT = -104.0
T5_FAR_BUCKET = 15
DIFF_FAR_GROUP = 4
V7X_VMEM_BYTES = 64 * 1024 * 1024
VMEM_LIMIT = V7X_VMEM_BYTES - 8 * 1024 * 1024


def _cparams(*sem):
    return pltpu.CompilerParams(dimension_semantics=sem, vmem_limit_bytes=VMEM_LIMIT)


def _rms(x, g):
    return x * lax.rsqrt(jnp.mean(x * x, axis=-1, keepdims=True) + EPS) * g


def _dot(a, b):
    return jnp.dot(a, b, preferred_element_type=F32)


def _dot_nt(a, b):
    return lax.dot_general(a, b, (((1,), (1,)), ((), ())), preferred_element_type=F32)


def _norm_proj_kernel(n_stack, heads_cols, x_ref, g_ref, w_ref, *refs):
    outs, xn_ref = refs[n_stack:-1], refs[-1]
    j = pl.program_id(1)

    @pl.when(j == 0)
    def _():
        xn_ref[...] = _rms(x_ref[...], g_ref[...]).astype(BF16)

    res = _dot(xn_ref[...], w_ref[...])
    outs[0][...] = res
    if heads_cols is not None:
        tile, first, heads = heads_cols

        @pl.when(j == tile)
        def _():
            for h in range(heads):
                lo = first + h * LANES
                outs[1][pl.ds(h, res.shape[0], stride=heads), :] = res[:, lo:lo + LANES]


def _norm_proj(x, g, w, tm, tn, layer, depth, stacked, heads_group=None):
    t, d = x.shape
    n = w.shape[1]
    in_specs = [pl.BlockSpec((tm, d), lambda i, j: (i, 0)),
                pl.BlockSpec((1, d), lambda i, j: (0, 0)),
                pl.BlockSpec((d, tn), lambda i, j: (0, j))]
    args = [x, g.reshape(1, d), w]
    out_specs = [pl.BlockSpec((None, tm, tn), lambda i, j: (layer, i, j))]
    out_shape = [jax.ShapeDtypeStruct((depth, t, n), F32)]
    heads_cols = None
    if heads_group is not None:
        first, heads = heads_group
        heads_cols = (first // tn, first % tn, heads)
        out_specs.append(pl.BlockSpec((None, tm * heads, LANES), lambda i, j: (layer, i, 0)))
        out_shape.append(jax.ShapeDtypeStruct((depth, t * heads, LANES), F32))
    n_stack = 0 if stacked is None else len(out_shape)
    if stacked is not None:
        in_specs += [pl.BlockSpec(memory_space=pl.ANY)] * n_stack
        args += list(stacked)
    return pl.pallas_call(
        functools.partial(_norm_proj_kernel, n_stack, heads_cols),
        grid=(t // tm, n // tn),
        in_specs=in_specs,
        out_specs=out_specs,
        out_shape=out_shape,
        scratch_shapes=[pltpu.VMEM((tm, d), BF16)],
        input_output_aliases={3 + k: k for k in range(n_stack)},
        compiler_params=_cparams("arbitrary", "arbitrary"),
        name="norm_proj",
    )(*args)


def _norm_proj_t_kernel(ngroups, x_ref, g_ref, wt_ref, *refs):
    o_refs = refs[-ngroups:]
    xn = _rms(x_ref[...], g_ref[...]).astype(BF16)
    for g in range(ngroups):
        o_refs[g][...] = _dot_nt(wt_ref[g * BRANCH_W:(g + 1) * BRANCH_W, :], xn)


def _norm_proj_t(x, g, wt, tm, layer, depth, bn, stacked):
    t, d = x.shape
    s = t // bn
    ngroups = wt.shape[0] // BRANCH_W
    tiles = s // tm
    in_specs = [pl.BlockSpec((tm, d), lambda i: (i, 0)),
                pl.BlockSpec((1, d), lambda i: (0, 0)),
                pl.BlockSpec(wt.shape, lambda i: (0, 0))]
    args = [x, g.reshape(1, d), wt]
    aliases = {}
    if stacked is not None:
        in_specs += [pl.BlockSpec(memory_space=pl.ANY)] * ngroups
        args += list(stacked)
        aliases = {3 + n: n for n in range(ngroups)}
    out_spec = pl.BlockSpec((None, None, BRANCH_W, tm), lambda i: (layer, i // tiles, 0, i % tiles))
    return pl.pallas_call(
        functools.partial(_norm_proj_t_kernel, ngroups),
        grid=(t // tm,),
        in_specs=in_specs,
        out_specs=[out_spec] * ngroups,
        out_shape=[jax.ShapeDtypeStruct((depth, bn, BRANCH_W, s), F32)] * ngroups,
        input_output_aliases=aliases,
        compiler_params=_cparams("arbitrary"),
        name="norm_proj_t",
    )(*args)


def _abias_kernel(tbl_ref, o_ref):
    h = pl.program_id(0)
    sub = 8
    ql = lax.broadcasted_iota(jnp.int32, (ATT_BLK, ATT_BLK), 0)
    kl = lax.broadcasted_iota(jnp.int32, (ATT_BLK, ATT_BLK), 1)
    far = jnp.full((ATT_BLK, ATT_BLK), tbl_ref[2 * A_REL_CLIP * A_HEADS + h] * LOG2E, F32)
    o_ref[0, 0] = jnp.where((kl >> 6) >= (ql >> 6), far, NEG_INF)
    lead = ATT_BLK - sub
    b = lax.broadcasted_iota(jnp.int32, (sub, 2 * ATT_BLK), 0)
    x = lax.broadcasted_iota(jnp.int32, (sub, 2 * ATT_BLK), 1)
    for seg, off in ((1, ATT_BLK), (2, 0)):
        idx = jnp.clip(b + lead - x + off, -A_REL_CLIP, A_REL_CLIP) + A_REL_CLIP

        def body(r, acc, idx=idx):
            return jnp.where(idx == r, tbl_ref[r * A_HEADS + h] * LOG2E, acc)

        strip = lax.fori_loop(0, 2 * A_REL_CLIP + 1, body, jnp.zeros((sub, 2 * ATT_BLK), F32))
        for a in range(ATT_BLK // sub):
            o_ref[0, seg, a * sub:(a + 1) * sub, :] = strip[:, lead - a * sub:lead - a * sub + ATT_BLK]
    o_ref[0, 2] = jnp.where((kl >> 6) <= (ql >> 6), o_ref[0, 2], NEG_INF)


def _build_abias(table):
    return pl.pallas_call(
        _abias_kernel,
        grid=(A_HEADS,),
        in_specs=[pl.BlockSpec(memory_space=pltpu.SMEM)],
        out_specs=pl.BlockSpec((1, 3, ATT_BLK, ATT_BLK), lambda h: (h, 0, 0, 0)),
        out_shape=jax.ShapeDtypeStruct((A_HEADS, 3, ATT_BLK, ATT_BLK), F32),
        compiler_params=_cparams("parallel"),
        name="band_bias_tiles",
    )(table.reshape(-1))


_T5_LARGE_STEPS = (12, 16, 23, 32, 46, 64, 91)


def _t5bias_kernel(tbl_ref, o_ref):
    h = pl.program_id(0)
    rows = 64
    for seg, off in ((0, 0), (1, -ATT_BLK)):
        for rc in range(ATT_BLK // rows):
            ql = lax.broadcasted_iota(jnp.int32, (rows, ATT_BLK), 0) + rc * rows
            kl = lax.broadcasted_iota(jnp.int32, (rows, ATT_BLK), 1)
            rel = kl - ql + off
            n = jnp.abs(rel)
            large = jnp.full((rows, ATT_BLK), T5_BUCKETS // 4, jnp.int32)
            for th in _T5_LARGE_STEPS:
                large = large + jnp.where(n >= th, 1, 0)
            bucket = jnp.where(rel > 0, T5_BUCKETS // 2, 0) + jnp.where(n < T5_BUCKETS // 4, n, large)
            tile = jnp.zeros((rows, ATT_BLK), F32)
            for r in range(T5_BUCKETS):
                tile = jnp.where(bucket == r, tbl_ref[r * C_HEADS + h] * LOG2E, tile)
            if seg == 0:
                tile = jnp.where((kl >> 6) <= (ql >> 6), tile, NEG_INF)
            o_ref[0, seg, rc * rows:(rc + 1) * rows, :] = tile


def _build_t5bias(table):
    return pl.pallas_call(
        _t5bias_kernel,
        grid=(C_HEADS,),
        in_specs=[pl.BlockSpec(memory_space=pltpu.SMEM)],
        out_specs=pl.BlockSpec((1, 2, ATT_BLK, ATT_BLK), lambda h: (h, 0, 0, 0)),
        out_shape=jax.ShapeDtypeStruct((C_HEADS, 2, ATT_BLK, ATT_BLK), F32),
        compiler_params=_cparams("parallel"),
        name="t5_bias_tiles",
    )(table.reshape(-1))


def _pspec(block, index_map, layer):
    return pl.BlockSpec((None,) + block, lambda *g: (layer,) + tuple(index_map(*g)))


def _lane_tile(x, n):
    return x if n == 1 else jnp.concatenate([x] * n, axis=1)


def _half_lanes(q, upper):
    lane = lax.broadcasted_iota(jnp.int32, q.shape, 1)
    return jnp.where((lane >= HEAD_DIM) == upper, q, 0.0).astype(BF16)


def _stack_heads(q, width):
    group = lax.broadcasted_iota(jnp.int32, q.shape, 1) // width
    return jnp.concatenate([jnp.where(group == h, q, 0.0) for h in range(q.shape[1] // width)],
                           axis=0).astype(BF16)


def _unstack_heads(x, t, width):
    group = lax.broadcasted_iota(jnp.int32, (t, x.shape[1]), 1) // width
    out = jnp.zeros((t, x.shape[1]), x.dtype)
    for h in range(x.shape[1] // width):
        out = jnp.where(group == h, x[h * t:(h + 1) * t, :], out)
    return out


def _qk(q, k, k_t):
    return _dot(q, k) if k_t else _dot_nt(q, k)


def _pv(p, v, v_t):
    return _dot_nt(p, v) if v_t else _dot(p, v)


def _softmax_segments(qh, kb, vb, biases, k_t, v_t):
    s = [_qk(qh, k, t) + b for k, b, t in zip(kb, biases, k_t)]
    m = functools.reduce(jnp.maximum, [jnp.max(sj, axis=1, keepdims=True) for sj in s])
    p = [jnp.exp2(sj - m) for sj in s]
    l = functools.reduce(jnp.add, [jnp.sum(pj, axis=1, keepdims=True) for pj in p])
    acc = functools.reduce(jnp.add, [_pv(pj.astype(BF16), v, t) for pj, v, t in zip(p, vb, v_t)])
    return acc / l


def _band_kernel(nseg, min_qi, q_ref, *refs):
    k_refs, v_refs, b_refs = refs[:nseg], refs[nseg:2 * nseg], refs[2 * nseg:3 * nseg]
    o_ref = refs[3 * nseg]
    qi = pl.program_id(2)
    q = q_ref[...] * SOFTMAX_QSCALE
    kb = [k_refs[j][...].astype(BF16) for j in range(nseg)]
    vb = [v_refs[j][...].astype(BF16) for j in range(nseg)]
    outs = []
    for h in range(2):
        biases = [b_refs[j][h] if min_qi[j] == 0 else jnp.where(qi >= min_qi[j], b_refs[j][h], NEG_INF)
                  for j in range(nseg)]
        outs.append(_softmax_segments(_half_lanes(q, bool(h)), kb, vb, biases, (True,) * nseg, (True,) * nseg))
    lane = lax.broadcasted_iota(jnp.int32, q.shape, 1)
    o_ref[...] = jnp.where(lane < HEAD_DIM, outs[0], outs[1]).astype(o_ref.dtype)


def _band_attn_prompt(pn4, layer, k_t, v_t, abias):
    _, bn, s, _ = pn4.shape
    nq = s // ATT_BLK
    blk = (None, ATT_BLK, LANES)

    def kv_spec(back):
        return pl.BlockSpec((None, None, LANES, ATT_BLK),
                            lambda b, hp, qi: (layer, b, hp, jnp.maximum(qi - back, 0)))

    def bias_spec(seg):
        return pl.BlockSpec((2, None, ATT_BLK, ATT_BLK), lambda b, hp, qi: (hp, seg, 0, 0))

    return pl.pallas_call(
        functools.partial(_band_kernel, 3, (2, 1, 0)),
        grid=(bn, A_HEADS // 2, nq),
        in_specs=([_pspec(blk, lambda b, hp, qi: (b, qi, _CBP["qa"] + hp), layer)]
                  + [kv_spec(back) for back in (2, 1, 0)]
                  + [kv_spec(back) for back in (2, 1, 0)]
                  + [bias_spec(seg) for seg in range(3)]),
        out_specs=pl.BlockSpec(blk, lambda b, hp, qi: (b, qi, hp)),
        out_shape=jax.ShapeDtypeStruct((bn, s, BRANCH_W), BF16),
        compiler_params=_cparams("parallel", "parallel", "arbitrary"),
        name="band_attn_prompt",
    )(pn4, k_t, k_t, k_t, v_t, v_t, v_t, abias, abias, abias)


def _band_sample_kernel(q_ref, kc_ref, kn_ref, vc_ref, vn_ref, bc_ref, bn_ref, o_ref):
    t = q_ref.shape[0]
    qs = _stack_heads(q_ref[...] * SOFTMAX_QSCALE, HEAD_DIM)
    kb = [kc_ref[...].astype(BF16), kn_ref[...].astype(BF16)]
    vb = [vc_ref[...].astype(BF16), vn_ref[...].astype(BF16)]
    o = _softmax_segments(qs, kb, vb, [bc_ref[...], bn_ref[...]], (True, False), (True, False))
    o_ref[...] = _unstack_heads(o, t, HEAD_DIM).astype(o_ref.dtype)


def _band_attn_sample(p4, layer, k_cache_t, v_cache_t, bias_cache, bias_new):
    _, bn, t, _ = p4.shape
    keep = k_cache_t.shape[-1]
    new = lambda name: _pspec((None, t, BRANCH_W), lambda b: (b, 0, _CB[name] * LANES // BRANCH_W), layer)
    old = pl.BlockSpec((None, None, BRANCH_W, keep), lambda b: (layer, b, 0, 0))
    return pl.pallas_call(
        _band_sample_kernel,
        grid=(bn,),
        in_specs=[new("qa"), old, new("ka"), old, new("va"),
                  pl.BlockSpec((A_HEADS * t, keep), lambda b: (0, 0)),
                  pl.BlockSpec((A_HEADS * t, t), lambda b: (0, 0))],
        out_specs=pl.BlockSpec((None, t, BRANCH_W), lambda b: (b, 0, 0)),
        out_shape=jax.ShapeDtypeStruct((bn, t, BRANCH_W), BF16),
        compiler_params=_cparams("parallel"),
        name="band_attn_sample",
    )(p4, k_cache_t, p4, v_cache_t, p4, bias_cache.reshape(A_HEADS * t, keep), bias_new.reshape(A_HEADS * t, t))


def _strict_upper(n):
    r = lax.broadcasted_iota(jnp.int32, (n, n), 0)
    c = lax.broadcasted_iota(jnp.int32, (n, n), 1)
    return jnp.where(r > c, 1.0, 0.0).astype(BF16)


def _sb_fold(qh, kb, vb, kv_t, upper, mask, carry_ref, acc_ref, first):
    tq, tk = qh[0].shape[0], upper.shape[0]
    top = None
    for h in range(len(qh)):
        z = _qk(qh[h], kb[0], kv_t)
        soft = jnp.log(1.0 + jnp.exp(-jnp.abs(z)))
        log_keep = -(jnp.maximum(z, 0.0) + soft)
        log_take = jnp.minimum(z, 0.0) - soft
        if mask is not None:
            log_keep = jnp.where(mask, log_keep, 0.0)
        hi = log_keep.astype(BF16)
        lo = (log_keep - hi.astype(F32)).astype(BF16)
        after = _dot(hi, upper) + _dot(lo, upper)
        if not first:
            after = after + _lane_tile(carry_ref[h], tk // LANES)
        w = jnp.exp(log_take + after)
        if mask is not None:
            w = jnp.where(mask, w, 0.0)
        pv = _pv(w.astype(BF16), vb[0], kv_t)
        total = jnp.sum(log_keep, axis=1, keepdims=True)
        if first:
            acc_ref[h] = pv
            carry = jnp.broadcast_to(total, (tq, LANES))
        else:
            acc_ref[h] += pv
            carry = carry_ref[h] + total
        carry_ref[h] = carry
        top = jnp.max(carry) if top is None else jnp.maximum(top, jnp.max(carry))
    return (top > SB_LOG_CUT).astype(jnp.int32)


def _sb_kernel(tk, n_past_static, stacked, q_ref, kd_ref, vd_ref, kp_ref, vp_ref, o_ref, carry_ref, acc_ref):
    n_past = pl.program_id(2) if n_past_static is None else n_past_static
    q = q_ref[...] * SCALE
    tq = q.shape[0]
    rq = lax.broadcasted_iota(jnp.int32, (tq, tq), 0)
    ck = lax.broadcasted_iota(jnp.int32, (tq, tq), 1)
    causal = ck < rq
    if stacked:
        qh = [_stack_heads(q, HEAD_DIM)]
        causal = jnp.concatenate([causal] * (q.shape[1] // HEAD_DIM), axis=0)
    else:
        qh = [_half_lanes(q, False), _half_lanes(q, True)]
    kd, vd = [kd_ref[...].astype(BF16)], [vd_ref[...].astype(BF16)]
    live = _sb_fold(qh, kd, vd, not stacked, _strict_upper(tq), causal, carry_ref, acc_ref, True)

    def cond(st):
        return jnp.logical_and(st[0] >= 0, st[1] > 0)

    def body(st):
        start = pl.multiple_of(st[0] * tk, tk)
        kp, vp = [kp_ref[:, pl.ds(start, tk)].astype(BF16)], [vp_ref[:, pl.ds(start, tk)].astype(BF16)]
        live = _sb_fold(qh, kp, vp, True, _strict_upper(tk), None, carry_ref, acc_ref, False)
        return st[0] - 1, live

    lax.while_loop(cond, body, (n_past - 1, live))
    if stacked:
        o_ref[...] = _unstack_heads(acc_ref[0], tq, HEAD_DIM).astype(o_ref.dtype)
    else:
        lane = lax.broadcasted_iota(jnp.int32, q.shape, 1)
        o_ref[...] = jnp.where(lane < HEAD_DIM, acc_ref[0], acc_ref[1]).astype(o_ref.dtype)


def _sb_attn_prompt(pn4, layer, k_t, v_t):
    _, bn, s, _ = pn4.shape
    blk = (None, ATT_BLK, LANES)
    own = pl.BlockSpec((None, None, LANES, ATT_BLK), lambda b, hp, qi: (layer, b, hp, qi))
    older = pl.BlockSpec((None, None, LANES, s), lambda b, hp, qi: (layer, b, hp, 0))
    return pl.pallas_call(
        functools.partial(_sb_kernel, ATT_BLK, None, False),
        grid=(bn, B_HEADS // 2, s // ATT_BLK),
        in_specs=[_pspec(blk, lambda b, hp, qi: (b, qi, _CBP["qb"] + hp), layer), own, own, older, older],
        out_specs=pl.BlockSpec(blk, lambda b, hp, qi: (b, qi, hp)),
        out_shape=jax.ShapeDtypeStruct((bn, s, BRANCH_W), BF16),
        scratch_shapes=[pltpu.VMEM((2, ATT_BLK, LANES), F32), pltpu.VMEM((2, ATT_BLK, LANES), F32)],
        compiler_params=_cparams("parallel", "parallel", "arbitrary"),
        name="stick_attn_prompt",
    )(pn4, k_t, v_t, k_t, v_t)


def _sb_attn_sample(p4, layer, k_cache_t, v_cache_t):
    _, bn, t, _ = p4.shape
    past = k_cache_t.shape[-1]
    new = lambda name: _pspec((None, t, BRANCH_W), lambda b: (b, 0, _CB[name] * LANES // BRANCH_W), layer)
    older = pl.BlockSpec((None, None, BRANCH_W, past), lambda b: (layer, b, 0, 0))
    rows = B_HEADS * t
    return pl.pallas_call(
        functools.partial(_sb_kernel, ATT_BLK, past // ATT_BLK, True),
        grid=(bn,),
        in_specs=[new("qb"), new("kb"), new("vb"), older, older],
        out_specs=pl.BlockSpec((None, t, BRANCH_W), lambda b: (b, 0, 0)),
        out_shape=jax.ShapeDtypeStruct((bn, t, BRANCH_W), BF16),
        scratch_shapes=[pltpu.VMEM((1, rows, LANES), F32), pltpu.VMEM((1, rows, BRANCH_W), F32)],
        compiler_params=_cparams("parallel"),
        name="stick_attn_sample",
    )(p4, p4, p4, k_cache_t, v_cache_t)


def _diff_combine(o0, o1, lam_init, lam_ref, gain_ref):
    lp = lam_ref[...]
    lam = (jnp.exp(jnp.sum(lp[0:1] * lp[1:2], axis=1, keepdims=True))
           - jnp.exp(jnp.sum(lp[2:3] * lp[3:4], axis=1, keepdims=True)) + lam_init)
    return _rms(o0 - lam * o1, gain_ref[...]) * (1.0 - lam_init)


def _diff_fold(qm, k_t, v, bias, m_ref, l_ref, acc_ref, first):
    kb = k_t.astype(BF16)
    vb = v.astype(BF16)
    tq, tk = qm[0].shape[0], vb.shape[0]
    for m in range(2):
        s = _dot(qm[m], kb) + bias
        m_cur = jnp.max(s, axis=1, keepdims=True)
        m_new = jnp.broadcast_to(m_cur, (tq, LANES)) if first else jnp.maximum(m_ref[m], m_cur)
        p = jnp.exp2(s - _lane_tile(m_new, tk // LANES))
        row = jnp.sum(p, axis=1, keepdims=True)
        pv = _dot(p.astype(BF16), vb)
        if first:
            l_ref[m] = jnp.broadcast_to(row, (tq, LANES))
            acc_ref[m] = pv
        else:
            alpha = jnp.exp2(m_ref[m] - m_new)
            l_ref[m] = alpha * l_ref[m] + row
            acc_ref[m] = alpha * acc_ref[m] + pv
        m_ref[m] = m_new


def _diff_prompt_kernel(lam_init, tbl_ref, lam_ref, gain_ref, q_ref, kd_ref, vd_ref, kp_ref, vp_ref,
                        bd_ref, bp_ref, o_ref, m_ref, l_ref, acc_ref):
    h = pl.program_id(1)
    qi = pl.program_id(2)
    q = q_ref[...] * SOFTMAX_QSCALE
    blk = q.shape[0]
    qm = [_half_lanes(q, False), _half_lanes(q, True)]
    far = tbl_ref[T5_FAR_BUCKET * C_HEADS + h] * LOG2E

    near = pl.multiple_of(jnp.maximum(qi - 1, 0) * blk, blk)
    k0 = jnp.concatenate([kp_ref[:, pl.ds(near, blk)], kd_ref[...]], axis=1)
    v0 = jnp.concatenate([vp_ref[pl.ds(near, blk), :], vd_ref[...]], axis=0)
    b0 = jnp.concatenate([jnp.where(qi >= 1, bp_ref[...], NEG_INF), bd_ref[...]], axis=1)
    _diff_fold(qm, k0, v0, b0, m_ref, l_ref, acc_ref, True)

    n_far = jnp.maximum(qi - 1, 0)
    n_group = n_far // DIFF_FAR_GROUP
    wide = DIFF_FAR_GROUP * blk

    def body(j, c):
        start = pl.multiple_of(j * wide, wide)
        _diff_fold(qm, kp_ref[:, pl.ds(start, wide)], vp_ref[pl.ds(start, wide), :], far,
                   m_ref, l_ref, acc_ref, False)
        return c

    lax.fori_loop(0, n_group, body, 0)
    rest = pl.multiple_of(n_group * wide, wide)
    for r in range(1, DIFF_FAR_GROUP):
        @pl.when(n_far - n_group * DIFF_FAR_GROUP == r)
        def _(r=r):
            _diff_fold(qm, kp_ref[:, pl.ds(rest, r * blk)], vp_ref[pl.ds(rest, r * blk), :], far,
                       m_ref, l_ref, acc_ref, False)

    o = _diff_combine(acc_ref[0] / l_ref[0], acc_ref[1] / l_ref[1], lam_init, lam_ref, gain_ref)
    o_ref[...] = o.astype(o_ref.dtype)


def _diff_sample_kernel(lam_init, lam_ref, gain_ref, q_ref, kc_ref, vc_ref, kn_ref, vn_ref,
                        bc_ref, bn_ref, o_ref):
    q_all = q_ref[...] * SOFTMAX_QSCALE
    past = kc_ref.shape[1]
    k_t, v_t = (True, False), (False, False)
    for h in range(C_HEADS):
        cols = slice(h * LANES, (h + 1) * LANES)
        q = q_all[:, cols]
        kb = [kc_ref[cols, :].astype(BF16), kn_ref[:, cols].astype(BF16)]
        vb = [vc_ref[pl.ds(h, past, stride=C_HEADS), :].astype(BF16), vn_ref[:, cols].astype(BF16)]
        biases = [bc_ref[h], bn_ref[h]]
        o0 = _softmax_segments(_half_lanes(q, False), kb, vb, biases, k_t, v_t)
        o1 = _softmax_segments(_half_lanes(q, True), kb, vb, biases, k_t, v_t)
        o_ref[:, cols] = _diff_combine(o0, o1, lam_init, lam_ref, gain_ref).astype(o_ref.dtype)


def _diff_attn_prompt(pn4, layer, k_t, t5tiles, lam_init, t5_flat, lam_params, gain):
    _, bn, s, _ = pn4.shape
    blk = (None, ATT_BLK, LANES)
    full = (None, s, LANES)
    state = pltpu.VMEM((2, ATT_BLK, LANES), F32)
    return pl.pallas_call(
        functools.partial(_diff_prompt_kernel, lam_init),
        grid=(bn, C_HEADS, s // ATT_BLK),
        in_specs=[pl.BlockSpec(memory_space=pltpu.SMEM),
                  pl.BlockSpec((4, HEAD_DIM), lambda b, h, qi: (0, 0)),
                  pl.BlockSpec((1, C_VDIM), lambda b, h, qi: (0, 0)),
                  _pspec(blk, lambda b, h, qi: (b, qi, _CBP["qc"] + h), layer),
                  pl.BlockSpec((None, None, LANES, ATT_BLK), lambda b, h, qi: (layer, b, h, qi)),
                  _pspec(blk, lambda b, h, qi: (b, qi, _CBP["vc"] + h), layer),
                  pl.BlockSpec((None, None, LANES, s), lambda b, h, qi: (layer, b, h, 0)),
                  _pspec(full, lambda b, h, qi: (b, 0, _CBP["vc"] + h), layer),
                  pl.BlockSpec((None, None, ATT_BLK, ATT_BLK), lambda b, h, qi: (h, 0, 0, 0)),
                  pl.BlockSpec((None, None, ATT_BLK, ATT_BLK), lambda b, h, qi: (h, 1, 0, 0))],
        out_specs=pl.BlockSpec(blk, lambda b, h, qi: (b, qi, h)),
        out_shape=jax.ShapeDtypeStruct((bn, s, BRANCH_W), BF16),
        scratch_shapes=[state, state, state],
        compiler_params=_cparams("parallel", "parallel", "arbitrary"),
        name="diff_attn_prompt",
    )(t5_flat, lam_params, gain.reshape(1, C_VDIM), pn4, k_t, pn4, k_t, pn4, t5tiles, t5tiles)


def _diff_attn_sample(p4, layer, k_cache_t, v_cache, bias_cache, bias_new, lam_init, lam_params, gain):
    _, bn, t, _ = p4.shape
    past = k_cache_t.shape[-1]
    new = lambda name: _pspec((None, t, BRANCH_W), lambda b: (b, 0, _CB[name] * LANES // BRANCH_W), layer)
    return pl.pallas_call(
        functools.partial(_diff_sample_kernel, lam_init),
        grid=(bn,),
        in_specs=[pl.BlockSpec((4, HEAD_DIM), lambda b: (0, 0)),
                  pl.BlockSpec((1, C_VDIM), lambda b: (0, 0)),
                  new("qc"),
                  pl.BlockSpec((None, None, BRANCH_W, past), lambda b: (layer, b, 0, 0)),
                  pl.BlockSpec((None, None, past * C_HEADS, C_VDIM), lambda b: (layer, b, 0, 0)),
                  new("kc"), new("vc"),
                  pl.BlockSpec((C_HEADS, t, past), lambda b: (0, 0, 0)),
                  pl.BlockSpec((C_HEADS, t, t), lambda b: (0, 0, 0))],
        out_specs=pl.BlockSpec((None, t, BRANCH_W), lambda b: (b, 0, 0)),
        out_shape=jax.ShapeDtypeStruct((bn, t, BRANCH_W), BF16),
        compiler_params=_cparams("parallel"),
        name="diff_attn_sample",
    )(lam_params, gain.reshape(1, C_VDIM), p4, k_cache_t, v_cache, p4, p4, bias_cache, bias_new)


def _merge_kernel(x_ref, oa_ref, ob_ref, oc_ref, g0_ref, g1_ref, g2_ref, bg_ref, wb_ref, wo_ref, o_ref):
    h = None
    for n, (o_r, g_r) in enumerate(((oa_ref, g0_ref), (ob_ref, g1_ref), (oc_ref, g2_ref))):
        gate = 1.0 / (1.0 + jnp.exp(-(g_r[...] + bg_ref[n:n + 1, :])))
        t = gate * _dot(o_r[...], wb_ref[n])
        h = t if h is None else h + t
    o_ref[...] = x_ref[...] + _dot(h.astype(BF16), wo_ref[...])


def _merge(x, o_a, o_b, o_c, p, layer, b_gate, w_branch, w_out, tm):
    t, d = x.shape
    row = lambda i: (i, 0)
    fixed2 = lambda i: (0, 0)
    return pl.pallas_call(
        _merge_kernel,
        grid=(t // tm,),
        in_specs=[pl.BlockSpec((tm, d), row),
                  pl.BlockSpec((tm, BRANCH_W), row),
                  pl.BlockSpec((tm, BRANCH_W), row),
                  pl.BlockSpec((tm, BRANCH_W), row),
                  pl.BlockSpec((None, tm, d), lambda i: (layer, i, 0)),
                  pl.BlockSpec((None, tm, d), lambda i: (layer, i, 1)),
                  pl.BlockSpec((None, tm, d), lambda i: (layer, i, 2)),
                  pl.BlockSpec((N_BRANCH, d), fixed2),
                  pl.BlockSpec((N_BRANCH, BRANCH_W, d), lambda i: (0, 0, 0)),
                  pl.BlockSpec((d, d), fixed2)],
        out_specs=pl.BlockSpec((tm, d), row),
        out_shape=jax.ShapeDtypeStruct((t, d), F32),
        compiler_params=_cparams("parallel"),
        name="gated_merge",
    )(x, o_a, o_b, o_c, p, p, p, b_gate, w_branch, w_out)


FFN_TF = D_FF // 2
FFN_PREV = 16


def _gelu_tanh(x):
    return x * (0.5 * (1.0 + jnp.tanh(math.sqrt(2.0 / math.pi) * (x + 0.044715 * (x * x * x)))))


def _ffn_kernel(tiles_per_seq, x_ref, xp_ref, gn_ref, wg_ref, wu_ref, cw_ref, cb_ref, wd_ref, st_ref,
                o_ref, so_ref, xn_ref, xpn_ref):
    i = pl.program_id(0)
    j = pl.program_id(1)
    whole_seqs = tiles_per_seq == 0

    @pl.when(j == 0)
    def _():
        xn_ref[...] = _rms(x_ref[...], gn_ref[...]).astype(BF16)
        if not whole_seqs:
            xpn_ref[...] = _rms(xp_ref[...], gn_ref[...]).astype(BF16)

    xn = xn_ref[...]
    tm = xn.shape[0]
    tf = wg_ref.shape[1]
    g = _dot(xn, wg_ref[...])
    u = _dot(xn, wu_ref[...])
    if whole_seqs:
        ns = st_ref.shape[0]
        st = st_ref[...]
        g = g.reshape(ns, tm // ns, tf)
        u = u.reshape(ns, tm // ns, tf)
        pm2, pm1 = st[:, 0:1, :], st[:, 1:2, :]
    else:
        gp = _dot(xpn_ref[...], wg_ref[...])
        first = (i % tiles_per_seq) == 0
        st = st_ref[0]
        pm2 = jnp.where(first, st[0:1, :], gp[FFN_PREV - 2:FFN_PREV - 1, :])
        pm1 = jnp.where(first, st[1:2, :], gp[FFN_PREV - 1:FFN_PREV, :])
    ax = g.ndim - 2
    row = lax.broadcasted_iota(jnp.int32, g.shape, ax)
    g1 = jnp.where(row == 0, pm1, pltpu.roll(g, 1, ax))
    g2 = jnp.where(row == 0, pm2, jnp.where(row == 1, pm1, pltpu.roll(g, 2, ax)))
    cw = cw_ref[...]
    c = cb_ref[...] + cw[0:1, :] * g2 + cw[1:2, :] * g1 + cw[2:3, :] * g
    hid = (_gelu_tanh(c) * u).reshape(tm, tf).astype(BF16)
    contrib = _dot(hid, wd_ref[...])

    @pl.when(j == 0)
    def _():
        o_ref[...] = x_ref[...] + contrib

    @pl.when(j > 0)
    def _():
        o_ref[...] += contrib

    seq_len = g.shape[ax]
    if whole_seqs:
        so_ref[...] = g[:, seq_len - 2:seq_len, :]
    else:
        so_ref[0] = g[seq_len - 2:seq_len, :]


def _conv_ffn(x, g_norm, w_gate, w_up, conv_w, conv_b, w_down, state, seq_len, tm):
    t, d = x.shape
    nseq = t // seq_len
    nf = D_FF // FFN_TF
    if seq_len >= tm:
        tiles_per_seq = seq_len // tm
        st_spec = pl.BlockSpec((1, 2, FFN_TF), lambda i, j: (i // tiles_per_seq, 0, j))
    else:
        tiles_per_seq = 0
        st_spec = pl.BlockSpec((tm // seq_len, 2, FFN_TF), lambda i, j: (i, 0, j))
    prev_blocks = tm // FFN_PREV
    tail_spec = pl.BlockSpec(st_spec.block_shape, lambda i, j: (i, 0, j))
    n_tail = t // tm * st_spec.block_shape[0]
    y, tails = pl.pallas_call(
        functools.partial(_ffn_kernel, tiles_per_seq),
        grid=(t // tm, nf),
        in_specs=[pl.BlockSpec((tm, d), lambda i, j: (i, 0)),
                  pl.BlockSpec((FFN_PREV, d), lambda i, j: (jnp.maximum(i * prev_blocks - 1, 0), 0)),
                  pl.BlockSpec((1, d), lambda i, j: (0, 0)),
                  pl.BlockSpec((d, FFN_TF), lambda i, j: (0, j)),
                  pl.BlockSpec((d, FFN_TF), lambda i, j: (0, j)),
                  pl.BlockSpec((3, FFN_TF), lambda i, j: (0, j)),
                  pl.BlockSpec((1, FFN_TF), lambda i, j: (0, j)),
                  pl.BlockSpec((FFN_TF, d), lambda i, j: (j, 0)),
                  st_spec],
        out_specs=[pl.BlockSpec((tm, d), lambda i, j: (i, 0)), tail_spec],
        out_shape=[jax.ShapeDtypeStruct((t, d), F32),
                   jax.ShapeDtypeStruct((n_tail, 2, D_FF), F32)],
        scratch_shapes=[pltpu.VMEM((tm, d), BF16), pltpu.VMEM((FFN_PREV, d), BF16)],
        compiler_params=_cparams("arbitrary", "arbitrary"),
        name="conv_ffn_prompt" if tiles_per_seq else "conv_ffn_sample",
    )(x, x, g_norm.reshape(1, d), w_gate, w_up, conv_w, conv_b.reshape(1, D_FF), w_down, state)
    return y, tails[n_tail // nseq - 1::n_tail // nseq]


def _final_norm_kernel(x_ref, g_ref, o_ref):
    o_ref[...] = _rms(x_ref[...], g_ref[...])


def _final_norm(x, g, tm):
    t, d = x.shape
    return pl.pallas_call(
        _final_norm_kernel,
        grid=(t // tm,),
        in_specs=[pl.BlockSpec((tm, d), lambda i: (i, 0)), pl.BlockSpec((1, d), lambda i: (0, 0))],
        out_specs=pl.BlockSpec((tm, d), lambda i: (i, 0)),
        out_shape=jax.ShapeDtypeStruct((t, d), F32),
        compiler_params=_cparams("parallel"),
        name="final_norm",
    )(x, g.reshape(1, d))


PROJ_TM, PROJ_TN = 1024, 1280
ROW_TM = 512


def _lam_init(layer):
    return 0.8 - 0.6 * math.exp(-0.3 * layer)


def _cols(p4, name):
    c0 = _CB[name] * LANES
    return p4[..., c0:c0 + BRANCH_W]


def _layer_tail(x2, o_a, o_b, o_c, p_rows, layer, prm, conv_state, seq_len):
    flat = lambda a: a.reshape(x2.shape[0], BRANCH_W)
    x2 = _merge(x2, flat(o_a), flat(o_b), flat(o_c), p_rows, layer, prm["b_gate"][layer],
                prm["w_branch"][layer], prm["w_out"][layer], ROW_TM)
    return _conv_ffn(x2, prm["norm_ffn"][layer], prm["w_gate"][layer], prm["w_up"][layer],
                     prm["conv_w"][layer], prm["conv_b"][layer], prm["w_down"][layer], conv_state, seq_len, ROW_TM)


def _trunk_prompt(x, prm, abias, t5tiles, t5_flat):
    bn, s, d = x.shape
    x2 = x.reshape(bn * s, d)
    depth = prm["w_rows"].shape[0]
    conv_states = []
    rows, kv_t = None, None
    for l in range(depth):
        rows = _norm_proj(x2, prm["norm_mix"][l], prm["w_rows"][l], PROJ_TM, PROJ_TN, l, depth, rows,
                          heads_group=(_CBP["vc"] * LANES, C_HEADS))
        p_rows, vc_heads = rows
        kv_t = _norm_proj_t(x2, prm["norm_mix"][l], prm["w_feat_t"][l], PROJ_TM, l, depth, bn, kv_t)
        ka_t, va_t, kb_t, vb_t, kc_t = kv_t
        pn4 = p_rows.reshape(depth, bn, s, ROW_COLS)
        o_a = _band_attn_prompt(pn4, l, ka_t, va_t, abias[l])
        o_b = _sb_attn_prompt(pn4, l, kb_t, vb_t)
        o_c = _diff_attn_prompt(pn4, l, kc_t, t5tiles, _lam_init(l), t5_flat, prm["c_lambda"][l], prm["c_subln"][l])
        x2, st_conv = _layer_tail(x2, o_a, o_b, o_c, p_rows, l, prm, jnp.zeros((bn, 2, D_FF), F32), s)
        conv_states.append(st_conv)
    y = _final_norm(x2, prm["norm_final"], ROW_TM).reshape(bn, s, d)
    heads_t = lambda a: jnp.transpose(a.reshape(depth, bn, A_HEADS, HEAD_DIM, a.shape[-1]), (0, 1, 4, 2, 3))
    keep = min(A_PAST_CHUNKS * CHUNK, s)
    states = (heads_t(ka_t[..., s - keep:]), heads_t(va_t[..., s - keep:]), heads_t(kb_t), heads_t(vb_t),
              jnp.transpose(kc_t.reshape(depth, bn, C_HEADS, 2, HEAD_DIM, s), (0, 1, 5, 2, 3, 4)),
              vc_heads.reshape(depth, bn, s, C_HEADS, C_VDIM),
              jnp.stack(conv_states, axis=0))
    return y, states


def _feature_major(cache):
    nd = cache.ndim
    t = jnp.transpose(cache, (0, 1) + tuple(range(3, nd)) + (2,))
    return t.reshape(cache.shape[0], cache.shape[1], BRANCH_W, cache.shape[2])


def _trunk_sample(x, caches, prm, abias, t5tiles):
    bn, t, d = x.shape
    x2 = x.reshape(bn * t, d)
    depth = prm["w_in"].shape[0]
    ca_k, ca_v, cb_k, cb_v, cc_k, cc_v, st_conv_in = caches
    ca_kt, ca_vt, cb_kt, cb_vt, cc_kt = (_feature_major(c) for c in (ca_k, ca_v, cb_k, cb_v, cc_k))
    keep, past = ca_k.shape[2], cc_k.shape[2]
    cc_vr = cc_v.reshape(depth, bn, past * C_HEADS, C_VDIM)
    conv_states = []
    p_all = None
    for l in range(depth):
        p_all, = _norm_proj(x2, prm["norm_mix"][l], prm["w_in"][l], PROJ_TM, PROJ_TN, l, depth,
                            None if p_all is None else [p_all])
        p4 = p_all.reshape(depth, bn, t, IN_COLS)
        bias_cache = jnp.concatenate([abias[l][:, 0, :t, :], abias[l][:, 1, :t, :]], axis=-1)[..., -keep:]
        o_a = _band_attn_sample(p4, l, ca_kt, ca_vt, bias_cache, abias[l][:, 2, :t, :t])
        o_b = _sb_attn_sample(p4, l, cb_kt, cb_vt)
        far = jnp.broadcast_to(t5tiles[:, 1, :1, :1], (C_HEADS, t, past - ATT_BLK))
        bias_c = jnp.concatenate([far, t5tiles[:, 1, :t, :]], axis=-1)
        o_c = _diff_attn_sample(p4, l, cc_kt, cc_vr, bias_c, t5tiles[:, 0, :t, :t],
                                _lam_init(l), prm["c_lambda"][l], prm["c_subln"][l])
        x2, st_conv = _layer_tail(x2, o_a, o_b, o_c, p_all, l, prm, st_conv_in[l], t)
        conv_states.append(st_conv)
    y = _final_norm(x2, prm["norm_final"], ROW_TM).reshape(bn, t, d)
    ka, va, kb, vb, kc, vc = (_cols(p4, n) for n in ("ka", "va", "kb", "vb", "kc", "vc"))
    heads = lambda a: a.reshape(a.shape[:3] + (A_HEADS, HEAD_DIM))
    states = (jnp.concatenate([ca_k[:, :, t:], heads(ka)], axis=2),
              jnp.concatenate([ca_v[:, :, t:], heads(va)], axis=2), heads(kb), heads(vb),
              kc.reshape(depth, bn, t, C_HEADS, 2, HEAD_DIM), vc.reshape(depth, bn, t, C_HEADS, C_VDIM),
              jnp.stack(conv_states, axis=0))
    return y, states


def kernel(x_prompt, x_sample, cache_a_k, cache_a_v, cache_b_k, cache_b_v, cache_c_k, cache_c_v,
           state_ffn_conv, norm_mix, w_in, b_gate, a_rel_bias, t5_bias, c_lambda, c_subln,
           w_branch, w_out, norm_ffn, w_up, conv_w, conv_b, w_down, norm_final):
    group = {name: w_in[..., i * BRANCH_W:(i + 1) * BRANCH_W] for i, name in enumerate(_GROUPS)}
    gate = w_in[..., len(_GROUPS) * BRANCH_W:]
    prm = dict(
        norm_mix=norm_mix,
        w_in=jnp.concatenate([gate] + [group[n] for n in _GROUPS], axis=-1).astype(BF16),
        w_rows=jnp.concatenate([gate] + [group[n] for n in _ROW_GROUPS], axis=-1).astype(BF16),
        w_feat_t=jnp.swapaxes(jnp.concatenate([group[n] for n in _T_GROUPS], axis=-1), 1, 2).astype(BF16),
        b_gate=b_gate, c_lambda=c_lambda, c_subln=c_subln,
        w_branch=w_branch.astype(BF16), w_out=w_out.astype(BF16), norm_ffn=norm_ffn,
        w_gate=w_up[..., :D_FF].astype(BF16), w_up=w_up[..., D_FF:].astype(BF16),
        conv_w=conv_w, conv_b=conv_b, w_down=w_down.astype(BF16), norm_final=norm_final)
    abias = [_build_abias(a_rel_bias[l]) for l in range(a_rel_bias.shape[0])]
    t5tiles = _build_t5bias(t5_bias)
    y_p, p_states = _trunk_prompt(x_prompt, prm, abias, t5tiles, t5_bias.reshape(-1))
    caches = (cache_a_k, cache_a_v, cache_b_k, cache_b_v, cache_c_k, cache_c_v, state_ffn_conv)
    y_s, s_states = _trunk_sample(x_sample, caches, prm, abias, t5tiles)
    return (y_p, y_s) + p_states + s_states
```

```python
import functools
import math

import jax
import jax.numpy as jnp
from jax import lax
from jax.experimental import pallas as pl
from jax.experimental.pallas import tpu as pltpu

F32 = jnp.float32
BF16 = jnp.bfloat16

D_MODEL = 1024
CHUNK = 64
HEAD_DIM = 64
A_HEADS = 8
A_PAST_CHUNKS = 8
A_REL_CLIP = 128
B_HEADS = 8
C_HEADS = 4
C_VDIM = 128
T5_BUCKETS = 32
D_FF = 2816
N_BRANCH = 3
BRANCH_W = 512
GATE_W = N_BRANCH * D_MODEL
IN_COLS = 9 * BRANCH_W + GATE_W
EPS = 1e-6
NEG_INF = -1e30
SCALE = HEAD_DIM ** -0.5
LOG2E = math.log2(math.e)
SOFTMAX_QSCALE = SCALE * LOG2E

LANES = 128
_GROUPS = ("qa", "ka", "va", "qb", "kb", "vb", "qc", "kc", "vc")
_CB = {name: (GATE_W + i * BRANCH_W) // LANES for i, name in enumerate(_GROUPS)}
_ROW_GROUPS = ("qa", "qb", "qc", "vc")
_T_GROUPS = ("ka", "va", "kb", "vb", "kc")
_CBP = {name: (GATE_W + i * BRANCH_W) // LANES for i, name in enumerate(_ROW_GROUPS)}
ROW_COLS = GATE_W + len(_ROW_GROUPS) * BRANCH_W

ATT_BLK = 256
SB_LOG_CUT = -104.0
T5_FAR_BUCKET = 15
DIFF_FAR_GROUP = 4
V7X_VMEM_BYTES = 64 * 1024 * 1024
VMEM_LIMIT = V7X_VMEM_BYTES - 8 * 1024 * 1024


def _cparams(*sem):
    return pltpu.CompilerParams(dimension_semantics=sem, vmem_limit_bytes=VMEM_LIMIT)


def _rms(x, g):
    return x * lax.rsqrt(jnp.mean(x * x, axis=-1, keepdims=True) + EPS) * g


def _dot(a, b):
    return jnp.dot(a, b, preferred_element_type=F32)


def _dot_nt(a, b):
    return lax.dot_general(a, b, (((1,), (1,)), ((), ())), preferred_element_type=F32)


def _norm_proj_kernel(n_stack, heads_cols, x_ref, g_ref, w_ref, *refs):
    outs, xn_ref = refs[n_stack:-1], refs[-1]
    j = pl.program_id(1)

    @pl.when(j == 0)
    def _():
        xn_ref[...] = _rms(x_ref[...], g_ref[...]).astype(BF16)

    res = _dot(xn_ref[...], w_ref[...])
    outs[0][...] = res
    if heads_cols is not None:
        tile, first, heads = heads_cols

        @pl.when(j == tile)
        def _():
            for h in range(heads):
                lo = first + h * LANES
                outs[1][pl.ds(h, res.shape[0], stride=heads), :] = res[:, lo:lo + LANES]


def _norm_proj(x, g, w, tm, tn, layer, depth, stacked, heads_group=None):
    t, d = x.shape
    n = w.shape[1]
    in_specs = [pl.BlockSpec((tm, d), lambda i, j: (i, 0)),
                pl.BlockSpec((1, d), lambda i, j: (0, 0)),
                pl.BlockSpec((d, tn), lambda i, j: (0, j))]
    args = [x, g.reshape(1, d), w]
    out_specs = [pl.BlockSpec((None, tm, tn), lambda i, j: (layer, i, j))]
    out_shape = [jax.ShapeDtypeStruct((depth, t, n), F32)]
    heads_cols = None
    if heads_group is not None:
        first, heads = heads_group
        heads_cols = (first // tn, first % tn, heads)
        out_specs.append(pl.BlockSpec((None, tm * heads, LANES), lambda i, j: (layer, i, 0)))
        out_shape.append(jax.ShapeDtypeStruct((depth, t * heads, LANES), F32))
    n_stack = 0 if stacked is None else len(out_shape)
    if stacked is not None:
        in_specs += [pl.BlockSpec(memory_space=pl.ANY)] * n_stack
        args += list(stacked)
    return pl.pallas_call(
        functools.partial(_norm_proj_kernel, n_stack, heads_cols),
        grid=(t // tm, n // tn),
        in_specs=in_specs,
        out_specs=out_specs,
        out_shape=out_shape,
        scratch_shapes=[pltpu.VMEM((tm, d), BF16)],
        input_output_aliases={3 + k: k for k in range(n_stack)},
        compiler_params=_cparams("arbitrary", "arbitrary"),
        name="norm_proj",
    )(*args)


def _norm_proj_t_kernel(ngroups, x_ref, g_ref, wt_ref, *refs):
    o_refs = refs[-ngroups:]
    xn = _rms(x_ref[...], g_ref[...]).astype(BF16)
    for g in range(ngroups):
        o_refs[g][...] = _dot_nt(wt_ref[g * BRANCH_W:(g + 1) * BRANCH_W, :], xn)


def _norm_proj_t(x, g, wt, tm, layer, depth, bn, stacked):
    t, d = x.shape
    s = t // bn
    ngroups = wt.shape[0] // BRANCH_W
    tiles = s // tm
    in_specs = [pl.BlockSpec((tm, d), lambda i: (i, 0)),
                pl.BlockSpec((1, d), lambda i: (0, 0)),
                pl.BlockSpec(wt.shape, lambda i: (0, 0))]
    args = [x, g.reshape(1, d), wt]
    aliases = {}
    if stacked is not None:
        in_specs += [pl.BlockSpec(memory_space=pl.ANY)] * ngroups
        args += list(stacked)
        aliases = {3 + n: n for n in range(ngroups)}
    out_spec = pl.BlockSpec((None, None, BRANCH_W, tm), lambda i: (layer, i // tiles, 0, i % tiles))
    return pl.pallas_call(
        functools.partial(_norm_proj_t_kernel, ngroups),
        grid=(t // tm,),
        in_specs=in_specs,
        out_specs=[out_spec] * ngroups,
        out_shape=[jax.ShapeDtypeStruct((depth, bn, BRANCH_W, s), F32)] * ngroups,
        input_output_aliases=aliases,
        compiler_params=_cparams("arbitrary"),
        name="norm_proj_t",
    )(*args)


def _abias_kernel(tbl_ref, o_ref):
    h = pl.program_id(0)
    sub = 8
    ql = lax.broadcasted_iota(jnp.int32, (ATT_BLK, ATT_BLK), 0)
    kl = lax.broadcasted_iota(jnp.int32, (ATT_BLK, ATT_BLK), 1)
    far = jnp.full((ATT_BLK, ATT_BLK), tbl_ref[2 * A_REL_CLIP * A_HEADS + h] * LOG2E, F32)
    o_ref[0, 0] = jnp.where((kl >> 6) >= (ql >> 6), far, NEG_INF)
    lead = ATT_BLK - sub
    b = lax.broadcasted_iota(jnp.int32, (sub, 2 * ATT_BLK), 0)
    x = lax.broadcasted_iota(jnp.int32, (sub, 2 * ATT_BLK), 1)
    for seg, off in ((1, ATT_BLK), (2, 0)):
        idx = jnp.clip(b + lead - x + off, -A_REL_CLIP, A_REL_CLIP) + A_REL_CLIP

        def body(r, acc, idx=idx):
            return jnp.where(idx == r, tbl_ref[r * A_HEADS + h] * LOG2E, acc)

        strip = lax.fori_loop(0, 2 * A_REL_CLIP + 1, body, jnp.zeros((sub, 2 * ATT_BLK), F32))
        for a in range(ATT_BLK // sub):
            o_ref[0, seg, a * sub:(a + 1) * sub, :] = strip[:, lead - a * sub:lead - a * sub + ATT_BLK]
    o_ref[0, 2] = jnp.where((kl >> 6) <= (ql >> 6), o_ref[0, 2], NEG_INF)


def _build_abias(table):
    return pl.pallas_call(
        _abias_kernel,
        grid=(A_HEADS,),
        in_specs=[pl.BlockSpec(memory_space=pltpu.SMEM)],
        out_specs=pl.BlockSpec((1, 3, ATT_BLK, ATT_BLK), lambda h: (h, 0, 0, 0)),
        out_shape=jax.ShapeDtypeStruct((A_HEADS, 3, ATT_BLK, ATT_BLK), F32),
        compiler_params=_cparams("parallel"),
        name="band_bias_tiles",
    )(table.reshape(-1))


_T5_LARGE_STEPS = (12, 16, 23, 32, 46, 64, 91)


def _t5bias_kernel(tbl_ref, o_ref):
    h = pl.program_id(0)
    rows = 64
    for seg, off in ((0, 0), (1, -ATT_BLK)):
        for rc in range(ATT_BLK // rows):
            ql = lax.broadcasted_iota(jnp.int32, (rows, ATT_BLK), 0) + rc * rows
            kl = lax.broadcasted_iota(jnp.int32, (rows, ATT_BLK), 1)
            rel = kl - ql + off
            n = jnp.abs(rel)
            large = jnp.full((rows, ATT_BLK), T5_BUCKETS // 4, jnp.int32)
            for th in _T5_LARGE_STEPS:
                large = large + jnp.where(n >= th, 1, 0)
            bucket = jnp.where(rel > 0, T5_BUCKETS // 2, 0) + jnp.where(n < T5_BUCKETS // 4, n, large)
            tile = jnp.zeros((rows, ATT_BLK), F32)
            for r in range(T5_BUCKETS):
                tile = jnp.where(bucket == r, tbl_ref[r * C_HEADS + h] * LOG2E, tile)
            if seg == 0:
                tile = jnp.where((kl >> 6) <= (ql >> 6), tile, NEG_INF)
            o_ref[0, seg, rc * rows:(rc + 1) * rows, :] = tile


def _build_t5bias(table):
    return pl.pallas_call(
        _t5bias_kernel,
        grid=(C_HEADS,),
        in_specs=[pl.BlockSpec(memory_space=pltpu.SMEM)],
        out_specs=pl.BlockSpec((1, 2, ATT_BLK, ATT_BLK), lambda h: (h, 0, 0, 0)),
        out_shape=jax.ShapeDtypeStruct((C_HEADS, 2, ATT_BLK, ATT_BLK), F32),
        compiler_params=_cparams("parallel"),
        name="t5_bias_tiles",
    )(table.reshape(-1))


def _pspec(block, index_map, layer):
    return pl.BlockSpec((None,) + block, lambda *g: (layer,) + tuple(index_map(*g)))


def _lane_tile(x, n):
    return x if n == 1 else jnp.concatenate([x] * n, axis=1)


def _half_lanes(q, upper):
    lane = lax.broadcasted_iota(jnp.int32, q.shape, 1)
    return jnp.where((lane >= HEAD_DIM) == upper, q, 0.0).astype(BF16)


def _stack_heads(q, width):
    group = lax.broadcasted_iota(jnp.int32, q.shape, 1) // width
    return jnp.concatenate([jnp.where(group == h, q, 0.0) for h in range(q.shape[1] // width)],
                           axis=0).astype(BF16)


def _unstack_heads(x, t, width):
    group = lax.broadcasted_iota(jnp.int32, (t, x.shape[1]), 1) // width
    out = jnp.zeros((t, x.shape[1]), x.dtype)
    for h in range(x.shape[1] // width):
        out = jnp.where(group == h, x[h * t:(h + 1) * t, :], out)
    return out


def _qk(q, k, k_t):
    return _dot(q, k) if k_t else _dot_nt(q, k)


def _pv(p, v, v_t):
    return _dot_nt(p, v) if v_t else _dot(p, v)


def _softmax_segments(qh, kb, vb, biases, k_t, v_t):
    s = [_qk(qh, k, t) + b for k, b, t in zip(kb, biases, k_t)]
    m = functools.reduce(jnp.maximum, [jnp.max(sj, axis=1, keepdims=True) for sj in s])
    p = [jnp.exp2(sj - m) for sj in s]
    l = functools.reduce(jnp.add, [jnp.sum(pj, axis=1, keepdims=True) for pj in p])
    acc = functools.reduce(jnp.add, [_pv(pj.astype(BF16), v, t) for pj, v, t in zip(p, vb, v_t)])
    return acc / l


def _band_kernel(nseg, min_qi, q_ref, *refs):
    k_refs, v_refs, b_refs = refs[:nseg], refs[nseg:2 * nseg], refs[2 * nseg:3 * nseg]
    o_ref = refs[3 * nseg]
    qi = pl.program_id(2)
    q = q_ref[...] * SOFTMAX_QSCALE
    kb = [k_refs[j][...].astype(BF16) for j in range(nseg)]
    vb = [v_refs[j][...].astype(BF16) for j in range(nseg)]
    outs = []
    for h in range(2):
        biases = [b_refs[j][h] if min_qi[j] == 0 else jnp.where(qi >= min_qi[j], b_refs[j][h], NEG_INF)
                  for j in range(nseg)]
        outs.append(_softmax_segments(_half_lanes(q, bool(h)), kb, vb, biases, (True,) * nseg, (True,) * nseg))
    lane = lax.broadcasted_iota(jnp.int32, q.shape, 1)
    o_ref[...] = jnp.where(lane < HEAD_DIM, outs[0], outs[1]).astype(o_ref.dtype)


def _band_attn_prompt(pn4, layer, k_t, v_t, abias):
    _, bn, s, _ = pn4.shape
    nq = s // ATT_BLK
    blk = (None, ATT_BLK, LANES)

    def kv_spec(back):
        return pl.BlockSpec((None, None, LANES, ATT_BLK),
                            lambda b, hp, qi: (layer, b, hp, jnp.maximum(qi - back, 0)))

    def bias_spec(seg):
        return pl.BlockSpec((2, None, ATT_BLK, ATT_BLK), lambda b, hp, qi: (hp, seg, 0, 0))

    return pl.pallas_call(
        functools.partial(_band_kernel, 3, (2, 1, 0)),
        grid=(bn, A_HEADS // 2, nq),
        in_specs=([_pspec(blk, lambda b, hp, qi: (b, qi, _CBP["qa"] + hp), layer)]
                  + [kv_spec(back) for back in (2, 1, 0)]
                  + [kv_spec(back) for back in (2, 1, 0)]
                  + [bias_spec(seg) for seg in range(3)]),
        out_specs=pl.BlockSpec(blk, lambda b, hp, qi: (b, qi, hp)),
        out_shape=jax.ShapeDtypeStruct((bn, s, BRANCH_W), BF16),
        compiler_params=_cparams("parallel", "parallel", "arbitrary"),
        name="band_attn_prompt",
    )(pn4, k_t, k_t, k_t, v_t, v_t, v_t, abias, abias, abias)


def _band_sample_kernel(q_ref, kc_ref, kn_ref, vc_ref, vn_ref, bc_ref, bn_ref, o_ref):
    t = q_ref.shape[0]
    qs = _stack_heads(q_ref[...] * SOFTMAX_QSCALE, HEAD_DIM)
    kb = [kc_ref[...].astype(BF16), kn_ref[...].astype(BF16)]
    vb = [vc_ref[...].astype(BF16), vn_ref[...].astype(BF16)]
    o = _softmax_segments(qs, kb, vb, [bc_ref[...], bn_ref[...]], (True, False), (True, False))
    o_ref[...] = _unstack_heads(o, t, HEAD_DIM).astype(o_ref.dtype)


def _band_attn_sample(p4, layer, k_cache_t, v_cache_t, bias_cache, bias_new):
    _, bn, t, _ = p4.shape
    keep = k_cache_t.shape[-1]
    new = lambda name: _pspec((None, t, BRANCH_W), lambda b: (b, 0, _CB[name] * LANES // BRANCH_W), layer)
    old = pl.BlockSpec((None, None, BRANCH_W, keep), lambda b: (layer, b, 0, 0))
    return pl.pallas_call(
        _band_sample_kernel,
        grid=(bn,),
        in_specs=[new("qa"), old, new("ka"), old, new("va"),
                  pl.BlockSpec((A_HEADS * t, keep), lambda b: (0, 0)),
                  pl.BlockSpec((A_HEADS * t, t), lambda b: (0, 0))],
        out_specs=pl.BlockSpec((None, t, BRANCH_W), lambda b: (b, 0, 0)),
        out_shape=jax.ShapeDtypeStruct((bn, t, BRANCH_W), BF16),
        compiler_params=_cparams("parallel"),
        name="band_attn_sample",
    )(p4, k_cache_t, p4, v_cache_t, p4, bias_cache.reshape(A_HEADS * t, keep), bias_new.reshape(A_HEADS * t, t))


def _strict_upper(n):
    r = lax.broadcasted_iota(jnp.int32, (n, n), 0)
    c = lax.broadcasted_iota(jnp.int32, (n, n), 1)
    return jnp.where(r > c, 1.0, 0.0).astype(BF16)


def _sb_fold(qh, kb, vb, kv_t, upper, mask, carry_ref, acc_ref, first):
    tq, tk = qh[0].shape[0], upper.shape[0]
    top = None
    for h in range(len(qh)):
        z = _qk(qh[h], kb[0], kv_t)
        soft = jnp.log(1.0 + jnp.exp(-jnp.abs(z)))
        log_keep = -(jnp.maximum(z, 0.0) + soft)
        log_take = jnp.minimum(z, 0.0) - soft
        if mask is not None:
            log_keep = jnp.where(mask, log_keep, 0.0)
        hi = log_keep.astype(BF16)
        lo = (log_keep - hi.astype(F32)).astype(BF16)
        after = _dot(hi, upper) + _dot(lo, upper)
        if not first:
            after = after + _lane_tile(carry_ref[h], tk // LANES)
        w = jnp.exp(log_take + after)
        if mask is not None:
            w = jnp.where(mask, w, 0.0)
        pv = _pv(w.astype(BF16), vb[0], kv_t)
        total = jnp.sum(log_keep, axis=1, keepdims=True)
        if first:
            acc_ref[h] = pv
            carry = jnp.broadcast_to(total, (tq, LANES))
        else:
            acc_ref[h] += pv
            carry = carry_ref[h] + total
        carry_ref[h] = carry
        top = jnp.max(carry) if top is None else jnp.maximum(top, jnp.max(carry))
    return (top > SB_LOG_CUT).astype(jnp.int32)


def _sb_kernel(tk, n_past_static, stacked, q_ref, kd_ref, vd_ref, kp_ref, vp_ref, o_ref, carry_ref, acc_ref):
    n_past = pl.program_id(2) if n_past_static is None else n_past_static
    q = q_ref[...] * SCALE
    tq = q.shape[0]
    rq = lax.broadcasted_iota(jnp.int32, (tq, tq), 0)
    ck = lax.broadcasted_iota(jnp.int32, (tq, tq), 1)
    causal = ck < rq
    if stacked:
        qh = [_stack_heads(q, HEAD_DIM)]
        causal = jnp.concatenate([causal] * (q.shape[1] // HEAD_DIM), axis=0)
    else:
        qh = [_half_lanes(q, False), _half_lanes(q, True)]
    if stacked:
        kd, vd = [kd_ref[...].astype(BF16)], [vd_ref[...].astype(BF16)]
        live = _sb_fold(qh, kd, vd, False, _strict_upper(tq), causal, carry_ref, acc_ref, True)
        n_left = n_past
    else:
        near = pl.multiple_of(jnp.maximum(n_past - 1, 0) * tq, tq)
        kd = [jnp.concatenate([kp_ref[:, pl.ds(near, tq)], kd_ref[...]], axis=1).astype(BF16)]
        vd = [jnp.concatenate([vp_ref[:, pl.ds(near, tq)], vd_ref[...]], axis=1).astype(BF16)]
        mask = jnp.concatenate([jnp.broadcast_to(n_past >= 1, (tq, tq)), causal], axis=1)
        live = _sb_fold(qh, kd, vd, True, _strict_upper(2 * tq), mask, carry_ref, acc_ref, True)
        n_left = n_past - 1

    def cond(st):
        return jnp.logical_and(st[0] >= 0, st[1] > 0)

    def body(st):
        start = pl.multiple_of(st[0] * tk, tk)
        kp, vp = [kp_ref[:, pl.ds(start, tk)].astype(BF16)], [vp_ref[:, pl.ds(start, tk)].astype(BF16)]
        live = _sb_fold(qh, kp, vp, True, _strict_upper(tk), None, carry_ref, acc_ref, False)
        return st[0] - 1, live

    lax.while_loop(cond, body, (n_left - 1, live))
    if stacked:
        o_ref[...] = _unstack_heads(acc_ref[0], tq, HEAD_DIM).astype(o_ref.dtype)
    else:
        lane = lax.broadcasted_iota(jnp.int32, q.shape, 1)
        o_ref[...] = jnp.where(lane < HEAD_DIM, acc_ref[0], acc_ref[1]).astype(o_ref.dtype)


def _sb_attn_prompt(pn4, layer, k_t, v_t):
    _, bn, s, _ = pn4.shape
    blk = (None, ATT_BLK, LANES)
    own = pl.BlockSpec((None, None, LANES, ATT_BLK), lambda b, hp, qi: (layer, b, hp, qi))
    older = pl.BlockSpec((None, None, LANES, s), lambda b, hp, qi: (layer, b, hp, 0))
    return pl.pallas_call(
        functools.partial(_sb_kernel, ATT_BLK, None, False),
        grid=(bn, B_HEADS // 2, s // ATT_BLK),
        in_specs=[_pspec(blk, lambda b, hp, qi: (b, qi, _CBP["qb"] + hp), layer), own, own, older, older],
        out_specs=pl.BlockSpec(blk, lambda b, hp, qi: (b, qi, hp)),
        out_shape=jax.ShapeDtypeStruct((bn, s, BRANCH_W), BF16),
        scratch_shapes=[pltpu.VMEM((2, ATT_BLK, LANES), F32), pltpu.VMEM((2, ATT_BLK, LANES), F32)],
        compiler_params=_cparams("parallel", "parallel", "arbitrary"),
        name="stick_attn_prompt",
    )(pn4, k_t, v_t, k_t, v_t)


def _sb_attn_sample(p4, layer, k_cache_t, v_cache_t):
    _, bn, t, _ = p4.shape
    past = k_cache_t.shape[-1]
    new = lambda name: _pspec((None, t, BRANCH_W), lambda b: (b, 0, _CB[name] * LANES // BRANCH_W), layer)
    older = pl.BlockSpec((None, None, BRANCH_W, past), lambda b: (layer, b, 0, 0))
    rows = B_HEADS * t
    return pl.pallas_call(
        functools.partial(_sb_kernel, ATT_BLK, past // ATT_BLK, True),
        grid=(bn,),
        in_specs=[new("qb"), new("kb"), new("vb"), older, older],
        out_specs=pl.BlockSpec((None, t, BRANCH_W), lambda b: (b, 0, 0)),
        out_shape=jax.ShapeDtypeStruct((bn, t, BRANCH_W), BF16),
        scratch_shapes=[pltpu.VMEM((1, rows, LANES), F32), pltpu.VMEM((1, rows, BRANCH_W), F32)],
        compiler_params=_cparams("parallel"),
        name="stick_attn_sample",
    )(p4, p4, p4, k_cache_t, v_cache_t)


def _diff_combine(o0, o1, lam_init, lam_ref, gain_ref):
    lp = lam_ref[...]
    lam = (jnp.exp(jnp.sum(lp[0:1] * lp[1:2], axis=1, keepdims=True))
           - jnp.exp(jnp.sum(lp[2:3] * lp[3:4], axis=1, keepdims=True)) + lam_init)
    return _rms(o0 - lam * o1, gain_ref[...]) * (1.0 - lam_init)


def _diff_fold(qm, k_t, v, bias, m_ref, l_ref, acc_ref, first):
    kb = k_t.astype(BF16)
    vb = v.astype(BF16)
    tq, tk = qm[0].shape[0], vb.shape[0]
    for m in range(2):
        s = _dot(qm[m], kb) + bias
        m_cur = jnp.max(s, axis=1, keepdims=True)
        m_new = jnp.broadcast_to(m_cur, (tq, LANES)) if first else jnp.maximum(m_ref[m], m_cur)
        p = jnp.exp2(s - _lane_tile(m_new, tk // LANES))
        row = jnp.sum(p, axis=1, keepdims=True)
        pv = _dot(p.astype(BF16), vb)
        if first:
            l_ref[m] = jnp.broadcast_to(row, (tq, LANES))
            acc_ref[m] = pv
        else:
            alpha = jnp.exp2(m_ref[m] - m_new)
            l_ref[m] = alpha * l_ref[m] + row
            acc_ref[m] = alpha * acc_ref[m] + pv
        m_ref[m] = m_new


def _diff_prompt_kernel(lam_init, tbl_ref, lam_ref, gain_ref, q_ref, kd_ref, vd_ref, kp_ref, vp_ref,
                        bd_ref, bp_ref, o_ref, m_ref, l_ref, acc_ref):
    h = pl.program_id(1)
    qi = pl.program_id(2)
    q = q_ref[...] * SOFTMAX_QSCALE
    blk = q.shape[0]
    qm = [_half_lanes(q, False), _half_lanes(q, True)]
    far = tbl_ref[T5_FAR_BUCKET * C_HEADS + h] * LOG2E

    near = pl.multiple_of(jnp.maximum(qi - 1, 0) * blk, blk)
    k0 = jnp.concatenate([kp_ref[:, pl.ds(near, blk)], kd_ref[...]], axis=1)
    v0 = jnp.concatenate([vp_ref[pl.ds(near, blk), :], vd_ref[...]], axis=0)
    b0 = jnp.concatenate([jnp.where(qi >= 1, bp_ref[...], NEG_INF), bd_ref[...]], axis=1)
    _diff_fold(qm, k0, v0, b0, m_ref, l_ref, acc_ref, True)

    n_far = jnp.maximum(qi - 1, 0)
    n_group = n_far // DIFF_FAR_GROUP
    wide = DIFF_FAR_GROUP * blk

    def body(j, c):
        start = pl.multiple_of(j * wide, wide)
        _diff_fold(qm, kp_ref[:, pl.ds(start, wide)], vp_ref[pl.ds(start, wide), :], far,
                   m_ref, l_ref, acc_ref, False)
        return c

    lax.fori_loop(0, n_group, body, 0)
    rest = pl.multiple_of(n_group * wide, wide)
    for r in range(1, DIFF_FAR_GROUP):
        @pl.when(n_far - n_group * DIFF_FAR_GROUP == r)
        def _(r=r):
            _diff_fold(qm, kp_ref[:, pl.ds(rest, r * blk)], vp_ref[pl.ds(rest, r * blk), :], far,
                       m_ref, l_ref, acc_ref, False)

    o = _diff_combine(acc_ref[0] / l_ref[0], acc_ref[1] / l_ref[1], lam_init, lam_ref, gain_ref)
    o_ref[...] = o.astype(o_ref.dtype)


def _diff_sample_kernel(lam_init, lam_ref, gain_ref, q_ref, kc_ref, vc_ref, kn_ref, vn_ref,
                        bc_ref, bn_ref, o_ref):
    q_all = q_ref[...] * SOFTMAX_QSCALE
    past = kc_ref.shape[1]
    k_t, v_t = (True, False), (False, False)
    for h in range(C_HEADS):
        cols = slice(h * LANES, (h + 1) * LANES)
        q = q_all[:, cols]
        kb = [kc_ref[cols, :].astype(BF16), kn_ref[:, cols].astype(BF16)]
        vb = [vc_ref[pl.ds(h, past, stride=C_HEADS), :].astype(BF16), vn_ref[:, cols].astype(BF16)]
        biases = [bc_ref[h], bn_ref[h]]
        o0 = _softmax_segments(_half_lanes(q, False), kb, vb, biases, k_t, v_t)
        o1 = _softmax_segments(_half_lanes(q, True), kb, vb, biases, k_t, v_t)
        o_ref[:, cols] = _diff_combine(o0, o1, lam_init, lam_ref, gain_ref).astype(o_ref.dtype)


def _diff_attn_prompt(pn4, layer, k_t, t5tiles, lam_init, t5_flat, lam_params, gain):
    _, bn, s, _ = pn4.shape
    blk = (None, ATT_BLK, LANES)
    full = (None, s, LANES)
    state = pltpu.VMEM((2, ATT_BLK, LANES), F32)
    return pl.pallas_call(
        functools.partial(_diff_prompt_kernel, lam_init),
        grid=(bn, C_HEADS, s // ATT_BLK),
        in_specs=[pl.BlockSpec(memory_space=pltpu.SMEM),
                  pl.BlockSpec((4, HEAD_DIM), lambda b, h, qi: (0, 0)),
                  pl.BlockSpec((1, C_VDIM), lambda b, h, qi: (0, 0)),
                  _pspec(blk, lambda b, h, qi: (b, qi, _CBP["qc"] + h), layer),
                  pl.BlockSpec((None, None, LANES, ATT_BLK), lambda b, h, qi: (layer, b, h, qi)),
                  _pspec(blk, lambda b, h, qi: (b, qi, _CBP["vc"] + h), layer),
                  pl.BlockSpec((None, None, LANES, s), lambda b, h, qi: (layer, b, h, 0)),
                  _pspec(full, lambda b, h, qi: (b, 0, _CBP["vc"] + h), layer),
                  pl.BlockSpec((None, None, ATT_BLK, ATT_BLK), lambda b, h, qi: (h, 0, 0, 0)),
                  pl.BlockSpec((None, None, ATT_BLK, ATT_BLK), lambda b, h, qi: (h, 1, 0, 0))],
        out_specs=pl.BlockSpec(blk, lambda b, h, qi: (b, qi, h)),
        out_shape=jax.ShapeDtypeStruct((bn, s, BRANCH_W), BF16),
        scratch_shapes=[state, state, state],
        compiler_params=_cparams("parallel", "parallel", "arbitrary"),
        name="diff_attn_prompt",
    )(t5_flat, lam_params, gain.reshape(1, C_VDIM), pn4, k_t, pn4, k_t, pn4, t5tiles, t5tiles)


def _diff_attn_sample(p4, layer, k_cache_t, v_cache, bias_cache, bias_new, lam_init, lam_params, gain):
    _, bn, t, _ = p4.shape
    past = k_cache_t.shape[-1]
    new = lambda name: _pspec((None, t, BRANCH_W), lambda b: (b, 0, _CB[name] * LANES // BRANCH_W), layer)
    return pl.pallas_call(
        functools.partial(_diff_sample_kernel, lam_init),
        grid=(bn,),
        in_specs=[pl.BlockSpec((4, HEAD_DIM), lambda b: (0, 0)),
                  pl.BlockSpec((1, C_VDIM), lambda b: (0, 0)),
                  new("qc"),
                  pl.BlockSpec((None, None, BRANCH_W, past), lambda b: (layer, b, 0, 0)),
                  pl.BlockSpec((None, None, past * C_HEADS, C_VDIM), lambda b: (layer, b, 0, 0)),
                  new("kc"), new("vc"),
                  pl.BlockSpec((C_HEADS, t, past), lambda b: (0, 0, 0)),
                  pl.BlockSpec((C_HEADS, t, t), lambda b: (0, 0, 0))],
        out_specs=pl.BlockSpec((None, t, BRANCH_W), lambda b: (b, 0, 0)),
        out_shape=jax.ShapeDtypeStruct((bn, t, BRANCH_W), BF16),
        compiler_params=_cparams("parallel"),
        name="diff_attn_sample",
    )(lam_params, gain.reshape(1, C_VDIM), p4, k_cache_t, v_cache, p4, p4, bias_cache, bias_new)


def _merge_kernel(x_ref, oa_ref, ob_ref, oc_ref, g0_ref, g1_ref, g2_ref, bg_ref, wb_ref, wo_ref, o_ref):
    h = None
    for n, (o_r, g_r) in enumerate(((oa_ref, g0_ref), (ob_ref, g1_ref), (oc_ref, g2_ref))):
        gate = 1.0 / (1.0 + jnp.exp(-(g_r[...] + bg_ref[n:n + 1, :])))
        t = gate * _dot(o_r[...], wb_ref[n])
        h = t if h is None else h + t
    o_ref[...] = x_ref[...] + _dot(h.astype(BF16), wo_ref[...])


def _merge(x, o_a, o_b, o_c, p, layer, b_gate, w_branch, w_out, tm):
    t, d = x.shape
    row = lambda i: (i, 0)
    fixed2 = lambda i: (0, 0)
    return pl.pallas_call(
        _merge_kernel,
        grid=(t // tm,),
        in_specs=[pl.BlockSpec((tm, d), row),
                  pl.BlockSpec((tm, BRANCH_W), row),
                  pl.BlockSpec((tm, BRANCH_W), row),
                  pl.BlockSpec((tm, BRANCH_W), row),
                  pl.BlockSpec((None, tm, d), lambda i: (layer, i, 0)),
                  pl.BlockSpec((None, tm, d), lambda i: (layer, i, 1)),
                  pl.BlockSpec((None, tm, d), lambda i: (layer, i, 2)),
                  pl.BlockSpec((N_BRANCH, d), fixed2),
                  pl.BlockSpec((N_BRANCH, BRANCH_W, d), lambda i: (0, 0, 0)),
                  pl.BlockSpec((d, d), fixed2)],
        out_specs=pl.BlockSpec((tm, d), row),
        out_shape=jax.ShapeDtypeStruct((t, d), F32),
        compiler_params=_cparams("parallel"),
        name="gated_merge",
    )(x, o_a, o_b, o_c, p, p, p, b_gate, w_branch, w_out)


FFN_TF = D_FF // 2
FFN_PREV = 16


def _gelu_tanh(x):
    return x * (0.5 * (1.0 + jnp.tanh(math.sqrt(2.0 / math.pi) * (x + 0.044715 * (x * x * x)))))


def _ffn_kernel(tiles_per_seq, final, x_ref, xp_ref, gn_ref, wg_ref, wu_ref, cw_ref, cb_ref, wd_ref, st_ref, gf_ref,
                o_ref, so_ref, xn_ref, xpn_ref):
    i = pl.program_id(0)
    j = pl.program_id(1)
    whole_seqs = tiles_per_seq == 0

    @pl.when(j == 0)
    def _():
        xn_ref[...] = _rms(x_ref[...], gn_ref[...]).astype(BF16)
        if not whole_seqs:
            xpn_ref[...] = _rms(xp_ref[...], gn_ref[...]).astype(BF16)

    xn = xn_ref[...]
    tm = xn.shape[0]
    tf = wg_ref.shape[1]
    g = _dot(xn, wg_ref[...])
    u = _dot(xn, wu_ref[...])
    if whole_seqs:
        ns = st_ref.shape[0]
        st = st_ref[...]
        g = g.reshape(ns, tm // ns, tf)
        u = u.reshape(ns, tm // ns, tf)
        pm2, pm1 = st[:, 0:1, :], st[:, 1:2, :]
    else:
        gp = _dot(xpn_ref[...], wg_ref[...])
        first = (i % tiles_per_seq) == 0
        st = st_ref[0]
        pm2 = jnp.where(first, st[0:1, :], gp[FFN_PREV - 2:FFN_PREV - 1, :])
        pm1 = jnp.where(first, st[1:2, :], gp[FFN_PREV - 1:FFN_PREV, :])
    ax = g.ndim - 2
    row = lax.broadcasted_iota(jnp.int32, g.shape, ax)
    g1 = jnp.where(row == 0, pm1, pltpu.roll(g, 1, ax))
    g2 = jnp.where(row == 0, pm2, jnp.where(row == 1, pm1, pltpu.roll(g, 2, ax)))
    cw = cw_ref[...]
    c = cb_ref[...] + cw[0:1, :] * g2 + cw[1:2, :] * g1 + cw[2:3, :] * g
    hid = (_gelu_tanh(c) * u).reshape(tm, tf).astype(BF16)
    contrib = _dot(hid, wd_ref[...])

    last = pl.num_programs(1) - 1
    finish = (lambda y: _rms(y, gf_ref[...])) if final else (lambda y: y)

    @pl.when(j == 0)
    def _():
        o_ref[...] = x_ref[...] + contrib

    @pl.when(jnp.logical_and(j > 0, j < last))
    def _():
        o_ref[...] += contrib

    @pl.when(jnp.logical_and(j > 0, j == last))
    def _():
        o_ref[...] = finish(o_ref[...] + contrib)

    seq_len = g.shape[ax]
    if whole_seqs:
        so_ref[...] = g[:, seq_len - 2:seq_len, :]
    else:
        so_ref[0] = g[seq_len - 2:seq_len, :]


def _conv_ffn(x, g_norm, w_gate, w_up, conv_w, conv_b, w_down, state, seq_len, tm, g_final, final):
    t, d = x.shape
    nseq = t // seq_len
    nf = D_FF // FFN_TF
    assert nf > 1
    if seq_len >= tm:
        tiles_per_seq = seq_len // tm
        st_spec = pl.BlockSpec((1, 2, FFN_TF), lambda i, j: (i // tiles_per_seq, 0, j))
    else:
        tiles_per_seq = 0
        st_spec = pl.BlockSpec((tm // seq_len, 2, FFN_TF), lambda i, j: (i, 0, j))
    prev_blocks = tm // FFN_PREV
    tail_spec = pl.BlockSpec(st_spec.block_shape, lambda i, j: (i, 0, j))
    n_tail = t // tm * st_spec.block_shape[0]
    y, tails = pl.pallas_call(
        functools.partial(_ffn_kernel, tiles_per_seq, final),
        grid=(t // tm, nf),
        in_specs=[pl.BlockSpec((tm, d), lambda i, j: (i, 0)),
                  pl.BlockSpec((FFN_PREV, d), lambda i, j: (jnp.maximum(i * prev_blocks - 1, 0), 0)),
                  pl.BlockSpec((1, d), lambda i, j: (0, 0)),
                  pl.BlockSpec((d, FFN_TF), lambda i, j: (0, j)),
                  pl.BlockSpec((d, FFN_TF), lambda i, j: (0, j)),
                  pl.BlockSpec((3, FFN_TF), lambda i, j: (0, j)),
                  pl.BlockSpec((1, FFN_TF), lambda i, j: (0, j)),
                  pl.BlockSpec((FFN_TF, d), lambda i, j: (j, 0)),
                  st_spec,
                  pl.BlockSpec((1, d), lambda i, j: (0, 0))],
        out_specs=[pl.BlockSpec((tm, d), lambda i, j: (i, 0)), tail_spec],
        out_shape=[jax.ShapeDtypeStruct((t, d), F32),
                   jax.ShapeDtypeStruct((n_tail, 2, D_FF), F32)],
        scratch_shapes=[pltpu.VMEM((tm, d), BF16), pltpu.VMEM((FFN_PREV, d), BF16)],
        compiler_params=_cparams("arbitrary", "arbitrary"),
        name="conv_ffn_prompt" if tiles_per_seq else "conv_ffn_sample",
    )(x, x, g_norm.reshape(1, d), w_gate, w_up, conv_w, conv_b.reshape(1, D_FF), w_down, state, g_final.reshape(1, d))
    return y, tails[n_tail // nseq - 1::n_tail // nseq]


PROJ_TM, PROJ_TN = 1024, 1280
ROW_TM = 512


def _lam_init(layer):
    return 0.8 - 0.6 * math.exp(-0.3 * layer)


def _cols(p4, name):
    c0 = _CB[name] * LANES
    return p4[..., c0:c0 + BRANCH_W]


def _layer_tail(x2, o_a, o_b, o_c, p_rows, layer, prm, conv_state, seq_len):
    flat = lambda a: a.reshape(x2.shape[0], BRANCH_W)
    x2 = _merge(x2, flat(o_a), flat(o_b), flat(o_c), p_rows, layer, prm["b_gate"][layer],
                prm["w_branch"][layer], prm["w_out"][layer], ROW_TM)
    depth = prm["norm_ffn"].shape[0]
    return _conv_ffn(x2, prm["norm_ffn"][layer], prm["w_gate"][layer], prm["w_up"][layer],
                     prm["conv_w"][layer], prm["conv_b"][layer], prm["w_down"][layer], conv_state, seq_len, ROW_TM,
                     prm["norm_final"], layer == depth - 1)


def _trunk_prompt(x, prm, abias, t5tiles, t5_flat):
    bn, s, d = x.shape
    x2 = x.reshape(bn * s, d)
    depth = prm["w_rows"].shape[0]
    conv_states = []
    rows, kv_t = None, None
    for l in range(depth):
        rows = _norm_proj(x2, prm["norm_mix"][l], prm["w_rows"][l], PROJ_TM, PROJ_TN, l, depth, rows,
                          heads_group=(_CBP["vc"] * LANES, C_HEADS))
        p_rows, vc_heads = rows
        kv_t = _norm_proj_t(x2, prm["norm_mix"][l], prm["w_feat_t"][l], PROJ_TM, l, depth, bn, kv_t)
        ka_t, va_t, kb_t, vb_t, kc_t = kv_t
        pn4 = p_rows.reshape(depth, bn, s, ROW_COLS)
        o_a = _band_attn_prompt(pn4, l, ka_t, va_t, abias[l])
        o_b = _sb_attn_prompt(pn4, l, kb_t, vb_t)
        o_c = _diff_attn_prompt(pn4, l, kc_t, t5tiles, _lam_init(l), t5_flat, prm["c_lambda"][l], prm["c_subln"][l])
        x2, st_conv = _layer_tail(x2, o_a, o_b, o_c, p_rows, l, prm, jnp.zeros((bn, 2, D_FF), F32), s)
        conv_states.append(st_conv)
    y = x2.reshape(bn, s, d)
    heads_t = lambda a: jnp.transpose(a.reshape(depth, bn, A_HEADS, HEAD_DIM, a.shape[-1]), (0, 1, 4, 2, 3))
    keep = min(A_PAST_CHUNKS * CHUNK, s)
    states = (heads_t(ka_t[..., s - keep:]), heads_t(va_t[..., s - keep:]), heads_t(kb_t), heads_t(vb_t),
              jnp.transpose(kc_t.reshape(depth, bn, C_HEADS, 2, HEAD_DIM, s), (0, 1, 5, 2, 3, 4)),
              vc_heads.reshape(depth, bn, s, C_HEADS, C_VDIM),
              jnp.stack(conv_states, axis=0))
    return y, states


def _feature_major(cache):
    nd = cache.ndim
    t = jnp.transpose(cache, (0, 1) + tuple(range(3, nd)) + (2,))
    return t.reshape(cache.shape[0], cache.shape[1], BRANCH_W, cache.shape[2])


def _trunk_sample(x, caches, prm, abias, t5tiles):
    bn, t, d = x.shape
    x2 = x.reshape(bn * t, d)
    depth = prm["w_in"].shape[0]
    ca_k, ca_v, cb_k, cb_v, cc_k, cc_v, st_conv_in = caches
    ca_kt, ca_vt, cb_kt, cb_vt, cc_kt = (_feature_major(c) for c in (ca_k, ca_v, cb_k, cb_v, cc_k))
    keep, past = ca_k.shape[2], cc_k.shape[2]
    cc_vr = cc_v.reshape(depth, bn, past * C_HEADS, C_VDIM)
    conv_states = []
    p_all = None
    for l in range(depth):
        p_all, = _norm_proj(x2, prm["norm_mix"][l], prm["w_in"][l], PROJ_TM, PROJ_TN, l, depth,
                            None if p_all is None else [p_all])
        p4 = p_all.reshape(depth, bn, t, IN_COLS)
        bias_cache = jnp.concatenate([abias[l][:, 0, :t, :], abias[l][:, 1, :t, :]], axis=-1)[..., -keep:]
        o_a = _band_attn_sample(p4, l, ca_kt, ca_vt, bias_cache, abias[l][:, 2, :t, :t])
        o_b = _sb_attn_sample(p4, l, cb_kt, cb_vt)
        far = jnp.broadcast_to(t5tiles[:, 1, :1, :1], (C_HEADS, t, past - ATT_BLK))
        bias_c = jnp.concatenate([far, t5tiles[:, 1, :t, :]], axis=-1)
        o_c = _diff_attn_sample(p4, l, cc_kt, cc_vr, bias_c, t5tiles[:, 0, :t, :t],
                                _lam_init(l), prm["c_lambda"][l], prm["c_subln"][l])
        x2, st_conv = _layer_tail(x2, o_a, o_b, o_c, p_all, l, prm, st_conv_in[l], t)
        conv_states.append(st_conv)
    y = x2.reshape(bn, t, d)
    ka, va, kb, vb, kc, vc = (_cols(p4, n) for n in ("ka", "va", "kb", "vb", "kc", "vc"))
    heads = lambda a: a.reshape(a.shape[:3] + (A_HEADS, HEAD_DIM))
    states = (jnp.concatenate([ca_k[:, :, t:], heads(ka)], axis=2),
              jnp.concatenate([ca_v[:, :, t:], heads(va)], axis=2), heads(kb), heads(vb),
              kc.reshape(depth, bn, t, C_HEADS, 2, HEAD_DIM), vc.reshape(depth, bn, t, C_HEADS, C_VDIM),
              jnp.stack(conv_states, axis=0))
    return y, states


def kernel(x_prompt, x_sample, cache_a_k, cache_a_v, cache_b_k, cache_b_v, cache_c_k, cache_c_v,
           state_ffn_conv, norm_mix, w_in, b_gate, a_rel_bias, t5_bias, c_lambda, c_subln,
           w_branch, w_out, norm_ffn, w_up, conv_w, conv_b, w_down, norm_final):
    group = {name: w_in[..., i * BRANCH_W:(i + 1) * BRANCH_W] for i, name in enumerate(_GROUPS)}
    gate = w_in[..., len(_GROUPS) * BRANCH_W:]
    prm = dict(
        norm_mix=norm_mix,
        w_in=jnp.concatenate([gate] + [group[n] for n in _GROUPS], axis=-1).astype(BF16),
        w_rows=jnp.concatenate([gate] + [group[n] for n in _ROW_GROUPS], axis=-1).astype(BF16),
        w_feat_t=jnp.swapaxes(jnp.concatenate([group[n] for n in _T_GROUPS], axis=-1), 1, 2).astype(BF16),
        b_gate=b_gate, c_lambda=c_lambda, c_subln=c_subln,
        w_branch=w_branch.astype(BF16), w_out=w_out.astype(BF16), norm_ffn=norm_ffn,
        w_gate=w_up[..., :D_FF].astype(BF16), w_up=w_up[..., D_FF:].astype(BF16),
        conv_w=conv_w, conv_b=conv_b, w_down=w_down.astype(BF16), norm_final=norm_final)
    abias = [_build_abias(a_rel_bias[l]) for l in range(a_rel_bias.shape[0])]
    t5tiles = _build_t5bias(t5_bias)
    y_p, p_states = _trunk_prompt(x_prompt, prm, abias, t5tiles, t5_bias.reshape(-1))
    caches = (cache_a_k, cache_a_v, cache_b_k, cache_b_v, cache_c_k, cache_c_v, state_ffn_conv)
    y_s, s_states = _trunk_sample(x_sample, caches, prm, abias, t5tiles)
    return (y_p, y_s) + p_states + s_states
```

```python
import functools
import math

import jax
import jax.numpy as jnp
from jax import lax
from jax.experimental import pallas as pl
from jax.experimental.pallas import tpu as pltpu

F32 = jnp.float32
BF16 = jnp.bfloat16

D_MODEL = 1024
CHUNK = 64
HEAD_DIM = 64
A_HEADS = 8
A_PAST_CHUNKS = 8
A_REL_CLIP = 128
B_HEADS = 8
C_HEADS = 4
C_VDIM = 128
T5_BUCKETS = 32
D_FF = 2816
N_BRANCH = 3
BRANCH_W = 512
GATE_W = N_BRANCH * D_MODEL
IN_COLS = 9 * BRANCH_W + GATE_W
EPS = 1e-6
NEG_INF = -1e30
SCALE = HEAD_DIM ** -0.5
LOG2E = math.log2(math.e)
SOFTMAX_QSCALE = SCALE * LOG2E

LANES = 128
_GROUPS = ("qa", "ka", "va", "qb", "kb", "vb", "qc", "kc", "vc")
_CB = {name: (GATE_W + i * BRANCH_W) // LANES for i, name in enumerate(_GROUPS)}
_ROW_GROUPS = ("qa", "qb", "qc", "vc")
_T_GROUPS = ("ka", "va", "kb", "vb", "kc")
_CBP = {name: (GATE_W + i * BRANCH_W) // LANES for i, name in enumerate(_ROW_GROUPS)}
ROW_COLS = GATE_W + len(_ROW_GROUPS) * BRANCH_W

ATT_BLK = 256
SB_LOG_CUT = -104.0
T5_FAR_BUCKET = 15
DIFF_FAR_GROUP = 4
V7X_VMEM_BYTES = 64 * 1024 * 1024
VMEM_LIMIT = V7X_VMEM_BYTES - 8 * 1024 * 1024


def _cparams(*sem):
    return pltpu.CompilerParams(dimension_semantics=sem, vmem_limit_bytes=VMEM_LIMIT)


def _rms(x, g):
    return x * lax.rsqrt(jnp.mean(x * x, axis=-1, keepdims=True) + EPS) * g


def _dot(a, b):
    return jnp.dot(a, b, preferred_element_type=F32)


def _dot_nt(a, b):
    return lax.dot_general(a, b, (((1,), (1,)), ((), ())), preferred_element_type=F32)


def _norm_proj_kernel(n_stack, heads_cols, x_ref, g_ref, w_ref, *refs):
    outs, xn_ref = refs[n_stack:-1], refs[-1]
    j = pl.program_id(1)

    @pl.when(j == 0)
    def _():
        xn_ref[...] = _rms(x_ref[...], g_ref[...]).astype(BF16)

    res = _dot(xn_ref[...], w_ref[...])
    outs[0][...] = res
    if heads_cols is not None:
        tile, first, heads = heads_cols

        @pl.when(j == tile)
        def _():
            for h in range(heads):
                lo = first + h * LANES
                outs[1][pl.ds(h, res.shape[0], stride=heads), :] = res[:, lo:lo + LANES]


def _norm_proj(x, g, w, tm, tn, layer, depth, stacked, heads_group=None):
    t, d = x.shape
    n = w.shape[1]
    in_specs = [pl.BlockSpec((tm, d), lambda i, j: (i, 0)),
                pl.BlockSpec((1, d), lambda i, j: (0, 0)),
                pl.BlockSpec((d, tn), lambda i, j: (0, j))]
    args = [x, g.reshape(1, d), w]
    out_specs = [pl.BlockSpec((None, tm, tn), lambda i, j: (layer, i, j))]
    out_shape = [jax.ShapeDtypeStruct((depth, t, n), F32)]
    heads_cols = None
    if heads_group is not None:
        first, heads = heads_group
        heads_cols = (first // tn, first % tn, heads)
        out_specs.append(pl.BlockSpec((None, tm * heads, LANES), lambda i, j: (layer, i, 0)))
        out_shape.append(jax.ShapeDtypeStruct((depth, t * heads, LANES), F32))
    n_stack = 0 if stacked is None else len(out_shape)
    if stacked is not None:
        in_specs += [pl.BlockSpec(memory_space=pl.ANY)] * n_stack
        args += list(stacked)
    return pl.pallas_call(
        functools.partial(_norm_proj_kernel, n_stack, heads_cols),
        grid=(t // tm, n // tn),
        in_specs=in_specs,
        out_specs=out_specs,
        out_shape=out_shape,
        scratch_shapes=[pltpu.VMEM((tm, d), BF16)],
        input_output_aliases={3 + k: k for k in range(n_stack)},
        compiler_params=_cparams("arbitrary", "arbitrary"),
        name="norm_proj",
    )(*args)


def _norm_proj_t_kernel(ngroups, x_ref, g_ref, wt_ref, *refs):
    o_refs = refs[-ngroups:]
    xn = _rms(x_ref[...], g_ref[...]).astype(BF16)
    for g in range(ngroups):
        o_refs[g][...] = _dot_nt(wt_ref[g * BRANCH_W:(g + 1) * BRANCH_W, :], xn)


def _norm_proj_t(x, g, wt, tm, layer, depth, bn, stacked):
    t, d = x.shape
    s = t // bn
    ngroups = wt.shape[0] // BRANCH_W
    tiles = s // tm
    in_specs = [pl.BlockSpec((tm, d), lambda i: (i, 0)),
                pl.BlockSpec((1, d), lambda i: (0, 0)),
                pl.BlockSpec(wt.shape, lambda i: (0, 0))]
    args = [x, g.reshape(1, d), wt]
    aliases = {}
    if stacked is not None:
        in_specs += [pl.BlockSpec(memory_space=pl.ANY)] * ngroups
        args += list(stacked)
        aliases = {3 + n: n for n in range(ngroups)}
    out_spec = pl.BlockSpec((None, None, BRANCH_W, tm), lambda i: (layer, i // tiles, 0, i % tiles))
    return pl.pallas_call(
        functools.partial(_norm_proj_t_kernel, ngroups),
        grid=(t // tm,),
        in_specs=in_specs,
        out_specs=[out_spec] * ngroups,
        out_shape=[jax.ShapeDtypeStruct((depth, bn, BRANCH_W, s), F32)] * ngroups,
        input_output_aliases=aliases,
        compiler_params=_cparams("arbitrary"),
        name="norm_proj_t",
    )(*args)


def _abias_kernel(tbl_ref, o_ref):
    h = pl.program_id(0)
    sub = 8
    ql = lax.broadcasted_iota(jnp.int32, (ATT_BLK, ATT_BLK), 0)
    kl = lax.broadcasted_iota(jnp.int32, (ATT_BLK, ATT_BLK), 1)
    far = jnp.full((ATT_BLK, ATT_BLK), tbl_ref[2 * A_REL_CLIP * A_HEADS + h] * LOG2E, F32)
    o_ref[0, 0] = jnp.where((kl >> 6) >= (ql >> 6), far, NEG_INF)
    lead = ATT_BLK - sub
    b = lax.broadcasted_iota(jnp.int32, (sub, 2 * ATT_BLK), 0)
    x = lax.broadcasted_iota(jnp.int32, (sub, 2 * ATT_BLK), 1)
    for seg, off in ((1, ATT_BLK), (2, 0)):
        idx = jnp.clip(b + lead - x + off, -A_REL_CLIP, A_REL_CLIP) + A_REL_CLIP

        def body(r, acc, idx=idx):
            return jnp.where(idx == r, tbl_ref[r * A_HEADS + h] * LOG2E, acc)

        strip = lax.fori_loop(0, 2 * A_REL_CLIP + 1, body, jnp.zeros((sub, 2 * ATT_BLK), F32))
        for a in range(ATT_BLK // sub):
            o_ref[0, seg, a * sub:(a + 1) * sub, :] = strip[:, lead - a * sub:lead - a * sub + ATT_BLK]
    o_ref[0, 2] = jnp.where((kl >> 6) <= (ql >> 6), o_ref[0, 2], NEG_INF)


def _build_abias(table):
    return pl.pallas_call(
        _abias_kernel,
        grid=(A_HEADS,),
        in_specs=[pl.BlockSpec(memory_space=pltpu.SMEM)],
        out_specs=pl.BlockSpec((1, 3, ATT_BLK, ATT_BLK), lambda h: (h, 0, 0, 0)),
        out_shape=jax.ShapeDtypeStruct((A_HEADS, 3, ATT_BLK, ATT_BLK), F32),
        compiler_params=_cparams("parallel"),
        name="band_bias_tiles",
    )(table.reshape(-1))


_T5_LARGE_STEPS = (12, 16, 23, 32, 46, 64, 91)


def _t5bias_kernel(tbl_ref, o_ref):
    h = pl.program_id(0)
    rows = 64
    for seg, off in ((0, 0), (1, -ATT_BLK)):
        for rc in range(ATT_BLK // rows):
            ql = lax.broadcasted_iota(jnp.int32, (rows, ATT_BLK), 0) + rc * rows
            kl = lax.broadcasted_iota(jnp.int32, (rows, ATT_BLK), 1)
            rel = kl - ql + off
            n = jnp.abs(rel)
            large = jnp.full((rows, ATT_BLK), T5_BUCKETS // 4, jnp.int32)
            for th in _T5_LARGE_STEPS:
                large = large + jnp.where(n >= th, 1, 0)
            bucket = jnp.where(rel > 0, T5_BUCKETS // 2, 0) + jnp.where(n < T5_BUCKETS // 4, n, large)
            tile = jnp.zeros((rows, ATT_BLK), F32)
            for r in range(T5_BUCKETS):
                tile = jnp.where(bucket == r, tbl_ref[r * C_HEADS + h] * LOG2E, tile)
            if seg == 0:
                tile = jnp.where((kl >> 6) <= (ql >> 6), tile, NEG_INF)
            o_ref[0, seg, rc * rows:(rc + 1) * rows, :] = tile


def _build_t5bias(table):
    return pl.pallas_call(
        _t5bias_kernel,
        grid=(C_HEADS,),
        in_specs=[pl.BlockSpec(memory_space=pltpu.SMEM)],
        out_specs=pl.BlockSpec((1, 2, ATT_BLK, ATT_BLK), lambda h: (h, 0, 0, 0)),
        out_shape=jax.ShapeDtypeStruct((C_HEADS, 2, ATT_BLK, ATT_BLK), F32),
        compiler_params=_cparams("parallel"),
        name="t5_bias_tiles",
    )(table.reshape(-1))


def _pspec(block, index_map, layer):
    return pl.BlockSpec((None,) + block, lambda *g: (layer,) + tuple(index_map(*g)))


def _lane_tile(x, n):
    return x if n == 1 else jnp.concatenate([x] * n, axis=1)


def _half_lanes(q, upper):
    lane = lax.broadcasted_iota(jnp.int32, q.shape, 1)
    return jnp.where((lane >= HEAD_DIM) == upper, q, 0.0).astype(BF16)


def _stack_heads(q, width):
    group = lax.broadcasted_iota(jnp.int32, q.shape, 1) // width
    return jnp.concatenate([jnp.where(group == h, q, 0.0) for h in range(q.shape[1] // width)],
                           axis=0).astype(BF16)


def _unstack_heads(x, t, width):
    group = lax.broadcasted_iota(jnp.int32, (t, x.shape[1]), 1) // width
    out = jnp.zeros((t, x.shape[1]), x.dtype)
    for h in range(x.shape[1] // width):
        out = jnp.where(group == h, x[h * t:(h + 1) * t, :], out)
    return out


def _qk(q, k, k_t):
    return _dot(q, k) if k_t else _dot_nt(q, k)


def _pv(p, v, v_t):
    return _dot_nt(p, v) if v_t else _dot(p, v)


def _softmax_segments(qh, kb, vb, biases, k_t, v_t):
    s = [_qk(qh, k, t) + b for k, b, t in zip(kb, biases, k_t)]
    m = functools.reduce(jnp.maximum, [jnp.max(sj, axis=1, keepdims=True) for sj in s])
    p = [jnp.exp2(sj - m) for sj in s]
    l = functools.reduce(jnp.add, [jnp.sum(pj, axis=1, keepdims=True) for pj in p])
    acc = functools.reduce(jnp.add, [_pv(pj.astype(BF16), v, t) for pj, v, t in zip(p, vb, v_t)])
    return acc / l


def _band_kernel(nseg, min_qi, q_ref, *refs):
    k_refs, v_refs, b_refs = refs[:nseg], refs[nseg:2 * nseg], refs[2 * nseg:3 * nseg]
    o_ref = refs[3 * nseg]
    qi = pl.program_id(2)
    q = q_ref[...] * SOFTMAX_QSCALE
    tq = q.shape[0]
    kb = [k_refs[j][...].astype(BF16) for j in range(nseg)]
    vb = [v_refs[j][...].astype(BF16) for j in range(nseg)]
    qs = _stack_heads(q, HEAD_DIM)
    biases = []
    for j in range(nseg):
        tile = b_refs[j][...].reshape(2 * tq, b_refs[j].shape[-1])
        biases.append(tile if min_qi[j] == 0 else jnp.where(qi >= min_qi[j], tile, NEG_INF))
    o = _softmax_segments(qs, kb, vb, biases, (True,) * nseg, (True,) * nseg)
    o_ref[...] = _unstack_heads(o, tq, HEAD_DIM).astype(o_ref.dtype)


def _band_attn_prompt(pn4, layer, k_t, v_t, abias):
    _, bn, s, _ = pn4.shape
    nq = s // ATT_BLK
    blk = (None, ATT_BLK, LANES)

    def kv_spec(back):
        return pl.BlockSpec((None, None, LANES, ATT_BLK),
                            lambda b, hp, qi: (layer, b, hp, jnp.maximum(qi - back, 0)))

    def bias_spec(seg):
        return pl.BlockSpec((2, None, ATT_BLK, ATT_BLK), lambda b, hp, qi: (hp, seg, 0, 0))

    return pl.pallas_call(
        functools.partial(_band_kernel, 3, (2, 1, 0)),
        grid=(bn, A_HEADS // 2, nq),
        in_specs=([_pspec(blk, lambda b, hp, qi: (b, qi, _CBP["qa"] + hp), layer)]
                  + [kv_spec(back) for back in (2, 1, 0)]
                  + [kv_spec(back) for back in (2, 1, 0)]
                  + [bias_spec(seg) for seg in range(3)]),
        out_specs=pl.BlockSpec(blk, lambda b, hp, qi: (b, qi, hp)),
        out_shape=jax.ShapeDtypeStruct((bn, s, BRANCH_W), BF16),
        compiler_params=_cparams("parallel", "parallel", "arbitrary"),
        name="band_attn_prompt",
    )(pn4, k_t, k_t, k_t, v_t, v_t, v_t, abias, abias, abias)


def _band_sample_kernel(q_ref, kc_ref, kn_ref, vc_ref, vn_ref, bc_ref, bn_ref, o_ref):
    t = q_ref.shape[0]
    qs = _stack_heads(q_ref[...] * SOFTMAX_QSCALE, HEAD_DIM)
    kb = [kc_ref[...].astype(BF16), kn_ref[...].astype(BF16)]
    vb = [vc_ref[...].astype(BF16), vn_ref[...].astype(BF16)]
    o = _softmax_segments(qs, kb, vb, [bc_ref[...], bn_ref[...]], (True, False), (True, False))
    o_ref[...] = _unstack_heads(o, t, HEAD_DIM).astype(o_ref.dtype)


def _band_attn_sample(p4, layer, k_cache_t, v_cache_t, bias_cache, bias_new):
    _, bn, t, _ = p4.shape
    keep = k_cache_t.shape[-1]
    new = lambda name: _pspec((None, t, BRANCH_W), lambda b: (b, 0, _CB[name] * LANES // BRANCH_W), layer)
    old = pl.BlockSpec((None, None, BRANCH_W, keep), lambda b: (layer, b, 0, 0))
    return pl.pallas_call(
        _band_sample_kernel,
        grid=(bn,),
        in_specs=[new("qa"), old, new("ka"), old, new("va"),
                  pl.BlockSpec((A_HEADS * t, keep), lambda b: (0, 0)),
                  pl.BlockSpec((A_HEADS * t, t), lambda b: (0, 0))],
        out_specs=pl.BlockSpec((None, t, BRANCH_W), lambda b: (b, 0, 0)),
        out_shape=jax.ShapeDtypeStruct((bn, t, BRANCH_W), BF16),
        compiler_params=_cparams("parallel"),
        name="band_attn_sample",
    )(p4, k_cache_t, p4, v_cache_t, p4, bias_cache.reshape(A_HEADS * t, keep), bias_new.reshape(A_HEADS * t, t))


def _strict_upper(n):
    r = lax.broadcasted_iota(jnp.int32, (n, n), 0)
    c = lax.broadcasted_iota(jnp.int32, (n, n), 1)
    return jnp.where(r > c, 1.0, 0.0).astype(BF16)


def _sb_fold(qh, kb, vb, kv_t, upper, mask, carry_ref, acc_ref, first):
    tq, tk = qh[0].shape[0], upper.shape[0]
    top = None
    for h in range(len(qh)):
        z = _qk(qh[h], kb[0], kv_t)
        soft = jnp.log(1.0 + jnp.exp(-jnp.abs(z)))
        log_keep = -(jnp.maximum(z, 0.0) + soft)
        log_take = jnp.minimum(z, 0.0) - soft
        if mask is not None:
            log_keep = jnp.where(mask, log_keep, 0.0)
        hi = log_keep.astype(BF16)
        lo = (log_keep - hi.astype(F32)).astype(BF16)
        after = _dot(hi, upper) + _dot(lo, upper)
        if not first:
            after = after + _lane_tile(carry_ref[h], tk // LANES)
        w = jnp.exp(log_take + after)
        if mask is not None:
            w = jnp.where(mask, w, 0.0)
        pv = _pv(w.astype(BF16), vb[0], kv_t)
        total = jnp.sum(log_keep, axis=1, keepdims=True)
        if first:
            acc_ref[h] = pv
            carry = jnp.broadcast_to(total, (tq, LANES))
        else:
            acc_ref[h] += pv
            carry = carry_ref[h] + total
        carry_ref[h] = carry
        top = jnp.max(carry) if top is None else jnp.maximum(top, jnp.max(carry))
    return (top > SB_LOG_CUT).astype(jnp.int32)


def _sb_kernel(tk, n_past_static, own_rows, q_ref, kd_ref, vd_ref, kp_ref, vp_ref, o_ref, carry_ref, acc_ref):
    n_past = pl.program_id(2) if n_past_static is None else n_past_static
    q = q_ref[...] * SCALE
    tq = q.shape[0]
    rq = lax.broadcasted_iota(jnp.int32, (tq, tq), 0)
    ck = lax.broadcasted_iota(jnp.int32, (tq, tq), 1)
    qh = [_stack_heads(q, HEAD_DIM)]
    causal = jnp.concatenate([ck < rq] * (q.shape[1] // HEAD_DIM), axis=0)
    if own_rows:
        kd, vd = [kd_ref[...].astype(BF16)], [vd_ref[...].astype(BF16)]
        live = _sb_fold(qh, kd, vd, False, _strict_upper(tq), causal, carry_ref, acc_ref, True)
        n_left = n_past
    else:
        near = pl.multiple_of(jnp.maximum(n_past - 1, 0) * tq, tq)
        kd = [jnp.concatenate([kp_ref[:, pl.ds(near, tq)], kd_ref[...]], axis=1).astype(BF16)]
        vd = [jnp.concatenate([vp_ref[:, pl.ds(near, tq)], vd_ref[...]], axis=1).astype(BF16)]
        mask = jnp.concatenate([jnp.broadcast_to(n_past >= 1, causal.shape), causal], axis=1)
        live = _sb_fold(qh, kd, vd, True, _strict_upper(2 * tq), mask, carry_ref, acc_ref, True)
        n_left = n_past - 1

    def cond(st):
        return jnp.logical_and(st[0] >= 0, st[1] > 0)

    def body(st):
        start = pl.multiple_of(st[0] * tk, tk)
        kp, vp = [kp_ref[:, pl.ds(start, tk)].astype(BF16)], [vp_ref[:, pl.ds(start, tk)].astype(BF16)]
        live = _sb_fold(qh, kp, vp, True, _strict_upper(tk), None, carry_ref, acc_ref, False)
        return st[0] - 1, live

    lax.while_loop(cond, body, (n_left - 1, live))
    o_ref[...] = _unstack_heads(acc_ref[0], tq, HEAD_DIM).astype(o_ref.dtype)


def _sb_attn_prompt(pn4, layer, k_t, v_t):
    _, bn, s, _ = pn4.shape
    blk = (None, ATT_BLK, LANES)
    own = pl.BlockSpec((None, None, LANES, ATT_BLK), lambda b, hp, qi: (layer, b, hp, qi))
    older = pl.BlockSpec((None, None, LANES, s), lambda b, hp, qi: (layer, b, hp, 0))
    return pl.pallas_call(
        functools.partial(_sb_kernel, ATT_BLK, None, False),
        grid=(bn, B_HEADS // 2, s // ATT_BLK),
        in_specs=[_pspec(blk, lambda b, hp, qi: (b, qi, _CBP["qb"] + hp), layer), own, own, older, older],
        out_specs=pl.BlockSpec(blk, lambda b, hp, qi: (b, qi, hp)),
        out_shape=jax.ShapeDtypeStruct((bn, s, BRANCH_W), BF16),
        scratch_shapes=[pltpu.VMEM((1, 2 * ATT_BLK, LANES), F32), pltpu.VMEM((1, 2 * ATT_BLK, LANES), F32)],
        compiler_params=_cparams("parallel", "parallel", "arbitrary"),
        name="stick_attn_prompt",
    )(pn4, k_t, v_t, k_t, v_t)


def _sb_attn_sample(p4, layer, k_cache_t, v_cache_t):
    _, bn, t, _ = p4.shape
    past = k_cache_t.shape[-1]
    new = lambda name: _pspec((None, t, BRANCH_W), lambda b: (b, 0, _CB[name] * LANES // BRANCH_W), layer)
    older = pl.BlockSpec((None, None, BRANCH_W, past), lambda b: (layer, b, 0, 0))
    rows = B_HEADS * t
    return pl.pallas_call(
        functools.partial(_sb_kernel, ATT_BLK, past // ATT_BLK, True),
        grid=(bn,),
        in_specs=[new("qb"), new("kb"), new("vb"), older, older],
        out_specs=pl.BlockSpec((None, t, BRANCH_W), lambda b: (b, 0, 0)),
        out_shape=jax.ShapeDtypeStruct((bn, t, BRANCH_W), BF16),
        scratch_shapes=[pltpu.VMEM((1, rows, LANES), F32), pltpu.VMEM((1, rows, BRANCH_W), F32)],
        compiler_params=_cparams("parallel"),
        name="stick_attn_sample",
    )(p4, p4, p4, k_cache_t, v_cache_t)


def _diff_combine(o0, o1, lam_init, lam_ref, gain_ref):
    lp = lam_ref[...]
    lam = (jnp.exp(jnp.sum(lp[0:1] * lp[1:2], axis=1, keepdims=True))
           - jnp.exp(jnp.sum(lp[2:3] * lp[3:4], axis=1, keepdims=True)) + lam_init)
    return _rms(o0 - lam * o1, gain_ref[...]) * (1.0 - lam_init)


def _diff_fold(qs, k_t, v, bias, m_ref, l_ref, acc_ref, first):
    kb = k_t.astype(BF16)
    vb = v.astype(BF16)
    rows, tk = qs.shape[0], vb.shape[0]
    s = _dot(qs, kb) + bias
    m_cur = jnp.max(s, axis=1, keepdims=True)
    m_new = jnp.broadcast_to(m_cur, (rows, LANES)) if first else jnp.maximum(m_ref[...], m_cur)
    p = jnp.exp2(s - _lane_tile(m_new, tk // LANES))
    row = jnp.sum(p, axis=1, keepdims=True)
    pv = _dot(p.astype(BF16), vb)
    if first:
        l_ref[...] = jnp.broadcast_to(row, (rows, LANES))
        acc_ref[...] = pv
    else:
        alpha = jnp.exp2(m_ref[...] - m_new)
        l_ref[...] = alpha * l_ref[...] + row
        acc_ref[...] = alpha * acc_ref[...] + pv
    m_ref[...] = m_new


def _diff_prompt_kernel(lam_init, tbl_ref, lam_ref, gain_ref, q_ref, kd_ref, vd_ref, kp_ref, vp_ref,
                        bd_ref, bp_ref, o_ref, m_ref, l_ref, acc_ref):
    h = pl.program_id(1)
    qi = pl.program_id(2)
    q = q_ref[...] * SOFTMAX_QSCALE
    blk = q.shape[0]
    qm = _stack_heads(q, HEAD_DIM)
    far = tbl_ref[T5_FAR_BUCKET * C_HEADS + h] * LOG2E

    near = pl.multiple_of(jnp.maximum(qi - 1, 0) * blk, blk)
    k0 = jnp.concatenate([kp_ref[:, pl.ds(near, blk)], kd_ref[...]], axis=1)
    v0 = jnp.concatenate([vp_ref[pl.ds(near, blk), :], vd_ref[...]], axis=0)
    b0 = jnp.concatenate([jnp.where(qi >= 1, bp_ref[...], NEG_INF), bd_ref[...]], axis=1)
    b0 = jnp.concatenate([b0, b0], axis=0)
    _diff_fold(qm, k0, v0, b0, m_ref, l_ref, acc_ref, True)

    n_far = jnp.maximum(qi - 1, 0)
    n_group = n_far // DIFF_FAR_GROUP
    wide = DIFF_FAR_GROUP * blk

    def body(j, c):
        start = pl.multiple_of(j * wide, wide)
        _diff_fold(qm, kp_ref[:, pl.ds(start, wide)], vp_ref[pl.ds(start, wide), :], far,
                   m_ref, l_ref, acc_ref, False)
        return c

    lax.fori_loop(0, n_group, body, 0)
    rest = pl.multiple_of(n_group * wide, wide)
    for r in range(1, DIFF_FAR_GROUP):
        @pl.when(n_far - n_group * DIFF_FAR_GROUP == r)
        def _(r=r):
            _diff_fold(qm, kp_ref[:, pl.ds(rest, r * blk)], vp_ref[pl.ds(rest, r * blk), :], far,
                       m_ref, l_ref, acc_ref, False)

    o = acc_ref[...] / l_ref[...]
    o = _diff_combine(o[:blk], o[blk:], lam_init, lam_ref, gain_ref)
    o_ref[...] = o.astype(o_ref.dtype)


def _diff_sample_kernel(lam_init, lam_ref, gain_ref, q_ref, kc_ref, vc_ref, kn_ref, vn_ref,
                        bc_ref, bn_ref, o_ref):
    q_all = q_ref[...] * SOFTMAX_QSCALE
    past = kc_ref.shape[1]
    k_t, v_t = (True, False), (False, False)
    for h in range(C_HEADS):
        cols = slice(h * LANES, (h + 1) * LANES)
        q = q_all[:, cols]
        kb = [kc_ref[cols, :].astype(BF16), kn_ref[:, cols].astype(BF16)]
        vb = [vc_ref[pl.ds(h, past, stride=C_HEADS), :].astype(BF16), vn_ref[:, cols].astype(BF16)]
        biases = [bc_ref[h], bn_ref[h]]
        o0 = _softmax_segments(_half_lanes(q, False), kb, vb, biases, k_t, v_t)
        o1 = _softmax_segments(_half_lanes(q, True), kb, vb, biases, k_t, v_t)
        o_ref[:, cols] = _diff_combine(o0, o1, lam_init, lam_ref, gain_ref).astype(o_ref.dtype)


def _diff_attn_prompt(pn4, layer, k_t, t5tiles, lam_init, t5_flat, lam_params, gain):
    _, bn, s, _ = pn4.shape
    blk = (None, ATT_BLK, LANES)
    full = (None, s, LANES)
    state = pltpu.VMEM((2 * ATT_BLK, LANES), F32)
    return pl.pallas_call(
        functools.partial(_diff_prompt_kernel, lam_init),
        grid=(bn, C_HEADS, s // ATT_BLK),
        in_specs=[pl.BlockSpec(memory_space=pltpu.SMEM),
                  pl.BlockSpec((4, HEAD_DIM), lambda b, h, qi: (0, 0)),
                  pl.BlockSpec((1, C_VDIM), lambda b, h, qi: (0, 0)),
                  _pspec(blk, lambda b, h, qi: (b, qi, _CBP["qc"] + h), layer),
                  pl.BlockSpec((None, None, LANES, ATT_BLK), lambda b, h, qi: (layer, b, h, qi)),
                  _pspec(blk, lambda b, h, qi: (b, qi, _CBP["vc"] + h), layer),
                  pl.BlockSpec((None, None, LANES, s), lambda b, h, qi: (layer, b, h, 0)),
                  _pspec(full, lambda b, h, qi: (b, 0, _CBP["vc"] + h), layer),
                  pl.BlockSpec((None, None, ATT_BLK, ATT_BLK), lambda b, h, qi: (h, 0, 0, 0)),
                  pl.BlockSpec((None, None, ATT_BLK, ATT_BLK), lambda b, h, qi: (h, 1, 0, 0))],
        out_specs=pl.BlockSpec(blk, lambda b, h, qi: (b, qi, h)),
        out_shape=jax.ShapeDtypeStruct((bn, s, BRANCH_W), BF16),
        scratch_shapes=[state, state, state],
        compiler_params=_cparams("parallel", "parallel", "arbitrary"),
        name="diff_attn_prompt",
    )(t5_flat, lam_params, gain.reshape(1, C_VDIM), pn4, k_t, pn4, k_t, pn4, t5tiles, t5tiles)


def _diff_attn_sample(p4, layer, k_cache_t, v_cache, bias_cache, bias_new, lam_init, lam_params, gain):
    _, bn, t, _ = p4.shape
    past = k_cache_t.shape[-1]
    new = lambda name: _pspec((None, t, BRANCH_W), lambda b: (b, 0, _CB[name] * LANES // BRANCH_W), layer)
    return pl.pallas_call(
        functools.partial(_diff_sample_kernel, lam_init),
        grid=(bn,),
        in_specs=[pl.BlockSpec((4, HEAD_DIM), lambda b: (0, 0)),
                  pl.BlockSpec((1, C_VDIM), lambda b: (0, 0)),
                  new("qc"),
                  pl.BlockSpec((None, None, BRANCH_W, past), lambda b: (layer, b, 0, 0)),
                  pl.BlockSpec((None, None, past * C_HEADS, C_VDIM), lambda b: (layer, b, 0, 0)),
                  new("kc"), new("vc"),
                  pl.BlockSpec((C_HEADS, t, past), lambda b: (0, 0, 0)),
                  pl.BlockSpec((C_HEADS, t, t), lambda b: (0, 0, 0))],
        out_specs=pl.BlockSpec((None, t, BRANCH_W), lambda b: (b, 0, 0)),
        out_shape=jax.ShapeDtypeStruct((bn, t, BRANCH_W), BF16),
        compiler_params=_cparams("parallel"),
        name="diff_attn_sample",
    )(lam_params, gain.reshape(1, C_VDIM), p4, k_cache_t, v_cache, p4, p4, bias_cache, bias_new)


def _merge_kernel(x_ref, oa_ref, ob_ref, oc_ref, g0_ref, g1_ref, g2_ref, bg_ref, wb_ref, wo_ref, o_ref):
    h = None
    for n, (o_r, g_r) in enumerate(((oa_ref, g0_ref), (ob_ref, g1_ref), (oc_ref, g2_ref))):
        gate = 1.0 / (1.0 + jnp.exp(-(g_r[...] + bg_ref[n:n + 1, :])))
        t = gate * _dot(o_r[...], wb_ref[n])
        h = t if h is None else h + t
    o_ref[...] = x_ref[...] + _dot(h.astype(BF16), wo_ref[...])


def _merge(x, o_a, o_b, o_c, p, layer, b_gate, w_branch, w_out, tm):
    t, d = x.shape
    row = lambda i: (i, 0)
    fixed2 = lambda i: (0, 0)
    return pl.pallas_call(
        _merge_kernel,
        grid=(t // tm,),
        in_specs=[pl.BlockSpec((tm, d), row),
                  pl.BlockSpec((tm, BRANCH_W), row),
                  pl.BlockSpec((tm, BRANCH_W), row),
                  pl.BlockSpec((tm, BRANCH_W), row),
                  pl.BlockSpec((None, tm, d), lambda i: (layer, i, 0)),
                  pl.BlockSpec((None, tm, d), lambda i: (layer, i, 1)),
                  pl.BlockSpec((None, tm, d), lambda i: (layer, i, 2)),
                  pl.BlockSpec((N_BRANCH, d), fixed2),
                  pl.BlockSpec((N_BRANCH, BRANCH_W, d), lambda i: (0, 0, 0)),
                  pl.BlockSpec((d, d), fixed2)],
        out_specs=pl.BlockSpec((tm, d), row),
        out_shape=jax.ShapeDtypeStruct((t, d), F32),
        compiler_params=_cparams("parallel"),
        name="gated_merge",
    )(x, o_a, o_b, o_c, p, p, p, b_gate, w_branch, w_out)


FFN_TF = D_FF // 2
FFN_PREV = 16


def _gelu_tanh(x):
    return x * (0.5 * (1.0 + jnp.tanh(math.sqrt(2.0 / math.pi) * (x + 0.044715 * (x * x * x)))))


def _ffn_kernel(tiles_per_seq, final, x_ref, xp_ref, gn_ref, wg_ref, wu_ref, cw_ref, cb_ref, wd_ref, st_ref, gf_ref,
                o_ref, so_ref, xn_ref, xpn_ref):
    i = pl.program_id(0)
    j = pl.program_id(1)
    whole_seqs = tiles_per_seq == 0

    @pl.when(j == 0)
    def _():
        xn_ref[...] = _rms(x_ref[...], gn_ref[...]).astype(BF16)
        if not whole_seqs:
            xpn_ref[...] = _rms(xp_ref[...], gn_ref[...]).astype(BF16)

    xn = xn_ref[...]
    tm = xn.shape[0]
    tf = wg_ref.shape[1]
    g = _dot(xn, wg_ref[...])
    u = _dot(xn, wu_ref[...])
    if whole_seqs:
        ns = st_ref.shape[0]
        st = st_ref[...]
        g = g.reshape(ns, tm // ns, tf)
        u = u.reshape(ns, tm // ns, tf)
        pm2, pm1 = st[:, 0:1, :], st[:, 1:2, :]
    else:
        gp = _dot(xpn_ref[...], wg_ref[...])
        first = (i % tiles_per_seq) == 0
        st = st_ref[0]
        pm2 = jnp.where(first, st[0:1, :], gp[FFN_PREV - 2:FFN_PREV - 1, :])
        pm1 = jnp.where(first, st[1:2, :], gp[FFN_PREV - 1:FFN_PREV, :])
    ax = g.ndim - 2
    row = lax.broadcasted_iota(jnp.int32, g.shape, ax)
    g1 = jnp.where(row == 0, pm1, pltpu.roll(g, 1, ax))
    g2 = jnp.where(row == 0, pm2, jnp.where(row == 1, pm1, pltpu.roll(g, 2, ax)))
    cw = cw_ref[...]
    c = cb_ref[...] + cw[0:1, :] * g2 + cw[1:2, :] * g1 + cw[2:3, :] * g
    hid = (_gelu_tanh(c) * u).reshape(tm, tf).astype(BF16)
    contrib = _dot(hid, wd_ref[...])

    last = pl.num_programs(1) - 1
    finish = (lambda y: _rms(y, gf_ref[...])) if final else (lambda y: y)

    @pl.when(j == 0)
    def _():
        o_ref[...] = x_ref[...] + contrib

    @pl.when(jnp.logical_and(j > 0, j < last))
    def _():
        o_ref[...] += contrib

    @pl.when(jnp.logical_and(j > 0, j == last))
    def _():
        o_ref[...] = finish(o_ref[...] + contrib)

    seq_len = g.shape[ax]
    if whole_seqs:
        so_ref[...] = g[:, seq_len - 2:seq_len, :]
    else:
        so_ref[0] = g[seq_len - 2:seq_len, :]


def _conv_ffn(x, g_norm, w_gate, w_up, conv_w, conv_b, w_down, state, seq_len, tm, g_final, final):
    t, d = x.shape
    nseq = t // seq_len
    nf = D_FF // FFN_TF
    assert nf > 1
    if seq_len >= tm:
        tiles_per_seq = seq_len // tm
        st_spec = pl.BlockSpec((1, 2, FFN_TF), lambda i, j: (i // tiles_per_seq, 0, j))
    else:
        tiles_per_seq = 0
        st_spec = pl.BlockSpec((tm // seq_len, 2, FFN_TF), lambda i, j: (i, 0, j))
    prev_blocks = tm // FFN_PREV
    tail_spec = pl.BlockSpec(st_spec.block_shape, lambda i, j: (i, 0, j))
    n_tail = t // tm * st_spec.block_shape[0]
    y, tails = pl.pallas_call(
        functools.partial(_ffn_kernel, tiles_per_seq, final),
        grid=(t // tm, nf),
        in_specs=[pl.BlockSpec((tm, d), lambda i, j: (i, 0)),
                  pl.BlockSpec((FFN_PREV, d), lambda i, j: (jnp.maximum(i * prev_blocks - 1, 0), 0)),
                  pl.BlockSpec((1, d), lambda i, j: (0, 0)),
                  pl.BlockSpec((d, FFN_TF), lambda i, j: (0, j)),
                  pl.BlockSpec((d, FFN_TF), lambda i, j: (0, j)),
                  pl.BlockSpec((3, FFN_TF), lambda i, j: (0, j)),
                  pl.BlockSpec((1, FFN_TF), lambda i, j: (0, j)),
                  pl.BlockSpec((FFN_TF, d), lambda i, j: (j, 0)),
                  st_spec,
                  pl.BlockSpec((1, d), lambda i, j: (0, 0))],
        out_specs=[pl.BlockSpec((tm, d), lambda i, j: (i, 0)), tail_spec],
        out_shape=[jax.ShapeDtypeStruct((t, d), F32),
                   jax.ShapeDtypeStruct((n_tail, 2, D_FF), F32)],
        scratch_shapes=[pltpu.VMEM((tm, d), BF16), pltpu.VMEM((FFN_PREV, d), BF16)],
        compiler_params=_cparams("arbitrary", "arbitrary"),
        name="conv_ffn_prompt" if tiles_per_seq else "conv_ffn_sample",
    )(x, x, g_norm.reshape(1, d), w_gate, w_up, conv_w, conv_b.reshape(1, D_FF), w_down, state, g_final.reshape(1, d))
    return y, tails[n_tail // nseq - 1::n_tail // nseq]


PROJ_TM, PROJ_TN = 1024, 1280
ROW_TM = 512


def _lam_init(layer):
    return 0.8 - 0.6 * math.exp(-0.3 * layer)


def _cols(p4, name):
    c0 = _CB[name] * LANES
    return p4[..., c0:c0 + BRANCH_W]


def _layer_tail(x2, o_a, o_b, o_c, p_rows, layer, prm, conv_state, seq_len):
    flat = lambda a: a.reshape(x2.shape[0], BRANCH_W)
    x2 = _merge(x2, flat(o_a), flat(o_b), flat(o_c), p_rows, layer, prm["b_gate"][layer],
                prm["w_branch"][layer], prm["w_out"][layer], ROW_TM)
    depth = prm["norm_ffn"].shape[0]
    return _conv_ffn(x2, prm["norm_ffn"][layer], prm["w_gate"][layer], prm["w_up"][layer],
                     prm["conv_w"][layer], prm["conv_b"][layer], prm["w_down"][layer], conv_state, seq_len, ROW_TM,
                     prm["norm_final"], layer == depth - 1)


def _trunk_prompt(x, prm, abias, t5tiles, t5_flat):
    bn, s, d = x.shape
    x2 = x.reshape(bn * s, d)
    depth = prm["w_rows"].shape[0]
    conv_states = []
    rows, kv_t = None, None
    for l in range(depth):
        rows = _norm_proj(x2, prm["norm_mix"][l], prm["w_rows"][l], PROJ_TM, PROJ_TN, l, depth, rows,
                          heads_group=(_CBP["vc"] * LANES, C_HEADS))
        p_rows, vc_heads = rows
        kv_t = _norm_proj_t(x2, prm["norm_mix"][l], prm["w_feat_t"][l], PROJ_TM, l, depth, bn, kv_t)
        ka_t, va_t, kb_t, vb_t, kc_t = kv_t
        pn4 = p_rows.reshape(depth, bn, s, ROW_COLS)
        o_a = _band_attn_prompt(pn4, l, ka_t, va_t, abias[l])
        o_b = _sb_attn_prompt(pn4, l, kb_t, vb_t)
        o_c = _diff_attn_prompt(pn4, l, kc_t, t5tiles, _lam_init(l), t5_flat, prm["c_lambda"][l], prm["c_subln"][l])
        x2, st_conv = _layer_tail(x2, o_a, o_b, o_c, p_rows, l, prm, jnp.zeros((bn, 2, D_FF), F32), s)
        conv_states.append(st_conv)
    y = x2.reshape(bn, s, d)
    heads_t = lambda a: jnp.transpose(a.reshape(depth, bn, A_HEADS, HEAD_DIM, a.shape[-1]), (0, 1, 4, 2, 3))
    keep = min(A_PAST_CHUNKS * CHUNK, s)
    states = (heads_t(ka_t[..., s - keep:]), heads_t(va_t[..., s - keep:]), heads_t(kb_t), heads_t(vb_t),
              jnp.transpose(kc_t.reshape(depth, bn, C_HEADS, 2, HEAD_DIM, s), (0, 1, 5, 2, 3, 4)),
              vc_heads.reshape(depth, bn, s, C_HEADS, C_VDIM),
              jnp.stack(conv_states, axis=0))
    return y, states


def _feature_major(cache):
    nd = cache.ndim
    t = jnp.transpose(cache, (0, 1) + tuple(range(3, nd)) + (2,))
    return t.reshape(cache.shape[0], cache.shape[1], BRANCH_W, cache.shape[2])


def _trunk_sample(x, caches, prm, abias, t5tiles):
    bn, t, d = x.shape
    x2 = x.reshape(bn * t, d)
    depth = prm["w_in"].shape[0]
    ca_k, ca_v, cb_k, cb_v, cc_k, cc_v, st_conv_in = caches
    ca_kt, ca_vt, cb_kt, cb_vt, cc_kt = (_feature_major(c) for c in (ca_k, ca_v, cb_k, cb_v, cc_k))
    keep, past = ca_k.shape[2], cc_k.shape[2]
    cc_vr = cc_v.reshape(depth, bn, past * C_HEADS, C_VDIM)
    conv_states = []
    p_all = None
    for l in range(depth):
        p_all, = _norm_proj(x2, prm["norm_mix"][l], prm["w_in"][l], PROJ_TM, PROJ_TN, l, depth,
                            None if p_all is None else [p_all])
        p4 = p_all.reshape(depth, bn, t, IN_COLS)
        bias_cache = jnp.concatenate([abias[l][:, 0, :t, :], abias[l][:, 1, :t, :]], axis=-1)[..., -keep:]
        o_a = _band_attn_sample(p4, l, ca_kt, ca_vt, bias_cache, abias[l][:, 2, :t, :t])
        o_b = _sb_attn_sample(p4, l, cb_kt, cb_vt)
        far = jnp.broadcast_to(t5tiles[:, 1, :1, :1], (C_HEADS, t, past - ATT_BLK))
        bias_c = jnp.concatenate([far, t5tiles[:, 1, :t, :]], axis=-1)
        o_c = _diff_attn_sample(p4, l, cc_kt, cc_vr, bias_c, t5tiles[:, 0, :t, :t],
                                _lam_init(l), prm["c_lambda"][l], prm["c_subln"][l])
        x2, st_conv = _layer_tail(x2, o_a, o_b, o_c, p_all, l, prm, st_conv_in[l], t)
        conv_states.append(st_conv)
    y = x2.reshape(bn, t, d)
    ka, va, kb, vb, kc, vc = (_cols(p4, n) for n in ("ka", "va", "kb", "vb", "kc", "vc"))
    heads = lambda a: a.reshape(a.shape[:3] + (A_HEADS, HEAD_DIM))
    states = (jnp.concatenate([ca_k[:, :, t:], heads(ka)], axis=2),
              jnp.concatenate([ca_v[:, :, t:], heads(va)], axis=2), heads(kb), heads(vb),
              kc.reshape(depth, bn, t, C_HEADS, 2, HEAD_DIM), vc.reshape(depth, bn, t, C_HEADS, C_VDIM),
              jnp.stack(conv_states, axis=0))
    return y, states


def kernel(x_prompt, x_sample, cache_a_k, cache_a_v, cache_b_k, cache_b_v, cache_c_k, cache_c_v,
           state_ffn_conv, norm_mix, w_in, b_gate, a_rel_bias, t5_bias, c_lambda, c_subln,
           w_branch, w_out, norm_ffn, w_up, conv_w, conv_b, w_down, norm_final):
    group = {name: w_in[..., i * BRANCH_W:(i + 1) * BRANCH_W] for i, name in enumerate(_GROUPS)}
    gate = w_in[..., len(_GROUPS) * BRANCH_W:]
    prm = dict(
        norm_mix=norm_mix,
        w_in=jnp.concatenate([gate] + [group[n] for n in _GROUPS], axis=-1).astype(BF16),
        w_rows=jnp.concatenate([gate] + [group[n] for n in _ROW_GROUPS], axis=-1).astype(BF16),
        w_feat_t=jnp.swapaxes(jnp.concatenate([group[n] for n in _T_GROUPS], axis=-1), 1, 2).astype(BF16),
        b_gate=b_gate, c_lambda=c_lambda, c_subln=c_subln,
        w_branch=w_branch.astype(BF16), w_out=w_out.astype(BF16), norm_ffn=norm_ffn,
        w_gate=w_up[..., :D_FF].astype(BF16), w_up=w_up[..., D_FF:].astype(BF16),
        conv_w=conv_w, conv_b=conv_b, w_down=w_down.astype(BF16), norm_final=norm_final)
    abias = [_build_abias(a_rel_bias[l]) for l in range(a_rel_bias.shape[0])]
    t5tiles = _build_t5bias(t5_bias)
    y_p, p_states = _trunk_prompt(x_prompt, prm, abias, t5tiles, t5_bias.reshape(-1))
    caches = (cache_a_k, cache_a_v, cache_b_k, cache_b_v, cache_c_k, cache_c_v, state_ffn_conv)
    y_s, s_states = _trunk_sample(x_sample, caches, prm, abias, t5tiles)
    return (y_p, y_s) + p_states + s_states
```

```python
import functools
import math

import jax
import jax.numpy as jnp
from jax import lax
from jax.experimental import pallas as pl
from jax.experimental.pallas import tpu as pltpu

F32 = jnp.float32
BF16 = jnp.bfloat16

D_MODEL = 1024
CHUNK = 64
HEAD_DIM = 64
A_HEADS = 8
A_PAST_CHUNKS = 8
A_REL_CLIP = 128
B_HEADS = 8
C_HEADS = 4
C_VDIM = 128
T5_BUCKETS = 32
D_FF = 2816
N_BRANCH = 3
BRANCH_W = 512
GATE_W = N_BRANCH * D_MODEL
IN_COLS = 9 * BRANCH_W + GATE_W
EPS = 1e-6
NEG_INF = -1e30
SCALE = HEAD_DIM ** -0.5
LOG2E = math.log2(math.e)
SOFTMAX_QSCALE = SCALE * LOG2E

LANES = 128
_GROUPS = ("qa", "ka", "va", "qb", "kb", "vb", "qc", "kc", "vc")
_CB = {name: (GATE_W + i * BRANCH_W) // LANES for i, name in enumerate(_GROUPS)}
_ROW_GROUPS = ("qa", "qb", "qc", "vc")
_T_GROUPS = ("ka", "va", "kb", "vb", "kc")
_CBP = {name: (GATE_W + i * BRANCH_W) // LANES for i, name in enumerate(_ROW_GROUPS)}
ROW_COLS = GATE_W + len(_ROW_GROUPS) * BRANCH_W

ATT_BLK = 256
SB_LOG_CUT = -104.0
T5_FAR_BUCKET = 15
DIFF_FAR_GROUP = 4
V7X_VMEM_BYTES = 64 * 1024 * 1024
VMEM_LIMIT = V7X_VMEM_BYTES - 8 * 1024 * 1024


def _cparams(*sem):
    return pltpu.CompilerParams(dimension_semantics=sem, vmem_limit_bytes=VMEM_LIMIT)


def _rms(x, g):
    return x * lax.rsqrt(jnp.mean(x * x, axis=-1, keepdims=True) + EPS) * g


def _dot(a, b):
    return jnp.dot(a, b, preferred_element_type=F32)


def _dot_nt(a, b):
    return lax.dot_general(a, b, (((1,), (1,)), ((), ())), preferred_element_type=F32)


def _norm_proj_kernel(n_stack, heads_cols, x_ref, g_ref, w_ref, *refs):
    outs, xn_ref = refs[n_stack:-1], refs[-1]
    j = pl.program_id(1)

    @pl.when(j == 0)
    def _():
        xn_ref[...] = _rms(x_ref[...], g_ref[...]).astype(BF16)

    res = _dot(xn_ref[...], w_ref[...])
    outs[0][...] = res
    if heads_cols is not None:
        tile, first, heads = heads_cols

        @pl.when(j == tile)
        def _():
            for h in range(heads):
                lo = first + h * LANES
                outs[1][pl.ds(h, res.shape[0], stride=heads), :] = res[:, lo:lo + LANES]


def _norm_proj(x, g, w, tm, tn, layer, depth, stacked, heads_group=None):
    t, d = x.shape
    n = w.shape[1]
    in_specs = [pl.BlockSpec((tm, d), lambda i, j: (i, 0)),
                pl.BlockSpec((1, d), lambda i, j: (0, 0)),
                pl.BlockSpec((d, tn), lambda i, j: (0, j))]
    args = [x, g.reshape(1, d), w]
    out_specs = [pl.BlockSpec((None, tm, tn), lambda i, j: (layer, i, j))]
    out_shape = [jax.ShapeDtypeStruct((depth, t, n), F32)]
    heads_cols = None
    if heads_group is not None:
        first, heads = heads_group
        heads_cols = (first // tn, first % tn, heads)
        out_specs.append(pl.BlockSpec((None, tm * heads, LANES), lambda i, j: (layer, i, 0)))
        out_shape.append(jax.ShapeDtypeStruct((depth, t * heads, LANES), F32))
    n_stack = 0 if stacked is None else len(out_shape)
    if stacked is not None:
        in_specs += [pl.BlockSpec(memory_space=pl.ANY)] * n_stack
        args += list(stacked)
    return pl.pallas_call(
        functools.partial(_norm_proj_kernel, n_stack, heads_cols),
        grid=(t // tm, n // tn),
        in_specs=in_specs,
        out_specs=out_specs,
        out_shape=out_shape,
        scratch_shapes=[pltpu.VMEM((tm, d), BF16)],
        input_output_aliases={3 + k: k for k in range(n_stack)},
        compiler_params=_cparams("arbitrary", "arbitrary"),
        name="norm_proj",
    )(*args)


def _norm_proj_t_kernel(ngroups, x_ref, g_ref, wt_ref, *refs):
    o_refs = refs[-ngroups:]
    xn = _rms(x_ref[...], g_ref[...]).astype(BF16)
    for g in range(ngroups):
        o_refs[g][...] = _dot_nt(wt_ref[g * BRANCH_W:(g + 1) * BRANCH_W, :], xn)


def _norm_proj_t(x, g, wt, tm, layer, depth, bn, stacked):
    t, d = x.shape
    s = t // bn
    ngroups = wt.shape[0] // BRANCH_W
    tiles = s // tm
    in_specs = [pl.BlockSpec((tm, d), lambda i: (i, 0)),
                pl.BlockSpec((1, d), lambda i: (0, 0)),
                pl.BlockSpec(wt.shape, lambda i: (0, 0))]
    args = [x, g.reshape(1, d), wt]
    aliases = {}
    if stacked is not None:
        in_specs += [pl.BlockSpec(memory_space=pl.ANY)] * ngroups
        args += list(stacked)
        aliases = {3 + n: n for n in range(ngroups)}
    out_spec = pl.BlockSpec((None, None, BRANCH_W, tm), lambda i: (layer, i // tiles, 0, i % tiles))
    return pl.pallas_call(
        functools.partial(_norm_proj_t_kernel, ngroups),
        grid=(t // tm,),
        in_specs=in_specs,
        out_specs=[out_spec] * ngroups,
        out_shape=[jax.ShapeDtypeStruct((depth, bn, BRANCH_W, s), F32)] * ngroups,
        input_output_aliases=aliases,
        compiler_params=_cparams("arbitrary"),
        name="norm_proj_t",
    )(*args)


def _abias_kernel(tbl_ref, o_ref):
    h = pl.program_id(0)
    sub = 8
    ql = lax.broadcasted_iota(jnp.int32, (ATT_BLK, ATT_BLK), 0)
    kl = lax.broadcasted_iota(jnp.int32, (ATT_BLK, ATT_BLK), 1)
    far = jnp.full((ATT_BLK, ATT_BLK), tbl_ref[2 * A_REL_CLIP * A_HEADS + h] * LOG2E, F32)
    o_ref[0, 0] = jnp.where((kl >> 6) >= (ql >> 6), far, NEG_INF)
    lead = ATT_BLK - sub
    b = lax.broadcasted_iota(jnp.int32, (sub, 2 * ATT_BLK), 0)
    x = lax.broadcasted_iota(jnp.int32, (sub, 2 * ATT_BLK), 1)
    for seg, off in ((1, ATT_BLK), (2, 0)):
        idx = jnp.clip(b + lead - x + off, -A_REL_CLIP, A_REL_CLIP) + A_REL_CLIP

        def body(r, acc, idx=idx):
            return jnp.where(idx == r, tbl_ref[r * A_HEADS + h] * LOG2E, acc)

        strip = lax.fori_loop(0, 2 * A_REL_CLIP + 1, body, jnp.zeros((sub, 2 * ATT_BLK), F32))
        for a in range(ATT_BLK // sub):
            o_ref[0, seg, a * sub:(a + 1) * sub, :] = strip[:, lead - a * sub:lead - a * sub + ATT_BLK]
    o_ref[0, 2] = jnp.where((kl >> 6) <= (ql >> 6), o_ref[0, 2], NEG_INF)


def _build_abias(table):
    return pl.pallas_call(
        _abias_kernel,
        grid=(A_HEADS,),
        in_specs=[pl.BlockSpec(memory_space=pltpu.SMEM)],
        out_specs=pl.BlockSpec((1, 3, ATT_BLK, ATT_BLK), lambda h: (h, 0, 0, 0)),
        out_shape=jax.ShapeDtypeStruct((A_HEADS, 3, ATT_BLK, ATT_BLK), F32),
        compiler_params=_cparams("parallel"),
        name="band_bias_tiles",
    )(table.reshape(-1))


_T5_LARGE_STEPS = (12, 16, 23, 32, 46, 64, 91)


def _t5bias_kernel(tbl_ref, o_ref):
    h = pl.program_id(0)
    rows = 64
    for seg, off in ((0, 0), (1, -ATT_BLK)):
        for rc in range(ATT_BLK // rows):
            ql = lax.broadcasted_iota(jnp.int32, (rows, ATT_BLK), 0) + rc * rows
            kl = lax.broadcasted_iota(jnp.int32, (rows, ATT_BLK), 1)
            rel = kl - ql + off
            n = jnp.abs(rel)
            large = jnp.full((rows, ATT_BLK), T5_BUCKETS // 4, jnp.int32)
            for th in _T5_LARGE_STEPS:
                large = large + jnp.where(n >= th, 1, 0)
            bucket = jnp.where(rel > 0, T5_BUCKETS // 2, 0) + jnp.where(n < T5_BUCKETS // 4, n, large)
            tile = jnp.zeros((rows, ATT_BLK), F32)
            for r in range(T5_BUCKETS):
                tile = jnp.where(bucket == r, tbl_ref[r * C_HEADS + h] * LOG2E, tile)
            if seg == 0:
                tile = jnp.where((kl >> 6) <= (ql >> 6), tile, NEG_INF)
            o_ref[0, seg, rc * rows:(rc + 1) * rows, :] = tile


def _build_t5bias(table):
    return pl.pallas_call(
        _t5bias_kernel,
        grid=(C_HEADS,),
        in_specs=[pl.BlockSpec(memory_space=pltpu.SMEM)],
        out_specs=pl.BlockSpec((1, 2, ATT_BLK, ATT_BLK), lambda h: (h, 0, 0, 0)),
        out_shape=jax.ShapeDtypeStruct((C_HEADS, 2, ATT_BLK, ATT_BLK), F32),
        compiler_params=_cparams("parallel"),
        name="t5_bias_tiles",
    )(table.reshape(-1))


def _pspec(block, index_map, layer):
    return pl.BlockSpec((None,) + block, lambda *g: (layer,) + tuple(index_map(*g)))


def _lane_tile(x, n):
    return x if n == 1 else jnp.concatenate([x] * n, axis=1)


def _stack_heads(q, width):
    group = lax.broadcasted_iota(jnp.int32, q.shape, 1) // width
    return jnp.concatenate([jnp.where(group == h, q, 0.0) for h in range(q.shape[1] // width)],
                           axis=0).astype(BF16)


def _unstack_heads(x, t, width):
    group = lax.broadcasted_iota(jnp.int32, (t, x.shape[1]), 1) // width
    out = jnp.zeros((t, x.shape[1]), x.dtype)
    for h in range(x.shape[1] // width):
        out = jnp.where(group == h, x[h * t:(h + 1) * t, :], out)
    return out


def _qk(q, k, k_t):
    return _dot(q, k) if k_t else _dot_nt(q, k)


def _pv(p, v, v_t):
    return _dot_nt(p, v) if v_t else _dot(p, v)


def _softmax_segments(qh, kb, vb, biases, k_t, v_t):
    s = [_qk(qh, k, t) + b for k, b, t in zip(kb, biases, k_t)]
    m = functools.reduce(jnp.maximum, [jnp.max(sj, axis=1, keepdims=True) for sj in s])
    p = [jnp.exp2(sj - m) for sj in s]
    l = functools.reduce(jnp.add, [jnp.sum(pj, axis=1, keepdims=True) for pj in p])
    acc = functools.reduce(jnp.add, [_pv(pj.astype(BF16), v, t) for pj, v, t in zip(p, vb, v_t)])
    return acc / l


def _band_kernel(nseg, min_qi, q_ref, *refs):
    k_refs, v_refs, b_refs = refs[:nseg], refs[nseg:2 * nseg], refs[2 * nseg:3 * nseg]
    o_ref = refs[3 * nseg]
    qi = pl.program_id(2)
    q = q_ref[...] * SOFTMAX_QSCALE
    tq = q.shape[0]
    kb = [k_refs[j][...].astype(BF16) for j in range(nseg)]
    vb = [v_refs[j][...].astype(BF16) for j in range(nseg)]
    qs = _stack_heads(q, HEAD_DIM)
    biases = []
    for j in range(nseg):
        tile = b_refs[j][...].reshape(2 * tq, b_refs[j].shape[-1])
        biases.append(tile if min_qi[j] == 0 else jnp.where(qi >= min_qi[j], tile, NEG_INF))
    o = _softmax_segments(qs, kb, vb, biases, (True,) * nseg, (True,) * nseg)
    o_ref[...] = _unstack_heads(o, tq, HEAD_DIM).astype(o_ref.dtype)


def _band_attn_prompt(pn4, layer, k_t, v_t, abias):
    _, bn, s, _ = pn4.shape
    nq = s // ATT_BLK
    blk = (None, ATT_BLK, LANES)

    def kv_spec(back):
        return pl.BlockSpec((None, None, LANES, ATT_BLK),
                            lambda b, hp, qi: (layer, b, hp, jnp.maximum(qi - back, 0)))

    def bias_spec(seg):
        return pl.BlockSpec((2, None, ATT_BLK, ATT_BLK), lambda b, hp, qi: (hp, seg, 0, 0))

    return pl.pallas_call(
        functools.partial(_band_kernel, 3, (2, 1, 0)),
        grid=(bn, A_HEADS // 2, nq),
        in_specs=([_pspec(blk, lambda b, hp, qi: (b, qi, _CBP["qa"] + hp), layer)]
                  + [kv_spec(back) for back in (2, 1, 0)]
                  + [kv_spec(back) for back in (2, 1, 0)]
                  + [bias_spec(seg) for seg in range(3)]),
        out_specs=pl.BlockSpec(blk, lambda b, hp, qi: (b, qi, hp)),
        out_shape=jax.ShapeDtypeStruct((bn, s, BRANCH_W), BF16),
        compiler_params=_cparams("parallel", "parallel", "arbitrary"),
        name="band_attn_prompt",
    )(pn4, k_t, k_t, k_t, v_t, v_t, v_t, abias, abias, abias)


def _band_sample_kernel(q_ref, kc_ref, kn_ref, vc_ref, vn_ref, bc_ref, bn_ref, o_ref):
    t = q_ref.shape[0]
    qs = _stack_heads(q_ref[...] * SOFTMAX_QSCALE, HEAD_DIM)
    kb = [kc_ref[...].astype(BF16), kn_ref[...].astype(BF16)]
    vb = [vc_ref[...].astype(BF16), vn_ref[...].astype(BF16)]
    o = _softmax_segments(qs, kb, vb, [bc_ref[...], bn_ref[...]], (True, False), (True, False))
    o_ref[...] = _unstack_heads(o, t, HEAD_DIM).astype(o_ref.dtype)


def _band_attn_sample(p4, layer, k_cache_t, v_cache_t, bias_cache, bias_new):
    _, bn, t, _ = p4.shape
    keep = k_cache_t.shape[-1]
    new = lambda name: _pspec((None, t, BRANCH_W), lambda b: (b, 0, _CB[name] * LANES // BRANCH_W), layer)
    old = pl.BlockSpec((None, None, BRANCH_W, keep), lambda b: (layer, b, 0, 0))
    return pl.pallas_call(
        _band_sample_kernel,
        grid=(bn,),
        in_specs=[new("qa"), old, new("ka"), old, new("va"),
                  pl.BlockSpec((A_HEADS * t, keep), lambda b: (0, 0)),
                  pl.BlockSpec((A_HEADS * t, t), lambda b: (0, 0))],
        out_specs=pl.BlockSpec((None, t, BRANCH_W), lambda b: (b, 0, 0)),
        out_shape=jax.ShapeDtypeStruct((bn, t, BRANCH_W), BF16),
        compiler_params=_cparams("parallel"),
        name="band_attn_sample",
    )(p4, k_cache_t, p4, v_cache_t, p4, bias_cache.reshape(A_HEADS * t, keep), bias_new.reshape(A_HEADS * t, t))


def _strict_upper(n):
    r = lax.broadcasted_iota(jnp.int32, (n, n), 0)
    c = lax.broadcasted_iota(jnp.int32, (n, n), 1)
    return jnp.where(r > c, 1.0, 0.0).astype(BF16)


def _sb_fold(qh, kb, vb, kv_t, upper, mask, carry_ref, acc_ref, first):
    tq = qh[0].shape[0]
    tk = kb[0].shape[1] if kv_t else kb[0].shape[0]
    cw = upper.shape[0]
    top = None
    for h in range(len(qh)):
        z = _qk(qh[h], kb[0], kv_t)
        soft = jnp.log(1.0 + jnp.exp(-jnp.abs(z)))
        log_keep = -(jnp.maximum(z, 0.0) + soft)
        log_take = jnp.minimum(z, 0.0) - soft
        if mask is not None:
            log_keep = jnp.where(mask, log_keep, 0.0)
        hi = log_keep.astype(BF16)
        lo = (log_keep - hi.astype(F32)).astype(BF16)
        pieces, total = [], None
        for c0 in reversed(range(0, tk, cw)):
            piece = _dot(hi[:, c0:c0 + cw], upper) + _dot(lo[:, c0:c0 + cw], upper)
            pieces.insert(0, piece if total is None else piece + total)
            part = jnp.sum(log_keep[:, c0:c0 + cw], axis=1, keepdims=True)
            total = part if total is None else total + part
        after = pieces[0] if len(pieces) == 1 else jnp.concatenate(pieces, axis=1)
        if not first:
            after = after + _lane_tile(carry_ref[h], tk // LANES)
        w = jnp.exp(log_take + after)
        if mask is not None:
            w = jnp.where(mask, w, 0.0)
        pv = _pv(w.astype(BF16), vb[0], kv_t)
        if first:
            acc_ref[h] = pv
            carry = jnp.broadcast_to(total, (tq, LANES))
        else:
            acc_ref[h] += pv
            carry = carry_ref[h] + total
        carry_ref[h] = carry
        top = jnp.max(carry) if top is None else jnp.maximum(top, jnp.max(carry))
    return (top > SB_LOG_CUT).astype(jnp.int32)


def _sb_kernel(tk, n_past_static, own_rows, q_ref, kd_ref, vd_ref, kp_ref, vp_ref, o_ref, carry_ref, acc_ref):
    n_past = pl.program_id(2) if n_past_static is None else n_past_static
    q = q_ref[...] * SCALE
    tq = q.shape[0]
    rq = lax.broadcasted_iota(jnp.int32, (tq, tq), 0)
    ck = lax.broadcasted_iota(jnp.int32, (tq, tq), 1)
    qh = [_stack_heads(q, HEAD_DIM)]
    causal = jnp.concatenate([ck < rq] * (q.shape[1] // HEAD_DIM), axis=0)
    if own_rows:
        kd, vd = [kd_ref[...].astype(BF16)], [vd_ref[...].astype(BF16)]
        live = _sb_fold(qh, kd, vd, False, _strict_upper(tq), causal, carry_ref, acc_ref, True)
        n_left = n_past
    else:
        near = pl.multiple_of(jnp.maximum(n_past - 1, 0) * tq, tq)
        kd = [jnp.concatenate([kp_ref[:, pl.ds(near, tq)], kd_ref[...]], axis=1).astype(BF16)]
        vd = [jnp.concatenate([vp_ref[:, pl.ds(near, tq)], vd_ref[...]], axis=1).astype(BF16)]
        mask = jnp.concatenate([jnp.broadcast_to(n_past >= 1, causal.shape), causal], axis=1)
        live = _sb_fold(qh, kd, vd, True, _strict_upper(tq), mask, carry_ref, acc_ref, True)
        n_left = n_past - 1

    def cond(st):
        return jnp.logical_and(st[0] >= 0, st[1] > 0)

    def body(st):
        start = pl.multiple_of(st[0] * tk, tk)
        kp, vp = [kp_ref[:, pl.ds(start, tk)].astype(BF16)], [vp_ref[:, pl.ds(start, tk)].astype(BF16)]
        live = _sb_fold(qh, kp, vp, True, _strict_upper(tk), None, carry_ref, acc_ref, False)
        return st[0] - 1, live

    lax.while_loop(cond, body, (n_left - 1, live))
    o_ref[...] = _unstack_heads(acc_ref[0], tq, HEAD_DIM).astype(o_ref.dtype)


def _sb_attn_prompt(pn4, layer, k_t, v_t):
    _, bn, s, _ = pn4.shape
    blk = (None, ATT_BLK, LANES)
    own = pl.BlockSpec((None, None, LANES, ATT_BLK), lambda b, hp, qi: (layer, b, hp, qi))
    older = pl.BlockSpec((None, None, LANES, s), lambda b, hp, qi: (layer, b, hp, 0))
    return pl.pallas_call(
        functools.partial(_sb_kernel, ATT_BLK, None, False),
        grid=(bn, B_HEADS // 2, s // ATT_BLK),
        in_specs=[_pspec(blk, lambda b, hp, qi: (b, qi, _CBP["qb"] + hp), layer), own, own, older, older],
        out_specs=pl.BlockSpec(blk, lambda b, hp, qi: (b, qi, hp)),
        out_shape=jax.ShapeDtypeStruct((bn, s, BRANCH_W), BF16),
        scratch_shapes=[pltpu.VMEM((1, 2 * ATT_BLK, LANES), F32), pltpu.VMEM((1, 2 * ATT_BLK, LANES), F32)],
        compiler_params=_cparams("parallel", "parallel", "arbitrary"),
        name="stick_attn_prompt",
    )(pn4, k_t, v_t, k_t, v_t)


def _sb_attn_sample(p4, layer, k_cache_t, v_cache_t):
    _, bn, t, _ = p4.shape
    past = k_cache_t.shape[-1]
    new = lambda name: _pspec((None, t, BRANCH_W), lambda b: (b, 0, _CB[name] * LANES // BRANCH_W), layer)
    older = pl.BlockSpec((None, None, BRANCH_W, past), lambda b: (layer, b, 0, 0))
    rows = B_HEADS * t
    return pl.pallas_call(
        functools.partial(_sb_kernel, ATT_BLK, past // ATT_BLK, True),
        grid=(bn,),
        in_specs=[new("qb"), new("kb"), new("vb"), older, older],
        out_specs=pl.BlockSpec((None, t, BRANCH_W), lambda b: (b, 0, 0)),
        out_shape=jax.ShapeDtypeStruct((bn, t, BRANCH_W), BF16),
        scratch_shapes=[pltpu.VMEM((1, rows, LANES), F32), pltpu.VMEM((1, rows, BRANCH_W), F32)],
        compiler_params=_cparams("parallel"),
        name="stick_attn_sample",
    )(p4, p4, p4, k_cache_t, v_cache_t)


def _diff_combine(o0, o1, lam_init, lam_ref, gain_ref):
    lp = lam_ref[...]
    lam = (jnp.exp(jnp.sum(lp[0:1] * lp[1:2], axis=1, keepdims=True))
           - jnp.exp(jnp.sum(lp[2:3] * lp[3:4], axis=1, keepdims=True)) + lam_init)
    return _rms(o0 - lam * o1, gain_ref[...]) * (1.0 - lam_init)


def _diff_fold(qs, k_t, v, bias, m_ref, l_ref, acc_ref, first):
    kb = k_t.astype(BF16)
    vb = v.astype(BF16)
    rows, tk = qs.shape[0], vb.shape[0]
    s = _dot(qs, kb) + bias
    m_cur = jnp.max(s, axis=1, keepdims=True)
    m_new = jnp.broadcast_to(m_cur, (rows, LANES)) if first else jnp.maximum(m_ref[...], m_cur)
    p = jnp.exp2(s - _lane_tile(m_new, tk // LANES))
    row = jnp.sum(p, axis=1, keepdims=True)
    pv = _dot(p.astype(BF16), vb)
    if first:
        l_ref[...] = jnp.broadcast_to(row, (rows, LANES))
        acc_ref[...] = pv
    else:
        alpha = jnp.exp2(m_ref[...] - m_new)
        l_ref[...] = alpha * l_ref[...] + row
        acc_ref[...] = alpha * acc_ref[...] + pv
    m_ref[...] = m_new


def _diff_prompt_kernel(lam_init, tbl_ref, lam_ref, gain_ref, q_ref, kd_ref, vd_ref, kp_ref, vp_ref,
                        bd_ref, bp_ref, o_ref, m_ref, l_ref, acc_ref):
    h = pl.program_id(1)
    qi = pl.program_id(2)
    q = q_ref[...] * SOFTMAX_QSCALE
    blk = q.shape[0]
    qm = _stack_heads(q, HEAD_DIM)
    far = tbl_ref[T5_FAR_BUCKET * C_HEADS + h] * LOG2E

    near = pl.multiple_of(jnp.maximum(qi - 1, 0) * blk, blk)
    k0 = jnp.concatenate([kp_ref[:, pl.ds(near, blk)], kd_ref[...]], axis=1)
    v0 = jnp.concatenate([vp_ref[pl.ds(near, blk), :], vd_ref[...]], axis=0)
    b0 = jnp.concatenate([jnp.where(qi >= 1, bp_ref[...], NEG_INF), bd_ref[...]], axis=1)
    b0 = jnp.concatenate([b0, b0], axis=0)
    _diff_fold(qm, k0, v0, b0, m_ref, l_ref, acc_ref, True)

    n_far = jnp.maximum(qi - 1, 0)
    n_group = n_far // DIFF_FAR_GROUP
    wide = DIFF_FAR_GROUP * blk

    def body(j, c):
        start = pl.multiple_of(j * wide, wide)
        _diff_fold(qm, kp_ref[:, pl.ds(start, wide)], vp_ref[pl.ds(start, wide), :], far,
                   m_ref, l_ref, acc_ref, False)
        return c

    lax.fori_loop(0, n_group, body, 0)
    rest = pl.multiple_of(n_group * wide, wide)
    for r in range(1, DIFF_FAR_GROUP):
        @pl.when(n_far - n_group * DIFF_FAR_GROUP == r)
        def _(r=r):
            _diff_fold(qm, kp_ref[:, pl.ds(rest, r * blk)], vp_ref[pl.ds(rest, r * blk), :], far,
                       m_ref, l_ref, acc_ref, False)

    o = acc_ref[...] / l_ref[...]
    o = _diff_combine(o[:blk], o[blk:], lam_init, lam_ref, gain_ref)
    o_ref[...] = o.astype(o_ref.dtype)


def _diff_sample_kernel(lam_init, lam_ref, gain_ref, q_ref, kc_ref, vc_ref, kn_ref, vn_ref,
                        bc_ref, bn_ref, o_ref):
    q_all = q_ref[...] * SOFTMAX_QSCALE
    past = kc_ref.shape[1]
    k_t, v_t = (True, False), (False, False)
    for h in range(C_HEADS):
        cols = slice(h * LANES, (h + 1) * LANES)
        q = q_all[:, cols]
        kb = [kc_ref[cols, :].astype(BF16), kn_ref[:, cols].astype(BF16)]
        vb = [vc_ref[pl.ds(h, past, stride=C_HEADS), :].astype(BF16), vn_ref[:, cols].astype(BF16)]
        biases = [jnp.concatenate([b[h], b[h]], axis=0) for b in (bc_ref, bn_ref)]
        t = q.shape[0]
        o = _softmax_segments(_stack_heads(q, HEAD_DIM), kb, vb, biases, k_t, v_t)
        o_ref[:, cols] = _diff_combine(o[:t], o[t:], lam_init, lam_ref, gain_ref).astype(o_ref.dtype)


def _diff_attn_prompt(pn4, layer, k_t, t5tiles, lam_init, t5_flat, lam_params, gain):
    _, bn, s, _ = pn4.shape
    blk = (None, ATT_BLK, LANES)
    full = (None, s, LANES)
    state = pltpu.VMEM((2 * ATT_BLK, LANES), F32)
    return pl.pallas_call(
        functools.partial(_diff_prompt_kernel, lam_init),
        grid=(bn, C_HEADS, s // ATT_BLK),
        in_specs=[pl.BlockSpec(memory_space=pltpu.SMEM),
                  pl.BlockSpec((4, HEAD_DIM), lambda b, h, qi: (0, 0)),
                  pl.BlockSpec((1, C_VDIM), lambda b, h, qi: (0, 0)),
                  _pspec(blk, lambda b, h, qi: (b, qi, _CBP["qc"] + h), layer),
                  pl.BlockSpec((None, None, LANES, ATT_BLK), lambda b, h, qi: (layer, b, h, qi)),
                  _pspec(blk, lambda b, h, qi: (b, qi, _CBP["vc"] + h), layer),
                  pl.BlockSpec((None, None, LANES, s), lambda b, h, qi: (layer, b, h, 0)),
                  _pspec(full, lambda b, h, qi: (b, 0, _CBP["vc"] + h), layer),
                  pl.BlockSpec((None, None, ATT_BLK, ATT_BLK), lambda b, h, qi: (h, 0, 0, 0)),
                  pl.BlockSpec((None, None, ATT_BLK, ATT_BLK), lambda b, h, qi: (h, 1, 0, 0))],
        out_specs=pl.BlockSpec(blk, lambda b, h, qi: (b, qi, h)),
        out_shape=jax.ShapeDtypeStruct((bn, s, BRANCH_W), BF16),
        scratch_shapes=[state, state, state],
        compiler_params=_cparams("parallel", "parallel", "arbitrary"),
        name="diff_attn_prompt",
    )(t5_flat, lam_params, gain.reshape(1, C_VDIM), pn4, k_t, pn4, k_t, pn4, t5tiles, t5tiles)


def _diff_attn_sample(p4, layer, k_cache_t, v_cache, bias_cache, bias_new, lam_init, lam_params, gain):
    _, bn, t, _ = p4.shape
    past = k_cache_t.shape[-1]
    new = lambda name: _pspec((None, t, BRANCH_W), lambda b: (b, 0, _CB[name] * LANES // BRANCH_W), layer)
    return pl.pallas_call(
        functools.partial(_diff_sample_kernel, lam_init),
        grid=(bn,),
        in_specs=[pl.BlockSpec((4, HEAD_DIM), lambda b: (0, 0)),
                  pl.BlockSpec((1, C_VDIM), lambda b: (0, 0)),
                  new("qc"),
                  pl.BlockSpec((None, None, BRANCH_W, past), lambda b: (layer, b, 0, 0)),
                  pl.BlockSpec((None, None, past * C_HEADS, C_VDIM), lambda b: (layer, b, 0, 0)),
                  new("kc"), new("vc"),
                  pl.BlockSpec((C_HEADS, t, past), lambda b: (0, 0, 0)),
                  pl.BlockSpec((C_HEADS, t, t), lambda b: (0, 0, 0))],
        out_specs=pl.BlockSpec((None, t, BRANCH_W), lambda b: (b, 0, 0)),
        out_shape=jax.ShapeDtypeStruct((bn, t, BRANCH_W), BF16),
        compiler_params=_cparams("parallel"),
        name="diff_attn_sample",
    )(lam_params, gain.reshape(1, C_VDIM), p4, k_cache_t, v_cache, p4, p4, bias_cache, bias_new)


def _merge_kernel(x_ref, oa_ref, ob_ref, oc_ref, g0_ref, g1_ref, g2_ref, bg_ref, wb_ref, wo_ref, o_ref):
    h = None
    for n, (o_r, g_r) in enumerate(((oa_ref, g0_ref), (ob_ref, g1_ref), (oc_ref, g2_ref))):
        gate = 1.0 / (1.0 + jnp.exp(-(g_r[...] + bg_ref[n:n + 1, :])))
        t = gate * _dot(o_r[...], wb_ref[n])
        h = t if h is None else h + t
    o_ref[...] = x_ref[...] + _dot(h.astype(BF16), wo_ref[...])


def _merge(x, o_a, o_b, o_c, p, layer, b_gate, w_branch, w_out, tm):
    t, d = x.shape
    row = lambda i: (i, 0)
    fixed2 = lambda i: (0, 0)
    return pl.pallas_call(
        _merge_kernel,
        grid=(t // tm,),
        in_specs=[pl.BlockSpec((tm, d), row),
                  pl.BlockSpec((tm, BRANCH_W), row),
                  pl.BlockSpec((tm, BRANCH_W), row),
                  pl.BlockSpec((tm, BRANCH_W), row),
                  pl.BlockSpec((None, tm, d), lambda i: (layer, i, 0)),
                  pl.BlockSpec((None, tm, d), lambda i: (layer, i, 1)),
                  pl.BlockSpec((None, tm, d), lambda i: (layer, i, 2)),
                  pl.BlockSpec((N_BRANCH, d), fixed2),
                  pl.BlockSpec((N_BRANCH, BRANCH_W, d), lambda i: (0, 0, 0)),
                  pl.BlockSpec((d, d), fixed2)],
        out_specs=pl.BlockSpec((tm, d), row),
        out_shape=jax.ShapeDtypeStruct((t, d), F32),
        compiler_params=_cparams("parallel"),
        name="gated_merge",
    )(x, o_a, o_b, o_c, p, p, p, b_gate, w_branch, w_out)


FFN_TF = D_FF // 2
FFN_PREV = 16


def _gelu_tanh(x):
    return x * (0.5 * (1.0 + jnp.tanh(math.sqrt(2.0 / math.pi) * (x + 0.044715 * (x * x * x)))))


def _ffn_kernel(tiles_per_seq, final, x_ref, xp_ref, gn_ref, wg_ref, wu_ref, cw_ref, cb_ref, wd_ref, st_ref, gf_ref,
                o_ref, so_ref, xn_ref, xpn_ref):
    i = pl.program_id(0)
    j = pl.program_id(1)
    whole_seqs = tiles_per_seq == 0

    @pl.when(j == 0)
    def _():
        xn_ref[...] = _rms(x_ref[...], gn_ref[...]).astype(BF16)
        if not whole_seqs:
            xpn_ref[...] = _rms(xp_ref[...], gn_ref[...]).astype(BF16)

    xn = xn_ref[...]
    tm = xn.shape[0]
    tf = wg_ref.shape[1]
    g = _dot(xn, wg_ref[...])
    u = _dot(xn, wu_ref[...])
    if whole_seqs:
        ns = st_ref.shape[0]
        st = st_ref[...]
        g = g.reshape(ns, tm // ns, tf)
        u = u.reshape(ns, tm // ns, tf)
        pm2, pm1 = st[:, 0:1, :], st[:, 1:2, :]
    else:
        gp = _dot(xpn_ref[...], wg_ref[...])
        first = (i % tiles_per_seq) == 0
        st = st_ref[0]
        pm2 = jnp.where(first, st[0:1, :], gp[FFN_PREV - 2:FFN_PREV - 1, :])
        pm1 = jnp.where(first, st[1:2, :], gp[FFN_PREV - 1:FFN_PREV, :])
    ax = g.ndim - 2
    row = lax.broadcasted_iota(jnp.int32, g.shape, ax)
    g1 = jnp.where(row == 0, pm1, pltpu.roll(g, 1, ax))
    g2 = jnp.where(row == 0, pm2, jnp.where(row == 1, pm1, pltpu.roll(g, 2, ax)))
    cw = cw_ref[...]
    c = cb_ref[...] + cw[0:1, :] * g2 + cw[1:2, :] * g1 + cw[2:3, :] * g
    hid = (_gelu_tanh(c) * u).reshape(tm, tf).astype(BF16)
    contrib = _dot(hid, wd_ref[...])

    last = pl.num_programs(1) - 1
    finish = (lambda y: _rms(y, gf_ref[...])) if final else (lambda y: y)

    @pl.when(j == 0)
    def _():
        o_ref[...] = x_ref[...] + contrib

    @pl.when(jnp.logical_and(j > 0, j < last))
    def _():
        o_ref[...] += contrib

    @pl.when(jnp.logical_and(j > 0, j == last))
    def _():
        o_ref[...] = finish(o_ref[...] + contrib)

    seq_len = g.shape[ax]
    if whole_seqs:
        so_ref[...] = g[:, seq_len - 2:seq_len, :]
    else:
        so_ref[0] = g[seq_len - 2:seq_len, :]


def _conv_ffn(x, g_norm, w_gate, w_up, conv_w, conv_b, w_down, state, seq_len, tm, g_final, final):
    t, d = x.shape
    nseq = t // seq_len
    nf = D_FF // FFN_TF
    assert nf > 1
    if seq_len >= tm:
        tiles_per_seq = seq_len // tm
        st_spec = pl.BlockSpec((1, 2, FFN_TF), lambda i, j: (i // tiles_per_seq, 0, j))
    else:
        tiles_per_seq = 0
        st_spec = pl.BlockSpec((tm // seq_len, 2, FFN_TF), lambda i, j: (i, 0, j))
    prev_blocks = tm // FFN_PREV
    tail_spec = pl.BlockSpec(st_spec.block_shape, lambda i, j: (i, 0, j))
    n_tail = t // tm * st_spec.block_shape[0]
    y, tails = pl.pallas_call(
        functools.partial(_ffn_kernel, tiles_per_seq, final),
        grid=(t // tm, nf),
        in_specs=[pl.BlockSpec((tm, d), lambda i, j: (i, 0)),
                  pl.BlockSpec((FFN_PREV, d), lambda i, j: (jnp.maximum(i * prev_blocks - 1, 0), 0)),
                  pl.BlockSpec((1, d), lambda i, j: (0, 0)),
                  pl.BlockSpec((d, FFN_TF), lambda i, j: (0, j)),
                  pl.BlockSpec((d, FFN_TF), lambda i, j: (0, j)),
                  pl.BlockSpec((3, FFN_TF), lambda i, j: (0, j)),
                  pl.BlockSpec((1, FFN_TF), lambda i, j: (0, j)),
                  pl.BlockSpec((FFN_TF, d), lambda i, j: (j, 0)),
                  st_spec,
                  pl.BlockSpec((1, d), lambda i, j: (0, 0))],
        out_specs=[pl.BlockSpec((tm, d), lambda i, j: (i, 0)), tail_spec],
        out_shape=[jax.ShapeDtypeStruct((t, d), F32),
                   jax.ShapeDtypeStruct((n_tail, 2, D_FF), F32)],
        scratch_shapes=[pltpu.VMEM((tm, d), BF16), pltpu.VMEM((FFN_PREV, d), BF16)],
        compiler_params=_cparams("arbitrary", "arbitrary"),
        name="conv_ffn_prompt" if tiles_per_seq else "conv_ffn_sample",
    )(x, x, g_norm.reshape(1, d), w_gate, w_up, conv_w, conv_b.reshape(1, D_FF), w_down, state, g_final.reshape(1, d))
    return y, tails[n_tail // nseq - 1::n_tail // nseq]


PROJ_TM, PROJ_TN = 1024, 1280
ROW_TM = 512


def _lam_init(layer):
    return 0.8 - 0.6 * math.exp(-0.3 * layer)


def _cols(p4, name):
    c0 = _CB[name] * LANES
    return p4[..., c0:c0 + BRANCH_W]


def _layer_tail(x2, o_a, o_b, o_c, p_rows, layer, prm, conv_state, seq_len):
    flat = lambda a: a.reshape(x2.shape[0], BRANCH_W)
    x2 = _merge(x2, flat(o_a), flat(o_b), flat(o_c), p_rows, layer, prm["b_gate"][layer],
                prm["w_branch"][layer], prm["w_out"][layer], ROW_TM)
    depth = prm["norm_ffn"].shape[0]
    return _conv_ffn(x2, prm["norm_ffn"][layer], prm["w_gate"][layer], prm["w_up"][layer],
                     prm["conv_w"][layer], prm["conv_b"][layer], prm["w_down"][layer], conv_state, seq_len, ROW_TM,
                     prm["norm_final"], layer == depth - 1)


def _trunk_prompt(x, prm, abias, t5tiles, t5_flat):
    bn, s, d = x.shape
    x2 = x.reshape(bn * s, d)
    depth = prm["w_rows"].shape[0]
    conv_states = []
    rows, kv_t = None, None
    for l in range(depth):
        rows = _norm_proj(x2, prm["norm_mix"][l], prm["w_rows"][l], PROJ_TM, PROJ_TN, l, depth, rows,
                          heads_group=(_CBP["vc"] * LANES, C_HEADS))
        p_rows, vc_heads = rows
        kv_t = _norm_proj_t(x2, prm["norm_mix"][l], prm["w_feat_t"][l], PROJ_TM, l, depth, bn, kv_t)
        ka_t, va_t, kb_t, vb_t, kc_t = kv_t
        pn4 = p_rows.reshape(depth, bn, s, ROW_COLS)
        o_a = _band_attn_prompt(pn4, l, ka_t, va_t, abias[l])
        o_b = _sb_attn_prompt(pn4, l, kb_t, vb_t)
        o_c = _diff_attn_prompt(pn4, l, kc_t, t5tiles, _lam_init(l), t5_flat, prm["c_lambda"][l], prm["c_subln"][l])
        x2, st_conv = _layer_tail(x2, o_a, o_b, o_c, p_rows, l, prm, jnp.zeros((bn, 2, D_FF), F32), s)
        conv_states.append(st_conv)
    y = x2.reshape(bn, s, d)
    heads_t = lambda a: jnp.transpose(a.reshape(depth, bn, A_HEADS, HEAD_DIM, a.shape[-1]), (0, 1, 4, 2, 3))
    keep = min(A_PAST_CHUNKS * CHUNK, s)
    states = (heads_t(ka_t[..., s - keep:]), heads_t(va_t[..., s - keep:]), heads_t(kb_t), heads_t(vb_t),
              jnp.transpose(kc_t.reshape(depth, bn, C_HEADS, 2, HEAD_DIM, s), (0, 1, 5, 2, 3, 4)),
              vc_heads.reshape(depth, bn, s, C_HEADS, C_VDIM),
              jnp.stack(conv_states, axis=0))
    return y, states


def _feature_major(cache):
    nd = cache.ndim
    t = jnp.transpose(cache, (0, 1) + tuple(range(3, nd)) + (2,))
    return t.reshape(cache.shape[0], cache.shape[1], BRANCH_W, cache.shape[2])


def _trunk_sample(x, caches, prm, abias, t5tiles):
    bn, t, d = x.shape
    x2 = x.reshape(bn * t, d)
    depth = prm["w_in"].shape[0]
    ca_k, ca_v, cb_k, cb_v, cc_k, cc_v, st_conv_in = caches
    ca_kt, ca_vt, cb_kt, cb_vt, cc_kt = (_feature_major(c) for c in (ca_k, ca_v, cb_k, cb_v, cc_k))
    keep, past = ca_k.shape[2], cc_k.shape[2]
    cc_vr = cc_v.reshape(depth, bn, past * C_HEADS, C_VDIM)
    conv_states = []
    p_all = None
    for l in range(depth):
        p_all, = _norm_proj(x2, prm["norm_mix"][l], prm["w_in"][l], PROJ_TM, PROJ_TN, l, depth,
                            None if p_all is None else [p_all])
        p4 = p_all.reshape(depth, bn, t, IN_COLS)
        bias_cache = jnp.concatenate([abias[l][:, 0, :t, :], abias[l][:, 1, :t, :]], axis=-1)[..., -keep:]
        o_a = _band_attn_sample(p4, l, ca_kt, ca_vt, bias_cache, abias[l][:, 2, :t, :t])
        o_b = _sb_attn_sample(p4, l, cb_kt, cb_vt)
        far = jnp.broadcast_to(t5tiles[:, 1, :1, :1], (C_HEADS, t, past - ATT_BLK))
        bias_c = jnp.concatenate([far, t5tiles[:, 1, :t, :]], axis=-1)
        o_c = _diff_attn_sample(p4, l, cc_kt, cc_vr, bias_c, t5tiles[:, 0, :t, :t],
                                _lam_init(l), prm["c_lambda"][l], prm["c_subln"][l])
        x2, st_conv = _layer_tail(x2, o_a, o_b, o_c, p_all, l, prm, st_conv_in[l], t)
        conv_states.append(st_conv)
    y = x2.reshape(bn, t, d)
    ka, va, kb, vb, kc, vc = (_cols(p4, n) for n in ("ka", "va", "kb", "vb", "kc", "vc"))
    heads = lambda a: a.reshape(a.shape[:3] + (A_HEADS, HEAD_DIM))
    states = (jnp.concatenate([ca_k[:, :, t:], heads(ka)], axis=2),
              jnp.concatenate([ca_v[:, :, t:], heads(va)], axis=2), heads(kb), heads(vb),
              kc.reshape(depth, bn, t, C_HEADS, 2, HEAD_DIM), vc.reshape(depth, bn, t, C_HEADS, C_VDIM),
              jnp.stack(conv_states, axis=0))
    return y, states


def kernel(x_prompt, x_sample, cache_a_k, cache_a_v, cache_b_k, cache_b_v, cache_c_k, cache_c_v,
           state_ffn_conv, norm_mix, w_in, b_gate, a_rel_bias, t5_bias, c_lambda, c_subln,
           w_branch, w_out, norm_ffn, w_up, conv_w, conv_b, w_down, norm_final):
    group = {name: w_in[..., i * BRANCH_W:(i + 1) * BRANCH_W] for i, name in enumerate(_GROUPS)}
    gate = w_in[..., len(_GROUPS) * BRANCH_W:]
    prm = dict(
        norm_mix=norm_mix,
        w_in=jnp.concatenate([gate] + [group[n] for n in _GROUPS], axis=-1).astype(BF16),
        w_rows=jnp.concatenate([gate] + [group[n] for n in _ROW_GROUPS], axis=-1).astype(BF16),
        w_feat_t=jnp.swapaxes(jnp.concatenate([group[n] for n in _T_GROUPS], axis=-1), 1, 2).astype(BF16),
        b_gate=b_gate, c_lambda=c_lambda, c_subln=c_subln,
        w_branch=w_branch.astype(BF16), w_out=w_out.astype(BF16), norm_ffn=norm_ffn,
        w_gate=w_up[..., :D_FF].astype(BF16), w_up=w_up[..., D_FF:].astype(BF16),
        conv_w=conv_w, conv_b=conv_b, w_down=w_down.astype(BF16), norm_final=norm_final)
    abias = [_build_abias(a_rel_bias[l]) for l in range(a_rel_bias.shape[0])]
    t5tiles = _build_t5bias(t5_bias)
    y_p, p_states = _trunk_prompt(x_prompt, prm, abias, t5tiles, t5_bias.reshape(-1))
    caches = (cache_a_k, cache_a_v, cache_b_k, cache_b_v, cache_c_k, cache_c_v, state_ffn_conv)
    y_s, s_states = _trunk_sample(x_sample, caches, prm, abias, t5tiles)
    return (y_p, y_s) + p_states + s_states
```

```python
import functools
import math

import jax
import jax.numpy as jnp
from jax import lax
from jax.experimental import pallas as pl
from jax.experimental.pallas import tpu as pltpu

F32 = jnp.float32
BF16 = jnp.bfloat16

D_MODEL = 1024
CHUNK = 64
HEAD_DIM = 64
A_HEADS = 8
A_PAST_CHUNKS = 8
A_REL_CLIP = 128
B_HEADS = 8
C_HEADS = 4
C_VDIM = 128
T5_BUCKETS = 32
D_FF = 2816
N_BRANCH = 3
BRANCH_W = 512
GATE_W = N_BRANCH * D_MODEL
IN_COLS = 9 * BRANCH_W + GATE_W
EPS = 1e-6
NEG_INF = -1e30
SCALE = HEAD_DIM ** -0.5
LOG2E = math.log2(math.e)
SOFTMAX_QSCALE = SCALE * LOG2E

LANES = 128
_GROUPS = ("qa", "ka", "va", "qb", "kb", "vb", "qc", "kc", "vc")
_CB = {name: (GATE_W + i * BRANCH_W) // LANES for i, name in enumerate(_GROUPS)}
_ROW_GROUPS = ("qa", "qb", "qc", "vc")
_T_GROUPS = ("ka", "va", "kb", "vb", "kc")
_CBP = {name: (GATE_W + i * BRANCH_W) // LANES for i, name in enumerate(_ROW_GROUPS)}
ROW_COLS = GATE_W + len(_ROW_GROUPS) * BRANCH_W

ATT_BLK = 256
ATT_HEADS = 4
SB_LOG_CUT = -104.0
T5_FAR_BUCKET = 15
DIFF_FAR_GROUP = 4
V7X_VMEM_BYTES = 64 * 1024 * 1024
VMEM_LIMIT = V7X_VMEM_BYTES - 8 * 1024 * 1024


def _cparams(*sem):
    return pltpu.CompilerParams(dimension_semantics=sem, vmem_limit_bytes=VMEM_LIMIT)


def _rms(x, g):
    return x * lax.rsqrt(jnp.mean(x * x, axis=-1, keepdims=True) + EPS) * g


def _dot(a, b):
    return jnp.dot(a, b, preferred_element_type=F32)


def _dot_nt(a, b):
    return lax.dot_general(a, b, (((1,), (1,)), ((), ())), preferred_element_type=F32)


def _norm_proj_kernel(n_stack, heads_cols, x_ref, g_ref, w_ref, *refs):
    outs, xn_ref = refs[n_stack:-1], refs[-1]
    j = pl.program_id(1)

    @pl.when(j == 0)
    def _():
        xn_ref[...] = _rms(x_ref[...], g_ref[...]).astype(BF16)

    res = _dot(xn_ref[...], w_ref[...])
    outs[0][...] = res
    if heads_cols is not None:
        tile, first, heads = heads_cols

        @pl.when(j == tile)
        def _():
            for h in range(heads):
                lo = first + h * LANES
                outs[1][pl.ds(h, res.shape[0], stride=heads), :] = res[:, lo:lo + LANES]


def _norm_proj(x, g, w, tm, tn, layer, depth, stacked, heads_group=None):
    t, d = x.shape
    n = w.shape[1]
    in_specs = [pl.BlockSpec((tm, d), lambda i, j: (i, 0)),
                pl.BlockSpec((1, d), lambda i, j: (0, 0)),
                pl.BlockSpec((d, tn), lambda i, j: (0, j))]
    args = [x, g.reshape(1, d), w]
    out_specs = [pl.BlockSpec((None, tm, tn), lambda i, j: (layer, i, j))]
    out_shape = [jax.ShapeDtypeStruct((depth, t, n), F32)]
    heads_cols = None
    if heads_group is not None:
        first, heads = heads_group
        heads_cols = (first // tn, first % tn, heads)
        out_specs.append(pl.BlockSpec((None, tm * heads, LANES), lambda i, j: (layer, i, 0)))
        out_shape.append(jax.ShapeDtypeStruct((depth, t * heads, LANES), F32))
    n_stack = 0 if stacked is None else len(out_shape)
    if stacked is not None:
        in_specs += [pl.BlockSpec(memory_space=pl.ANY)] * n_stack
        args += list(stacked)
    return pl.pallas_call(
        functools.partial(_norm_proj_kernel, n_stack, heads_cols),
        grid=(t // tm, n // tn),
        in_specs=in_specs,
        out_specs=out_specs,
        out_shape=out_shape,
        scratch_shapes=[pltpu.VMEM((tm, d), BF16)],
        input_output_aliases={3 + k: k for k in range(n_stack)},
        compiler_params=_cparams("arbitrary", "arbitrary"),
        name="norm_proj",
    )(*args)


def _norm_proj_t_kernel(ngroups, x_ref, g_ref, wt_ref, *refs):
    o_refs = refs[-ngroups:]
    xn = _rms(x_ref[...], g_ref[...]).astype(BF16)
    for g in range(ngroups):
        o_refs[g][...] = _dot_nt(wt_ref[g * BRANCH_W:(g + 1) * BRANCH_W, :], xn)


def _norm_proj_t(x, g, wt, tm, layer, depth, bn, stacked):
    t, d = x.shape
    s = t // bn
    ngroups = wt.shape[0] // BRANCH_W
    tiles = s // tm
    in_specs = [pl.BlockSpec((tm, d), lambda i: (i, 0)),
                pl.BlockSpec((1, d), lambda i: (0, 0)),
                pl.BlockSpec(wt.shape, lambda i: (0, 0))]
    args = [x, g.reshape(1, d), wt]
    aliases = {}
    if stacked is not None:
        in_specs += [pl.BlockSpec(memory_space=pl.ANY)] * ngroups
        args += list(stacked)
        aliases = {3 + n: n for n in range(ngroups)}
    out_spec = pl.BlockSpec((None, None, BRANCH_W, tm), lambda i: (layer, i // tiles, 0, i % tiles))
    return pl.pallas_call(
        functools.partial(_norm_proj_t_kernel, ngroups),
        grid=(t // tm,),
        in_specs=in_specs,
        out_specs=[out_spec] * ngroups,
        out_shape=[jax.ShapeDtypeStruct((depth, bn, BRANCH_W, s), F32)] * ngroups,
        input_output_aliases=aliases,
        compiler_params=_cparams("arbitrary"),
        name="norm_proj_t",
    )(*args)


def _abias_kernel(tbl_ref, o_ref):
    h = pl.program_id(0)
    sub = 8
    ql = lax.broadcasted_iota(jnp.int32, (ATT_BLK, ATT_BLK), 0)
    kl = lax.broadcasted_iota(jnp.int32, (ATT_BLK, ATT_BLK), 1)
    far = jnp.full((ATT_BLK, ATT_BLK), tbl_ref[2 * A_REL_CLIP * A_HEADS + h] * LOG2E, F32)
    o_ref[0, 0] = jnp.where((kl >> 6) >= (ql >> 6), far, NEG_INF)
    lead = ATT_BLK - sub
    b = lax.broadcasted_iota(jnp.int32, (sub, 2 * ATT_BLK), 0)
    x = lax.broadcasted_iota(jnp.int32, (sub, 2 * ATT_BLK), 1)
    for seg, off in ((1, ATT_BLK), (2, 0)):
        idx = jnp.clip(b + lead - x + off, -A_REL_CLIP, A_REL_CLIP) + A_REL_CLIP

        def body(r, acc, idx=idx):
            return jnp.where(idx == r, tbl_ref[r * A_HEADS + h] * LOG2E, acc)

        strip = lax.fori_loop(0, 2 * A_REL_CLIP + 1, body, jnp.zeros((sub, 2 * ATT_BLK), F32))
        for a in range(ATT_BLK // sub):
            o_ref[0, seg, a * sub:(a + 1) * sub, :] = strip[:, lead - a * sub:lead - a * sub + ATT_BLK]
    o_ref[0, 2] = jnp.where((kl >> 6) <= (ql >> 6), o_ref[0, 2], NEG_INF)


def _build_abias(table):
    return pl.pallas_call(
        _abias_kernel,
        grid=(A_HEADS,),
        in_specs=[pl.BlockSpec(memory_space=pltpu.SMEM)],
        out_specs=pl.BlockSpec((1, 3, ATT_BLK, ATT_BLK), lambda h: (h, 0, 0, 0)),
        out_shape=jax.ShapeDtypeStruct((A_HEADS, 3, ATT_BLK, ATT_BLK), F32),
        compiler_params=_cparams("parallel"),
        name="band_bias_tiles",
    )(table.reshape(-1))


_T5_LARGE_STEPS = (12, 16, 23, 32, 46, 64, 91)


def _t5bias_kernel(tbl_ref, o_ref):
    h = pl.program_id(0)
    rows = 64
    for seg, off in ((0, 0), (1, -ATT_BLK)):
        for rc in range(ATT_BLK // rows):
            ql = lax.broadcasted_iota(jnp.int32, (rows, ATT_BLK), 0) + rc * rows
            kl = lax.broadcasted_iota(jnp.int32, (rows, ATT_BLK), 1)
            rel = kl - ql + off
            n = jnp.abs(rel)
            large = jnp.full((rows, ATT_BLK), T5_BUCKETS // 4, jnp.int32)
            for th in _T5_LARGE_STEPS:
                large = large + jnp.where(n >= th, 1, 0)
            bucket = jnp.where(rel > 0, T5_BUCKETS // 2, 0) + jnp.where(n < T5_BUCKETS // 4, n, large)
            tile = jnp.zeros((rows, ATT_BLK), F32)
            for r in range(T5_BUCKETS):
                tile = jnp.where(bucket == r, tbl_ref[r * C_HEADS + h] * LOG2E, tile)
            if seg == 0:
                tile = jnp.where((kl >> 6) <= (ql >> 6), tile, NEG_INF)
            o_ref[0, seg, rc * rows:(rc + 1) * rows, :] = tile


def _build_t5bias(table):
    return pl.pallas_call(
        _t5bias_kernel,
        grid=(C_HEADS,),
        in_specs=[pl.BlockSpec(memory_space=pltpu.SMEM)],
        out_specs=pl.BlockSpec((1, 2, ATT_BLK, ATT_BLK), lambda h: (h, 0, 0, 0)),
        out_shape=jax.ShapeDtypeStruct((C_HEADS, 2, ATT_BLK, ATT_BLK), F32),
        compiler_params=_cparams("parallel"),
        name="t5_bias_tiles",
    )(table.reshape(-1))


def _pspec(block, index_map, layer):
    return pl.BlockSpec((None,) + block, lambda *g: (layer,) + tuple(index_map(*g)))


def _lane_tile(x, n):
    return x if n == 1 else jnp.concatenate([x] * n, axis=1)


def _stack_heads(q, width):
    group = lax.broadcasted_iota(jnp.int32, q.shape, 1) // width
    return jnp.concatenate([jnp.where(group == h, q, 0.0) for h in range(q.shape[1] // width)],
                           axis=0).astype(BF16)


def _unstack_heads(x, t, width):
    group = lax.broadcasted_iota(jnp.int32, (t, x.shape[1]), 1) // width
    out = jnp.zeros((t, x.shape[1]), x.dtype)
    for h in range(x.shape[1] // width):
        out = jnp.where(group == h, x[h * t:(h + 1) * t, :], out)
    return out


def _qk(q, k, k_t):
    return _dot(q, k) if k_t else _dot_nt(q, k)


def _pv(p, v, v_t):
    return _dot_nt(p, v) if v_t else _dot(p, v)


def _softmax_segments(qh, kb, vb, biases, k_t, v_t):
    s = [_qk(qh, k, t) + b for k, b, t in zip(kb, biases, k_t)]
    m = functools.reduce(jnp.maximum, [jnp.max(sj, axis=1, keepdims=True) for sj in s])
    p = [jnp.exp2(sj - m) for sj in s]
    l = functools.reduce(jnp.add, [jnp.sum(pj, axis=1, keepdims=True) for pj in p])
    acc = functools.reduce(jnp.add, [_pv(pj.astype(BF16), v, t) for pj, v, t in zip(p, vb, v_t)])
    return acc / l


def _band_kernel(nseg, min_qi, q_ref, *refs):
    k_refs, v_refs, b_refs = refs[:nseg], refs[nseg:2 * nseg], refs[2 * nseg:3 * nseg]
    o_ref = refs[3 * nseg]
    qi = pl.program_id(2)
    q = q_ref[...] * SOFTMAX_QSCALE
    tq = q.shape[0]
    kb = [k_refs[j][...].astype(BF16) for j in range(nseg)]
    vb = [v_refs[j][...].astype(BF16) for j in range(nseg)]
    qs = _stack_heads(q, HEAD_DIM)
    biases = []
    for j in range(nseg):
        tile = b_refs[j][...].reshape(b_refs[j].shape[0] * tq, b_refs[j].shape[-1])
        biases.append(tile if min_qi[j] == 0 else jnp.where(qi >= min_qi[j], tile, NEG_INF))
    o = _softmax_segments(qs, kb, vb, biases, (True,) * nseg, (True,) * nseg)
    o_ref[...] = _unstack_heads(o, tq, HEAD_DIM).astype(o_ref.dtype)


def _band_attn_prompt(pn4, layer, k_t, v_t, abias):
    _, bn, s, _ = pn4.shape
    nq = s // ATT_BLK
    width = ATT_HEADS * HEAD_DIM
    blk = (None, ATT_BLK, width)

    def kv_spec(back):
        return pl.BlockSpec((None, None, width, ATT_BLK),
                            lambda b, hg, qi: (layer, b, hg, jnp.maximum(qi - back, 0)))

    def bias_spec(seg):
        return pl.BlockSpec((ATT_HEADS, None, ATT_BLK, ATT_BLK), lambda b, hg, qi: (hg, seg, 0, 0))

    return pl.pallas_call(
        functools.partial(_band_kernel, 3, (2, 1, 0)),
        grid=(bn, A_HEADS // ATT_HEADS, nq),
        in_specs=([_pspec(blk, lambda b, hg, qi: (b, qi, _CBP["qa"] * LANES // width + hg), layer)]
                  + [kv_spec(back) for back in (2, 1, 0)]
                  + [kv_spec(back) for back in (2, 1, 0)]
                  + [bias_spec(seg) for seg in range(3)]),
        out_specs=pl.BlockSpec(blk, lambda b, hg, qi: (b, qi, hg)),
        out_shape=jax.ShapeDtypeStruct((bn, s, BRANCH_W), BF16),
        compiler_params=_cparams("parallel", "parallel", "arbitrary"),
        name="band_attn_prompt",
    )(pn4, k_t, k_t, k_t, v_t, v_t, v_t, abias, abias, abias)


def _band_sample_kernel(q_ref, kc_ref, kn_ref, vc_ref, vn_ref, bc_ref, bn_ref, o_ref):
    t = q_ref.shape[0]
    qs = _stack_heads(q_ref[...] * SOFTMAX_QSCALE, HEAD_DIM)
    kb = [kc_ref[...].astype(BF16), kn_ref[...].astype(BF16)]
    vb = [vc_ref[...].astype(BF16), vn_ref[...].astype(BF16)]
    o = _softmax_segments(qs, kb, vb, [bc_ref[...], bn_ref[...]], (True, False), (True, False))
    o_ref[...] = _unstack_heads(o, t, HEAD_DIM).astype(o_ref.dtype)


def _band_attn_sample(p4, layer, k_cache_t, v_cache_t, bias_cache, bias_new):
    _, bn, t, _ = p4.shape
    keep = k_cache_t.shape[-1]
    new = lambda name: _pspec((None, t, BRANCH_W), lambda b: (b, 0, _CB[name] * LANES // BRANCH_W), layer)
    old = pl.BlockSpec((None, None, BRANCH_W, keep), lambda b: (layer, b, 0, 0))
    return pl.pallas_call(
        _band_sample_kernel,
        grid=(bn,),
        in_specs=[new("qa"), old, new("ka"), old, new("va"),
                  pl.BlockSpec((A_HEADS * t, keep), lambda b: (0, 0)),
                  pl.BlockSpec((A_HEADS * t, t), lambda b: (0, 0))],
        out_specs=pl.BlockSpec((None, t, BRANCH_W), lambda b: (b, 0, 0)),
        out_shape=jax.ShapeDtypeStruct((bn, t, BRANCH_W), BF16),
        compiler_params=_cparams("parallel"),
        name="band_attn_sample",
    )(p4, k_cache_t, p4, v_cache_t, p4, bias_cache.reshape(A_HEADS * t, keep), bias_new.reshape(A_HEADS * t, t))


def _strict_upper(n):
    r = lax.broadcasted_iota(jnp.int32, (n, n), 0)
    c = lax.broadcasted_iota(jnp.int32, (n, n), 1)
    return jnp.where(r > c, 1.0, 0.0).astype(BF16)


def _sb_fold(qh, kb, vb, kv_t, upper, mask, carry_ref, acc_ref, first):
    tq = qh[0].shape[0]
    tk = kb[0].shape[1] if kv_t else kb[0].shape[0]
    cw = upper.shape[0]
    top = None
    for h in range(len(qh)):
        z = _qk(qh[h], kb[0], kv_t)
        soft = jnp.log(1.0 + jnp.exp(-jnp.abs(z)))
        log_keep = -(jnp.maximum(z, 0.0) + soft)
        log_take = jnp.minimum(z, 0.0) - soft
        if mask is not None:
            log_keep = jnp.where(mask, log_keep, 0.0)
        hi = log_keep.astype(BF16)
        lo = (log_keep - hi.astype(F32)).astype(BF16)
        pieces, total = [], None
        for c0 in reversed(range(0, tk, cw)):
            piece = _dot(hi[:, c0:c0 + cw], upper) + _dot(lo[:, c0:c0 + cw], upper)
            pieces.insert(0, piece if total is None else piece + total)
            part = jnp.sum(log_keep[:, c0:c0 + cw], axis=1, keepdims=True)
            total = part if total is None else total + part
        after = pieces[0] if len(pieces) == 1 else jnp.concatenate(pieces, axis=1)
        if not first:
            after = after + _lane_tile(carry_ref[h], tk // LANES)
        w = jnp.exp(log_take + after)
        if mask is not None:
            w = jnp.where(mask, w, 0.0)
        pv = _pv(w.astype(BF16), vb[0], kv_t)
        if first:
            acc_ref[h] = pv
            carry = jnp.broadcast_to(total, (tq, LANES))
        else:
            acc_ref[h] += pv
            carry = carry_ref[h] + total
        carry_ref[h] = carry
        top = jnp.max(carry) if top is None else jnp.maximum(top, jnp.max(carry))
    return (top > SB_LOG_CUT).astype(jnp.int32)


def _sb_kernel(tk, n_past_static, own_rows, q_ref, kd_ref, vd_ref, kp_ref, vp_ref, o_ref, carry_ref, acc_ref):
    n_past = pl.program_id(2) if n_past_static is None else n_past_static
    q = q_ref[...] * SCALE
    tq = q.shape[0]
    rq = lax.broadcasted_iota(jnp.int32, (tq, tq), 0)
    ck = lax.broadcasted_iota(jnp.int32, (tq, tq), 1)
    qh = [_stack_heads(q, HEAD_DIM)]
    causal = jnp.concatenate([ck < rq] * (q.shape[1] // HEAD_DIM), axis=0)
    if own_rows:
        kd, vd = [kd_ref[...].astype(BF16)], [vd_ref[...].astype(BF16)]
        live = _sb_fold(qh, kd, vd, False, _strict_upper(tq), causal, carry_ref, acc_ref, True)
        n_left = n_past
    else:
        near = pl.multiple_of(jnp.maximum(n_past - 1, 0) * tq, tq)
        kd = [jnp.concatenate([kp_ref[:, pl.ds(near, tq)], kd_ref[...]], axis=1).astype(BF16)]
        vd = [jnp.concatenate([vp_ref[:, pl.ds(near, tq)], vd_ref[...]], axis=1).astype(BF16)]
        mask = jnp.concatenate([jnp.broadcast_to(n_past >= 1, causal.shape), causal], axis=1)
        live = _sb_fold(qh, kd, vd, True, _strict_upper(tq), mask, carry_ref, acc_ref, True)
        n_left = n_past - 1

    def cond(st):
        return jnp.logical_and(st[0] >= 0, st[1] > 0)

    def body(st):
        start = pl.multiple_of(st[0] * tk, tk)
        kp, vp = [kp_ref[:, pl.ds(start, tk)].astype(BF16)], [vp_ref[:, pl.ds(start, tk)].astype(BF16)]
        live = _sb_fold(qh, kp, vp, True, _strict_upper(tk), None, carry_ref, acc_ref, False)
        return st[0] - 1, live

    lax.while_loop(cond, body, (n_left - 1, live))
    o_ref[...] = _unstack_heads(acc_ref[0], tq, HEAD_DIM).astype(o_ref.dtype)


def _sb_attn_prompt(pn4, layer, k_t, v_t):
    _, bn, s, _ = pn4.shape
    width = ATT_HEADS * HEAD_DIM
    blk = (None, ATT_BLK, width)
    own = pl.BlockSpec((None, None, width, ATT_BLK), lambda b, hg, qi: (layer, b, hg, qi))
    older = pl.BlockSpec((None, None, width, s), lambda b, hg, qi: (layer, b, hg, 0))
    rows = ATT_HEADS * ATT_BLK
    return pl.pallas_call(
        functools.partial(_sb_kernel, ATT_BLK, None, False),
        grid=(bn, B_HEADS // ATT_HEADS, s // ATT_BLK),
        in_specs=[_pspec(blk, lambda b, hg, qi: (b, qi, _CBP["qb"] * LANES // width + hg), layer),
                  own, own, older, older],
        out_specs=pl.BlockSpec(blk, lambda b, hg, qi: (b, qi, hg)),
        out_shape=jax.ShapeDtypeStruct((bn, s, BRANCH_W), BF16),
        scratch_shapes=[pltpu.VMEM((1, rows, LANES), F32), pltpu.VMEM((1, rows, width), F32)],
        compiler_params=_cparams("parallel", "parallel", "arbitrary"),
        name="stick_attn_prompt",
    )(pn4, k_t, v_t, k_t, v_t)


def _sb_attn_sample(p4, layer, k_cache_t, v_cache_t):
    _, bn, t, _ = p4.shape
    past = k_cache_t.shape[-1]
    new = lambda name: _pspec((None, t, BRANCH_W), lambda b: (b, 0, _CB[name] * LANES // BRANCH_W), layer)
    older = pl.BlockSpec((None, None, BRANCH_W, past), lambda b: (layer, b, 0, 0))
    rows = B_HEADS * t
    return pl.pallas_call(
        functools.partial(_sb_kernel, ATT_BLK, past // ATT_BLK, True),
        grid=(bn,),
        in_specs=[new("qb"), new("kb"), new("vb"), older, older],
        out_specs=pl.BlockSpec((None, t, BRANCH_W), lambda b: (b, 0, 0)),
        out_shape=jax.ShapeDtypeStruct((bn, t, BRANCH_W), BF16),
        scratch_shapes=[pltpu.VMEM((1, rows, LANES), F32), pltpu.VMEM((1, rows, BRANCH_W), F32)],
        compiler_params=_cparams("parallel"),
        name="stick_attn_sample",
    )(p4, p4, p4, k_cache_t, v_cache_t)


def _diff_combine(o0, o1, lam_init, lam_ref, gain_ref):
    lp = lam_ref[...]
    lam = (jnp.exp(jnp.sum(lp[0:1] * lp[1:2], axis=1, keepdims=True))
           - jnp.exp(jnp.sum(lp[2:3] * lp[3:4], axis=1, keepdims=True)) + lam_init)
    return _rms(o0 - lam * o1, gain_ref[...]) * (1.0 - lam_init)


def _diff_fold(qs, k_t, v, bias, m_ref, l_ref, acc_ref, first):
    kb = k_t.astype(BF16)
    vb = v.astype(BF16)
    rows, tk = qs.shape[0], vb.shape[0]
    s = _dot(qs, kb) + bias
    m_cur = jnp.max(s, axis=1, keepdims=True)
    m_new = jnp.broadcast_to(m_cur, (rows, LANES)) if first else jnp.maximum(m_ref[...], m_cur)
    p = jnp.exp2(s - _lane_tile(m_new, tk // LANES))
    row = jnp.sum(p, axis=1, keepdims=True)
    pv = _dot(p.astype(BF16), vb)
    if first:
        l_ref[...] = jnp.broadcast_to(row, (rows, LANES))
        acc_ref[...] = pv
    else:
        alpha = jnp.exp2(m_ref[...] - m_new)
        l_ref[...] = alpha * l_ref[...] + row
        acc_ref[...] = alpha * acc_ref[...] + pv
    m_ref[...] = m_new


def _diff_prompt_kernel(lam_init, tbl_ref, lam_ref, gain_ref, q_ref, kd_ref, vd_ref, kp_ref, vp_ref,
                        bd_ref, bp_ref, o_ref, m_ref, l_ref, acc_ref):
    h = pl.program_id(1)
    qi = pl.program_id(2)
    q = q_ref[...] * SOFTMAX_QSCALE
    blk = q.shape[0]
    qm = _stack_heads(q, HEAD_DIM)
    far = tbl_ref[T5_FAR_BUCKET * C_HEADS + h] * LOG2E

    near = pl.multiple_of(jnp.maximum(qi - 1, 0) * blk, blk)
    k0 = jnp.concatenate([kp_ref[:, pl.ds(near, blk)], kd_ref[...]], axis=1)
    v0 = jnp.concatenate([vp_ref[pl.ds(near, blk), :], vd_ref[...]], axis=0)
    b0 = jnp.concatenate([jnp.where(qi >= 1, bp_ref[...], NEG_INF), bd_ref[...]], axis=1)
    b0 = jnp.concatenate([b0, b0], axis=0)
    _diff_fold(qm, k0, v0, b0, m_ref, l_ref, acc_ref, True)

    n_far = jnp.maximum(qi - 1, 0)
    n_group = n_far // DIFF_FAR_GROUP
    wide = DIFF_FAR_GROUP * blk

    def body(j, c):
        start = pl.multiple_of(j * wide, wide)
        _diff_fold(qm, kp_ref[:, pl.ds(start, wide)], vp_ref[pl.ds(start, wide), :], far,
                   m_ref, l_ref, acc_ref, False)
        return c

    lax.fori_loop(0, n_group, body, 0)
    rest = pl.multiple_of(n_group * wide, wide)
    for r in range(1, DIFF_FAR_GROUP):
        @pl.when(n_far - n_group * DIFF_FAR_GROUP == r)
        def _(r=r):
            _diff_fold(qm, kp_ref[:, pl.ds(rest, r * blk)], vp_ref[pl.ds(rest, r * blk), :], far,
                       m_ref, l_ref, acc_ref, False)

    o = acc_ref[...] / l_ref[...]
    o = _diff_combine(o[:blk], o[blk:], lam_init, lam_ref, gain_ref)
    o_ref[...] = o.astype(o_ref.dtype)


def _diff_sample_kernel(lam_init, lam_ref, gain_ref, q_ref, kc_ref, vc_ref, kn_ref, vn_ref,
                        bc_ref, bn_ref, o_ref):
    q_all = q_ref[...] * SOFTMAX_QSCALE
    past = kc_ref.shape[1]
    k_t, v_t = (True, False), (False, False)
    for h in range(C_HEADS):
        cols = slice(h * LANES, (h + 1) * LANES)
        q = q_all[:, cols]
        kb = [kc_ref[cols, :].astype(BF16), kn_ref[:, cols].astype(BF16)]
        vb = [vc_ref[pl.ds(h, past, stride=C_HEADS), :].astype(BF16), vn_ref[:, cols].astype(BF16)]
        biases = [jnp.concatenate([b[h], b[h]], axis=0) for b in (bc_ref, bn_ref)]
        t = q.shape[0]
        o = _softmax_segments(_stack_heads(q, HEAD_DIM), kb, vb, biases, k_t, v_t)
        o_ref[:, cols] = _diff_combine(o[:t], o[t:], lam_init, lam_ref, gain_ref).astype(o_ref.dtype)


def _diff_attn_prompt(pn4, layer, k_t, t5tiles, lam_init, t5_flat, lam_params, gain):
    _, bn, s, _ = pn4.shape
    blk = (None, ATT_BLK, LANES)
    full = (None, s, LANES)
    state = pltpu.VMEM((2 * ATT_BLK, LANES), F32)
    return pl.pallas_call(
        functools.partial(_diff_prompt_kernel, lam_init),
        grid=(bn, C_HEADS, s // ATT_BLK),
        in_specs=[pl.BlockSpec(memory_space=pltpu.SMEM),
                  pl.BlockSpec((4, HEAD_DIM), lambda b, h, qi: (0, 0)),
                  pl.BlockSpec((1, C_VDIM), lambda b, h, qi: (0, 0)),
                  _pspec(blk, lambda b, h, qi: (b, qi, _CBP["qc"] + h), layer),
                  pl.BlockSpec((None, None, LANES, ATT_BLK), lambda b, h, qi: (layer, b, h, qi)),
                  _pspec(blk, lambda b, h, qi: (b, qi, _CBP["vc"] + h), layer),
                  pl.BlockSpec((None, None, LANES, s), lambda b, h, qi: (layer, b, h, 0)),
                  _pspec(full, lambda b, h, qi: (b, 0, _CBP["vc"] + h), layer),
                  pl.BlockSpec((None, None, ATT_BLK, ATT_BLK), lambda b, h, qi: (h, 0, 0, 0)),
                  pl.BlockSpec((None, None, ATT_BLK, ATT_BLK), lambda b, h, qi: (h, 1, 0, 0))],
        out_specs=pl.BlockSpec(blk, lambda b, h, qi: (b, qi, h)),
        out_shape=jax.ShapeDtypeStruct((bn, s, BRANCH_W), BF16),
        scratch_shapes=[state, state, state],
        compiler_params=_cparams("parallel", "parallel", "arbitrary"),
        name="diff_attn_prompt",
    )(t5_flat, lam_params, gain.reshape(1, C_VDIM), pn4, k_t, pn4, k_t, pn4, t5tiles, t5tiles)


def _diff_attn_sample(p4, layer, k_cache_t, v_cache, bias_cache, bias_new, lam_init, lam_params, gain):
    _, bn, t, _ = p4.shape
    past = k_cache_t.shape[-1]
    new = lambda name: _pspec((None, t, BRANCH_W), lambda b: (b, 0, _CB[name] * LANES // BRANCH_W), layer)
    return pl.pallas_call(
        functools.partial(_diff_sample_kernel, lam_init),
        grid=(bn,),
        in_specs=[pl.BlockSpec((4, HEAD_DIM), lambda b: (0, 0)),
                  pl.BlockSpec((1, C_VDIM), lambda b: (0, 0)),
                  new("qc"),
                  pl.BlockSpec((None, None, BRANCH_W, past), lambda b: (layer, b, 0, 0)),
                  pl.BlockSpec((None, None, past * C_HEADS, C_VDIM), lambda b: (layer, b, 0, 0)),
                  new("kc"), new("vc"),
                  pl.BlockSpec((C_HEADS, t, past), lambda b: (0, 0, 0)),
                  pl.BlockSpec((C_HEADS, t, t), lambda b: (0, 0, 0))],
        out_specs=pl.BlockSpec((None, t, BRANCH_W), lambda b: (b, 0, 0)),
        out_shape=jax.ShapeDtypeStruct((bn, t, BRANCH_W), BF16),
        compiler_params=_cparams("parallel"),
        name="diff_attn_sample",
    )(lam_params, gain.reshape(1, C_VDIM), p4, k_cache_t, v_cache, p4, p4, bias_cache, bias_new)


def _merge_kernel(x_ref, oa_ref, ob_ref, oc_ref, g0_ref, g1_ref, g2_ref, bg_ref, wb_ref, wo_ref, o_ref):
    h = None
    for n, (o_r, g_r) in enumerate(((oa_ref, g0_ref), (ob_ref, g1_ref), (oc_ref, g2_ref))):
        gate = 1.0 / (1.0 + jnp.exp(-(g_r[...] + bg_ref[n:n + 1, :])))
        t = gate * _dot(o_r[...], wb_ref[n])
        h = t if h is None else h + t
    o_ref[...] = x_ref[...] + _dot(h.astype(BF16), wo_ref[...])


def _merge(x, o_a, o_b, o_c, p, layer, b_gate, w_branch, w_out, tm):
    t, d = x.shape
    row = lambda i: (i, 0)
    fixed2 = lambda i: (0, 0)
    return pl.pallas_call(
        _merge_kernel,
        grid=(t // tm,),
        in_specs=[pl.BlockSpec((tm, d), row),
                  pl.BlockSpec((tm, BRANCH_W), row),
                  pl.BlockSpec((tm, BRANCH_W), row),
                  pl.BlockSpec((tm, BRANCH_W), row),
                  pl.BlockSpec((None, tm, d), lambda i: (layer, i, 0)),
                  pl.BlockSpec((None, tm, d), lambda i: (layer, i, 1)),
                  pl.BlockSpec((None, tm, d), lambda i: (layer, i, 2)),
                  pl.BlockSpec((N_BRANCH, d), fixed2),
                  pl.BlockSpec((N_BRANCH, BRANCH_W, d), lambda i: (0, 0, 0)),
                  pl.BlockSpec((d, d), fixed2)],
        out_specs=pl.BlockSpec((tm, d), row),
        out_shape=jax.ShapeDtypeStruct((t, d), F32),
        compiler_params=_cparams("parallel"),
        name="gated_merge",
    )(x, o_a, o_b, o_c, p, p, p, b_gate, w_branch, w_out)


FFN_TF = D_FF // 2
FFN_PREV = 16


def _gelu_tanh(x):
    return x * (0.5 * (1.0 + jnp.tanh(math.sqrt(2.0 / math.pi) * (x + 0.044715 * (x * x * x)))))


def _ffn_kernel(tiles_per_seq, final, x_ref, xp_ref, gn_ref, wg_ref, wu_ref, cw_ref, cb_ref, wd_ref, st_ref, gf_ref,
                o_ref, so_ref, xn_ref, xpn_ref):
    i = pl.program_id(0)
    j = pl.program_id(1)
    whole_seqs = tiles_per_seq == 0

    @pl.when(j == 0)
    def _():
        xn_ref[...] = _rms(x_ref[...], gn_ref[...]).astype(BF16)
        if not whole_seqs:
            xpn_ref[...] = _rms(xp_ref[...], gn_ref[...]).astype(BF16)

    xn = xn_ref[...]
    tm = xn.shape[0]
    tf = wg_ref.shape[1]
    g = _dot(xn, wg_ref[...])
    u = _dot(xn, wu_ref[...])
    if whole_seqs:
        ns = st_ref.shape[0]
        st = st_ref[...]
        g = g.reshape(ns, tm // ns, tf)
        u = u.reshape(ns, tm // ns, tf)
        pm2, pm1 = st[:, 0:1, :], st[:, 1:2, :]
    else:
        gp = _dot(xpn_ref[...], wg_ref[...])
        first = (i % tiles_per_seq) == 0
        st = st_ref[0]
        pm2 = jnp.where(first, st[0:1, :], gp[FFN_PREV - 2:FFN_PREV - 1, :])
        pm1 = jnp.where(first, st[1:2, :], gp[FFN_PREV - 1:FFN_PREV, :])
    ax = g.ndim - 2
    row = lax.broadcasted_iota(jnp.int32, g.shape, ax)
    g1 = jnp.where(row == 0, pm1, pltpu.roll(g, 1, ax))
    g2 = jnp.where(row == 0, pm2, jnp.where(row == 1, pm1, pltpu.roll(g, 2, ax)))
    cw = cw_ref[...]
    c = cb_ref[...] + cw[0:1, :] * g2 + cw[1:2, :] * g1 + cw[2:3, :] * g
    hid = (_gelu_tanh(c) * u).reshape(tm, tf).astype(BF16)
    contrib = _dot(hid, wd_ref[...])

    last = pl.num_programs(1) - 1
    finish = (lambda y: _rms(y, gf_ref[...])) if final else (lambda y: y)

    @pl.when(j == 0)
    def _():
        o_ref[...] = x_ref[...] + contrib

    @pl.when(jnp.logical_and(j > 0, j < last))
    def _():
        o_ref[...] += contrib

    @pl.when(jnp.logical_and(j > 0, j == last))
    def _():
        o_ref[...] = finish(o_ref[...] + contrib)

    seq_len = g.shape[ax]
    if whole_seqs:
        so_ref[...] = g[:, seq_len - 2:seq_len, :]
    else:
        so_ref[0] = g[seq_len - 2:seq_len, :]


def _conv_ffn(x, g_norm, w_gate, w_up, conv_w, conv_b, w_down, state, seq_len, tm, g_final, final):
    t, d = x.shape
    nseq = t // seq_len
    nf = D_FF // FFN_TF
    assert nf > 1
    if seq_len >= tm:
        tiles_per_seq = seq_len // tm
        st_spec = pl.BlockSpec((1, 2, FFN_TF), lambda i, j: (i // tiles_per_seq, 0, j))
    else:
        tiles_per_seq = 0
        st_spec = pl.BlockSpec((tm // seq_len, 2, FFN_TF), lambda i, j: (i, 0, j))
    prev_blocks = tm // FFN_PREV
    tail_spec = pl.BlockSpec(st_spec.block_shape, lambda i, j: (i, 0, j))
    n_tail = t // tm * st_spec.block_shape[0]
    y, tails = pl.pallas_call(
        functools.partial(_ffn_kernel, tiles_per_seq, final),
        grid=(t // tm, nf),
        in_specs=[pl.BlockSpec((tm, d), lambda i, j: (i, 0)),
                  pl.BlockSpec((FFN_PREV, d), lambda i, j: (jnp.maximum(i * prev_blocks - 1, 0), 0)),
                  pl.BlockSpec((1, d), lambda i, j: (0, 0)),
                  pl.BlockSpec((d, FFN_TF), lambda i, j: (0, j)),
                  pl.BlockSpec((d, FFN_TF), lambda i, j: (0, j)),
                  pl.BlockSpec((3, FFN_TF), lambda i, j: (0, j)),
                  pl.BlockSpec((1, FFN_TF), lambda i, j: (0, j)),
                  pl.BlockSpec((FFN_TF, d), lambda i, j: (j, 0)),
                  st_spec,
                  pl.BlockSpec((1, d), lambda i, j: (0, 0))],
        out_specs=[pl.BlockSpec((tm, d), lambda i, j: (i, 0)), tail_spec],
        out_shape=[jax.ShapeDtypeStruct((t, d), F32),
                   jax.ShapeDtypeStruct((n_tail, 2, D_FF), F32)],
        scratch_shapes=[pltpu.VMEM((tm, d), BF16), pltpu.VMEM((FFN_PREV, d), BF16)],
        compiler_params=_cparams("arbitrary", "arbitrary"),
        name="conv_ffn_prompt" if tiles_per_seq else "conv_ffn_sample",
    )(x, x, g_norm.reshape(1, d), w_gate, w_up, conv_w, conv_b.reshape(1, D_FF), w_down, state, g_final.reshape(1, d))
    return y, tails[n_tail // nseq - 1::n_tail // nseq]


PROJ_TM, PROJ_TN = 1024, 1280
ROW_TM = 512


def _lam_init(layer):
    return 0.8 - 0.6 * math.exp(-0.3 * layer)


def _cols(p4, name):
    c0 = _CB[name] * LANES
    return p4[..., c0:c0 + BRANCH_W]


def _layer_tail(x2, o_a, o_b, o_c, p_rows, layer, prm, conv_state, seq_len):
    flat = lambda a: a.reshape(x2.shape[0], BRANCH_W)
    x2 = _merge(x2, flat(o_a), flat(o_b), flat(o_c), p_rows, layer, prm["b_gate"][layer],
                prm["w_branch"][layer], prm["w_out"][layer], ROW_TM)
    depth = prm["norm_ffn"].shape[0]
    return _conv_ffn(x2, prm["norm_ffn"][layer], prm["w_gate"][layer], prm["w_up"][layer],
                     prm["conv_w"][layer], prm["conv_b"][layer], prm["w_down"][layer], conv_state, seq_len, ROW_TM,
                     prm["norm_final"], layer == depth - 1)


def _trunk_prompt(x, prm, abias, t5tiles, t5_flat):
    bn, s, d = x.shape
    x2 = x.reshape(bn * s, d)
    depth = prm["w_rows"].shape[0]
    conv_states = []
    rows, kv_t = None, None
    for l in range(depth):
        rows = _norm_proj(x2, prm["norm_mix"][l], prm["w_rows"][l], PROJ_TM, PROJ_TN, l, depth, rows,
                          heads_group=(_CBP["vc"] * LANES, C_HEADS))
        p_rows, vc_heads = rows
        kv_t = _norm_proj_t(x2, prm["norm_mix"][l], prm["w_feat_t"][l], PROJ_TM, l, depth, bn, kv_t)
        ka_t, va_t, kb_t, vb_t, kc_t = kv_t
        pn4 = p_rows.reshape(depth, bn, s, ROW_COLS)
        o_a = _band_attn_prompt(pn4, l, ka_t, va_t, abias[l])
        o_b = _sb_attn_prompt(pn4, l, kb_t, vb_t)
        o_c = _diff_attn_prompt(pn4, l, kc_t, t5tiles, _lam_init(l), t5_flat, prm["c_lambda"][l], prm["c_subln"][l])
        x2, st_conv = _layer_tail(x2, o_a, o_b, o_c, p_rows, l, prm, jnp.zeros((bn, 2, D_FF), F32), s)
        conv_states.append(st_conv)
    y = x2.reshape(bn, s, d)
    heads_t = lambda a: jnp.transpose(a.reshape(depth, bn, A_HEADS, HEAD_DIM, a.shape[-1]), (0, 1, 4, 2, 3))
    keep = min(A_PAST_CHUNKS * CHUNK, s)
    states = (heads_t(ka_t[..., s - keep:]), heads_t(va_t[..., s - keep:]), heads_t(kb_t), heads_t(vb_t),
              jnp.transpose(kc_t.reshape(depth, bn, C_HEADS, 2, HEAD_DIM, s), (0, 1, 5, 2, 3, 4)),
              vc_heads.reshape(depth, bn, s, C_HEADS, C_VDIM),
              jnp.stack(conv_states, axis=0))
    return y, states


def _feature_major(cache):
    nd = cache.ndim
    t = jnp.transpose(cache, (0, 1) + tuple(range(3, nd)) + (2,))
    return t.reshape(cache.shape[0], cache.shape[1], BRANCH_W, cache.shape[2])


def _trunk_sample(x, caches, prm, abias, t5tiles):
    bn, t, d = x.shape
    x2 = x.reshape(bn * t, d)
    depth = prm["w_in"].shape[0]
    ca_k, ca_v, cb_k, cb_v, cc_k, cc_v, st_conv_in = caches
    ca_kt, ca_vt, cb_kt, cb_vt, cc_kt = (_feature_major(c) for c in (ca_k, ca_v, cb_k, cb_v, cc_k))
    keep, past = ca_k.shape[2], cc_k.shape[2]
    cc_vr = cc_v.reshape(depth, bn, past * C_HEADS, C_VDIM)
    conv_states = []
    p_all = None
    for l in range(depth):
        p_all, = _norm_proj(x2, prm["norm_mix"][l], prm["w_in"][l], PROJ_TM, PROJ_TN, l, depth,
                            None if p_all is None else [p_all])
        p4 = p_all.reshape(depth, bn, t, IN_COLS)
        bias_cache = jnp.concatenate([abias[l][:, 0, :t, :], abias[l][:, 1, :t, :]], axis=-1)[..., -keep:]
        o_a = _band_attn_sample(p4, l, ca_kt, ca_vt, bias_cache, abias[l][:, 2, :t, :t])
        o_b = _sb_attn_sample(p4, l, cb_kt, cb_vt)
        far = jnp.broadcast_to(t5tiles[:, 1, :1, :1], (C_HEADS, t, past - ATT_BLK))
        bias_c = jnp.concatenate([far, t5tiles[:, 1, :t, :]], axis=-1)
        o_c = _diff_attn_sample(p4, l, cc_kt, cc_vr, bias_c, t5tiles[:, 0, :t, :t],
                                _lam_init(l), prm["c_lambda"][l], prm["c_subln"][l])
        x2, st_conv = _layer_tail(x2, o_a, o_b, o_c, p_all, l, prm, st_conv_in[l], t)
        conv_states.append(st_conv)
    y = x2.reshape(bn, t, d)
    ka, va, kb, vb, kc, vc = (_cols(p4, n) for n in ("ka", "va", "kb", "vb", "kc", "vc"))
    heads = lambda a: a.reshape(a.shape[:3] + (A_HEADS, HEAD_DIM))
    states = (jnp.concatenate([ca_k[:, :, t:], heads(ka)], axis=2),
              jnp.concatenate([ca_v[:, :, t:], heads(va)], axis=2), heads(kb), heads(vb),
              kc.reshape(depth, bn, t, C_HEADS, 2, HEAD_DIM), vc.reshape(depth, bn, t, C_HEADS, C_VDIM),
              jnp.stack(conv_states, axis=0))
    return y, states


def kernel(x_prompt, x_sample, cache_a_k, cache_a_v, cache_b_k, cache_b_v, cache_c_k, cache_c_v,
           state_ffn_conv, norm_mix, w_in, b_gate, a_rel_bias, t5_bias, c_lambda, c_subln,
           w_branch, w_out, norm_ffn, w_up, conv_w, conv_b, w_down, norm_final):
    group = {name: w_in[..., i * BRANCH_W:(i + 1) * BRANCH_W] for i, name in enumerate(_GROUPS)}
    gate = w_in[..., len(_GROUPS) * BRANCH_W:]
    prm = dict(
        norm_mix=norm_mix,
        w_in=jnp.concatenate([gate] + [group[n] for n in _GROUPS], axis=-1).astype(BF16),
        w_rows=jnp.concatenate([gate] + [group[n] for n in _ROW_GROUPS], axis=-1).astype(BF16),
        w_feat_t=jnp.swapaxes(jnp.concatenate([group[n] for n in _T_GROUPS], axis=-1), 1, 2).astype(BF16),
        b_gate=b_gate, c_lambda=c_lambda, c_subln=c_subln,
        w_branch=w_branch.astype(BF16), w_out=w_out.astype(BF16), norm_ffn=norm_ffn,
        w_gate=w_up[..., :D_FF].astype(BF16), w_up=w_up[..., D_FF:].astype(BF16),
        conv_w=conv_w, conv_b=conv_b, w_down=w_down.astype(BF16), norm_final=norm_final)
    abias = [_build_abias(a_rel_bias[l]) for l in range(a_rel_bias.shape[0])]
    t5tiles = _build_t5bias(t5_bias)
    y_p, p_states = _trunk_prompt(x_prompt, prm, abias, t5tiles, t5_bias.reshape(-1))
    caches = (cache_a_k, cache_a_v, cache_b_k, cache_b_v, cache_c_k, cache_c_v, state_ffn_conv)
    y_s, s_states = _trunk_sample(x_sample, caches, prm, abias, t5tiles)
    return (y_p, y_s) + p_states + s_states
```

```python
import functools
import math

import jax
import jax.numpy as jnp
from jax import lax
from jax.experimental import pallas as pl
from jax.experimental.pallas import tpu as pltpu

F32 = jnp.float32
BF16 = jnp.bfloat16

D_MODEL = 1024
CHUNK = 64
HEAD_DIM = 64
A_HEADS = 8
A_PAST_CHUNKS = 8
A_REL_CLIP = 128
B_HEADS = 8
C_HEADS = 4
C_VDIM = 128
T5_BUCKETS = 32
D_FF = 2816
N_BRANCH = 3
BRANCH_W = 512
GATE_W = N_BRANCH * D_MODEL
IN_COLS = 9 * BRANCH_W + GATE_W
EPS = 1e-6
NEG_INF = -1e30
SCALE = HEAD_DIM ** -0.5
LOG2E = math.log2(math.e)
SOFTMAX_QSCALE = SCALE * LOG2E

LANES = 128
_GROUPS = ("qa", "ka", "va", "qb", "kb", "vb", "qc", "kc", "vc")
_CB = {name: (GATE_W + i * BRANCH_W) // LANES for i, name in enumerate(_GROUPS)}
_ROW_GROUPS = ("qa", "qb", "qc", "vc")
_T_GROUPS = ("ka", "va", "kb", "vb", "kc")
_CBP = {name: (GATE_W + i * BRANCH_W) // LANES for i, name in enumerate(_ROW_GROUPS)}
ROW_COLS = GATE_W + len(_ROW_GROUPS) * BRANCH_W

ATT_BLK = 256
ATT_HEADS = 4
SB_LOG_CUT = -104.0
T5_FAR_BUCKET = 15
DIFF_FAR_GROUP = 4
V7X_VMEM_BYTES = 64 * 1024 * 1024
VMEM_LIMIT = V7X_VMEM_BYTES - 8 * 1024 * 1024


def _cparams(*sem):
    return pltpu.CompilerParams(dimension_semantics=sem, vmem_limit_bytes=VMEM_LIMIT)


def _rms(x, g):
    return x * lax.rsqrt(jnp.mean(x * x, axis=-1, keepdims=True) + EPS) * g


def _dot(a, b):
    return jnp.dot(a, b, preferred_element_type=F32)


def _dot_nt(a, b):
    return lax.dot_general(a, b, (((1,), (1,)), ((), ())), preferred_element_type=F32)


def _norm_proj_kernel(n_stack, heads_cols, x_ref, g_ref, w_ref, *refs):
    outs, xn_ref = refs[n_stack:-1], refs[-1]
    j = pl.program_id(1)

    @pl.when(j == 0)
    def _():
        xn_ref[...] = _rms(x_ref[...], g_ref[...]).astype(BF16)

    res = _dot(xn_ref[...], w_ref[...])
    outs[0][...] = res
    if heads_cols is not None:
        tile, first, heads = heads_cols

        @pl.when(j == tile)
        def _():
            for h in range(heads):
                lo = first + h * LANES
                outs[1][pl.ds(h, res.shape[0], stride=heads), :] = res[:, lo:lo + LANES]


def _norm_proj(x, g, w, tm, tn, layer, depth, stacked, heads_group=None):
    t, d = x.shape
    n = w.shape[1]
    in_specs = [pl.BlockSpec((tm, d), lambda i, j: (i, 0)),
                pl.BlockSpec((1, d), lambda i, j: (0, 0)),
                pl.BlockSpec((d, tn), lambda i, j: (0, j))]
    args = [x, g.reshape(1, d), w]
    out_specs = [pl.BlockSpec((None, tm, tn), lambda i, j: (layer, i, j))]
    out_shape = [jax.ShapeDtypeStruct((depth, t, n), F32)]
    heads_cols = None
    if heads_group is not None:
        first, heads = heads_group
        heads_cols = (first // tn, first % tn, heads)
        out_specs.append(pl.BlockSpec((None, tm * heads, LANES), lambda i, j: (layer, i, 0)))
        out_shape.append(jax.ShapeDtypeStruct((depth, t * heads, LANES), F32))
    n_stack = 0 if stacked is None else len(out_shape)
    if stacked is not None:
        in_specs += [pl.BlockSpec(memory_space=pl.ANY)] * n_stack
        args += list(stacked)
    return pl.pallas_call(
        functools.partial(_norm_proj_kernel, n_stack, heads_cols),
        grid=(t // tm, n // tn),
        in_specs=in_specs,
        out_specs=out_specs,
        out_shape=out_shape,
        scratch_shapes=[pltpu.VMEM((tm, d), BF16)],
        input_output_aliases={3 + k: k for k in range(n_stack)},
        compiler_params=_cparams("arbitrary", "arbitrary"),
        name="norm_proj",
    )(*args)


def _norm_proj_t_kernel(ngroups, x_ref, g_ref, wt_ref, *refs):
    o_refs = refs[-ngroups:]
    xn = _rms(x_ref[...], g_ref[...]).astype(BF16)
    for g in range(ngroups):
        o_refs[g][...] = _dot_nt(wt_ref[g * BRANCH_W:(g + 1) * BRANCH_W, :], xn)


def _norm_proj_t(x, g, wt, tm, layer, depth, bn, stacked):
    t, d = x.shape
    s = t // bn
    ngroups = wt.shape[0] // BRANCH_W
    tiles = s // tm
    in_specs = [pl.BlockSpec((tm, d), lambda i: (i, 0)),
                pl.BlockSpec((1, d), lambda i: (0, 0)),
                pl.BlockSpec(wt.shape, lambda i: (0, 0))]
    args = [x, g.reshape(1, d), wt]
    aliases = {}
    if stacked is not None:
        in_specs += [pl.BlockSpec(memory_space=pl.ANY)] * ngroups
        args += list(stacked)
        aliases = {3 + n: n for n in range(ngroups)}
    out_spec = pl.BlockSpec((None, None, BRANCH_W, tm), lambda i: (layer, i // tiles, 0, i % tiles))
    return pl.pallas_call(
        functools.partial(_norm_proj_t_kernel, ngroups),
        grid=(t // tm,),
        in_specs=in_specs,
        out_specs=[out_spec] * ngroups,
        out_shape=[jax.ShapeDtypeStruct((depth, bn, BRANCH_W, s), F32)] * ngroups,
        input_output_aliases=aliases,
        compiler_params=_cparams("arbitrary"),
        name="norm_proj_t",
    )(*args)


def _abias_kernel(tbl_ref, o_ref):
    h = pl.program_id(0)
    sub = 8
    ql = lax.broadcasted_iota(jnp.int32, (ATT_BLK, ATT_BLK), 0)
    kl = lax.broadcasted_iota(jnp.int32, (ATT_BLK, ATT_BLK), 1)
    far = jnp.full((ATT_BLK, ATT_BLK), tbl_ref[2 * A_REL_CLIP * A_HEADS + h] * LOG2E, F32)
    o_ref[0, 0] = jnp.where((kl >> 6) >= (ql >> 6), far, NEG_INF)
    lead = ATT_BLK - sub
    b = lax.broadcasted_iota(jnp.int32, (sub, 2 * ATT_BLK), 0)
    x = lax.broadcasted_iota(jnp.int32, (sub, 2 * ATT_BLK), 1)
    for seg, off in ((1, ATT_BLK), (2, 0)):
        idx = jnp.clip(b + lead - x + off, -A_REL_CLIP, A_REL_CLIP) + A_REL_CLIP

        def body(r, acc, idx=idx):
            return jnp.where(idx == r, tbl_ref[r * A_HEADS + h] * LOG2E, acc)

        strip = lax.fori_loop(0, 2 * A_REL_CLIP + 1, body, jnp.zeros((sub, 2 * ATT_BLK), F32))
        for a in range(ATT_BLK // sub):
            o_ref[0, seg, a * sub:(a + 1) * sub, :] = strip[:, lead - a * sub:lead - a * sub + ATT_BLK]
    o_ref[0, 2] = jnp.where((kl >> 6) <= (ql >> 6), o_ref[0, 2], NEG_INF)


def _build_abias(table):
    return pl.pallas_call(
        _abias_kernel,
        grid=(A_HEADS,),
        in_specs=[pl.BlockSpec(memory_space=pltpu.SMEM)],
        out_specs=pl.BlockSpec((1, 3, ATT_BLK, ATT_BLK), lambda h: (h, 0, 0, 0)),
        out_shape=jax.ShapeDtypeStruct((A_HEADS, 3, ATT_BLK, ATT_BLK), F32),
        compiler_params=_cparams("parallel"),
        name="band_bias_tiles",
    )(table.reshape(-1))


_T5_LARGE_STEPS = (12, 16, 23, 32, 46, 64, 91)


def _t5bias_kernel(tbl_ref, o_ref):
    h = pl.program_id(0)
    rows = 64
    for seg, off in ((0, 0), (1, -ATT_BLK)):
        for rc in range(ATT_BLK // rows):
            ql = lax.broadcasted_iota(jnp.int32, (rows, ATT_BLK), 0) + rc * rows
            kl = lax.broadcasted_iota(jnp.int32, (rows, ATT_BLK), 1)
            rel = kl - ql + off
            n = jnp.abs(rel)
            large = jnp.full((rows, ATT_BLK), T5_BUCKETS // 4, jnp.int32)
            for th in _T5_LARGE_STEPS:
                large = large + jnp.where(n >= th, 1, 0)
            bucket = jnp.where(rel > 0, T5_BUCKETS // 2, 0) + jnp.where(n < T5_BUCKETS // 4, n, large)
            tile = jnp.zeros((rows, ATT_BLK), F32)
            for r in range(T5_BUCKETS):
                tile = jnp.where(bucket == r, tbl_ref[r * C_HEADS + h] * LOG2E, tile)
            if seg == 0:
                tile = jnp.where((kl >> 6) <= (ql >> 6), tile, NEG_INF)
            o_ref[0, seg, rc * rows:(rc + 1) * rows, :] = tile


def _build_t5bias(table):
    return pl.pallas_call(
        _t5bias_kernel,
        grid=(C_HEADS,),
        in_specs=[pl.BlockSpec(memory_space=pltpu.SMEM)],
        out_specs=pl.BlockSpec((1, 2, ATT_BLK, ATT_BLK), lambda h: (h, 0, 0, 0)),
        out_shape=jax.ShapeDtypeStruct((C_HEADS, 2, ATT_BLK, ATT_BLK), F32),
        compiler_params=_cparams("parallel"),
        name="t5_bias_tiles",
    )(table.reshape(-1))


def _pspec(block, index_map, layer):
    return pl.BlockSpec((None,) + block, lambda *g: (layer,) + tuple(index_map(*g)))


def _lane_tile(x, n):
    return x if n == 1 else jnp.concatenate([x] * n, axis=1)


def _stack_heads(q, width):
    group = lax.broadcasted_iota(jnp.int32, q.shape, 1) // width
    return jnp.concatenate([jnp.where(group == h, q, 0.0) for h in range(q.shape[1] // width)],
                           axis=0).astype(BF16)


def _unstack_heads(x, t, width):
    group = lax.broadcasted_iota(jnp.int32, (t, x.shape[1]), 1) // width
    out = jnp.zeros((t, x.shape[1]), x.dtype)
    for h in range(x.shape[1] // width):
        out = jnp.where(group == h, x[h * t:(h + 1) * t, :], out)
    return out


def _qk(q, k, k_t):
    return _dot(q, k) if k_t else _dot_nt(q, k)


def _pv(p, v, v_t):
    return _dot_nt(p, v) if v_t else _dot(p, v)


def _softmax_segments(qh, kb, vb, biases, k_t, v_t):
    s = [_qk(qh, k, t) + b for k, b, t in zip(kb, biases, k_t)]
    m = functools.reduce(jnp.maximum, [jnp.max(sj, axis=1, keepdims=True) for sj in s])
    p = [jnp.exp2(sj - m) for sj in s]
    l = functools.reduce(jnp.add, [jnp.sum(pj, axis=1, keepdims=True) for pj in p])
    acc = functools.reduce(jnp.add, [_pv(pj.astype(BF16), v, t) for pj, v, t in zip(p, vb, v_t)])
    return acc / l


def _band_kernel(nseg, min_qi, q_ref, *refs):
    k_refs, v_refs, b_refs = refs[:nseg], refs[nseg:2 * nseg], refs[2 * nseg:3 * nseg]
    o_ref = refs[3 * nseg]
    qi = pl.program_id(2)
    q = q_ref[...] * SOFTMAX_QSCALE
    tq = q.shape[0]
    kb = [k_refs[j][...].astype(BF16) for j in range(nseg)]
    vb = [v_refs[j][...].astype(BF16) for j in range(nseg)]
    qs = _stack_heads(q, HEAD_DIM)
    biases = []
    for j in range(nseg):
        tile = b_refs[j][...].reshape(b_refs[j].shape[0] * tq, b_refs[j].shape[-1])
        biases.append(tile if min_qi[j] == 0 else jnp.where(qi >= min_qi[j], tile, NEG_INF))
    o = _softmax_segments(qs, kb, vb, biases, (True,) * nseg, (True,) * nseg)
    o_ref[...] = _unstack_heads(o, tq, HEAD_DIM).astype(o_ref.dtype)


def _band_attn_prompt(pn4, layer, k_t, v_t, abias):
    _, bn, s, _ = pn4.shape
    nq = s // ATT_BLK
    width = ATT_HEADS * HEAD_DIM
    blk = (None, ATT_BLK, width)

    def kv_spec(back):
        return pl.BlockSpec((None, None, width, ATT_BLK),
                            lambda b, hg, qi: (layer, b, hg, jnp.maximum(qi - back, 0)))

    def bias_spec(seg):
        return pl.BlockSpec((ATT_HEADS, None, ATT_BLK, ATT_BLK), lambda b, hg, qi: (hg, seg, 0, 0))

    return pl.pallas_call(
        functools.partial(_band_kernel, 3, (2, 1, 0)),
        grid=(bn, A_HEADS // ATT_HEADS, nq),
        in_specs=([_pspec(blk, lambda b, hg, qi: (b, qi, _CBP["qa"] * LANES // width + hg), layer)]
                  + [kv_spec(back) for back in (2, 1, 0)]
                  + [kv_spec(back) for back in (2, 1, 0)]
                  + [bias_spec(seg) for seg in range(3)]),
        out_specs=pl.BlockSpec(blk, lambda b, hg, qi: (b, qi, hg)),
        out_shape=jax.ShapeDtypeStruct((bn, s, BRANCH_W), BF16),
        compiler_params=_cparams("parallel", "parallel", "arbitrary"),
        name="band_attn_prompt",
    )(pn4, k_t, k_t, k_t, v_t, v_t, v_t, abias, abias, abias)


def _band_sample_kernel(q_ref, kc_ref, kn_ref, vc_ref, vn_ref, bc_ref, bn_ref, o_ref):
    t = q_ref.shape[0]
    width = ATT_HEADS * HEAD_DIM
    q = q_ref[...] * SOFTMAX_QSCALE
    for g in range(q.shape[1] // width):
        cols = slice(g * width, (g + 1) * width)
        rows = slice(g * ATT_HEADS * t, (g + 1) * ATT_HEADS * t)
        qs = _stack_heads(q[:, cols], HEAD_DIM)
        kb = [kc_ref[cols, :].astype(BF16), kn_ref[:, cols].astype(BF16)]
        vb = [vc_ref[cols, :].astype(BF16), vn_ref[:, cols].astype(BF16)]
        o = _softmax_segments(qs, kb, vb, [bc_ref[rows, :], bn_ref[rows, :]], (True, False), (True, False))
        o_ref[:, cols] = _unstack_heads(o, t, HEAD_DIM).astype(o_ref.dtype)


def _band_attn_sample(p4, layer, k_cache_t, v_cache_t, bias_cache, bias_new):
    _, bn, t, _ = p4.shape
    keep = k_cache_t.shape[-1]
    new = lambda name: _pspec((None, t, BRANCH_W), lambda b: (b, 0, _CB[name] * LANES // BRANCH_W), layer)
    old = pl.BlockSpec((None, None, BRANCH_W, keep), lambda b: (layer, b, 0, 0))
    return pl.pallas_call(
        _band_sample_kernel,
        grid=(bn,),
        in_specs=[new("qa"), old, new("ka"), old, new("va"),
                  pl.BlockSpec((A_HEADS * t, keep), lambda b: (0, 0)),
                  pl.BlockSpec((A_HEADS * t, t), lambda b: (0, 0))],
        out_specs=pl.BlockSpec((None, t, BRANCH_W), lambda b: (b, 0, 0)),
        out_shape=jax.ShapeDtypeStruct((bn, t, BRANCH_W), BF16),
        compiler_params=_cparams("parallel"),
        name="band_attn_sample",
    )(p4, k_cache_t, p4, v_cache_t, p4, bias_cache.reshape(A_HEADS * t, keep), bias_new.reshape(A_HEADS * t, t))


def _band_roll_kernel(kc_ref, vc_ref, kn_ref, vn_ref, ko_ref, vo_ref):
    t = kn_ref.shape[0]
    keep = kc_ref.shape[1]
    for c_ref, n_ref, o_ref in ((kc_ref, kn_ref, ko_ref), (vc_ref, vn_ref, vo_ref)):
        rolled = pltpu.roll(c_ref[...], keep - t, 1)
        new = jnp.concatenate([n_ref[...], jnp.zeros((LANES - t, n_ref.shape[1]), F32)], axis=0)
        new_t = pltpu.roll(new.T, LANES - t, 1)
        lane = lax.broadcasted_iota(jnp.int32, new_t.shape, 1)
        last = jnp.where(lane >= LANES - t, new_t, rolled[:, keep - LANES:])
        o_ref[...] = jnp.concatenate([rolled[:, :keep - LANES], last], axis=1)


def _band_roll(k_cache_t, v_cache_t, p4):
    depth, bn, t, _ = p4.shape
    keep = k_cache_t.shape[-1]
    assert t <= LANES <= keep
    old = pl.BlockSpec((None, None, BRANCH_W, keep), lambda l, b: (l, b, 0, 0))
    new = lambda name: pl.BlockSpec((None, None, t, BRANCH_W), lambda l, b: (l, b, 0, _CB[name] * LANES // BRANCH_W))
    return pl.pallas_call(
        _band_roll_kernel,
        grid=(depth, bn),
        in_specs=[old, old, new("ka"), new("va")],
        out_specs=[old, old],
        out_shape=[jax.ShapeDtypeStruct(k_cache_t.shape, F32)] * 2,
        compiler_params=_cparams("parallel", "parallel"),
        name="band_roll",
    )(k_cache_t, v_cache_t, p4, p4)


def _strict_upper(n):
    r = lax.broadcasted_iota(jnp.int32, (n, n), 0)
    c = lax.broadcasted_iota(jnp.int32, (n, n), 1)
    return jnp.where(r > c, 1.0, 0.0).astype(BF16)


def _sb_fold(qh, kb, vb, kv_t, upper, mask, carry_ref, acc_ref, first):
    tq = qh[0].shape[0]
    tk = kb[0].shape[1] if kv_t else kb[0].shape[0]
    cw = upper.shape[0]
    top = None
    for h in range(len(qh)):
        z = _qk(qh[h], kb[0], kv_t)
        soft = jnp.log(1.0 + jnp.exp(-jnp.abs(z)))
        log_keep = -(jnp.maximum(z, 0.0) + soft)
        log_take = jnp.minimum(z, 0.0) - soft
        if mask is not None:
            log_keep = jnp.where(mask, log_keep, 0.0)
        hi = log_keep.astype(BF16)
        lo = (log_keep - hi.astype(F32)).astype(BF16)
        pieces, total = [], None
        for c0 in reversed(range(0, tk, cw)):
            piece = _dot(hi[:, c0:c0 + cw], upper) + _dot(lo[:, c0:c0 + cw], upper)
            pieces.insert(0, piece if total is None else piece + total)
            part = jnp.sum(log_keep[:, c0:c0 + cw], axis=1, keepdims=True)
            total = part if total is None else total + part
        after = pieces[0] if len(pieces) == 1 else jnp.concatenate(pieces, axis=1)
        if not first:
            after = after + _lane_tile(carry_ref[h], tk // LANES)
        w = jnp.exp(log_take + after)
        if mask is not None:
            w = jnp.where(mask, w, 0.0)
        pv = _pv(w.astype(BF16), vb[0], kv_t)
        if first:
            acc_ref[h] = pv
            carry = jnp.broadcast_to(total, (tq, LANES))
        else:
            acc_ref[h] += pv
            carry = carry_ref[h] + total
        carry_ref[h] = carry
        top = jnp.max(carry) if top is None else jnp.maximum(top, jnp.max(carry))
    return (top > SB_LOG_CUT).astype(jnp.int32)


def _sb_kernel(tk, n_past_static, own_rows, q_ref, kd_ref, vd_ref, kp_ref, vp_ref, o_ref, carry_ref, acc_ref):
    n_past = pl.program_id(2) if n_past_static is None else n_past_static
    q = q_ref[...] * SCALE
    tq = q.shape[0]
    rq = lax.broadcasted_iota(jnp.int32, (tq, tq), 0)
    ck = lax.broadcasted_iota(jnp.int32, (tq, tq), 1)
    qh = [_stack_heads(q, HEAD_DIM)]
    causal = jnp.concatenate([ck < rq] * (q.shape[1] // HEAD_DIM), axis=0)
    if own_rows:
        kd, vd = [kd_ref[...].astype(BF16)], [vd_ref[...].astype(BF16)]
        live = _sb_fold(qh, kd, vd, False, _strict_upper(tq), causal, carry_ref, acc_ref, True)
        n_left = n_past
    else:
        near = pl.multiple_of(jnp.maximum(n_past - 1, 0) * tq, tq)
        kd = [jnp.concatenate([kp_ref[:, pl.ds(near, tq)], kd_ref[...]], axis=1).astype(BF16)]
        vd = [jnp.concatenate([vp_ref[:, pl.ds(near, tq)], vd_ref[...]], axis=1).astype(BF16)]
        mask = jnp.concatenate([jnp.broadcast_to(n_past >= 1, causal.shape), causal], axis=1)
        live = _sb_fold(qh, kd, vd, True, _strict_upper(tq), mask, carry_ref, acc_ref, True)
        n_left = n_past - 1

    def cond(st):
        return jnp.logical_and(st[0] >= 0, st[1] > 0)

    def body(st):
        start = pl.multiple_of(st[0] * tk, tk)
        kp, vp = [kp_ref[:, pl.ds(start, tk)].astype(BF16)], [vp_ref[:, pl.ds(start, tk)].astype(BF16)]
        live = _sb_fold(qh, kp, vp, True, _strict_upper(tk), None, carry_ref, acc_ref, False)
        return st[0] - 1, live

    lax.while_loop(cond, body, (n_left - 1, live))
    o_ref[...] = _unstack_heads(acc_ref[0], tq, HEAD_DIM).astype(o_ref.dtype)


def _sb_attn_prompt(pn4, layer, k_t, v_t):
    _, bn, s, _ = pn4.shape
    width = ATT_HEADS * HEAD_DIM
    blk = (None, ATT_BLK, width)
    own = pl.BlockSpec((None, None, width, ATT_BLK), lambda b, hg, qi: (layer, b, hg, qi))
    older = pl.BlockSpec((None, None, width, s), lambda b, hg, qi: (layer, b, hg, 0))
    rows = ATT_HEADS * ATT_BLK
    return pl.pallas_call(
        functools.partial(_sb_kernel, ATT_BLK, None, False),
        grid=(bn, B_HEADS // ATT_HEADS, s // ATT_BLK),
        in_specs=[_pspec(blk, lambda b, hg, qi: (b, qi, _CBP["qb"] * LANES // width + hg), layer),
                  own, own, older, older],
        out_specs=pl.BlockSpec(blk, lambda b, hg, qi: (b, qi, hg)),
        out_shape=jax.ShapeDtypeStruct((bn, s, BRANCH_W), BF16),
        scratch_shapes=[pltpu.VMEM((1, rows, LANES), F32), pltpu.VMEM((1, rows, width), F32)],
        compiler_params=_cparams("parallel", "parallel", "arbitrary"),
        name="stick_attn_prompt",
    )(pn4, k_t, v_t, k_t, v_t)


def _sb_attn_sample(p4, layer, k_cache_t, v_cache_t):
    _, bn, t, _ = p4.shape
    past = k_cache_t.shape[-1]
    new = lambda name: _pspec((None, t, BRANCH_W), lambda b: (b, 0, _CB[name] * LANES // BRANCH_W), layer)
    older = pl.BlockSpec((None, None, BRANCH_W, past), lambda b: (layer, b, 0, 0))
    rows = B_HEADS * t
    return pl.pallas_call(
        functools.partial(_sb_kernel, ATT_BLK, past // ATT_BLK, True),
        grid=(bn,),
        in_specs=[new("qb"), new("kb"), new("vb"), older, older],
        out_specs=pl.BlockSpec((None, t, BRANCH_W), lambda b: (b, 0, 0)),
        out_shape=jax.ShapeDtypeStruct((bn, t, BRANCH_W), BF16),
        scratch_shapes=[pltpu.VMEM((1, rows, LANES), F32), pltpu.VMEM((1, rows, BRANCH_W), F32)],
        compiler_params=_cparams("parallel"),
        name="stick_attn_sample",
    )(p4, p4, p4, k_cache_t, v_cache_t)


def _diff_combine(o0, o1, lam_init, lam_ref, gain_ref):
    lp = lam_ref[...]
    lam = (jnp.exp(jnp.sum(lp[0:1] * lp[1:2], axis=1, keepdims=True))
           - jnp.exp(jnp.sum(lp[2:3] * lp[3:4], axis=1, keepdims=True)) + lam_init)
    return _rms(o0 - lam * o1, gain_ref[...]) * (1.0 - lam_init)


def _diff_fold(qs, k_t, v, bias, m_ref, l_ref, acc_ref, first):
    kb = k_t.astype(BF16)
    vb = v.astype(BF16)
    rows, tk = qs.shape[0], vb.shape[0]
    s = _dot(qs, kb) + bias
    m_cur = jnp.max(s, axis=1, keepdims=True)
    m_new = jnp.broadcast_to(m_cur, (rows, LANES)) if first else jnp.maximum(m_ref[...], m_cur)
    p = jnp.exp2(s - _lane_tile(m_new, tk // LANES))
    row = jnp.sum(p, axis=1, keepdims=True)
    pv = _dot(p.astype(BF16), vb)
    if first:
        l_ref[...] = jnp.broadcast_to(row, (rows, LANES))
        acc_ref[...] = pv
    else:
        alpha = jnp.exp2(m_ref[...] - m_new)
        l_ref[...] = alpha * l_ref[...] + row
        acc_ref[...] = alpha * acc_ref[...] + pv
    m_ref[...] = m_new


def _diff_prompt_kernel(lam_init, tbl_ref, lam_ref, gain_ref, q_ref, kd_ref, vd_ref, kp_ref, vp_ref,
                        bd_ref, bp_ref, o_ref, m_ref, l_ref, acc_ref):
    h = pl.program_id(1)
    qi = pl.program_id(2)
    q = q_ref[...] * SOFTMAX_QSCALE
    blk = q.shape[0]
    qm = _stack_heads(q, HEAD_DIM)
    far = tbl_ref[T5_FAR_BUCKET * C_HEADS + h] * LOG2E

    near = pl.multiple_of(jnp.maximum(qi - 1, 0) * blk, blk)
    k0 = jnp.concatenate([kp_ref[:, pl.ds(near, blk)], kd_ref[...]], axis=1)
    v0 = jnp.concatenate([vp_ref[pl.ds(near, blk), :], vd_ref[...]], axis=0)
    b0 = jnp.concatenate([jnp.where(qi >= 1, bp_ref[...], NEG_INF), bd_ref[...]], axis=1)
    b0 = jnp.concatenate([b0, b0], axis=0)
    _diff_fold(qm, k0, v0, b0, m_ref, l_ref, acc_ref, True)

    n_far = jnp.maximum(qi - 1, 0)
    n_group = n_far // DIFF_FAR_GROUP
    wide = DIFF_FAR_GROUP * blk

    def body(j, c):
        start = pl.multiple_of(j * wide, wide)
        _diff_fold(qm, kp_ref[:, pl.ds(start, wide)], vp_ref[pl.ds(start, wide), :], far,
                   m_ref, l_ref, acc_ref, False)
        return c

    lax.fori_loop(0, n_group, body, 0)
    rest = pl.multiple_of(n_group * wide, wide)
    for r in range(1, DIFF_FAR_GROUP):
        @pl.when(n_far - n_group * DIFF_FAR_GROUP == r)
        def _(r=r):
            _diff_fold(qm, kp_ref[:, pl.ds(rest, r * blk)], vp_ref[pl.ds(rest, r * blk), :], far,
                       m_ref, l_ref, acc_ref, False)

    o = acc_ref[...] / l_ref[...]
    o = _diff_combine(o[:blk], o[blk:], lam_init, lam_ref, gain_ref)
    o_ref[...] = o.astype(o_ref.dtype)


def _diff_sample_kernel(lam_init, lam_ref, gain_ref, q_ref, kc_ref, vc_ref, kn_ref, vn_ref,
                        bc_ref, bn_ref, o_ref):
    q_all = q_ref[...] * SOFTMAX_QSCALE
    past = kc_ref.shape[1]
    k_t, v_t = (True, False), (False, False)
    for h in range(C_HEADS):
        cols = slice(h * LANES, (h + 1) * LANES)
        q = q_all[:, cols]
        kb = [kc_ref[cols, :].astype(BF16), kn_ref[:, cols].astype(BF16)]
        vb = [vc_ref[pl.ds(h, past, stride=C_HEADS), :].astype(BF16), vn_ref[:, cols].astype(BF16)]
        biases = [jnp.concatenate([b[h], b[h]], axis=0) for b in (bc_ref, bn_ref)]
        t = q.shape[0]
        o = _softmax_segments(_stack_heads(q, HEAD_DIM), kb, vb, biases, k_t, v_t)
        o_ref[:, cols] = _diff_combine(o[:t], o[t:], lam_init, lam_ref, gain_ref).astype(o_ref.dtype)


def _diff_attn_prompt(pn4, layer, k_t, t5tiles, lam_init, t5_flat, lam_params, gain):
    _, bn, s, _ = pn4.shape
    blk = (None, ATT_BLK, LANES)
    full = (None, s, LANES)
    state = pltpu.VMEM((2 * ATT_BLK, LANES), F32)
    return pl.pallas_call(
        functools.partial(_diff_prompt_kernel, lam_init),
        grid=(bn, C_HEADS, s // ATT_BLK),
        in_specs=[pl.BlockSpec(memory_space=pltpu.SMEM),
                  pl.BlockSpec((4, HEAD_DIM), lambda b, h, qi: (0, 0)),
                  pl.BlockSpec((1, C_VDIM), lambda b, h, qi: (0, 0)),
                  _pspec(blk, lambda b, h, qi: (b, qi, _CBP["qc"] + h), layer),
                  pl.BlockSpec((None, None, LANES, ATT_BLK), lambda b, h, qi: (layer, b, h, qi)),
                  _pspec(blk, lambda b, h, qi: (b, qi, _CBP["vc"] + h), layer),
                  pl.BlockSpec((None, None, LANES, s), lambda b, h, qi: (layer, b, h, 0)),
                  _pspec(full, lambda b, h, qi: (b, 0, _CBP["vc"] + h), layer),
                  pl.BlockSpec((None, None, ATT_BLK, ATT_BLK), lambda b, h, qi: (h, 0, 0, 0)),
                  pl.BlockSpec((None, None, ATT_BLK, ATT_BLK), lambda b, h, qi: (h, 1, 0, 0))],
        out_specs=pl.BlockSpec(blk, lambda b, h, qi: (b, qi, h)),
        out_shape=jax.ShapeDtypeStruct((bn, s, BRANCH_W), BF16),
        scratch_shapes=[state, state, state],
        compiler_params=_cparams("parallel", "parallel", "arbitrary"),
        name="diff_attn_prompt",
    )(t5_flat, lam_params, gain.reshape(1, C_VDIM), pn4, k_t, pn4, k_t, pn4, t5tiles, t5tiles)


def _diff_attn_sample(p4, layer, k_cache_t, v_cache, bias_cache, bias_new, lam_init, lam_params, gain):
    _, bn, t, _ = p4.shape
    past = k_cache_t.shape[-1]
    new = lambda name: _pspec((None, t, BRANCH_W), lambda b: (b, 0, _CB[name] * LANES // BRANCH_W), layer)
    return pl.pallas_call(
        functools.partial(_diff_sample_kernel, lam_init),
        grid=(bn,),
        in_specs=[pl.BlockSpec((4, HEAD_DIM), lambda b: (0, 0)),
                  pl.BlockSpec((1, C_VDIM), lambda b: (0, 0)),
                  new("qc"),
                  pl.BlockSpec((None, None, BRANCH_W, past), lambda b: (layer, b, 0, 0)),
                  pl.BlockSpec((None, None, past * C_HEADS, C_VDIM), lambda b: (layer, b, 0, 0)),
                  new("kc"), new("vc"),
                  pl.BlockSpec((C_HEADS, t, past), lambda b: (0, 0, 0)),
                  pl.BlockSpec((C_HEADS, t, t), lambda b: (0, 0, 0))],
        out_specs=pl.BlockSpec((None, t, BRANCH_W), lambda b: (b, 0, 0)),
        out_shape=jax.ShapeDtypeStruct((bn, t, BRANCH_W), BF16),
        compiler_params=_cparams("parallel"),
        name="diff_attn_sample",
    )(lam_params, gain.reshape(1, C_VDIM), p4, k_cache_t, v_cache, p4, p4, bias_cache, bias_new)


def _merge_kernel(x_ref, oa_ref, ob_ref, oc_ref, g0_ref, g1_ref, g2_ref, bg_ref, wb_ref, wo_ref, o_ref):
    h = None
    for n, (o_r, g_r) in enumerate(((oa_ref, g0_ref), (ob_ref, g1_ref), (oc_ref, g2_ref))):
        gate = 1.0 / (1.0 + jnp.exp(-(g_r[...] + bg_ref[n:n + 1, :])))
        t = gate * _dot(o_r[...], wb_ref[n])
        h = t if h is None else h + t
    o_ref[...] = x_ref[...] + _dot(h.astype(BF16), wo_ref[...])


def _merge(x, o_a, o_b, o_c, p, layer, b_gate, w_branch, w_out, tm):
    t, d = x.shape
    row = lambda i: (i, 0)
    fixed2 = lambda i: (0, 0)
    return pl.pallas_call(
        _merge_kernel,
        grid=(t // tm,),
        in_specs=[pl.BlockSpec((tm, d), row),
                  pl.BlockSpec((tm, BRANCH_W), row),
                  pl.BlockSpec((tm, BRANCH_W), row),
                  pl.BlockSpec((tm, BRANCH_W), row),
                  pl.BlockSpec((None, tm, d), lambda i: (layer, i, 0)),
                  pl.BlockSpec((None, tm, d), lambda i: (layer, i, 1)),
                  pl.BlockSpec((None, tm, d), lambda i: (layer, i, 2)),
                  pl.BlockSpec((N_BRANCH, d), fixed2),
                  pl.BlockSpec((N_BRANCH, BRANCH_W, d), lambda i: (0, 0, 0)),
                  pl.BlockSpec((d, d), fixed2)],
        out_specs=pl.BlockSpec((tm, d), row),
        out_shape=jax.ShapeDtypeStruct((t, d), F32),
        compiler_params=_cparams("parallel"),
        name="gated_merge",
    )(x, o_a, o_b, o_c, p, p, p, b_gate, w_branch, w_out)


FFN_TF = D_FF // 2
FFN_PREV = 16


def _gelu_tanh(x):
    return x * (0.5 * (1.0 + jnp.tanh(math.sqrt(2.0 / math.pi) * (x + 0.044715 * (x * x * x)))))


def _ffn_kernel(tiles_per_seq, final, x_ref, xp_ref, gn_ref, wg_ref, wu_ref, cw_ref, cb_ref, wd_ref, st_ref, gf_ref,
                o_ref, so_ref, xn_ref, xpn_ref):
    i = pl.program_id(0)
    j = pl.program_id(1)
    whole_seqs = tiles_per_seq == 0

    @pl.when(j == 0)
    def _():
        xn_ref[...] = _rms(x_ref[...], gn_ref[...]).astype(BF16)
        if not whole_seqs:
            xpn_ref[...] = _rms(xp_ref[...], gn_ref[...]).astype(BF16)

    xn = xn_ref[...]
    tm = xn.shape[0]
    tf = wg_ref.shape[1]
    g = _dot(xn, wg_ref[...])
    u = _dot(xn, wu_ref[...])
    if whole_seqs:
        ns = st_ref.shape[0]
        st = st_ref[...]
        g = g.reshape(ns, tm // ns, tf)
        u = u.reshape(ns, tm // ns, tf)
        pm2, pm1 = st[:, 0:1, :], st[:, 1:2, :]
    else:
        gp = _dot(xpn_ref[...], wg_ref[...])
        first = (i % tiles_per_seq) == 0
        st = st_ref[0]
        pm2 = jnp.where(first, st[0:1, :], gp[FFN_PREV - 2:FFN_PREV - 1, :])
        pm1 = jnp.where(first, st[1:2, :], gp[FFN_PREV - 1:FFN_PREV, :])
    ax = g.ndim - 2
    row = lax.broadcasted_iota(jnp.int32, g.shape, ax)
    g1 = jnp.where(row == 0, pm1, pltpu.roll(g, 1, ax))
    g2 = jnp.where(row == 0, pm2, jnp.where(row == 1, pm1, pltpu.roll(g, 2, ax)))
    cw = cw_ref[...]
    c = cb_ref[...] + cw[0:1, :] * g2 + cw[1:2, :] * g1 + cw[2:3, :] * g
    hid = (_gelu_tanh(c) * u).reshape(tm, tf).astype(BF16)
    contrib = _dot(hid, wd_ref[...])

    last = pl.num_programs(1) - 1
    finish = (lambda y: _rms(y, gf_ref[...])) if final else (lambda y: y)

    @pl.when(j == 0)
    def _():
        o_ref[...] = x_ref[...] + contrib

    @pl.when(jnp.logical_and(j > 0, j < last))
    def _():
        o_ref[...] += contrib

    @pl.when(jnp.logical_and(j > 0, j == last))
    def _():
        o_ref[...] = finish(o_ref[...] + contrib)

    seq_len = g.shape[ax]
    if whole_seqs:
        so_ref[...] = g[:, seq_len - 2:seq_len, :]
    else:
        so_ref[0] = g[seq_len - 2:seq_len, :]


def _conv_ffn(x, g_norm, w_gate, w_up, conv_w, conv_b, w_down, state, seq_len, tm, g_final, final):
    t, d = x.shape
    nseq = t // seq_len
    nf = D_FF // FFN_TF
    assert nf > 1
    if seq_len >= tm:
        tiles_per_seq = seq_len // tm
        st_spec = pl.BlockSpec((1, 2, FFN_TF), lambda i, j: (i // tiles_per_seq, 0, j))
    else:
        tiles_per_seq = 0
        st_spec = pl.BlockSpec((tm // seq_len, 2, FFN_TF), lambda i, j: (i, 0, j))
    prev_blocks = tm // FFN_PREV
    tail_spec = pl.BlockSpec(st_spec.block_shape, lambda i, j: (i, 0, j))
    n_tail = t // tm * st_spec.block_shape[0]
    y, tails = pl.pallas_call(
        functools.partial(_ffn_kernel, tiles_per_seq, final),
        grid=(t // tm, nf),
        in_specs=[pl.BlockSpec((tm, d), lambda i, j: (i, 0)),
                  pl.BlockSpec((FFN_PREV, d), lambda i, j: (jnp.maximum(i * prev_blocks - 1, 0), 0)),
                  pl.BlockSpec((1, d), lambda i, j: (0, 0)),
                  pl.BlockSpec((d, FFN_TF), lambda i, j: (0, j)),
                  pl.BlockSpec((d, FFN_TF), lambda i, j: (0, j)),
                  pl.BlockSpec((3, FFN_TF), lambda i, j: (0, j)),
                  pl.BlockSpec((1, FFN_TF), lambda i, j: (0, j)),
                  pl.BlockSpec((FFN_TF, d), lambda i, j: (j, 0)),
                  st_spec,
                  pl.BlockSpec((1, d), lambda i, j: (0, 0))],
        out_specs=[pl.BlockSpec((tm, d), lambda i, j: (i, 0)), tail_spec],
        out_shape=[jax.ShapeDtypeStruct((t, d), F32),
                   jax.ShapeDtypeStruct((n_tail, 2, D_FF), F32)],
        scratch_shapes=[pltpu.VMEM((tm, d), BF16), pltpu.VMEM((FFN_PREV, d), BF16)],
        compiler_params=_cparams("arbitrary", "arbitrary"),
        name="conv_ffn_prompt" if tiles_per_seq else "conv_ffn_sample",
    )(x, x, g_norm.reshape(1, d), w_gate, w_up, conv_w, conv_b.reshape(1, D_FF), w_down, state, g_final.reshape(1, d))
    return y, tails[n_tail // nseq - 1::n_tail // nseq]


PROJ_TM, PROJ_TN = 1024, 1280
ROW_TM = 512


def _lam_init(layer):
    return 0.8 - 0.6 * math.exp(-0.3 * layer)


def _cols(p4, name):
    c0 = _CB[name] * LANES
    return p4[..., c0:c0 + BRANCH_W]


def _layer_tail(x2, o_a, o_b, o_c, p_rows, layer, prm, conv_state, seq_len):
    flat = lambda a: a.reshape(x2.shape[0], BRANCH_W)
    x2 = _merge(x2, flat(o_a), flat(o_b), flat(o_c), p_rows, layer, prm["b_gate"][layer],
                prm["w_branch"][layer], prm["w_out"][layer], ROW_TM)
    depth = prm["norm_ffn"].shape[0]
    return _conv_ffn(x2, prm["norm_ffn"][layer], prm["w_gate"][layer], prm["w_up"][layer],
                     prm["conv_w"][layer], prm["conv_b"][layer], prm["w_down"][layer], conv_state, seq_len, ROW_TM,
                     prm["norm_final"], layer == depth - 1)


def _trunk_prompt(x, prm, abias, t5tiles, t5_flat):
    bn, s, d = x.shape
    x2 = x.reshape(bn * s, d)
    depth = prm["w_rows"].shape[0]
    conv_states = []
    rows, kv_t = None, None
    for l in range(depth):
        rows = _norm_proj(x2, prm["norm_mix"][l], prm["w_rows"][l], PROJ_TM, PROJ_TN, l, depth, rows,
                          heads_group=(_CBP["vc"] * LANES, C_HEADS))
        p_rows, vc_heads = rows
        kv_t = _norm_proj_t(x2, prm["norm_mix"][l], prm["w_feat_t"][l], PROJ_TM, l, depth, bn, kv_t)
        ka_t, va_t, kb_t, vb_t, kc_t = kv_t
        pn4 = p_rows.reshape(depth, bn, s, ROW_COLS)
        o_a = _band_attn_prompt(pn4, l, ka_t, va_t, abias[l])
        o_b = _sb_attn_prompt(pn4, l, kb_t, vb_t)
        o_c = _diff_attn_prompt(pn4, l, kc_t, t5tiles, _lam_init(l), t5_flat, prm["c_lambda"][l], prm["c_subln"][l])
        x2, st_conv = _layer_tail(x2, o_a, o_b, o_c, p_rows, l, prm, jnp.zeros((bn, 2, D_FF), F32), s)
        conv_states.append(st_conv)
    y = x2.reshape(bn, s, d)
    heads_t = lambda a: jnp.transpose(a.reshape(depth, bn, A_HEADS, HEAD_DIM, a.shape[-1]), (0, 1, 4, 2, 3))
    keep = min(A_PAST_CHUNKS * CHUNK, s)
    states = (heads_t(ka_t[..., s - keep:]), heads_t(va_t[..., s - keep:]), heads_t(kb_t), heads_t(vb_t),
              jnp.transpose(kc_t.reshape(depth, bn, C_HEADS, 2, HEAD_DIM, s), (0, 1, 5, 2, 3, 4)),
              vc_heads.reshape(depth, bn, s, C_HEADS, C_VDIM),
              jnp.stack(conv_states, axis=0))
    return y, states


def _feature_major(cache):
    nd = cache.ndim
    t = jnp.transpose(cache, (0, 1) + tuple(range(3, nd)) + (2,))
    return t.reshape(cache.shape[0], cache.shape[1], BRANCH_W, cache.shape[2])


def _trunk_sample(x, caches, prm, abias, t5tiles):
    bn, t, d = x.shape
    x2 = x.reshape(bn * t, d)
    depth = prm["w_in"].shape[0]
    ca_k, ca_v, cb_k, cb_v, cc_k, cc_v, st_conv_in = caches
    ca_kt, ca_vt, cb_kt, cb_vt, cc_kt = (_feature_major(c) for c in (ca_k, ca_v, cb_k, cb_v, cc_k))
    keep, past = ca_k.shape[2], cc_k.shape[2]
    cc_vr = cc_v.reshape(depth, bn, past * C_HEADS, C_VDIM)
    conv_states = []
    p_all = None
    for l in range(depth):
        p_all, = _norm_proj(x2, prm["norm_mix"][l], prm["w_in"][l], PROJ_TM, PROJ_TN, l, depth,
                            None if p_all is None else [p_all])
        p4 = p_all.reshape(depth, bn, t, IN_COLS)
        bias_cache = jnp.concatenate([abias[l][:, 0, :t, :], abias[l][:, 1, :t, :]], axis=-1)[..., -keep:]
        o_a = _band_attn_sample(p4, l, ca_kt, ca_vt, bias_cache, abias[l][:, 2, :t, :t])
        o_b = _sb_attn_sample(p4, l, cb_kt, cb_vt)
        far = jnp.broadcast_to(t5tiles[:, 1, :1, :1], (C_HEADS, t, past - ATT_BLK))
        bias_c = jnp.concatenate([far, t5tiles[:, 1, :t, :]], axis=-1)
        o_c = _diff_attn_sample(p4, l, cc_kt, cc_vr, bias_c, t5tiles[:, 0, :t, :t],
                                _lam_init(l), prm["c_lambda"][l], prm["c_subln"][l])
        x2, st_conv = _layer_tail(x2, o_a, o_b, o_c, p_all, l, prm, st_conv_in[l], t)
        conv_states.append(st_conv)
    y = x2.reshape(bn, t, d)
    kb, vb, kc, vc = (_cols(p4, n) for n in ("kb", "vb", "kc", "vc"))
    heads = lambda a: a.reshape(a.shape[:3] + (A_HEADS, HEAD_DIM))
    heads_t = lambda a: jnp.transpose(a.reshape(depth, bn, A_HEADS, HEAD_DIM, a.shape[-1]), (0, 1, 4, 2, 3))
    sa_kt, sa_vt = _band_roll(ca_kt, ca_vt, p4)
    states = (heads_t(sa_kt), heads_t(sa_vt), heads(kb), heads(vb),
              kc.reshape(depth, bn, t, C_HEADS, 2, HEAD_DIM), vc.reshape(depth, bn, t, C_HEADS, C_VDIM),
              jnp.stack(conv_states, axis=0))
    return y, states


def kernel(x_prompt, x_sample, cache_a_k, cache_a_v, cache_b_k, cache_b_v, cache_c_k, cache_c_v,
           state_ffn_conv, norm_mix, w_in, b_gate, a_rel_bias, t5_bias, c_lambda, c_subln,
           w_branch, w_out, norm_ffn, w_up, conv_w, conv_b, w_down, norm_final):
    group = {name: w_in[..., i * BRANCH_W:(i + 1) * BRANCH_W] for i, name in enumerate(_GROUPS)}
    gate = w_in[..., len(_GROUPS) * BRANCH_W:]
    prm = dict(
        norm_mix=norm_mix,
        w_in=jnp.concatenate([gate] + [group[n] for n in _GROUPS], axis=-1).astype(BF16),
        w_rows=jnp.concatenate([gate] + [group[n] for n in _ROW_GROUPS], axis=-1).astype(BF16),
        w_feat_t=jnp.swapaxes(jnp.concatenate([group[n] for n in _T_GROUPS], axis=-1), 1, 2).astype(BF16),
        b_gate=b_gate, c_lambda=c_lambda, c_subln=c_subln,
        w_branch=w_branch.astype(BF16), w_out=w_out.astype(BF16), norm_ffn=norm_ffn,
        w_gate=w_up[..., :D_FF].astype(BF16), w_up=w_up[..., D_FF:].astype(BF16),
        conv_w=conv_w, conv_b=conv_b, w_down=w_down.astype(BF16), norm_final=norm_final)
    abias = [_build_abias(a_rel_bias[l]) for l in range(a_rel_bias.shape[0])]
    t5tiles = _build_t5bias(t5_bias)
    y_p, p_states = _trunk_prompt(x_prompt, prm, abias, t5tiles, t5_bias.reshape(-1))
    caches = (cache_a_k, cache_a_v, cache_b_k, cache_b_v, cache_c_k, cache_c_v, state_ffn_conv)
    y_s, s_states = _trunk_sample(x_sample, caches, prm, abias, t5tiles)
    return (y_p, y_s) + p_states + s_states
```

```python
import functools
import math

import jax
import jax.numpy as jnp
from jax import lax
from jax.experimental import pallas as pl
from jax.experimental.pallas import tpu as pltpu

F32 = jnp.float32
BF16 = jnp.bfloat16

D_MODEL = 1024
CHUNK = 64
HEAD_DIM = 64
A_HEADS = 8
A_PAST_CHUNKS = 8
A_REL_CLIP = 128
B_HEADS = 8
C_HEADS = 4
C_VDIM = 128
T5_BUCKETS = 32
D_FF = 2816
N_BRANCH = 3
BRANCH_W = 512
IN_COLS = 9 * BRANCH_W
EPS = 1e-6
NEG_INF = -1e30
SCALE = HEAD_DIM ** -0.5
LOG2E = math.log2(math.e)
SOFTMAX_QSCALE = SCALE * LOG2E

LANES = 128
_GROUPS = ("qa", "ka", "va", "qb", "kb", "vb", "qc", "kc", "vc")
_CB = {name: i * BRANCH_W // LANES for i, name in enumerate(_GROUPS)}
_ROW_GROUPS = ("qa", "qb", "qc", "vc")
_T_GROUPS = ("ka", "va", "kb", "vb", "kc")
_CBP = {name: i * BRANCH_W // LANES for i, name in enumerate(_ROW_GROUPS)}
ROW_COLS = len(_ROW_GROUPS) * BRANCH_W

ATT_BLK = 256
ATT_HEADS = 4
SB_LOG_CUT = -104.0
T5_FAR_BUCKET = 15
DIFF_FAR_GROUP = 4
V7X_VMEM_BYTES = 64 * 1024 * 1024
VMEM_LIMIT = V7X_VMEM_BYTES - 8 * 1024 * 1024


def _cparams(*sem):
    return pltpu.CompilerParams(dimension_semantics=sem, vmem_limit_bytes=VMEM_LIMIT)


def _rms(x, g):
    return x * lax.rsqrt(jnp.mean(x * x, axis=-1, keepdims=True) + EPS) * g


def _dot(a, b):
    return jnp.dot(a, b, preferred_element_type=F32)


def _dot_nt(a, b):
    return lax.dot_general(a, b, (((1,), (1,)), ((), ())), preferred_element_type=F32)


def _norm_proj_kernel(n_stack, heads_cols, x_ref, g_ref, w_ref, *refs):
    outs, xn_ref = refs[n_stack:-1], refs[-1]
    j = pl.program_id(1)

    @pl.when(j == 0)
    def _():
        xn_ref[...] = _rms(x_ref[...], g_ref[...]).astype(BF16)

    res = _dot(xn_ref[...], w_ref[...])
    outs[0][...] = res
    if heads_cols is not None:
        tile, first, heads = heads_cols

        @pl.when(j == tile)
        def _():
            for h in range(heads):
                lo = first + h * LANES
                outs[1][pl.ds(h, res.shape[0], stride=heads), :] = res[:, lo:lo + LANES]


def _norm_proj(x, g, w, tm, tn, layer, depth, stacked, heads_group=None):
    t, d = x.shape
    n = w.shape[1]
    in_specs = [pl.BlockSpec((tm, d), lambda i, j: (i, 0)),
                pl.BlockSpec((1, d), lambda i, j: (0, 0)),
                pl.BlockSpec((d, tn), lambda i, j: (0, j))]
    args = [x, g.reshape(1, d), w]
    out_specs = [pl.BlockSpec((None, tm, tn), lambda i, j: (layer, i, j))]
    out_shape = [jax.ShapeDtypeStruct((depth, t, n), F32)]
    heads_cols = None
    if heads_group is not None:
        first, heads = heads_group
        heads_cols = (first // tn, first % tn, heads)
        out_specs.append(pl.BlockSpec((None, tm * heads, LANES), lambda i, j: (layer, i, 0)))
        out_shape.append(jax.ShapeDtypeStruct((depth, t * heads, LANES), F32))
    n_stack = 0 if stacked is None else len(out_shape)
    if stacked is not None:
        in_specs += [pl.BlockSpec(memory_space=pl.ANY)] * n_stack
        args += list(stacked)
    return pl.pallas_call(
        functools.partial(_norm_proj_kernel, n_stack, heads_cols),
        grid=(t // tm, n // tn),
        in_specs=in_specs,
        out_specs=out_specs,
        out_shape=out_shape,
        scratch_shapes=[pltpu.VMEM((tm, d), BF16)],
        input_output_aliases={3 + k: k for k in range(n_stack)},
        compiler_params=_cparams("arbitrary", "arbitrary"),
        name="norm_proj",
    )(*args)


def _norm_proj_t_kernel(ngroups, x_ref, g_ref, wt_ref, *refs):
    o_refs = refs[-ngroups:]
    xn = _rms(x_ref[...], g_ref[...]).astype(BF16)
    for g in range(ngroups):
        o_refs[g][...] = _dot_nt(wt_ref[g * BRANCH_W:(g + 1) * BRANCH_W, :], xn)


def _norm_proj_t(x, g, wt, tm, layer, depth, bn, stacked):
    t, d = x.shape
    s = t // bn
    ngroups = wt.shape[0] // BRANCH_W
    tiles = s // tm
    in_specs = [pl.BlockSpec((tm, d), lambda i: (i, 0)),
                pl.BlockSpec((1, d), lambda i: (0, 0)),
                pl.BlockSpec(wt.shape, lambda i: (0, 0))]
    args = [x, g.reshape(1, d), wt]
    aliases = {}
    if stacked is not None:
        in_specs += [pl.BlockSpec(memory_space=pl.ANY)] * ngroups
        args += list(stacked)
        aliases = {3 + n: n for n in range(ngroups)}
    out_spec = pl.BlockSpec((None, None, BRANCH_W, tm), lambda i: (layer, i // tiles, 0, i % tiles))
    return pl.pallas_call(
        functools.partial(_norm_proj_t_kernel, ngroups),
        grid=(t // tm,),
        in_specs=in_specs,
        out_specs=[out_spec] * ngroups,
        out_shape=[jax.ShapeDtypeStruct((depth, bn, BRANCH_W, s), F32)] * ngroups,
        input_output_aliases=aliases,
        compiler_params=_cparams("arbitrary"),
        name="norm_proj_t",
    )(*args)


def _abias_kernel(tbl_ref, o_ref):
    h = pl.program_id(0)
    sub = 8
    ql = lax.broadcasted_iota(jnp.int32, (ATT_BLK, ATT_BLK), 0)
    kl = lax.broadcasted_iota(jnp.int32, (ATT_BLK, ATT_BLK), 1)
    far = jnp.full((ATT_BLK, ATT_BLK), tbl_ref[2 * A_REL_CLIP * A_HEADS + h] * LOG2E, F32)
    o_ref[0, 0] = jnp.where((kl >> 6) >= (ql >> 6), far, NEG_INF)
    lead = ATT_BLK - sub
    b = lax.broadcasted_iota(jnp.int32, (sub, 2 * ATT_BLK), 0)
    x = lax.broadcasted_iota(jnp.int32, (sub, 2 * ATT_BLK), 1)
    for seg, off in ((1, ATT_BLK), (2, 0)):
        idx = jnp.clip(b + lead - x + off, -A_REL_CLIP, A_REL_CLIP) + A_REL_CLIP

        def body(r, acc, idx=idx):
            return jnp.where(idx == r, tbl_ref[r * A_HEADS + h] * LOG2E, acc)

        strip = lax.fori_loop(0, 2 * A_REL_CLIP + 1, body, jnp.zeros((sub, 2 * ATT_BLK), F32))
        for a in range(ATT_BLK // sub):
            o_ref[0, seg, a * sub:(a + 1) * sub, :] = strip[:, lead - a * sub:lead - a * sub + ATT_BLK]
    o_ref[0, 2] = jnp.where((kl >> 6) <= (ql >> 6), o_ref[0, 2], NEG_INF)


def _build_abias(table):
    return pl.pallas_call(
        _abias_kernel,
        grid=(A_HEADS,),
        in_specs=[pl.BlockSpec(memory_space=pltpu.SMEM)],
        out_specs=pl.BlockSpec((1, 3, ATT_BLK, ATT_BLK), lambda h: (h, 0, 0, 0)),
        out_shape=jax.ShapeDtypeStruct((A_HEADS, 3, ATT_BLK, ATT_BLK), F32),
        compiler_params=_cparams("parallel"),
        name="band_bias_tiles",
    )(table.reshape(-1))


_T5_LARGE_STEPS = (12, 16, 23, 32, 46, 64, 91)


def _t5bias_kernel(tbl_ref, o_ref):
    h = pl.program_id(0)
    rows = 64
    for seg, off in ((0, 0), (1, -ATT_BLK)):
        for rc in range(ATT_BLK // rows):
            ql = lax.broadcasted_iota(jnp.int32, (rows, ATT_BLK), 0) + rc * rows
            kl = lax.broadcasted_iota(jnp.int32, (rows, ATT_BLK), 1)
            rel = kl - ql + off
            n = jnp.abs(rel)
            large = jnp.full((rows, ATT_BLK), T5_BUCKETS // 4, jnp.int32)
            for th in _T5_LARGE_STEPS:
                large = large + jnp.where(n >= th, 1, 0)
            bucket = jnp.where(rel > 0, T5_BUCKETS // 2, 0) + jnp.where(n < T5_BUCKETS // 4, n, large)
            tile = jnp.zeros((rows, ATT_BLK), F32)
            for r in range(T5_BUCKETS):
                tile = jnp.where(bucket == r, tbl_ref[r * C_HEADS + h] * LOG2E, tile)
            if seg == 0:
                tile = jnp.where((kl >> 6) <= (ql >> 6), tile, NEG_INF)
            o_ref[0, seg, rc * rows:(rc + 1) * rows, :] = tile


def _build_t5bias(table):
    return pl.pallas_call(
        _t5bias_kernel,
        grid=(C_HEADS,),
        in_specs=[pl.BlockSpec(memory_space=pltpu.SMEM)],
        out_specs=pl.BlockSpec((1, 2, ATT_BLK, ATT_BLK), lambda h: (h, 0, 0, 0)),
        out_shape=jax.ShapeDtypeStruct((C_HEADS, 2, ATT_BLK, ATT_BLK), F32),
        compiler_params=_cparams("parallel"),
        name="t5_bias_tiles",
    )(table.reshape(-1))


def _pspec(block, index_map, layer):
    return pl.BlockSpec((None,) + block, lambda *g: (layer,) + tuple(index_map(*g)))


def _lane_tile(x, n):
    return x if n == 1 else jnp.concatenate([x] * n, axis=1)


def _stack_heads(q, width):
    group = lax.broadcasted_iota(jnp.int32, q.shape, 1) // width
    return jnp.concatenate([jnp.where(group == h, q, 0.0) for h in range(q.shape[1] // width)],
                           axis=0).astype(BF16)


def _unstack_heads(x, t, width):
    group = lax.broadcasted_iota(jnp.int32, (t, x.shape[1]), 1) // width
    out = jnp.zeros((t, x.shape[1]), x.dtype)
    for h in range(x.shape[1] // width):
        out = jnp.where(group == h, x[h * t:(h + 1) * t, :], out)
    return out


def _qk(q, k, k_t):
    return _dot(q, k) if k_t else _dot_nt(q, k)


def _pv(p, v, v_t):
    return _dot_nt(p, v) if v_t else _dot(p, v)


def _softmax_segments(qh, kb, vb, biases, k_t, v_t):
    s = [_qk(qh, k, t) + b for k, b, t in zip(kb, biases, k_t)]
    m = functools.reduce(jnp.maximum, [jnp.max(sj, axis=1, keepdims=True) for sj in s])
    p = [jnp.exp2(sj - m) for sj in s]
    l = functools.reduce(jnp.add, [jnp.sum(pj, axis=1, keepdims=True) for pj in p])
    acc = functools.reduce(jnp.add, [_pv(pj.astype(BF16), v, t) for pj, v, t in zip(p, vb, v_t)])
    return acc / l


def _band_kernel(nseg, min_qi, q_ref, *refs):
    k_refs, v_refs, b_refs = refs[:nseg], refs[nseg:2 * nseg], refs[2 * nseg:3 * nseg]
    o_ref = refs[3 * nseg]
    qi = pl.program_id(2)
    q = q_ref[...] * SOFTMAX_QSCALE
    tq = q.shape[0]
    kb = [k_refs[j][...].astype(BF16) for j in range(nseg)]
    vb = [v_refs[j][...].astype(BF16) for j in range(nseg)]
    qs = _stack_heads(q, HEAD_DIM)
    biases = []
    for j in range(nseg):
        tile = b_refs[j][...].reshape(b_refs[j].shape[0] * tq, b_refs[j].shape[-1])
        biases.append(tile if min_qi[j] == 0 else jnp.where(qi >= min_qi[j], tile, NEG_INF))
    o = _softmax_segments(qs, kb, vb, biases, (True,) * nseg, (True,) * nseg)
    o_ref[...] = _unstack_heads(o, tq, HEAD_DIM).astype(o_ref.dtype)


def _band_attn_prompt(pn4, layer, k_t, v_t, abias):
    _, bn, s, _ = pn4.shape
    nq = s // ATT_BLK
    width = ATT_HEADS * HEAD_DIM
    blk = (None, ATT_BLK, width)

    def kv_spec(back):
        return pl.BlockSpec((None, None, width, ATT_BLK),
                            lambda b, hg, qi: (layer, b, hg, jnp.maximum(qi - back, 0)))

    def bias_spec(seg):
        return pl.BlockSpec((ATT_HEADS, None, ATT_BLK, ATT_BLK), lambda b, hg, qi: (hg, seg, 0, 0))

    return pl.pallas_call(
        functools.partial(_band_kernel, 3, (2, 1, 0)),
        grid=(bn, A_HEADS // ATT_HEADS, nq),
        in_specs=([_pspec(blk, lambda b, hg, qi: (b, qi, _CBP["qa"] * LANES // width + hg), layer)]
                  + [kv_spec(back) for back in (2, 1, 0)]
                  + [kv_spec(back) for back in (2, 1, 0)]
                  + [bias_spec(seg) for seg in range(3)]),
        out_specs=pl.BlockSpec(blk, lambda b, hg, qi: (b, qi, hg)),
        out_shape=jax.ShapeDtypeStruct((bn, s, BRANCH_W), BF16),
        compiler_params=_cparams("parallel", "parallel", "arbitrary"),
        name="band_attn_prompt",
    )(pn4, k_t, k_t, k_t, v_t, v_t, v_t, abias, abias, abias)


def _band_sample_kernel(q_ref, kc_ref, kn_ref, vc_ref, vn_ref, bc_ref, bn_ref, o_ref):
    t = q_ref.shape[0]
    width = ATT_HEADS * HEAD_DIM
    q = q_ref[...] * SOFTMAX_QSCALE
    for g in range(q.shape[1] // width):
        cols = slice(g * width, (g + 1) * width)
        rows = slice(g * ATT_HEADS * t, (g + 1) * ATT_HEADS * t)
        qs = _stack_heads(q[:, cols], HEAD_DIM)
        kb = [kc_ref[cols, :].astype(BF16), kn_ref[:, cols].astype(BF16)]
        vb = [vc_ref[cols, :].astype(BF16), vn_ref[:, cols].astype(BF16)]
        o = _softmax_segments(qs, kb, vb, [bc_ref[rows, :], bn_ref[rows, :]], (True, False), (True, False))
        o_ref[:, cols] = _unstack_heads(o, t, HEAD_DIM).astype(o_ref.dtype)


def _band_attn_sample(p4, layer, k_cache_t, v_cache_t, bias_cache, bias_new):
    _, bn, t, _ = p4.shape
    keep = k_cache_t.shape[-1]
    new = lambda name: _pspec((None, t, BRANCH_W), lambda b: (b, 0, _CB[name] * LANES // BRANCH_W), layer)
    old = pl.BlockSpec((None, None, BRANCH_W, keep), lambda b: (layer, b, 0, 0))
    return pl.pallas_call(
        _band_sample_kernel,
        grid=(bn,),
        in_specs=[new("qa"), old, new("ka"), old, new("va"),
                  pl.BlockSpec((A_HEADS * t, keep), lambda b: (0, 0)),
                  pl.BlockSpec((A_HEADS * t, t), lambda b: (0, 0))],
        out_specs=pl.BlockSpec((None, t, BRANCH_W), lambda b: (b, 0, 0)),
        out_shape=jax.ShapeDtypeStruct((bn, t, BRANCH_W), BF16),
        compiler_params=_cparams("parallel"),
        name="band_attn_sample",
    )(p4, k_cache_t, p4, v_cache_t, p4, bias_cache.reshape(A_HEADS * t, keep), bias_new.reshape(A_HEADS * t, t))


def _band_roll_kernel(kc_ref, vc_ref, kn_ref, vn_ref, ko_ref, vo_ref):
    t = kn_ref.shape[0]
    keep = kc_ref.shape[1]
    for c_ref, n_ref, o_ref in ((kc_ref, kn_ref, ko_ref), (vc_ref, vn_ref, vo_ref)):
        rolled = pltpu.roll(c_ref[...], keep - t, 1)
        new = jnp.concatenate([n_ref[...], jnp.zeros((LANES - t, n_ref.shape[1]), F32)], axis=0)
        new_t = pltpu.roll(new.T, LANES - t, 1)
        lane = lax.broadcasted_iota(jnp.int32, new_t.shape, 1)
        last = jnp.where(lane >= LANES - t, new_t, rolled[:, keep - LANES:])
        o_ref[...] = jnp.concatenate([rolled[:, :keep - LANES], last], axis=1)


def _band_roll(k_cache_t, v_cache_t, p4):
    depth, bn, t, _ = p4.shape
    keep = k_cache_t.shape[-1]
    assert t <= LANES <= keep
    old = pl.BlockSpec((None, None, BRANCH_W, keep), lambda l, b: (l, b, 0, 0))
    new = lambda name: pl.BlockSpec((None, None, t, BRANCH_W), lambda l, b: (l, b, 0, _CB[name] * LANES // BRANCH_W))
    return pl.pallas_call(
        _band_roll_kernel,
        grid=(depth, bn),
        in_specs=[old, old, new("ka"), new("va")],
        out_specs=[old, old],
        out_shape=[jax.ShapeDtypeStruct(k_cache_t.shape, F32)] * 2,
        compiler_params=_cparams("parallel", "parallel"),
        name="band_roll",
    )(k_cache_t, v_cache_t, p4, p4)


def _strict_upper(n):
    r = lax.broadcasted_iota(jnp.int32, (n, n), 0)
    c = lax.broadcasted_iota(jnp.int32, (n, n), 1)
    return jnp.where(r > c, 1.0, 0.0).astype(BF16)


def _sb_fold(qh, kb, vb, kv_t, upper, mask, carry_ref, acc_ref, first):
    tq = qh[0].shape[0]
    tk = kb[0].shape[1] if kv_t else kb[0].shape[0]
    cw = upper.shape[0]
    top = None
    for h in range(len(qh)):
        z = _qk(qh[h], kb[0], kv_t)
        soft = jnp.log(1.0 + jnp.exp(-jnp.abs(z)))
        log_keep = -(jnp.maximum(z, 0.0) + soft)
        log_take = jnp.minimum(z, 0.0) - soft
        if mask is not None:
            log_keep = jnp.where(mask, log_keep, 0.0)
        hi = log_keep.astype(BF16)
        lo = (log_keep - hi.astype(F32)).astype(BF16)
        pieces, total = [], None
        for c0 in reversed(range(0, tk, cw)):
            piece = _dot(hi[:, c0:c0 + cw], upper) + _dot(lo[:, c0:c0 + cw], upper)
            pieces.insert(0, piece if total is None else piece + total)
            part = jnp.sum(log_keep[:, c0:c0 + cw], axis=1, keepdims=True)
            total = part if total is None else total + part
        after = pieces[0] if len(pieces) == 1 else jnp.concatenate(pieces, axis=1)
        if not first:
            after = after + _lane_tile(carry_ref[h], tk // LANES)
        w = jnp.exp(log_take + after)
        if mask is not None:
            w = jnp.where(mask, w, 0.0)
        pv = _pv(w.astype(BF16), vb[0], kv_t)
        if first:
            acc_ref[h] = pv
            carry = jnp.broadcast_to(total, (tq, LANES))
        else:
            acc_ref[h] += pv
            carry = carry_ref[h] + total
        carry_ref[h] = carry
        top = jnp.max(carry) if top is None else jnp.maximum(top, jnp.max(carry))
    return (top > SB_LOG_CUT).astype(jnp.int32)


def _sb_kernel(tk, n_past_static, own_rows, q_ref, kd_ref, vd_ref, kp_ref, vp_ref, o_ref, carry_ref, acc_ref):
    n_past = pl.program_id(2) if n_past_static is None else n_past_static
    q = q_ref[...] * SCALE
    tq = q.shape[0]
    rq = lax.broadcasted_iota(jnp.int32, (tq, tq), 0)
    ck = lax.broadcasted_iota(jnp.int32, (tq, tq), 1)
    qh = [_stack_heads(q, HEAD_DIM)]
    causal = jnp.concatenate([ck < rq] * (q.shape[1] // HEAD_DIM), axis=0)
    if own_rows:
        kd, vd = [kd_ref[...].astype(BF16)], [vd_ref[...].astype(BF16)]
        live = _sb_fold(qh, kd, vd, False, _strict_upper(tq), causal, carry_ref, acc_ref, True)
        n_left = n_past
    else:
        near = pl.multiple_of(jnp.maximum(n_past - 1, 0) * tq, tq)
        kd = [jnp.concatenate([kp_ref[:, pl.ds(near, tq)], kd_ref[...]], axis=1).astype(BF16)]
        vd = [jnp.concatenate([vp_ref[:, pl.ds(near, tq)], vd_ref[...]], axis=1).astype(BF16)]
        mask = jnp.concatenate([jnp.broadcast_to(n_past >= 1, causal.shape), causal], axis=1)
        live = _sb_fold(qh, kd, vd, True, _strict_upper(tq), mask, carry_ref, acc_ref, True)
        n_left = n_past - 1

    def cond(st):
        return jnp.logical_and(st[0] >= 0, st[1] > 0)

    def body(st):
        start = pl.multiple_of(st[0] * tk, tk)
        kp, vp = [kp_ref[:, pl.ds(start, tk)].astype(BF16)], [vp_ref[:, pl.ds(start, tk)].astype(BF16)]
        live = _sb_fold(qh, kp, vp, True, _strict_upper(tk), None, carry_ref, acc_ref, False)
        return st[0] - 1, live

    lax.while_loop(cond, body, (n_left - 1, live))
    o_ref[...] = _unstack_heads(acc_ref[0], tq, HEAD_DIM).astype(o_ref.dtype)


def _sb_attn_prompt(pn4, layer, k_t, v_t):
    _, bn, s, _ = pn4.shape
    width = ATT_HEADS * HEAD_DIM
    blk = (None, ATT_BLK, width)
    own = pl.BlockSpec((None, None, width, ATT_BLK), lambda b, hg, qi: (layer, b, hg, qi))
    older = pl.BlockSpec((None, None, width, s), lambda b, hg, qi: (layer, b, hg, 0))
    rows = ATT_HEADS * ATT_BLK
    return pl.pallas_call(
        functools.partial(_sb_kernel, ATT_BLK, None, False),
        grid=(bn, B_HEADS // ATT_HEADS, s // ATT_BLK),
        in_specs=[_pspec(blk, lambda b, hg, qi: (b, qi, _CBP["qb"] * LANES // width + hg), layer),
                  own, own, older, older],
        out_specs=pl.BlockSpec(blk, lambda b, hg, qi: (b, qi, hg)),
        out_shape=jax.ShapeDtypeStruct((bn, s, BRANCH_W), BF16),
        scratch_shapes=[pltpu.VMEM((1, rows, LANES), F32), pltpu.VMEM((1, rows, width), F32)],
        compiler_params=_cparams("parallel", "parallel", "arbitrary"),
        name="stick_attn_prompt",
    )(pn4, k_t, v_t, k_t, v_t)


def _sb_attn_sample(p4, layer, k_cache_t, v_cache_t):
    _, bn, t, _ = p4.shape
    past = k_cache_t.shape[-1]
    new = lambda name: _pspec((None, t, BRANCH_W), lambda b: (b, 0, _CB[name] * LANES // BRANCH_W), layer)
    older = pl.BlockSpec((None, None, BRANCH_W, past), lambda b: (layer, b, 0, 0))
    rows = B_HEADS * t
    return pl.pallas_call(
        functools.partial(_sb_kernel, ATT_BLK, past // ATT_BLK, True),
        grid=(bn,),
        in_specs=[new("qb"), new("kb"), new("vb"), older, older],
        out_specs=pl.BlockSpec((None, t, BRANCH_W), lambda b: (b, 0, 0)),
        out_shape=jax.ShapeDtypeStruct((bn, t, BRANCH_W), BF16),
        scratch_shapes=[pltpu.VMEM((1, rows, LANES), F32), pltpu.VMEM((1, rows, BRANCH_W), F32)],
        compiler_params=_cparams("parallel"),
        name="stick_attn_sample",
    )(p4, p4, p4, k_cache_t, v_cache_t)


def _diff_combine(o0, o1, lam_init, lam_ref, gain_ref):
    lp = lam_ref[...]
    lam = (jnp.exp(jnp.sum(lp[0:1] * lp[1:2], axis=1, keepdims=True))
           - jnp.exp(jnp.sum(lp[2:3] * lp[3:4], axis=1, keepdims=True)) + lam_init)
    return _rms(o0 - lam * o1, gain_ref[...]) * (1.0 - lam_init)


def _diff_fold(qs, k_t, v, bias, m_ref, l_ref, acc_ref, first):
    kb = k_t.astype(BF16)
    vb = v.astype(BF16)
    rows, tk = qs.shape[0], vb.shape[0]
    s = _dot(qs, kb) + bias
    m_cur = jnp.max(s, axis=1, keepdims=True)
    m_new = jnp.broadcast_to(m_cur, (rows, LANES)) if first else jnp.maximum(m_ref[...], m_cur)
    p = jnp.exp2(s - _lane_tile(m_new, tk // LANES))
    row = jnp.sum(p, axis=1, keepdims=True)
    pv = _dot(p.astype(BF16), vb)
    if first:
        l_ref[...] = jnp.broadcast_to(row, (rows, LANES))
        acc_ref[...] = pv
    else:
        alpha = jnp.exp2(m_ref[...] - m_new)
        l_ref[...] = alpha * l_ref[...] + row
        acc_ref[...] = alpha * acc_ref[...] + pv
    m_ref[...] = m_new


def _diff_prompt_kernel(lam_init, tbl_ref, lam_ref, gain_ref, q_ref, kd_ref, vd_ref, kp_ref, vp_ref,
                        bd_ref, bp_ref, o_ref, m_ref, l_ref, acc_ref):
    h = pl.program_id(1)
    qi = pl.program_id(2)
    q = q_ref[...] * SOFTMAX_QSCALE
    blk = q.shape[0]
    qm = _stack_heads(q, HEAD_DIM)
    far = tbl_ref[T5_FAR_BUCKET * C_HEADS + h] * LOG2E

    near = pl.multiple_of(jnp.maximum(qi - 1, 0) * blk, blk)
    k0 = jnp.concatenate([kp_ref[:, pl.ds(near, blk)], kd_ref[...]], axis=1)
    v0 = jnp.concatenate([vp_ref[pl.ds(near, blk), :], vd_ref[...]], axis=0)
    b0 = jnp.concatenate([jnp.where(qi >= 1, bp_ref[...], NEG_INF), bd_ref[...]], axis=1)
    b0 = jnp.concatenate([b0, b0], axis=0)
    _diff_fold(qm, k0, v0, b0, m_ref, l_ref, acc_ref, True)

    n_far = jnp.maximum(qi - 1, 0)
    n_group = n_far // DIFF_FAR_GROUP
    wide = DIFF_FAR_GROUP * blk

    def body(j, c):
        start = pl.multiple_of(j * wide, wide)
        _diff_fold(qm, kp_ref[:, pl.ds(start, wide)], vp_ref[pl.ds(start, wide), :], far,
                   m_ref, l_ref, acc_ref, False)
        return c

    lax.fori_loop(0, n_group, body, 0)
    rest = pl.multiple_of(n_group * wide, wide)
    for r in range(1, DIFF_FAR_GROUP):
        @pl.when(n_far - n_group * DIFF_FAR_GROUP == r)
        def _(r=r):
            _diff_fold(qm, kp_ref[:, pl.ds(rest, r * blk)], vp_ref[pl.ds(rest, r * blk), :], far,
                       m_ref, l_ref, acc_ref, False)

    o = acc_ref[...] / l_ref[...]
    o = _diff_combine(o[:blk], o[blk:], lam_init, lam_ref, gain_ref)
    o_ref[...] = o.astype(o_ref.dtype)


def _diff_sample_kernel(lam_init, lam_ref, gain_ref, q_ref, kc_ref, vc_ref, kn_ref, vn_ref,
                        bc_ref, bn_ref, o_ref):
    q_all = q_ref[...] * SOFTMAX_QSCALE
    past = kc_ref.shape[1]
    k_t, v_t = (True, False), (False, False)
    for h in range(C_HEADS):
        cols = slice(h * LANES, (h + 1) * LANES)
        q = q_all[:, cols]
        kb = [kc_ref[cols, :].astype(BF16), kn_ref[:, cols].astype(BF16)]
        vb = [vc_ref[pl.ds(h, past, stride=C_HEADS), :].astype(BF16), vn_ref[:, cols].astype(BF16)]
        biases = [jnp.concatenate([b[h], b[h]], axis=0) for b in (bc_ref, bn_ref)]
        t = q.shape[0]
        o = _softmax_segments(_stack_heads(q, HEAD_DIM), kb, vb, biases, k_t, v_t)
        o_ref[:, cols] = _diff_combine(o[:t], o[t:], lam_init, lam_ref, gain_ref).astype(o_ref.dtype)


def _diff_attn_prompt(pn4, layer, k_t, t5tiles, lam_init, t5_flat, lam_params, gain):
    _, bn, s, _ = pn4.shape
    blk = (None, ATT_BLK, LANES)
    full = (None, s, LANES)
    state = pltpu.VMEM((2 * ATT_BLK, LANES), F32)
    return pl.pallas_call(
        functools.partial(_diff_prompt_kernel, lam_init),
        grid=(bn, C_HEADS, s // ATT_BLK),
        in_specs=[pl.BlockSpec(memory_space=pltpu.SMEM),
                  pl.BlockSpec((4, HEAD_DIM), lambda b, h, qi: (0, 0)),
                  pl.BlockSpec((1, C_VDIM), lambda b, h, qi: (0, 0)),
                  _pspec(blk, lambda b, h, qi: (b, qi, _CBP["qc"] + h), layer),
                  pl.BlockSpec((None, None, LANES, ATT_BLK), lambda b, h, qi: (layer, b, h, qi)),
                  _pspec(blk, lambda b, h, qi: (b, qi, _CBP["vc"] + h), layer),
                  pl.BlockSpec((None, None, LANES, s), lambda b, h, qi: (layer, b, h, 0)),
                  _pspec(full, lambda b, h, qi: (b, 0, _CBP["vc"] + h), layer),
                  pl.BlockSpec((None, None, ATT_BLK, ATT_BLK), lambda b, h, qi: (h, 0, 0, 0)),
                  pl.BlockSpec((None, None, ATT_BLK, ATT_BLK), lambda b, h, qi: (h, 1, 0, 0))],
        out_specs=pl.BlockSpec(blk, lambda b, h, qi: (b, qi, h)),
        out_shape=jax.ShapeDtypeStruct((bn, s, BRANCH_W), BF16),
        scratch_shapes=[state, state, state],
        compiler_params=_cparams("parallel", "parallel", "arbitrary"),
        name="diff_attn_prompt",
    )(t5_flat, lam_params, gain.reshape(1, C_VDIM), pn4, k_t, pn4, k_t, pn4, t5tiles, t5tiles)


def _diff_attn_sample(p4, layer, k_cache_t, v_cache, bias_cache, bias_new, lam_init, lam_params, gain):
    _, bn, t, _ = p4.shape
    past = k_cache_t.shape[-1]
    new = lambda name: _pspec((None, t, BRANCH_W), lambda b: (b, 0, _CB[name] * LANES // BRANCH_W), layer)
    return pl.pallas_call(
        functools.partial(_diff_sample_kernel, lam_init),
        grid=(bn,),
        in_specs=[pl.BlockSpec((4, HEAD_DIM), lambda b: (0, 0)),
                  pl.BlockSpec((1, C_VDIM), lambda b: (0, 0)),
                  new("qc"),
                  pl.BlockSpec((None, None, BRANCH_W, past), lambda b: (layer, b, 0, 0)),
                  pl.BlockSpec((None, None, past * C_HEADS, C_VDIM), lambda b: (layer, b, 0, 0)),
                  new("kc"), new("vc"),
                  pl.BlockSpec((C_HEADS, t, past), lambda b: (0, 0, 0)),
                  pl.BlockSpec((C_HEADS, t, t), lambda b: (0, 0, 0))],
        out_specs=pl.BlockSpec((None, t, BRANCH_W), lambda b: (b, 0, 0)),
        out_shape=jax.ShapeDtypeStruct((bn, t, BRANCH_W), BF16),
        compiler_params=_cparams("parallel"),
        name="diff_attn_sample",
    )(lam_params, gain.reshape(1, C_VDIM), p4, k_cache_t, v_cache, p4, p4, bias_cache, bias_new)


def _merge_kernel(x_ref, oa_ref, ob_ref, oc_ref, gm_ref, wg_ref, bg_ref, wb_ref, wo_ref, o_ref):
    x = x_ref[...]
    d = x.shape[1]
    xn = _rms(x, gm_ref[...]).astype(BF16)
    h = None
    for n, o_r in enumerate((oa_ref, ob_ref, oc_ref)):
        logits = _dot(xn, wg_ref[:, n * d:(n + 1) * d]) + bg_ref[n:n + 1, :]
        t = _dot(o_r[...], wb_ref[n]) / (1.0 + jnp.exp(-logits))
        h = t if h is None else h + t
    o_ref[...] = x + _dot(h.astype(BF16), wo_ref[...])


def _merge(x, o_a, o_b, o_c, g_mix, w_gate, b_gate, w_branch, w_out, tm):
    t, d = x.shape
    row = lambda i: (i, 0)
    fixed2 = lambda i: (0, 0)
    return pl.pallas_call(
        _merge_kernel,
        grid=(t // tm,),
        in_specs=[pl.BlockSpec((tm, d), row),
                  pl.BlockSpec((tm, BRANCH_W), row),
                  pl.BlockSpec((tm, BRANCH_W), row),
                  pl.BlockSpec((tm, BRANCH_W), row),
                  pl.BlockSpec((1, d), fixed2),
                  pl.BlockSpec((d, N_BRANCH * d), fixed2),
                  pl.BlockSpec((N_BRANCH, d), fixed2),
                  pl.BlockSpec((N_BRANCH, BRANCH_W, d), lambda i: (0, 0, 0)),
                  pl.BlockSpec((d, d), fixed2)],
        out_specs=pl.BlockSpec((tm, d), row),
        out_shape=jax.ShapeDtypeStruct((t, d), F32),
        compiler_params=_cparams("parallel"),
        name="gated_merge",
    )(x, o_a, o_b, o_c, g_mix.reshape(1, d), w_gate, b_gate, w_branch, w_out)


FFN_TF = D_FF // 2
FFN_PREV = 16


def _gelu_tanh(x):
    return x * (0.5 * (1.0 + jnp.tanh(math.sqrt(2.0 / math.pi) * (x + 0.044715 * (x * x * x)))))


def _ffn_kernel(tiles_per_seq, final, x_ref, xp_ref, gn_ref, wg_ref, wu_ref, cw_ref, cb_ref, wd_ref, st_ref, gf_ref,
                o_ref, so_ref, xn_ref, xpn_ref):
    i = pl.program_id(0)
    j = pl.program_id(1)
    whole_seqs = tiles_per_seq == 0

    @pl.when(j == 0)
    def _():
        xn_ref[...] = _rms(x_ref[...], gn_ref[...]).astype(BF16)
        if not whole_seqs:
            xpn_ref[...] = _rms(xp_ref[...], gn_ref[...]).astype(BF16)

    xn = xn_ref[...]
    tm = xn.shape[0]
    tf = wg_ref.shape[1]
    g = _dot(xn, wg_ref[...])
    u = _dot(xn, wu_ref[...])
    if whole_seqs:
        ns = st_ref.shape[0]
        st = st_ref[...]
        g = g.reshape(ns, tm // ns, tf)
        u = u.reshape(ns, tm // ns, tf)
        pm2, pm1 = st[:, 0:1, :], st[:, 1:2, :]
    else:
        gp = _dot(xpn_ref[...], wg_ref[...])
        first = (i % tiles_per_seq) == 0
        st = st_ref[0]
        pm2 = jnp.where(first, st[0:1, :], gp[FFN_PREV - 2:FFN_PREV - 1, :])
        pm1 = jnp.where(first, st[1:2, :], gp[FFN_PREV - 1:FFN_PREV, :])
    ax = g.ndim - 2
    row = lax.broadcasted_iota(jnp.int32, g.shape, ax)
    g1 = jnp.where(row == 0, pm1, pltpu.roll(g, 1, ax))
    g2 = jnp.where(row == 0, pm2, jnp.where(row == 1, pm1, pltpu.roll(g, 2, ax)))
    cw = cw_ref[...]
    c = cb_ref[...] + cw[0:1, :] * g2 + cw[1:2, :] * g1 + cw[2:3, :] * g
    hid = (_gelu_tanh(c) * u).reshape(tm, tf).astype(BF16)
    contrib = _dot(hid, wd_ref[...])

    last = pl.num_programs(1) - 1
    finish = (lambda y: _rms(y, gf_ref[...])) if final else (lambda y: y)

    @pl.when(j == 0)
    def _():
        o_ref[...] = x_ref[...] + contrib

    @pl.when(jnp.logical_and(j > 0, j < last))
    def _():
        o_ref[...] += contrib

    @pl.when(jnp.logical_and(j > 0, j == last))
    def _():
        o_ref[...] = finish(o_ref[...] + contrib)

    seq_len = g.shape[ax]
    if whole_seqs:
        so_ref[...] = g[:, seq_len - 2:seq_len, :]
    else:
        so_ref[0] = g[seq_len - 2:seq_len, :]


def _conv_ffn(x, g_norm, w_gate, w_up, conv_w, conv_b, w_down, state, seq_len, tm, g_final, final):
    t, d = x.shape
    nseq = t // seq_len
    nf = D_FF // FFN_TF
    assert nf > 1
    if seq_len >= tm:
        tiles_per_seq = seq_len // tm
        st_spec = pl.BlockSpec((1, 2, FFN_TF), lambda i, j: (i // tiles_per_seq, 0, j))
    else:
        tiles_per_seq = 0
        st_spec = pl.BlockSpec((tm // seq_len, 2, FFN_TF), lambda i, j: (i, 0, j))
    prev_blocks = tm // FFN_PREV
    tail_spec = pl.BlockSpec(st_spec.block_shape, lambda i, j: (i, 0, j))
    n_tail = t // tm * st_spec.block_shape[0]
    y, tails = pl.pallas_call(
        functools.partial(_ffn_kernel, tiles_per_seq, final),
        grid=(t // tm, nf),
        in_specs=[pl.BlockSpec((tm, d), lambda i, j: (i, 0)),
                  pl.BlockSpec((FFN_PREV, d), lambda i, j: (jnp.maximum(i * prev_blocks - 1, 0), 0)),
                  pl.BlockSpec((1, d), lambda i, j: (0, 0)),
                  pl.BlockSpec((d, FFN_TF), lambda i, j: (0, j)),
                  pl.BlockSpec((d, FFN_TF), lambda i, j: (0, j)),
                  pl.BlockSpec((3, FFN_TF), lambda i, j: (0, j)),
                  pl.BlockSpec((1, FFN_TF), lambda i, j: (0, j)),
                  pl.BlockSpec((FFN_TF, d), lambda i, j: (j, 0)),
                  st_spec,
                  pl.BlockSpec((1, d), lambda i, j: (0, 0))],
        out_specs=[pl.BlockSpec((tm, d), lambda i, j: (i, 0)), tail_spec],
        out_shape=[jax.ShapeDtypeStruct((t, d), F32),
                   jax.ShapeDtypeStruct((n_tail, 2, D_FF), F32)],
        scratch_shapes=[pltpu.VMEM((tm, d), BF16), pltpu.VMEM((FFN_PREV, d), BF16)],
        compiler_params=_cparams("arbitrary", "arbitrary"),
        name="conv_ffn_prompt" if tiles_per_seq else "conv_ffn_sample",
    )(x, x, g_norm.reshape(1, d), w_gate, w_up, conv_w, conv_b.reshape(1, D_FF), w_down, state, g_final.reshape(1, d))
    return y, tails[n_tail // nseq - 1::n_tail // nseq]


PROJ_TM = 1024
PROMPT_TN, SAMPLE_TN = ROW_COLS // 2, IN_COLS // 3
ROW_TM = 512


def _lam_init(layer):
    return 0.8 - 0.6 * math.exp(-0.3 * layer)


def _cols(p4, name):
    c0 = _CB[name] * LANES
    return p4[..., c0:c0 + BRANCH_W]


def _layer_tail(x2, o_a, o_b, o_c, layer, prm, conv_state, seq_len):
    flat = lambda a: a.reshape(x2.shape[0], BRANCH_W)
    x2 = _merge(x2, flat(o_a), flat(o_b), flat(o_c), prm["norm_mix"][layer], prm["w_gate_mix"][layer],
                prm["b_gate"][layer], prm["w_branch"][layer], prm["w_out"][layer], ROW_TM)
    depth = prm["norm_ffn"].shape[0]
    return _conv_ffn(x2, prm["norm_ffn"][layer], prm["w_gate"][layer], prm["w_up"][layer],
                     prm["conv_w"][layer], prm["conv_b"][layer], prm["w_down"][layer], conv_state, seq_len, ROW_TM,
                     prm["norm_final"], layer == depth - 1)


def _trunk_prompt(x, prm, abias, t5tiles, t5_flat):
    bn, s, d = x.shape
    x2 = x.reshape(bn * s, d)
    depth = prm["w_rows"].shape[0]
    conv_states = []
    rows, kv_t = None, None
    for l in range(depth):
        rows = _norm_proj(x2, prm["norm_mix"][l], prm["w_rows"][l], PROJ_TM, PROMPT_TN, l, depth, rows,
                          heads_group=(_CBP["vc"] * LANES, C_HEADS))
        p_rows, vc_heads = rows
        kv_t = _norm_proj_t(x2, prm["norm_mix"][l], prm["w_feat_t"][l], PROJ_TM, l, depth, bn, kv_t)
        ka_t, va_t, kb_t, vb_t, kc_t = kv_t
        pn4 = p_rows.reshape(depth, bn, s, ROW_COLS)
        o_a = _band_attn_prompt(pn4, l, ka_t, va_t, abias[l])
        o_b = _sb_attn_prompt(pn4, l, kb_t, vb_t)
        o_c = _diff_attn_prompt(pn4, l, kc_t, t5tiles, _lam_init(l), t5_flat, prm["c_lambda"][l], prm["c_subln"][l])
        x2, st_conv = _layer_tail(x2, o_a, o_b, o_c, l, prm, jnp.zeros((bn, 2, D_FF), F32), s)
        conv_states.append(st_conv)
    y = x2.reshape(bn, s, d)
    heads_t = lambda a: jnp.transpose(a.reshape(depth, bn, A_HEADS, HEAD_DIM, a.shape[-1]), (0, 1, 4, 2, 3))
    keep = min(A_PAST_CHUNKS * CHUNK, s)
    states = (heads_t(ka_t[..., s - keep:]), heads_t(va_t[..., s - keep:]), heads_t(kb_t), heads_t(vb_t),
              jnp.transpose(kc_t.reshape(depth, bn, C_HEADS, 2, HEAD_DIM, s), (0, 1, 5, 2, 3, 4)),
              vc_heads.reshape(depth, bn, s, C_HEADS, C_VDIM),
              jnp.stack(conv_states, axis=0))
    return y, states


def _feature_major(cache):
    nd = cache.ndim
    t = jnp.transpose(cache, (0, 1) + tuple(range(3, nd)) + (2,))
    return t.reshape(cache.shape[0], cache.shape[1], BRANCH_W, cache.shape[2])


def _trunk_sample(x, caches, prm, abias, t5tiles):
    bn, t, d = x.shape
    x2 = x.reshape(bn * t, d)
    depth = prm["w_in"].shape[0]
    ca_k, ca_v, cb_k, cb_v, cc_k, cc_v, st_conv_in = caches
    ca_kt, ca_vt, cb_kt, cb_vt, cc_kt = (_feature_major(c) for c in (ca_k, ca_v, cb_k, cb_v, cc_k))
    keep, past = ca_k.shape[2], cc_k.shape[2]
    cc_vr = cc_v.reshape(depth, bn, past * C_HEADS, C_VDIM)
    conv_states = []
    p_all = None
    for l in range(depth):
        p_all, = _norm_proj(x2, prm["norm_mix"][l], prm["w_in"][l], PROJ_TM, SAMPLE_TN, l, depth,
                            None if p_all is None else [p_all])
        p4 = p_all.reshape(depth, bn, t, IN_COLS)
        bias_cache = jnp.concatenate([abias[l][:, 0, :t, :], abias[l][:, 1, :t, :]], axis=-1)[..., -keep:]
        o_a = _band_attn_sample(p4, l, ca_kt, ca_vt, bias_cache, abias[l][:, 2, :t, :t])
        o_b = _sb_attn_sample(p4, l, cb_kt, cb_vt)
        far = jnp.broadcast_to(t5tiles[:, 1, :1, :1], (C_HEADS, t, past - ATT_BLK))
        bias_c = jnp.concatenate([far, t5tiles[:, 1, :t, :]], axis=-1)
        o_c = _diff_attn_sample(p4, l, cc_kt, cc_vr, bias_c, t5tiles[:, 0, :t, :t],
                                _lam_init(l), prm["c_lambda"][l], prm["c_subln"][l])
        x2, st_conv = _layer_tail(x2, o_a, o_b, o_c, l, prm, st_conv_in[l], t)
        conv_states.append(st_conv)
    y = x2.reshape(bn, t, d)
    kb, vb, kc, vc = (_cols(p4, n) for n in ("kb", "vb", "kc", "vc"))
    heads = lambda a: a.reshape(a.shape[:3] + (A_HEADS, HEAD_DIM))
    heads_t = lambda a: jnp.transpose(a.reshape(depth, bn, A_HEADS, HEAD_DIM, a.shape[-1]), (0, 1, 4, 2, 3))
    sa_kt, sa_vt = _band_roll(ca_kt, ca_vt, p4)
    states = (heads_t(sa_kt), heads_t(sa_vt), heads(kb), heads(vb),
              kc.reshape(depth, bn, t, C_HEADS, 2, HEAD_DIM), vc.reshape(depth, bn, t, C_HEADS, C_VDIM),
              jnp.stack(conv_states, axis=0))
    return y, states


def kernel(x_prompt, x_sample, cache_a_k, cache_a_v, cache_b_k, cache_b_v, cache_c_k, cache_c_v,
           state_ffn_conv, norm_mix, w_in, b_gate, a_rel_bias, t5_bias, c_lambda, c_subln,
           w_branch, w_out, norm_ffn, w_up, conv_w, conv_b, w_down, norm_final):
    group = {name: w_in[..., i * BRANCH_W:(i + 1) * BRANCH_W] for i, name in enumerate(_GROUPS)}
    gate = w_in[..., len(_GROUPS) * BRANCH_W:]
    prm = dict(
        norm_mix=norm_mix,
        w_in=w_in[..., :len(_GROUPS) * BRANCH_W].astype(BF16),
        w_rows=jnp.concatenate([group[n] for n in _ROW_GROUPS], axis=-1).astype(BF16),
        w_gate_mix=gate.astype(BF16),
        w_feat_t=jnp.swapaxes(jnp.concatenate([group[n] for n in _T_GROUPS], axis=-1), 1, 2).astype(BF16),
        b_gate=b_gate, c_lambda=c_lambda, c_subln=c_subln,
        w_branch=w_branch.astype(BF16), w_out=w_out.astype(BF16), norm_ffn=norm_ffn,
        w_gate=w_up[..., :D_FF].astype(BF16), w_up=w_up[..., D_FF:].astype(BF16),
        conv_w=conv_w, conv_b=conv_b, w_down=w_down.astype(BF16), norm_final=norm_final)
    abias = [_build_abias(a_rel_bias[l]) for l in range(a_rel_bias.shape[0])]
    t5tiles = _build_t5bias(t5_bias)
    y_p, p_states = _trunk_prompt(x_prompt, prm, abias, t5tiles, t5_bias.reshape(-1))
    caches = (cache_a_k, cache_a_v, cache_b_k, cache_b_v, cache_c_k, cache_c_v, state_ffn_conv)
    y_s, s_states = _trunk_sample(x_sample, caches, prm, abias, t5tiles)
    return (y_p, y_s) + p_states + s_states
```

```python
import functools
import math

import jax
import jax.numpy as jnp
from jax import lax
from jax.experimental import pallas as pl
from jax.experimental.pallas import tpu as pltpu

F32 = jnp.float32
BF16 = jnp.bfloat16

D_MODEL = 1024
CHUNK = 64
HEAD_DIM = 64
A_HEADS = 8
A_PAST_CHUNKS = 8
A_REL_CLIP = 128
B_HEADS = 8
C_HEADS = 4
C_VDIM = 128
T5_BUCKETS = 32
D_FF = 2816
N_BRANCH = 3
BRANCH_W = 512
IN_COLS = 9 * BRANCH_W
EPS = 1e-6
NEG_INF = -1e30
SCALE = HEAD_DIM ** -0.5
LOG2E = math.log2(math.e)
SOFTMAX_QSCALE = SCALE * LOG2E

LANES = 128
_GROUPS = ("qa", "ka", "va", "qb", "kb", "vb", "qc", "kc", "vc")
_CB = {name: i * BRANCH_W // LANES for i, name in enumerate(_GROUPS)}
_ROW_GROUPS = ("qa", "qb", "qc", "vc")
_T_GROUPS = ("ka", "va", "kb", "vb", "kc")
_CBP = {name: i * BRANCH_W // LANES for i, name in enumerate(_ROW_GROUPS)}
ROW_COLS = len(_ROW_GROUPS) * BRANCH_W

ATT_BLK = 256
ATT_HEADS = 4
SB_LOG_CUT = -104.0
T5_FAR_BUCKET = 15
DIFF_SUB = 4
DIFF_FAR_GROUP = 4
V7X_VMEM_BYTES = 64 * 1024 * 1024
VMEM_LIMIT = V7X_VMEM_BYTES - 8 * 1024 * 1024


def _cparams(*sem):
    return pltpu.CompilerParams(dimension_semantics=sem, vmem_limit_bytes=VMEM_LIMIT)


def _rms(x, g):
    return x * lax.rsqrt(jnp.mean(x * x, axis=-1, keepdims=True) + EPS) * g


def _dot(a, b):
    return jnp.dot(a, b, preferred_element_type=F32)


def _dot_nt(a, b):
    return lax.dot_general(a, b, (((1,), (1,)), ((), ())), preferred_element_type=F32)


def _norm_proj_kernel(n_stack, heads_cols, x_ref, g_ref, w_ref, *refs):
    outs, xn_ref = refs[n_stack:-1], refs[-1]
    j = pl.program_id(1)

    @pl.when(j == 0)
    def _():
        xn_ref[...] = _rms(x_ref[...], g_ref[...]).astype(BF16)

    res = _dot(xn_ref[...], w_ref[...])
    outs[0][...] = res
    if heads_cols is not None:
        tile, first, heads = heads_cols

        @pl.when(j == tile)
        def _():
            for h in range(heads):
                lo = first + h * LANES
                outs[1][pl.ds(h, res.shape[0], stride=heads), :] = res[:, lo:lo + LANES]


def _norm_proj(x, g, w, tm, tn, layer, depth, stacked, heads_group=None):
    t, d = x.shape
    n = w.shape[1]
    in_specs = [pl.BlockSpec((tm, d), lambda i, j: (i, 0)),
                pl.BlockSpec((1, d), lambda i, j: (0, 0)),
                pl.BlockSpec((d, tn), lambda i, j: (0, j))]
    args = [x, g.reshape(1, d), w]
    out_specs = [pl.BlockSpec((None, tm, tn), lambda i, j: (layer, i, j))]
    out_shape = [jax.ShapeDtypeStruct((depth, t, n), F32)]
    heads_cols = None
    if heads_group is not None:
        first, heads = heads_group
        heads_cols = (first // tn, first % tn, heads)
        out_specs.append(pl.BlockSpec((None, tm * heads, LANES), lambda i, j: (layer, i, 0)))
        out_shape.append(jax.ShapeDtypeStruct((depth, t * heads, LANES), F32))
    n_stack = 0 if stacked is None else len(out_shape)
    if stacked is not None:
        in_specs += [pl.BlockSpec(memory_space=pl.ANY)] * n_stack
        args += list(stacked)
    return pl.pallas_call(
        functools.partial(_norm_proj_kernel, n_stack, heads_cols),
        grid=(t // tm, n // tn),
        in_specs=in_specs,
        out_specs=out_specs,
        out_shape=out_shape,
        scratch_shapes=[pltpu.VMEM((tm, d), BF16)],
        input_output_aliases={3 + k: k for k in range(n_stack)},
        compiler_params=_cparams("arbitrary", "arbitrary"),
        name="norm_proj",
    )(*args)


def _norm_proj_t_kernel(ngroups, x_ref, g_ref, wt_ref, *refs):
    o_refs = refs[-ngroups:]
    xn = _rms(x_ref[...], g_ref[...]).astype(BF16)
    for g in range(ngroups):
        o_refs[g][...] = _dot_nt(wt_ref[g * BRANCH_W:(g + 1) * BRANCH_W, :], xn)


def _norm_proj_t(x, g, wt, tm, layer, depth, bn, stacked):
    t, d = x.shape
    s = t // bn
    ngroups = wt.shape[0] // BRANCH_W
    tiles = s // tm
    in_specs = [pl.BlockSpec((tm, d), lambda i: (i, 0)),
                pl.BlockSpec((1, d), lambda i: (0, 0)),
                pl.BlockSpec(wt.shape, lambda i: (0, 0))]
    args = [x, g.reshape(1, d), wt]
    aliases = {}
    if stacked is not None:
        in_specs += [pl.BlockSpec(memory_space=pl.ANY)] * ngroups
        args += list(stacked)
        aliases = {3 + n: n for n in range(ngroups)}
    out_spec = pl.BlockSpec((None, None, BRANCH_W, tm), lambda i: (layer, i // tiles, 0, i % tiles))
    return pl.pallas_call(
        functools.partial(_norm_proj_t_kernel, ngroups),
        grid=(t // tm,),
        in_specs=in_specs,
        out_specs=[out_spec] * ngroups,
        out_shape=[jax.ShapeDtypeStruct((depth, bn, BRANCH_W, s), F32)] * ngroups,
        input_output_aliases=aliases,
        compiler_params=_cparams("arbitrary"),
        name="norm_proj_t",
    )(*args)


def _abias_kernel(tbl_ref, o_ref):
    h = pl.program_id(0)
    sub = 8
    ql = lax.broadcasted_iota(jnp.int32, (ATT_BLK, ATT_BLK), 0)
    kl = lax.broadcasted_iota(jnp.int32, (ATT_BLK, ATT_BLK), 1)
    far = jnp.full((ATT_BLK, ATT_BLK), tbl_ref[2 * A_REL_CLIP * A_HEADS + h] * LOG2E, F32)
    o_ref[0, 0] = jnp.where((kl >> 6) >= (ql >> 6), far, NEG_INF)
    lead = ATT_BLK - sub
    b = lax.broadcasted_iota(jnp.int32, (sub, 2 * ATT_BLK), 0)
    x = lax.broadcasted_iota(jnp.int32, (sub, 2 * ATT_BLK), 1)
    for seg, off in ((1, ATT_BLK), (2, 0)):
        idx = jnp.clip(b + lead - x + off, -A_REL_CLIP, A_REL_CLIP) + A_REL_CLIP

        def body(r, acc, idx=idx):
            return jnp.where(idx == r, tbl_ref[r * A_HEADS + h] * LOG2E, acc)

        strip = lax.fori_loop(0, 2 * A_REL_CLIP + 1, body, jnp.zeros((sub, 2 * ATT_BLK), F32))
        for a in range(ATT_BLK // sub):
            o_ref[0, seg, a * sub:(a + 1) * sub, :] = strip[:, lead - a * sub:lead - a * sub + ATT_BLK]
    o_ref[0, 2] = jnp.where((kl >> 6) <= (ql >> 6), o_ref[0, 2], NEG_INF)


def _build_abias(table):
    return pl.pallas_call(
        _abias_kernel,
        grid=(A_HEADS,),
        in_specs=[pl.BlockSpec(memory_space=pltpu.SMEM)],
        out_specs=pl.BlockSpec((1, 3, ATT_BLK, ATT_BLK), lambda h: (h, 0, 0, 0)),
        out_shape=jax.ShapeDtypeStruct((A_HEADS, 3, ATT_BLK, ATT_BLK), F32),
        compiler_params=_cparams("parallel"),
        name="band_bias_tiles",
    )(table.reshape(-1))


_T5_LARGE_STEPS = (12, 16, 23, 32, 46, 64, 91)


def _t5bias_kernel(tbl_ref, o_ref):
    h = pl.program_id(0)
    rows = 64
    for seg, off in ((0, 0), (1, -ATT_BLK)):
        for rc in range(ATT_BLK // rows):
            ql = lax.broadcasted_iota(jnp.int32, (rows, ATT_BLK), 0) + rc * rows
            kl = lax.broadcasted_iota(jnp.int32, (rows, ATT_BLK), 1)
            rel = kl - ql + off
            n = jnp.abs(rel)
            large = jnp.full((rows, ATT_BLK), T5_BUCKETS // 4, jnp.int32)
            for th in _T5_LARGE_STEPS:
                large = large + jnp.where(n >= th, 1, 0)
            bucket = jnp.where(rel > 0, T5_BUCKETS // 2, 0) + jnp.where(n < T5_BUCKETS // 4, n, large)
            tile = jnp.zeros((rows, ATT_BLK), F32)
            for r in range(T5_BUCKETS):
                tile = jnp.where(bucket == r, tbl_ref[r * C_HEADS + h] * LOG2E, tile)
            if seg == 0:
                tile = jnp.where((kl >> 6) <= (ql >> 6), tile, NEG_INF)
            o_ref[0, seg, rc * rows:(rc + 1) * rows, :] = tile


def _build_t5bias(table):
    return pl.pallas_call(
        _t5bias_kernel,
        grid=(C_HEADS,),
        in_specs=[pl.BlockSpec(memory_space=pltpu.SMEM)],
        out_specs=pl.BlockSpec((1, 2, ATT_BLK, ATT_BLK), lambda h: (h, 0, 0, 0)),
        out_shape=jax.ShapeDtypeStruct((C_HEADS, 2, ATT_BLK, ATT_BLK), F32),
        compiler_params=_cparams("parallel"),
        name="t5_bias_tiles",
    )(table.reshape(-1))


def _pspec(block, index_map, layer):
    return pl.BlockSpec((None,) + block, lambda *g: (layer,) + tuple(index_map(*g)))


def _lane_tile(x, n):
    return x if n == 1 else jnp.concatenate([x] * n, axis=1)


def _stack_heads(q, width):
    group = lax.broadcasted_iota(jnp.int32, q.shape, 1) // width
    return jnp.concatenate([jnp.where(group == h, q, 0.0) for h in range(q.shape[1] // width)],
                           axis=0).astype(BF16)


def _unstack_heads(x, t, width):
    group = lax.broadcasted_iota(jnp.int32, (t, x.shape[1]), 1) // width
    out = jnp.zeros((t, x.shape[1]), x.dtype)
    for h in range(x.shape[1] // width):
        out = jnp.where(group == h, x[h * t:(h + 1) * t, :], out)
    return out


def _qk(q, k, k_t):
    return _dot(q, k) if k_t else _dot_nt(q, k)


def _pv(p, v, v_t):
    return _dot_nt(p, v) if v_t else _dot(p, v)


def _softmax_segments(qh, kb, vb, biases, k_t, v_t):
    s = [_qk(qh, k, t) + b for k, b, t in zip(kb, biases, k_t)]
    m = functools.reduce(jnp.maximum, [jnp.max(sj, axis=1, keepdims=True) for sj in s])
    p = [jnp.exp2(sj - m) for sj in s]
    l = functools.reduce(jnp.add, [jnp.sum(pj, axis=1, keepdims=True) for pj in p])
    acc = functools.reduce(jnp.add, [_pv(pj.astype(BF16), v, t) for pj, v, t in zip(p, vb, v_t)])
    return acc / l


def _band_kernel(nseg, min_qi, q_ref, *refs):
    k_refs, v_refs, b_refs = refs[:nseg], refs[nseg:2 * nseg], refs[2 * nseg:3 * nseg]
    o_ref = refs[3 * nseg]
    qi = pl.program_id(2)
    q = q_ref[...] * SOFTMAX_QSCALE
    tq = q.shape[0]
    kb = [k_refs[j][...].astype(BF16) for j in range(nseg)]
    vb = [v_refs[j][...].astype(BF16) for j in range(nseg)]
    qs = _stack_heads(q, HEAD_DIM)
    biases = []
    for j in range(nseg):
        tile = b_refs[j][...].reshape(b_refs[j].shape[0] * tq, b_refs[j].shape[-1])
        biases.append(tile if min_qi[j] == 0 else jnp.where(qi >= min_qi[j], tile, NEG_INF))
    o = _softmax_segments(qs, kb, vb, biases, (True,) * nseg, (True,) * nseg)
    o_ref[...] = _unstack_heads(o, tq, HEAD_DIM).astype(o_ref.dtype)


def _band_attn_prompt(pn4, layer, k_t, v_t, abias):
    _, bn, s, _ = pn4.shape
    nq = s // ATT_BLK
    width = ATT_HEADS * HEAD_DIM
    blk = (None, ATT_BLK, width)

    def kv_spec(back):
        return pl.BlockSpec((None, None, width, ATT_BLK),
                            lambda b, hg, qi: (layer, b, hg, jnp.maximum(qi - back, 0)))

    def bias_spec(seg):
        return pl.BlockSpec((ATT_HEADS, None, ATT_BLK, ATT_BLK), lambda b, hg, qi: (hg, seg, 0, 0))

    return pl.pallas_call(
        functools.partial(_band_kernel, 3, (2, 1, 0)),
        grid=(bn, A_HEADS // ATT_HEADS, nq),
        in_specs=([_pspec(blk, lambda b, hg, qi: (b, qi, _CBP["qa"] * LANES // width + hg), layer)]
                  + [kv_spec(back) for back in (2, 1, 0)]
                  + [kv_spec(back) for back in (2, 1, 0)]
                  + [bias_spec(seg) for seg in range(3)]),
        out_specs=pl.BlockSpec(blk, lambda b, hg, qi: (b, qi, hg)),
        out_shape=jax.ShapeDtypeStruct((bn, s, BRANCH_W), BF16),
        compiler_params=_cparams("parallel", "parallel", "arbitrary"),
        name="band_attn_prompt",
    )(pn4, k_t, k_t, k_t, v_t, v_t, v_t, abias, abias, abias)


def _band_sample_kernel(q_ref, kc_ref, kn_ref, vc_ref, vn_ref, bc_ref, bn_ref, o_ref):
    t = q_ref.shape[0]
    width = ATT_HEADS * HEAD_DIM
    q = q_ref[...] * SOFTMAX_QSCALE
    for g in range(q.shape[1] // width):
        cols = slice(g * width, (g + 1) * width)
        rows = slice(g * ATT_HEADS * t, (g + 1) * ATT_HEADS * t)
        qs = _stack_heads(q[:, cols], HEAD_DIM)
        kb = [kc_ref[cols, :].astype(BF16), kn_ref[:, cols].astype(BF16)]
        vb = [vc_ref[cols, :].astype(BF16), vn_ref[:, cols].astype(BF16)]
        o = _softmax_segments(qs, kb, vb, [bc_ref[rows, :], bn_ref[rows, :]], (True, False), (True, False))
        o_ref[:, cols] = _unstack_heads(o, t, HEAD_DIM).astype(o_ref.dtype)


def _band_attn_sample(p4, layer, k_cache_t, v_cache_t, bias_cache, bias_new):
    _, bn, t, _ = p4.shape
    keep = k_cache_t.shape[-1]
    new = lambda name: _pspec((None, t, BRANCH_W), lambda b: (b, 0, _CB[name] * LANES // BRANCH_W), layer)
    old = pl.BlockSpec((None, None, BRANCH_W, keep), lambda b: (layer, b, 0, 0))
    return pl.pallas_call(
        _band_sample_kernel,
        grid=(bn,),
        in_specs=[new("qa"), old, new("ka"), old, new("va"),
                  pl.BlockSpec((A_HEADS * t, keep), lambda b: (0, 0)),
                  pl.BlockSpec((A_HEADS * t, t), lambda b: (0, 0))],
        out_specs=pl.BlockSpec((None, t, BRANCH_W), lambda b: (b, 0, 0)),
        out_shape=jax.ShapeDtypeStruct((bn, t, BRANCH_W), BF16),
        compiler_params=_cparams("parallel"),
        name="band_attn_sample",
    )(p4, k_cache_t, p4, v_cache_t, p4, bias_cache.reshape(A_HEADS * t, keep), bias_new.reshape(A_HEADS * t, t))


def _band_roll_kernel(kc_ref, vc_ref, kn_ref, vn_ref, ko_ref, vo_ref):
    t = kn_ref.shape[0]
    keep = kc_ref.shape[1]
    for c_ref, n_ref, o_ref in ((kc_ref, kn_ref, ko_ref), (vc_ref, vn_ref, vo_ref)):
        rolled = pltpu.roll(c_ref[...], keep - t, 1)
        new = jnp.concatenate([n_ref[...], jnp.zeros((LANES - t, n_ref.shape[1]), F32)], axis=0)
        new_t = pltpu.roll(new.T, LANES - t, 1)
        lane = lax.broadcasted_iota(jnp.int32, new_t.shape, 1)
        last = jnp.where(lane >= LANES - t, new_t, rolled[:, keep - LANES:])
        o_ref[...] = jnp.concatenate([rolled[:, :keep - LANES], last], axis=1)


def _band_roll(k_cache_t, v_cache_t, p4):
    depth, bn, t, _ = p4.shape
    keep = k_cache_t.shape[-1]
    assert t <= LANES <= keep
    old = pl.BlockSpec((None, None, BRANCH_W, keep), lambda l, b: (l, b, 0, 0))
    new = lambda name: pl.BlockSpec((None, None, t, BRANCH_W), lambda l, b: (l, b, 0, _CB[name] * LANES // BRANCH_W))
    return pl.pallas_call(
        _band_roll_kernel,
        grid=(depth, bn),
        in_specs=[old, old, new("ka"), new("va")],
        out_specs=[old, old],
        out_shape=[jax.ShapeDtypeStruct(k_cache_t.shape, F32)] * 2,
        compiler_params=_cparams("parallel", "parallel"),
        name="band_roll",
    )(k_cache_t, v_cache_t, p4, p4)


def _strict_upper(n):
    r = lax.broadcasted_iota(jnp.int32, (n, n), 0)
    c = lax.broadcasted_iota(jnp.int32, (n, n), 1)
    return jnp.where(r > c, 1.0, 0.0).astype(BF16)


def _sb_fold(qh, kb, vb, kv_t, upper, mask, carry_ref, acc_ref, first):
    tq = qh[0].shape[0]
    tk = kb[0].shape[1] if kv_t else kb[0].shape[0]
    cw = upper.shape[0]
    top = None
    for h in range(len(qh)):
        z = _qk(qh[h], kb[0], kv_t)
        soft = jnp.log(1.0 + jnp.exp(-jnp.abs(z)))
        log_keep = -(jnp.maximum(z, 0.0) + soft)
        log_take = jnp.minimum(z, 0.0) - soft
        if mask is not None:
            log_keep = jnp.where(mask, log_keep, 0.0)
        hi = log_keep.astype(BF16)
        lo = (log_keep - hi.astype(F32)).astype(BF16)
        pieces, total = [], None
        for c0 in reversed(range(0, tk, cw)):
            piece = _dot(hi[:, c0:c0 + cw], upper) + _dot(lo[:, c0:c0 + cw], upper)
            pieces.insert(0, piece if total is None else piece + total)
            part = jnp.sum(log_keep[:, c0:c0 + cw], axis=1, keepdims=True)
            total = part if total is None else total + part
        after = pieces[0] if len(pieces) == 1 else jnp.concatenate(pieces, axis=1)
        if not first:
            after = after + _lane_tile(carry_ref[h], tk // LANES)
        w = jnp.exp(log_take + after)
        if mask is not None:
            w = jnp.where(mask, w, 0.0)
        pv = _pv(w.astype(BF16), vb[0], kv_t)
        if first:
            acc_ref[h] = pv
            carry = jnp.broadcast_to(total, (tq, LANES))
        else:
            acc_ref[h] += pv
            carry = carry_ref[h] + total
        carry_ref[h] = carry
        top = jnp.max(carry) if top is None else jnp.maximum(top, jnp.max(carry))
    return (top > SB_LOG_CUT).astype(jnp.int32)


def _sb_kernel(tk, n_past_static, own_rows, q_ref, kd_ref, vd_ref, kp_ref, vp_ref, o_ref, carry_ref, acc_ref):
    n_past = pl.program_id(2) if n_past_static is None else n_past_static
    q = q_ref[...] * SCALE
    tq = q.shape[0]
    rq = lax.broadcasted_iota(jnp.int32, (tq, tq), 0)
    ck = lax.broadcasted_iota(jnp.int32, (tq, tq), 1)
    qh = [_stack_heads(q, HEAD_DIM)]
    causal = jnp.concatenate([ck < rq] * (q.shape[1] // HEAD_DIM), axis=0)
    if own_rows:
        kd, vd = [kd_ref[...].astype(BF16)], [vd_ref[...].astype(BF16)]
        live = _sb_fold(qh, kd, vd, False, _strict_upper(tq), causal, carry_ref, acc_ref, True)
        n_left = n_past
    else:
        near = pl.multiple_of(jnp.maximum(n_past - 1, 0) * tq, tq)
        kd = [jnp.concatenate([kp_ref[:, pl.ds(near, tq)], kd_ref[...]], axis=1).astype(BF16)]
        vd = [jnp.concatenate([vp_ref[:, pl.ds(near, tq)], vd_ref[...]], axis=1).astype(BF16)]
        mask = jnp.concatenate([jnp.broadcast_to(n_past >= 1, causal.shape), causal], axis=1)
        live = _sb_fold(qh, kd, vd, True, _strict_upper(tq), mask, carry_ref, acc_ref, True)
        n_left = n_past - 1

    def cond(st):
        return jnp.logical_and(st[0] >= 0, st[1] > 0)

    def body(st):
        start = pl.multiple_of(st[0] * tk, tk)
        kp, vp = [kp_ref[:, pl.ds(start, tk)].astype(BF16)], [vp_ref[:, pl.ds(start, tk)].astype(BF16)]
        live = _sb_fold(qh, kp, vp, True, _strict_upper(tk), None, carry_ref, acc_ref, False)
        return st[0] - 1, live

    lax.while_loop(cond, body, (n_left - 1, live))
    o_ref[...] = _unstack_heads(acc_ref[0], tq, HEAD_DIM).astype(o_ref.dtype)


def _sb_attn_prompt(pn4, layer, k_t, v_t):
    _, bn, s, _ = pn4.shape
    width = ATT_HEADS * HEAD_DIM
    blk = (None, ATT_BLK, width)
    own = pl.BlockSpec((None, None, width, ATT_BLK), lambda b, hg, qi: (layer, b, hg, qi))
    older = pl.BlockSpec((None, None, width, s), lambda b, hg, qi: (layer, b, hg, 0))
    rows = ATT_HEADS * ATT_BLK
    return pl.pallas_call(
        functools.partial(_sb_kernel, ATT_BLK, None, False),
        grid=(bn, B_HEADS // ATT_HEADS, s // ATT_BLK),
        in_specs=[_pspec(blk, lambda b, hg, qi: (b, qi, _CBP["qb"] * LANES // width + hg), layer),
                  own, own, older, older],
        out_specs=pl.BlockSpec(blk, lambda b, hg, qi: (b, qi, hg)),
        out_shape=jax.ShapeDtypeStruct((bn, s, BRANCH_W), BF16),
        scratch_shapes=[pltpu.VMEM((1, rows, LANES), F32), pltpu.VMEM((1, rows, width), F32)],
        compiler_params=_cparams("parallel", "parallel", "arbitrary"),
        name="stick_attn_prompt",
    )(pn4, k_t, v_t, k_t, v_t)


def _sb_attn_sample(p4, layer, k_cache_t, v_cache_t):
    _, bn, t, _ = p4.shape
    past = k_cache_t.shape[-1]
    new = lambda name: _pspec((None, t, BRANCH_W), lambda b: (b, 0, _CB[name] * LANES // BRANCH_W), layer)
    older = pl.BlockSpec((None, None, BRANCH_W, past), lambda b: (layer, b, 0, 0))
    rows = B_HEADS * t
    return pl.pallas_call(
        functools.partial(_sb_kernel, ATT_BLK, past // ATT_BLK, True),
        grid=(bn,),
        in_specs=[new("qb"), new("kb"), new("vb"), older, older],
        out_specs=pl.BlockSpec((None, t, BRANCH_W), lambda b: (b, 0, 0)),
        out_shape=jax.ShapeDtypeStruct((bn, t, BRANCH_W), BF16),
        scratch_shapes=[pltpu.VMEM((1, rows, LANES), F32), pltpu.VMEM((1, rows, BRANCH_W), F32)],
        compiler_params=_cparams("parallel"),
        name="stick_attn_sample",
    )(p4, p4, p4, k_cache_t, v_cache_t)


def _diff_combine(o0, o1, lam_init, lam_ref, gain_ref):
    lp = lam_ref[...]
    lam = (jnp.exp(jnp.sum(lp[0:1] * lp[1:2], axis=1, keepdims=True))
           - jnp.exp(jnp.sum(lp[2:3] * lp[3:4], axis=1, keepdims=True)) + lam_init)
    return _rms(o0 - lam * o1, gain_ref[...]) * (1.0 - lam_init)


def _diff_fold(qs, k_t, v, bias, m_ref, l_ref, acc_ref, first):
    kb = k_t.astype(BF16)
    vb = v.astype(BF16)
    rows, tk = qs.shape[0], vb.shape[0]
    s = _dot(qs, kb) + bias
    m_cur = jnp.max(s, axis=1, keepdims=True)
    m_new = jnp.broadcast_to(m_cur, (rows, LANES)) if first else jnp.maximum(m_ref[...], m_cur)
    p = jnp.exp2(s - _lane_tile(m_new, tk // LANES))
    row = jnp.sum(p, axis=1, keepdims=True)
    pv = _dot(p.astype(BF16), vb)
    if first:
        l_ref[...] = jnp.broadcast_to(row, (rows, LANES))
        acc_ref[...] = pv
    else:
        alpha = jnp.exp2(m_ref[...] - m_new)
        l_ref[...] = alpha * l_ref[...] + row
        acc_ref[...] = alpha * acc_ref[...] + pv
    m_ref[...] = m_new


def _diff_prompt_kernel(lam_init, tbl_ref, lam_ref, gain_ref, q_ref, k_ref, v_ref, bd_ref, bp_ref, o_ref,
                        m_ref, l_ref, acc_ref):
    h = pl.program_id(1)
    blk = ATT_BLK
    far = tbl_ref[T5_FAR_BUCKET * C_HEADS + h] * LOG2E
    wide = DIFF_FAR_GROUP * blk

    def one_block(sub, carry):
        qi = pl.program_id(2) * DIFF_SUB + sub
        rows = pl.ds(pl.multiple_of(sub * blk, blk), blk)
        qm = _stack_heads(q_ref[rows, :] * SOFTMAX_QSCALE, HEAD_DIM)

        own = pl.multiple_of(qi * blk, blk)
        near = pl.multiple_of(jnp.maximum(qi - 1, 0) * blk, blk)
        k0 = jnp.concatenate([k_ref[:, pl.ds(near, blk)], k_ref[:, pl.ds(own, blk)]], axis=1)
        v0 = jnp.concatenate([v_ref[pl.ds(near, blk), :], v_ref[pl.ds(own, blk), :]], axis=0)
        b0 = jnp.concatenate([jnp.where(qi >= 1, bp_ref[...], NEG_INF), bd_ref[...]], axis=1)
        b0 = jnp.concatenate([b0, b0], axis=0)
        _diff_fold(qm, k0, v0, b0, m_ref, l_ref, acc_ref, True)

        n_far = jnp.maximum(qi - 1, 0)
        n_group = n_far // DIFF_FAR_GROUP

        def body(j, c):
            start = pl.multiple_of(j * wide, wide)
            _diff_fold(qm, k_ref[:, pl.ds(start, wide)], v_ref[pl.ds(start, wide), :], far,
                       m_ref, l_ref, acc_ref, False)
            return c

        lax.fori_loop(0, n_group, body, 0)
        rest = pl.multiple_of(n_group * wide, wide)
        for r in range(1, DIFF_FAR_GROUP):
            @pl.when(n_far - n_group * DIFF_FAR_GROUP == r)
            def _(r=r):
                _diff_fold(qm, k_ref[:, pl.ds(rest, r * blk)], v_ref[pl.ds(rest, r * blk), :], far,
                           m_ref, l_ref, acc_ref, False)

        o = acc_ref[...] / l_ref[...]
        o_ref[rows, :] = _diff_combine(o[:blk], o[blk:], lam_init, lam_ref, gain_ref).astype(o_ref.dtype)
        return carry

    lax.fori_loop(0, DIFF_SUB, one_block, 0)


def _diff_sample_kernel(lam_init, lam_ref, gain_ref, q_ref, kc_ref, vc_ref, kn_ref, vn_ref,
                        bc_ref, bn_ref, o_ref):
    q_all = q_ref[...] * SOFTMAX_QSCALE
    past = kc_ref.shape[1]
    k_t, v_t = (True, False), (False, False)
    for h in range(C_HEADS):
        cols = slice(h * LANES, (h + 1) * LANES)
        q = q_all[:, cols]
        kb = [kc_ref[cols, :].astype(BF16), kn_ref[:, cols].astype(BF16)]
        vb = [vc_ref[pl.ds(h, past, stride=C_HEADS), :].astype(BF16), vn_ref[:, cols].astype(BF16)]
        biases = [jnp.concatenate([b[h], b[h]], axis=0) for b in (bc_ref, bn_ref)]
        t = q.shape[0]
        o = _softmax_segments(_stack_heads(q, HEAD_DIM), kb, vb, biases, k_t, v_t)
        o_ref[:, cols] = _diff_combine(o[:t], o[t:], lam_init, lam_ref, gain_ref).astype(o_ref.dtype)


def _diff_attn_prompt(pn4, layer, k_t, t5tiles, lam_init, t5_flat, lam_params, gain):
    _, bn, s, _ = pn4.shape
    blk = (None, DIFF_SUB * ATT_BLK, LANES)
    full = (None, s, LANES)
    state = pltpu.VMEM((2 * ATT_BLK, LANES), F32)
    return pl.pallas_call(
        functools.partial(_diff_prompt_kernel, lam_init),
        grid=(bn, C_HEADS, s // (DIFF_SUB * ATT_BLK)),
        in_specs=[pl.BlockSpec(memory_space=pltpu.SMEM),
                  pl.BlockSpec((4, HEAD_DIM), lambda b, h, g: (0, 0)),
                  pl.BlockSpec((1, C_VDIM), lambda b, h, g: (0, 0)),
                  _pspec(blk, lambda b, h, g: (b, g, _CBP["qc"] + h), layer),
                  pl.BlockSpec((None, None, LANES, s), lambda b, h, g: (layer, b, h, 0)),
                  _pspec(full, lambda b, h, g: (b, 0, _CBP["vc"] + h), layer),
                  pl.BlockSpec((None, None, ATT_BLK, ATT_BLK), lambda b, h, g: (h, 0, 0, 0)),
                  pl.BlockSpec((None, None, ATT_BLK, ATT_BLK), lambda b, h, g: (h, 1, 0, 0))],
        out_specs=pl.BlockSpec(blk, lambda b, h, g: (b, g, h)),
        out_shape=jax.ShapeDtypeStruct((bn, s, BRANCH_W), BF16),
        scratch_shapes=[state, state, state],
        compiler_params=_cparams("parallel", "parallel", "arbitrary"),
        name="diff_attn_prompt",
    )(t5_flat, lam_params, gain.reshape(1, C_VDIM), pn4, k_t, pn4, t5tiles, t5tiles)


def _diff_attn_sample(p4, layer, k_cache_t, v_cache, bias_cache, bias_new, lam_init, lam_params, gain):
    _, bn, t, _ = p4.shape
    past = k_cache_t.shape[-1]
    new = lambda name: _pspec((None, t, BRANCH_W), lambda b: (b, 0, _CB[name] * LANES // BRANCH_W), layer)
    return pl.pallas_call(
        functools.partial(_diff_sample_kernel, lam_init),
        grid=(bn,),
        in_specs=[pl.BlockSpec((4, HEAD_DIM), lambda b: (0, 0)),
                  pl.BlockSpec((1, C_VDIM), lambda b: (0, 0)),
                  new("qc"),
                  pl.BlockSpec((None, None, BRANCH_W, past), lambda b: (layer, b, 0, 0)),
                  pl.BlockSpec((None, None, past * C_HEADS, C_VDIM), lambda b: (layer, b, 0, 0)),
                  new("kc"), new("vc"),
                  pl.BlockSpec((C_HEADS, t, past), lambda b: (0, 0, 0)),
                  pl.BlockSpec((C_HEADS, t, t), lambda b: (0, 0, 0))],
        out_specs=pl.BlockSpec((None, t, BRANCH_W), lambda b: (b, 0, 0)),
        out_shape=jax.ShapeDtypeStruct((bn, t, BRANCH_W), BF16),
        compiler_params=_cparams("parallel"),
        name="diff_attn_sample",
    )(lam_params, gain.reshape(1, C_VDIM), p4, k_cache_t, v_cache, p4, p4, bias_cache, bias_new)


def _merge_kernel(x_ref, oa_ref, ob_ref, oc_ref, gm_ref, wg_ref, bg_ref, wb_ref, wo_ref, o_ref):
    x = x_ref[...]
    d = x.shape[1]
    xn = _rms(x, gm_ref[...]).astype(BF16)
    h = None
    for n, o_r in enumerate((oa_ref, ob_ref, oc_ref)):
        logits = _dot(xn, wg_ref[:, n * d:(n + 1) * d]) + bg_ref[n:n + 1, :]
        t = _dot(o_r[...], wb_ref[n]) / (1.0 + jnp.exp(-logits))
        h = t if h is None else h + t
    o_ref[...] = x + _dot(h.astype(BF16), wo_ref[...])


def _merge(x, o_a, o_b, o_c, g_mix, w_gate, b_gate, w_branch, w_out, tm):
    t, d = x.shape
    row = lambda i: (i, 0)
    fixed2 = lambda i: (0, 0)
    return pl.pallas_call(
        _merge_kernel,
        grid=(t // tm,),
        in_specs=[pl.BlockSpec((tm, d), row),
                  pl.BlockSpec((tm, BRANCH_W), row),
                  pl.BlockSpec((tm, BRANCH_W), row),
                  pl.BlockSpec((tm, BRANCH_W), row),
                  pl.BlockSpec((1, d), fixed2),
                  pl.BlockSpec((d, N_BRANCH * d), fixed2),
                  pl.BlockSpec((N_BRANCH, d), fixed2),
                  pl.BlockSpec((N_BRANCH, BRANCH_W, d), lambda i: (0, 0, 0)),
                  pl.BlockSpec((d, d), fixed2)],
        out_specs=pl.BlockSpec((tm, d), row),
        out_shape=jax.ShapeDtypeStruct((t, d), F32),
        compiler_params=_cparams("parallel"),
        name="gated_merge",
    )(x, o_a, o_b, o_c, g_mix.reshape(1, d), w_gate, b_gate, w_branch, w_out)


FFN_TF = D_FF // 2
FFN_PREV = 16


def _gelu_tanh(x):
    return x * (0.5 * (1.0 + jnp.tanh(math.sqrt(2.0 / math.pi) * (x + 0.044715 * (x * x * x)))))


def _ffn_kernel(tiles_per_seq, final, x_ref, xp_ref, gn_ref, wg_ref, wu_ref, cw_ref, cb_ref, wd_ref, st_ref, gf_ref,
                o_ref, so_ref, xn_ref, xpn_ref):
    i = pl.program_id(0)
    j = pl.program_id(1)
    whole_seqs = tiles_per_seq == 0

    @pl.when(j == 0)
    def _():
        xn_ref[...] = _rms(x_ref[...], gn_ref[...]).astype(BF16)
        if not whole_seqs:
            xpn_ref[...] = _rms(xp_ref[...], gn_ref[...]).astype(BF16)

    xn = xn_ref[...]
    tm = xn.shape[0]
    tf = wg_ref.shape[1]
    g = _dot(xn, wg_ref[...])
    u = _dot(xn, wu_ref[...])
    if whole_seqs:
        ns = st_ref.shape[0]
        st = st_ref[...]
        g = g.reshape(ns, tm // ns, tf)
        u = u.reshape(ns, tm // ns, tf)
        pm2, pm1 = st[:, 0:1, :], st[:, 1:2, :]
    else:
        gp = _dot(xpn_ref[...], wg_ref[...])
        first = (i % tiles_per_seq) == 0
        st = st_ref[0]
        pm2 = jnp.where(first, st[0:1, :], gp[FFN_PREV - 2:FFN_PREV - 1, :])
        pm1 = jnp.where(first, st[1:2, :], gp[FFN_PREV - 1:FFN_PREV, :])
    ax = g.ndim - 2
    row = lax.broadcasted_iota(jnp.int32, g.shape, ax)
    g1 = jnp.where(row == 0, pm1, pltpu.roll(g, 1, ax))
    g2 = jnp.where(row == 0, pm2, jnp.where(row == 1, pm1, pltpu.roll(g, 2, ax)))
    cw = cw_ref[...]
    c = cb_ref[...] + cw[0:1, :] * g2 + cw[1:2, :] * g1 + cw[2:3, :] * g
    hid = (_gelu_tanh(c) * u).reshape(tm, tf).astype(BF16)
    contrib = _dot(hid, wd_ref[...])

    last = pl.num_programs(1) - 1
    finish = (lambda y: _rms(y, gf_ref[...])) if final else (lambda y: y)

    @pl.when(j == 0)
    def _():
        o_ref[...] = x_ref[...] + contrib

    @pl.when(jnp.logical_and(j > 0, j < last))
    def _():
        o_ref[...] += contrib

    @pl.when(jnp.logical_and(j > 0, j == last))
    def _():
        o_ref[...] = finish(o_ref[...] + contrib)

    seq_len = g.shape[ax]
    if whole_seqs:
        so_ref[...] = g[:, seq_len - 2:seq_len, :]
    else:
        so_ref[0] = g[seq_len - 2:seq_len, :]


def _conv_ffn(x, g_norm, w_gate, w_up, conv_w, conv_b, w_down, state, seq_len, tm, g_final, final):
    t, d = x.shape
    nseq = t // seq_len
    nf = D_FF // FFN_TF
    assert nf > 1
    if seq_len >= tm:
        tiles_per_seq = seq_len // tm
        st_spec = pl.BlockSpec((1, 2, FFN_TF), lambda i, j: (i // tiles_per_seq, 0, j))
    else:
        tiles_per_seq = 0
        st_spec = pl.BlockSpec((tm // seq_len, 2, FFN_TF), lambda i, j: (i, 0, j))
    prev_blocks = tm // FFN_PREV
    tail_spec = pl.BlockSpec(st_spec.block_shape, lambda i, j: (i, 0, j))
    n_tail = t // tm * st_spec.block_shape[0]
    y, tails = pl.pallas_call(
        functools.partial(_ffn_kernel, tiles_per_seq, final),
        grid=(t // tm, nf),
        in_specs=[pl.BlockSpec((tm, d), lambda i, j: (i, 0)),
                  pl.BlockSpec((FFN_PREV, d), lambda i, j: (jnp.maximum(i * prev_blocks - 1, 0), 0)),
                  pl.BlockSpec((1, d), lambda i, j: (0, 0)),
                  pl.BlockSpec((d, FFN_TF), lambda i, j: (0, j)),
                  pl.BlockSpec((d, FFN_TF), lambda i, j: (0, j)),
                  pl.BlockSpec((3, FFN_TF), lambda i, j: (0, j)),
                  pl.BlockSpec((1, FFN_TF), lambda i, j: (0, j)),
                  pl.BlockSpec((FFN_TF, d), lambda i, j: (j, 0)),
                  st_spec,
                  pl.BlockSpec((1, d), lambda i, j: (0, 0))],
        out_specs=[pl.BlockSpec((tm, d), lambda i, j: (i, 0)), tail_spec],
        out_shape=[jax.ShapeDtypeStruct((t, d), F32),
                   jax.ShapeDtypeStruct((n_tail, 2, D_FF), F32)],
        scratch_shapes=[pltpu.VMEM((tm, d), BF16), pltpu.VMEM((FFN_PREV, d), BF16)],
        compiler_params=_cparams("arbitrary", "arbitrary"),
        name="conv_ffn_prompt" if tiles_per_seq else "conv_ffn_sample",
    )(x, x, g_norm.reshape(1, d), w_gate, w_up, conv_w, conv_b.reshape(1, D_FF), w_down, state, g_final.reshape(1, d))
    return y, tails[n_tail // nseq - 1::n_tail // nseq]


PROJ_TM = 1024
PROMPT_TN, SAMPLE_TN = ROW_COLS // 2, IN_COLS // 3
ROW_TM = 512


def _lam_init(layer):
    return 0.8 - 0.6 * math.exp(-0.3 * layer)


def _cols(p4, name):
    c0 = _CB[name] * LANES
    return p4[..., c0:c0 + BRANCH_W]


def _layer_tail(x2, o_a, o_b, o_c, layer, prm, conv_state, seq_len):
    flat = lambda a: a.reshape(x2.shape[0], BRANCH_W)
    x2 = _merge(x2, flat(o_a), flat(o_b), flat(o_c), prm["norm_mix"][layer], prm["w_gate_mix"][layer],
                prm["b_gate"][layer], prm["w_branch"][layer], prm["w_out"][layer], ROW_TM)
    depth = prm["norm_ffn"].shape[0]
    return _conv_ffn(x2, prm["norm_ffn"][layer], prm["w_gate"][layer], prm["w_up"][layer],
                     prm["conv_w"][layer], prm["conv_b"][layer], prm["w_down"][layer], conv_state, seq_len, ROW_TM,
                     prm["norm_final"], layer == depth - 1)


def _trunk_prompt(x, prm, abias, t5tiles, t5_flat):
    bn, s, d = x.shape
    x2 = x.reshape(bn * s, d)
    depth = prm["w_rows"].shape[0]
    conv_states = []
    rows, kv_t = None, None
    for l in range(depth):
        rows = _norm_proj(x2, prm["norm_mix"][l], prm["w_rows"][l], PROJ_TM, PROMPT_TN, l, depth, rows,
                          heads_group=(_CBP["vc"] * LANES, C_HEADS))
        p_rows, vc_heads = rows
        kv_t = _norm_proj_t(x2, prm["norm_mix"][l], prm["w_feat_t"][l], PROJ_TM, l, depth, bn, kv_t)
        ka_t, va_t, kb_t, vb_t, kc_t = kv_t
        pn4 = p_rows.reshape(depth, bn, s, ROW_COLS)
        o_a = _band_attn_prompt(pn4, l, ka_t, va_t, abias[l])
        o_b = _sb_attn_prompt(pn4, l, kb_t, vb_t)
        o_c = _diff_attn_prompt(pn4, l, kc_t, t5tiles, _lam_init(l), t5_flat, prm["c_lambda"][l], prm["c_subln"][l])
        x2, st_conv = _layer_tail(x2, o_a, o_b, o_c, l, prm, jnp.zeros((bn, 2, D_FF), F32), s)
        conv_states.append(st_conv)
    y = x2.reshape(bn, s, d)
    heads_t = lambda a: jnp.transpose(a.reshape(depth, bn, A_HEADS, HEAD_DIM, a.shape[-1]), (0, 1, 4, 2, 3))
    keep = min(A_PAST_CHUNKS * CHUNK, s)
    states = (heads_t(ka_t[..., s - keep:]), heads_t(va_t[..., s - keep:]), heads_t(kb_t), heads_t(vb_t),
              jnp.transpose(kc_t.reshape(depth, bn, C_HEADS, 2, HEAD_DIM, s), (0, 1, 5, 2, 3, 4)),
              vc_heads.reshape(depth, bn, s, C_HEADS, C_VDIM),
              jnp.stack(conv_states, axis=0))
    return y, states


def _feature_major(cache):
    nd = cache.ndim
    t = jnp.transpose(cache, (0, 1) + tuple(range(3, nd)) + (2,))
    return t.reshape(cache.shape[0], cache.shape[1], BRANCH_W, cache.shape[2])


def _trunk_sample(x, caches, prm, abias, t5tiles):
    bn, t, d = x.shape
    x2 = x.reshape(bn * t, d)
    depth = prm["w_in"].shape[0]
    ca_k, ca_v, cb_k, cb_v, cc_k, cc_v, st_conv_in = caches
    ca_kt, ca_vt, cb_kt, cb_vt, cc_kt = (_feature_major(c) for c in (ca_k, ca_v, cb_k, cb_v, cc_k))
    keep, past = ca_k.shape[2], cc_k.shape[2]
    cc_vr = cc_v.reshape(depth, bn, past * C_HEADS, C_VDIM)
    conv_states = []
    p_all = None
    for l in range(depth):
        p_all, = _norm_proj(x2, prm["norm_mix"][l], prm["w_in"][l], PROJ_TM, SAMPLE_TN, l, depth,
                            None if p_all is None else [p_all])
        p4 = p_all.reshape(depth, bn, t, IN_COLS)
        bias_cache = jnp.concatenate([abias[l][:, 0, :t, :], abias[l][:, 1, :t, :]], axis=-1)[..., -keep:]
        o_a = _band_attn_sample(p4, l, ca_kt, ca_vt, bias_cache, abias[l][:, 2, :t, :t])
        o_b = _sb_attn_sample(p4, l, cb_kt, cb_vt)
        far = jnp.broadcast_to(t5tiles[:, 1, :1, :1], (C_HEADS, t, past - ATT_BLK))
        bias_c = jnp.concatenate([far, t5tiles[:, 1, :t, :]], axis=-1)
        o_c = _diff_attn_sample(p4, l, cc_kt, cc_vr, bias_c, t5tiles[:, 0, :t, :t],
                                _lam_init(l), prm["c_lambda"][l], prm["c_subln"][l])
        x2, st_conv = _layer_tail(x2, o_a, o_b, o_c, l, prm, st_conv_in[l], t)
        conv_states.append(st_conv)
    y = x2.reshape(bn, t, d)
    kb, vb, kc, vc = (_cols(p4, n) for n in ("kb", "vb", "kc", "vc"))
    heads = lambda a: a.reshape(a.shape[:3] + (A_HEADS, HEAD_DIM))
    heads_t = lambda a: jnp.transpose(a.reshape(depth, bn, A_HEADS, HEAD_DIM, a.shape[-1]), (0, 1, 4, 2, 3))
    sa_kt, sa_vt = _band_roll(ca_kt, ca_vt, p4)
    states = (heads_t(sa_kt), heads_t(sa_vt), heads(kb), heads(vb),
              kc.reshape(depth, bn, t, C_HEADS, 2, HEAD_DIM), vc.reshape(depth, bn, t, C_HEADS, C_VDIM),
              jnp.stack(conv_states, axis=0))
    return y, states


def kernel(x_prompt, x_sample, cache_a_k, cache_a_v, cache_b_k, cache_b_v, cache_c_k, cache_c_v,
           state_ffn_conv, norm_mix, w_in, b_gate, a_rel_bias, t5_bias, c_lambda, c_subln,
           w_branch, w_out, norm_ffn, w_up, conv_w, conv_b, w_down, norm_final):
    group = {name: w_in[..., i * BRANCH_W:(i + 1) * BRANCH_W] for i, name in enumerate(_GROUPS)}
    gate = w_in[..., len(_GROUPS) * BRANCH_W:]
    prm = dict(
        norm_mix=norm_mix,
        w_in=w_in[..., :len(_GROUPS) * BRANCH_W].astype(BF16),
        w_rows=jnp.concatenate([group[n] for n in _ROW_GROUPS], axis=-1).astype(BF16),
        w_gate_mix=gate.astype(BF16),
        w_feat_t=jnp.swapaxes(jnp.concatenate([group[n] for n in _T_GROUPS], axis=-1), 1, 2).astype(BF16),
        b_gate=b_gate, c_lambda=c_lambda, c_subln=c_subln,
        w_branch=w_branch.astype(BF16), w_out=w_out.astype(BF16), norm_ffn=norm_ffn,
        w_gate=w_up[..., :D_FF].astype(BF16), w_up=w_up[..., D_FF:].astype(BF16),
        conv_w=conv_w, conv_b=conv_b, w_down=w_down.astype(BF16), norm_final=norm_final)
    abias = [_build_abias(a_rel_bias[l]) for l in range(a_rel_bias.shape[0])]
    t5tiles = _build_t5bias(t5_bias)
    y_p, p_states = _trunk_prompt(x_prompt, prm, abias, t5tiles, t5_bias.reshape(-1))
    caches = (cache_a_k, cache_a_v, cache_b_k, cache_b_v, cache_c_k, cache_c_v, state_ffn_conv)
    y_s, s_states = _trunk_sample(x_sample, caches, prm, abias, t5tiles)
    return (y_p, y_s) + p_states + s_states
```

```python
import functools
import math

import jax
import jax.numpy as jnp
from jax import lax
from jax.experimental import pallas as pl
from jax.experimental.pallas import tpu as pltpu

F32 = jnp.float32
BF16 = jnp.bfloat16

D_MODEL = 1024
CHUNK = 64
HEAD_DIM = 64
A_HEADS = 8
A_PAST_CHUNKS = 8
A_REL_CLIP = 128
B_HEADS = 8
C_HEADS = 4
C_VDIM = 128
T5_BUCKETS = 32
D_FF = 2816
N_BRANCH = 3
BRANCH_W = 512
IN_COLS = 9 * BRANCH_W
EPS = 1e-6
NEG_INF = -1e30
SCALE = HEAD_DIM ** -0.5
LOG2E = math.log2(math.e)
SOFTMAX_QSCALE = SCALE * LOG2E

LANES = 128
_GROUPS = ("qa", "ka", "va", "qb", "kb", "vb", "qc", "kc", "vc")
_CB = {name: i * BRANCH_W // LANES for i, name in enumerate(_GROUPS)}
_ROW_GROUPS = ("qa", "qb", "qc", "vc")
_T_GROUPS = ("ka", "va", "kb", "vb", "kc")
_CBP = {name: i * BRANCH_W // LANES for i, name in enumerate(_ROW_GROUPS)}
ROW_COLS = len(_ROW_GROUPS) * BRANCH_W

ATT_BLK = 256
ATT_HEADS = 4
SB_LOG_CUT = -104.0
T5_FAR_BUCKET = 15
BAND_SUB = 4
DIFF_SUB = 4
DIFF_FAR_GROUP = 4
V7X_VMEM_BYTES = 64 * 1024 * 1024
VMEM_LIMIT = V7X_VMEM_BYTES - 8 * 1024 * 1024


def _cparams(*sem):
    return pltpu.CompilerParams(dimension_semantics=sem, vmem_limit_bytes=VMEM_LIMIT)


def _rms(x, g):
    return x * lax.rsqrt(jnp.mean(x * x, axis=-1, keepdims=True) + EPS) * g


def _dot(a, b):
    return jnp.dot(a, b, preferred_element_type=F32)


def _dot_nt(a, b):
    return lax.dot_general(a, b, (((1,), (1,)), ((), ())), preferred_element_type=F32)


def _norm_proj_kernel(n_stack, heads_cols, x_ref, g_ref, w_ref, *refs):
    outs, xn_ref = refs[n_stack:-1], refs[-1]
    j = pl.program_id(1)

    @pl.when(j == 0)
    def _():
        xn_ref[...] = _rms(x_ref[...], g_ref[...]).astype(BF16)

    res = _dot(xn_ref[...], w_ref[...])
    outs[0][...] = res
    if heads_cols is not None:
        tile, first, heads = heads_cols

        @pl.when(j == tile)
        def _():
            for h in range(heads):
                lo = first + h * LANES
                outs[1][pl.ds(h, res.shape[0], stride=heads), :] = res[:, lo:lo + LANES]


def _norm_proj(x, g, w, tm, tn, layer, depth, stacked, heads_group=None):
    t, d = x.shape
    n = w.shape[1]
    in_specs = [pl.BlockSpec((tm, d), lambda i, j: (i, 0)),
                pl.BlockSpec((1, d), lambda i, j: (0, 0)),
                pl.BlockSpec((d, tn), lambda i, j: (0, j))]
    args = [x, g.reshape(1, d), w]
    out_specs = [pl.BlockSpec((None, tm, tn), lambda i, j: (layer, i, j))]
    out_shape = [jax.ShapeDtypeStruct((depth, t, n), F32)]
    heads_cols = None
    if heads_group is not None:
        first, heads = heads_group
        heads_cols = (first // tn, first % tn, heads)
        out_specs.append(pl.BlockSpec((None, tm * heads, LANES), lambda i, j: (layer, i, 0)))
        out_shape.append(jax.ShapeDtypeStruct((depth, t * heads, LANES), F32))
    n_stack = 0 if stacked is None else len(out_shape)
    if stacked is not None:
        in_specs += [pl.BlockSpec(memory_space=pl.ANY)] * n_stack
        args += list(stacked)
    return pl.pallas_call(
        functools.partial(_norm_proj_kernel, n_stack, heads_cols),
        grid=(t // tm, n // tn),
        in_specs=in_specs,
        out_specs=out_specs,
        out_shape=out_shape,
        scratch_shapes=[pltpu.VMEM((tm, d), BF16)],
        input_output_aliases={3 + k: k for k in range(n_stack)},
        compiler_params=_cparams("arbitrary", "arbitrary"),
        name="norm_proj",
    )(*args)


def _norm_proj_t_kernel(ngroups, x_ref, g_ref, wt_ref, *refs):
    o_refs = refs[-ngroups:]
    xn = _rms(x_ref[...], g_ref[...]).astype(BF16)
    for g in range(ngroups):
        o_refs[g][...] = _dot_nt(wt_ref[g * BRANCH_W:(g + 1) * BRANCH_W, :], xn)


def _norm_proj_t(x, g, wt, tm, layer, depth, bn, stacked):
    t, d = x.shape
    s = t // bn
    ngroups = wt.shape[0] // BRANCH_W
    tiles = s // tm
    in_specs = [pl.BlockSpec((tm, d), lambda i: (i, 0)),
                pl.BlockSpec((1, d), lambda i: (0, 0)),
                pl.BlockSpec(wt.shape, lambda i: (0, 0))]
    args = [x, g.reshape(1, d), wt]
    aliases = {}
    if stacked is not None:
        in_specs += [pl.BlockSpec(memory_space=pl.ANY)] * ngroups
        args += list(stacked)
        aliases = {3 + n: n for n in range(ngroups)}
    out_spec = pl.BlockSpec((None, None, BRANCH_W, tm), lambda i: (layer, i // tiles, 0, i % tiles))
    return pl.pallas_call(
        functools.partial(_norm_proj_t_kernel, ngroups),
        grid=(t // tm,),
        in_specs=in_specs,
        out_specs=[out_spec] * ngroups,
        out_shape=[jax.ShapeDtypeStruct((depth, bn, BRANCH_W, s), F32)] * ngroups,
        input_output_aliases=aliases,
        compiler_params=_cparams("arbitrary"),
        name="norm_proj_t",
    )(*args)


def _abias_kernel(tbl_ref, o_ref):
    h = pl.program_id(0)
    sub = 8
    ql = lax.broadcasted_iota(jnp.int32, (ATT_BLK, ATT_BLK), 0)
    kl = lax.broadcasted_iota(jnp.int32, (ATT_BLK, ATT_BLK), 1)
    far = jnp.full((ATT_BLK, ATT_BLK), tbl_ref[2 * A_REL_CLIP * A_HEADS + h] * LOG2E, F32)
    o_ref[0, 0] = jnp.where((kl >> 6) >= (ql >> 6), far, NEG_INF)
    lead = ATT_BLK - sub
    b = lax.broadcasted_iota(jnp.int32, (sub, 2 * ATT_BLK), 0)
    x = lax.broadcasted_iota(jnp.int32, (sub, 2 * ATT_BLK), 1)
    for seg, off in ((1, ATT_BLK), (2, 0)):
        idx = jnp.clip(b + lead - x + off, -A_REL_CLIP, A_REL_CLIP) + A_REL_CLIP

        def body(r, acc, idx=idx):
            return jnp.where(idx == r, tbl_ref[r * A_HEADS + h] * LOG2E, acc)

        strip = lax.fori_loop(0, 2 * A_REL_CLIP + 1, body, jnp.zeros((sub, 2 * ATT_BLK), F32))
        for a in range(ATT_BLK // sub):
            o_ref[0, seg, a * sub:(a + 1) * sub, :] = strip[:, lead - a * sub:lead - a * sub + ATT_BLK]
    o_ref[0, 2] = jnp.where((kl >> 6) <= (ql >> 6), o_ref[0, 2], NEG_INF)


def _build_abias(table):
    return pl.pallas_call(
        _abias_kernel,
        grid=(A_HEADS,),
        in_specs=[pl.BlockSpec(memory_space=pltpu.SMEM)],
        out_specs=pl.BlockSpec((1, 3, ATT_BLK, ATT_BLK), lambda h: (h, 0, 0, 0)),
        out_shape=jax.ShapeDtypeStruct((A_HEADS, 3, ATT_BLK, ATT_BLK), F32),
        compiler_params=_cparams("parallel"),
        name="band_bias_tiles",
    )(table.reshape(-1))


_T5_LARGE_STEPS = (12, 16, 23, 32, 46, 64, 91)


def _t5bias_kernel(tbl_ref, o_ref):
    h = pl.program_id(0)
    rows = 64
    for seg, off in ((0, 0), (1, -ATT_BLK)):
        for rc in range(ATT_BLK // rows):
            ql = lax.broadcasted_iota(jnp.int32, (rows, ATT_BLK), 0) + rc * rows
            kl = lax.broadcasted_iota(jnp.int32, (rows, ATT_BLK), 1)
            rel = kl - ql + off
            n = jnp.abs(rel)
            large = jnp.full((rows, ATT_BLK), T5_BUCKETS // 4, jnp.int32)
            for th in _T5_LARGE_STEPS:
                large = large + jnp.where(n >= th, 1, 0)
            bucket = jnp.where(rel > 0, T5_BUCKETS // 2, 0) + jnp.where(n < T5_BUCKETS // 4, n, large)
            tile = jnp.zeros((rows, ATT_BLK), F32)
            for r in range(T5_BUCKETS):
                tile = jnp.where(bucket == r, tbl_ref[r * C_HEADS + h] * LOG2E, tile)
            if seg == 0:
                tile = jnp.where((kl >> 6) <= (ql >> 6), tile, NEG_INF)
            o_ref[0, seg, rc * rows:(rc + 1) * rows, :] = tile


def _build_t5bias(table):
    return pl.pallas_call(
        _t5bias_kernel,
        grid=(C_HEADS,),
        in_specs=[pl.BlockSpec(memory_space=pltpu.SMEM)],
        out_specs=pl.BlockSpec((1, 2, ATT_BLK, ATT_BLK), lambda h: (h, 0, 0, 0)),
        out_shape=jax.ShapeDtypeStruct((C_HEADS, 2, ATT_BLK, ATT_BLK), F32),
        compiler_params=_cparams("parallel"),
        name="t5_bias_tiles",
    )(table.reshape(-1))


def _pspec(block, index_map, layer):
    return pl.BlockSpec((None,) + block, lambda *g: (layer,) + tuple(index_map(*g)))


def _lane_tile(x, n):
    return x if n == 1 else jnp.concatenate([x] * n, axis=1)


def _stack_heads(q, width):
    group = lax.broadcasted_iota(jnp.int32, q.shape, 1) // width
    return jnp.concatenate([jnp.where(group == h, q, 0.0) for h in range(q.shape[1] // width)],
                           axis=0).astype(BF16)


def _unstack_heads(x, t, width):
    group = lax.broadcasted_iota(jnp.int32, (t, x.shape[1]), 1) // width
    out = jnp.zeros((t, x.shape[1]), x.dtype)
    for h in range(x.shape[1] // width):
        out = jnp.where(group == h, x[h * t:(h + 1) * t, :], out)
    return out


def _qk(q, k, k_t):
    return _dot(q, k) if k_t else _dot_nt(q, k)


def _pv(p, v, v_t):
    return _dot_nt(p, v) if v_t else _dot(p, v)


def _softmax_segments(qh, kb, vb, biases, k_t, v_t):
    s = [_qk(qh, k, t) + b for k, b, t in zip(kb, biases, k_t)]
    m = functools.reduce(jnp.maximum, [jnp.max(sj, axis=1, keepdims=True) for sj in s])
    p = [jnp.exp2(sj - m) for sj in s]
    l = functools.reduce(jnp.add, [jnp.sum(pj, axis=1, keepdims=True) for pj in p])
    acc = functools.reduce(jnp.add, [_pv(pj.astype(BF16), v, t) for pj, v, t in zip(p, vb, v_t)])
    return acc / l


def _band_kernel(q_ref, k_ref, v_ref, b_ref, o_ref):
    blk = ATT_BLK
    heads = b_ref.shape[0]

    def one_block(sub, carry):
        qi = pl.program_id(2) * BAND_SUB + sub
        rows = pl.ds(pl.multiple_of(sub * blk, blk), blk)
        qs = _stack_heads(q_ref[rows, :] * SOFTMAX_QSCALE, HEAD_DIM)
        kb, vb, biases = [], [], []
        for seg, back in enumerate((2, 1, 0)):
            keys = pl.ds(pl.multiple_of(jnp.maximum(qi - back, 0) * blk, blk), blk)
            kb.append(k_ref[:, keys].astype(BF16))
            vb.append(v_ref[:, keys].astype(BF16))
            tile = b_ref[:, seg].reshape(heads * blk, blk)
            biases.append(tile if back == 0 else jnp.where(qi >= back, tile, NEG_INF))
        o = _softmax_segments(qs, kb, vb, biases, (True,) * 3, (True,) * 3)
        o_ref[rows, :] = _unstack_heads(o, blk, HEAD_DIM).astype(o_ref.dtype)
        return carry

    lax.fori_loop(0, BAND_SUB, one_block, 0)


def _band_attn_prompt(pn4, layer, k_t, v_t, abias):
    _, bn, s, _ = pn4.shape
    width = ATT_HEADS * HEAD_DIM
    blk = (None, BAND_SUB * ATT_BLK, width)
    seq = pl.BlockSpec((None, None, width, s), lambda b, hg, g: (layer, b, hg, 0))
    return pl.pallas_call(
        _band_kernel,
        grid=(bn, A_HEADS // ATT_HEADS, s // (BAND_SUB * ATT_BLK)),
        in_specs=[_pspec(blk, lambda b, hg, g: (b, g, _CBP["qa"] * LANES // width + hg), layer), seq, seq,
                  pl.BlockSpec((ATT_HEADS, 3, ATT_BLK, ATT_BLK), lambda b, hg, g: (hg, 0, 0, 0))],
        out_specs=pl.BlockSpec(blk, lambda b, hg, g: (b, g, hg)),
        out_shape=jax.ShapeDtypeStruct((bn, s, BRANCH_W), BF16),
        compiler_params=_cparams("parallel", "parallel", "arbitrary"),
        name="band_attn_prompt",
    )(pn4, k_t, v_t, abias)


def _band_sample_kernel(q_ref, kc_ref, kn_ref, vc_ref, vn_ref, bc_ref, bn_ref, o_ref):
    t = q_ref.shape[0]
    width = ATT_HEADS * HEAD_DIM
    q = q_ref[...] * SOFTMAX_QSCALE
    for g in range(q.shape[1] // width):
        cols = slice(g * width, (g + 1) * width)
        rows = slice(g * ATT_HEADS * t, (g + 1) * ATT_HEADS * t)
        qs = _stack_heads(q[:, cols], HEAD_DIM)
        kb = [kc_ref[cols, :].astype(BF16), kn_ref[:, cols].astype(BF16)]
        vb = [vc_ref[cols, :].astype(BF16), vn_ref[:, cols].astype(BF16)]
        o = _softmax_segments(qs, kb, vb, [bc_ref[rows, :], bn_ref[rows, :]], (True, False), (True, False))
        o_ref[:, cols] = _unstack_heads(o, t, HEAD_DIM).astype(o_ref.dtype)


def _band_attn_sample(p4, layer, k_cache_t, v_cache_t, bias_cache, bias_new):
    _, bn, t, _ = p4.shape
    keep = k_cache_t.shape[-1]
    new = lambda name: _pspec((None, t, BRANCH_W), lambda b: (b, 0, _CB[name] * LANES // BRANCH_W), layer)
    old = pl.BlockSpec((None, None, BRANCH_W, keep), lambda b: (layer, b, 0, 0))
    return pl.pallas_call(
        _band_sample_kernel,
        grid=(bn,),
        in_specs=[new("qa"), old, new("ka"), old, new("va"),
                  pl.BlockSpec((A_HEADS * t, keep), lambda b: (0, 0)),
                  pl.BlockSpec((A_HEADS * t, t), lambda b: (0, 0))],
        out_specs=pl.BlockSpec((None, t, BRANCH_W), lambda b: (b, 0, 0)),
        out_shape=jax.ShapeDtypeStruct((bn, t, BRANCH_W), BF16),
        compiler_params=_cparams("parallel"),
        name="band_attn_sample",
    )(p4, k_cache_t, p4, v_cache_t, p4, bias_cache.reshape(A_HEADS * t, keep), bias_new.reshape(A_HEADS * t, t))


def _band_roll_kernel(kc_ref, vc_ref, kn_ref, vn_ref, ko_ref, vo_ref):
    t = kn_ref.shape[0]
    keep = kc_ref.shape[1]
    for c_ref, n_ref, o_ref in ((kc_ref, kn_ref, ko_ref), (vc_ref, vn_ref, vo_ref)):
        rolled = pltpu.roll(c_ref[...], keep - t, 1)
        new = jnp.concatenate([n_ref[...], jnp.zeros((LANES - t, n_ref.shape[1]), F32)], axis=0)
        new_t = pltpu.roll(new.T, LANES - t, 1)
        lane = lax.broadcasted_iota(jnp.int32, new_t.shape, 1)
        last = jnp.where(lane >= LANES - t, new_t, rolled[:, keep - LANES:])
        o_ref[...] = jnp.concatenate([rolled[:, :keep - LANES], last], axis=1)


def _band_roll(k_cache_t, v_cache_t, p4):
    depth, bn, t, _ = p4.shape
    keep = k_cache_t.shape[-1]
    assert t <= LANES <= keep
    old = pl.BlockSpec((None, None, BRANCH_W, keep), lambda l, b: (l, b, 0, 0))
    new = lambda name: pl.BlockSpec((None, None, t, BRANCH_W), lambda l, b: (l, b, 0, _CB[name] * LANES // BRANCH_W))
    return pl.pallas_call(
        _band_roll_kernel,
        grid=(depth, bn),
        in_specs=[old, old, new("ka"), new("va")],
        out_specs=[old, old],
        out_shape=[jax.ShapeDtypeStruct(k_cache_t.shape, F32)] * 2,
        compiler_params=_cparams("parallel", "parallel"),
        name="band_roll",
    )(k_cache_t, v_cache_t, p4, p4)


def _strict_upper(n):
    r = lax.broadcasted_iota(jnp.int32, (n, n), 0)
    c = lax.broadcasted_iota(jnp.int32, (n, n), 1)
    return jnp.where(r > c, 1.0, 0.0).astype(BF16)


def _sb_fold(qh, kb, vb, kv_t, upper, mask, carry_ref, acc_ref, first):
    tq = qh[0].shape[0]
    tk = kb[0].shape[1] if kv_t else kb[0].shape[0]
    cw = upper.shape[0]
    top = None
    for h in range(len(qh)):
        z = _qk(qh[h], kb[0], kv_t)
        soft = jnp.log(1.0 + jnp.exp(-jnp.abs(z)))
        log_keep = -(jnp.maximum(z, 0.0) + soft)
        log_take = jnp.minimum(z, 0.0) - soft
        if mask is not None:
            log_keep = jnp.where(mask, log_keep, 0.0)
        hi = log_keep.astype(BF16)
        lo = (log_keep - hi.astype(F32)).astype(BF16)
        pieces, total = [], None
        for c0 in reversed(range(0, tk, cw)):
            piece = _dot(hi[:, c0:c0 + cw], upper) + _dot(lo[:, c0:c0 + cw], upper)
            pieces.insert(0, piece if total is None else piece + total)
            part = jnp.sum(log_keep[:, c0:c0 + cw], axis=1, keepdims=True)
            total = part if total is None else total + part
        after = pieces[0] if len(pieces) == 1 else jnp.concatenate(pieces, axis=1)
        if not first:
            after = after + _lane_tile(carry_ref[h], tk // LANES)
        w = jnp.exp(log_take + after)
        if mask is not None:
            w = jnp.where(mask, w, 0.0)
        pv = _pv(w.astype(BF16), vb[0], kv_t)
        if first:
            acc_ref[h] = pv
            carry = jnp.broadcast_to(total, (tq, LANES))
        else:
            acc_ref[h] += pv
            carry = carry_ref[h] + total
        carry_ref[h] = carry
        top = jnp.max(carry) if top is None else jnp.maximum(top, jnp.max(carry))
    return (top > SB_LOG_CUT).astype(jnp.int32)


def _sb_kernel(tk, n_past_static, own_rows, q_ref, kd_ref, vd_ref, kp_ref, vp_ref, o_ref, carry_ref, acc_ref):
    n_past = pl.program_id(2) if n_past_static is None else n_past_static
    q = q_ref[...] * SCALE
    tq = q.shape[0]
    rq = lax.broadcasted_iota(jnp.int32, (tq, tq), 0)
    ck = lax.broadcasted_iota(jnp.int32, (tq, tq), 1)
    qh = [_stack_heads(q, HEAD_DIM)]
    causal = jnp.concatenate([ck < rq] * (q.shape[1] // HEAD_DIM), axis=0)
    if own_rows:
        kd, vd = [kd_ref[...].astype(BF16)], [vd_ref[...].astype(BF16)]
        live = _sb_fold(qh, kd, vd, False, _strict_upper(tq), causal, carry_ref, acc_ref, True)
        n_left = n_past
    else:
        near = pl.multiple_of(jnp.maximum(n_past - 1, 0) * tq, tq)
        kd = [jnp.concatenate([kp_ref[:, pl.ds(near, tq)], kd_ref[...]], axis=1).astype(BF16)]
        vd = [jnp.concatenate([vp_ref[:, pl.ds(near, tq)], vd_ref[...]], axis=1).astype(BF16)]
        mask = jnp.concatenate([jnp.broadcast_to(n_past >= 1, causal.shape), causal], axis=1)
        live = _sb_fold(qh, kd, vd, True, _strict_upper(tq), mask, carry_ref, acc_ref, True)
        n_left = n_past - 1

    def cond(st):
        return jnp.logical_and(st[0] >= 0, st[1] > 0)

    def body(st):
        start = pl.multiple_of(st[0] * tk, tk)
        kp, vp = [kp_ref[:, pl.ds(start, tk)].astype(BF16)], [vp_ref[:, pl.ds(start, tk)].astype(BF16)]
        live = _sb_fold(qh, kp, vp, True, _strict_upper(tk), None, carry_ref, acc_ref, False)
        return st[0] - 1, live

    lax.while_loop(cond, body, (n_left - 1, live))
    o_ref[...] = _unstack_heads(acc_ref[0], tq, HEAD_DIM).astype(o_ref.dtype)


def _sb_attn_prompt(pn4, layer, k_t, v_t):
    _, bn, s, _ = pn4.shape
    width = ATT_HEADS * HEAD_DIM
    blk = (None, ATT_BLK, width)
    own = pl.BlockSpec((None, None, width, ATT_BLK), lambda b, hg, qi: (layer, b, hg, qi))
    older = pl.BlockSpec((None, None, width, s), lambda b, hg, qi: (layer, b, hg, 0))
    rows = ATT_HEADS * ATT_BLK
    return pl.pallas_call(
        functools.partial(_sb_kernel, ATT_BLK, None, False),
        grid=(bn, B_HEADS // ATT_HEADS, s // ATT_BLK),
        in_specs=[_pspec(blk, lambda b, hg, qi: (b, qi, _CBP["qb"] * LANES // width + hg), layer),
                  own, own, older, older],
        out_specs=pl.BlockSpec(blk, lambda b, hg, qi: (b, qi, hg)),
        out_shape=jax.ShapeDtypeStruct((bn, s, BRANCH_W), BF16),
        scratch_shapes=[pltpu.VMEM((1, rows, LANES), F32), pltpu.VMEM((1, rows, width), F32)],
        compiler_params=_cparams("parallel", "parallel", "arbitrary"),
        name="stick_attn_prompt",
    )(pn4, k_t, v_t, k_t, v_t)


def _sb_attn_sample(p4, layer, k_cache_t, v_cache_t):
    _, bn, t, _ = p4.shape
    past = k_cache_t.shape[-1]
    new = lambda name: _pspec((None, t, BRANCH_W), lambda b: (b, 0, _CB[name] * LANES // BRANCH_W), layer)
    older = pl.BlockSpec((None, None, BRANCH_W, past), lambda b: (layer, b, 0, 0))
    rows = B_HEADS * t
    return pl.pallas_call(
        functools.partial(_sb_kernel, ATT_BLK, past // ATT_BLK, True),
        grid=(bn,),
        in_specs=[new("qb"), new("kb"), new("vb"), older, older],
        out_specs=pl.BlockSpec((None, t, BRANCH_W), lambda b: (b, 0, 0)),
        out_shape=jax.ShapeDtypeStruct((bn, t, BRANCH_W), BF16),
        scratch_shapes=[pltpu.VMEM((1, rows, LANES), F32), pltpu.VMEM((1, rows, BRANCH_W), F32)],
        compiler_params=_cparams("parallel"),
        name="stick_attn_sample",
    )(p4, p4, p4, k_cache_t, v_cache_t)


def _diff_combine(o0, o1, lam_init, lam_ref, gain_ref):
    lp = lam_ref[...]
    lam = (jnp.exp(jnp.sum(lp[0:1] * lp[1:2], axis=1, keepdims=True))
           - jnp.exp(jnp.sum(lp[2:3] * lp[3:4], axis=1, keepdims=True)) + lam_init)
    return _rms(o0 - lam * o1, gain_ref[...]) * (1.0 - lam_init)


def _diff_fold(qs, k_t, v, bias, m_ref, l_ref, acc_ref, first):
    kb = k_t.astype(BF16)
    vb = v.astype(BF16)
    rows, tk = qs.shape[0], vb.shape[0]
    s = _dot(qs, kb) + bias
    m_cur = jnp.max(s, axis=1, keepdims=True)
    m_new = jnp.broadcast_to(m_cur, (rows, LANES)) if first else jnp.maximum(m_ref[...], m_cur)
    p = jnp.exp2(s - _lane_tile(m_new, tk // LANES))
    row = jnp.sum(p, axis=1, keepdims=True)
    pv = _dot(p.astype(BF16), vb)
    if first:
        l_ref[...] = jnp.broadcast_to(row, (rows, LANES))
        acc_ref[...] = pv
    else:
        alpha = jnp.exp2(m_ref[...] - m_new)
        l_ref[...] = alpha * l_ref[...] + row
        acc_ref[...] = alpha * acc_ref[...] + pv
    m_ref[...] = m_new


def _diff_prompt_kernel(lam_init, tbl_ref, lam_ref, gain_ref, q_ref, k_ref, v_ref, bd_ref, bp_ref, o_ref,
                        m_ref, l_ref, acc_ref):
    h = pl.program_id(1)
    blk = ATT_BLK
    far = tbl_ref[T5_FAR_BUCKET * C_HEADS + h] * LOG2E
    wide = DIFF_FAR_GROUP * blk

    def one_block(sub, carry):
        qi = pl.program_id(2) * DIFF_SUB + sub
        rows = pl.ds(pl.multiple_of(sub * blk, blk), blk)
        qm = _stack_heads(q_ref[rows, :] * SOFTMAX_QSCALE, HEAD_DIM)

        own = pl.multiple_of(qi * blk, blk)
        near = pl.multiple_of(jnp.maximum(qi - 1, 0) * blk, blk)
        k0 = jnp.concatenate([k_ref[:, pl.ds(near, blk)], k_ref[:, pl.ds(own, blk)]], axis=1)
        v0 = jnp.concatenate([v_ref[pl.ds(near, blk), :], v_ref[pl.ds(own, blk), :]], axis=0)
        b0 = jnp.concatenate([jnp.where(qi >= 1, bp_ref[...], NEG_INF), bd_ref[...]], axis=1)
        b0 = jnp.concatenate([b0, b0], axis=0)
        _diff_fold(qm, k0, v0, b0, m_ref, l_ref, acc_ref, True)

        n_far = jnp.maximum(qi - 1, 0)
        n_group = n_far // DIFF_FAR_GROUP

        def body(j, c):
            start = pl.multiple_of(j * wide, wide)
            _diff_fold(qm, k_ref[:, pl.ds(start, wide)], v_ref[pl.ds(start, wide), :], far,
                       m_ref, l_ref, acc_ref, False)
            return c

        lax.fori_loop(0, n_group, body, 0)
        rest = pl.multiple_of(n_group * wide, wide)
        for r in range(1, DIFF_FAR_GROUP):
            @pl.when(n_far - n_group * DIFF_FAR_GROUP == r)
            def _(r=r):
                _diff_fold(qm, k_ref[:, pl.ds(rest, r * blk)], v_ref[pl.ds(rest, r * blk), :], far,
                           m_ref, l_ref, acc_ref, False)

        o = acc_ref[...] / l_ref[...]
        o_ref[rows, :] = _diff_combine(o[:blk], o[blk:], lam_init, lam_ref, gain_ref).astype(o_ref.dtype)
        return carry

    lax.fori_loop(0, DIFF_SUB, one_block, 0)


def _diff_sample_kernel(lam_init, lam_ref, gain_ref, q_ref, kc_ref, vc_ref, kn_ref, vn_ref,
                        bc_ref, bn_ref, o_ref):
    q_all = q_ref[...] * SOFTMAX_QSCALE
    past = kc_ref.shape[1]
    k_t, v_t = (True, False), (False, False)
    for h in range(C_HEADS):
        cols = slice(h * LANES, (h + 1) * LANES)
        q = q_all[:, cols]
        kb = [kc_ref[cols, :].astype(BF16), kn_ref[:, cols].astype(BF16)]
        vb = [vc_ref[pl.ds(h, past, stride=C_HEADS), :].astype(BF16), vn_ref[:, cols].astype(BF16)]
        biases = [jnp.concatenate([b[h], b[h]], axis=0) for b in (bc_ref, bn_ref)]
        t = q.shape[0]
        o = _softmax_segments(_stack_heads(q, HEAD_DIM), kb, vb, biases, k_t, v_t)
        o_ref[:, cols] = _diff_combine(o[:t], o[t:], lam_init, lam_ref, gain_ref).astype(o_ref.dtype)


def _diff_attn_prompt(pn4, layer, k_t, t5tiles, lam_init, t5_flat, lam_params, gain):
    _, bn, s, _ = pn4.shape
    blk = (None, DIFF_SUB * ATT_BLK, LANES)
    full = (None, s, LANES)
    state = pltpu.VMEM((2 * ATT_BLK, LANES), F32)
    return pl.pallas_call(
        functools.partial(_diff_prompt_kernel, lam_init),
        grid=(bn, C_HEADS, s // (DIFF_SUB * ATT_BLK)),
        in_specs=[pl.BlockSpec(memory_space=pltpu.SMEM),
                  pl.BlockSpec((4, HEAD_DIM), lambda b, h, g: (0, 0)),
                  pl.BlockSpec((1, C_VDIM), lambda b, h, g: (0, 0)),
                  _pspec(blk, lambda b, h, g: (b, g, _CBP["qc"] + h), layer),
                  pl.BlockSpec((None, None, LANES, s), lambda b, h, g: (layer, b, h, 0)),
                  _pspec(full, lambda b, h, g: (b, 0, _CBP["vc"] + h), layer),
                  pl.BlockSpec((None, None, ATT_BLK, ATT_BLK), lambda b, h, g: (h, 0, 0, 0)),
                  pl.BlockSpec((None, None, ATT_BLK, ATT_BLK), lambda b, h, g: (h, 1, 0, 0))],
        out_specs=pl.BlockSpec(blk, lambda b, h, g: (b, g, h)),
        out_shape=jax.ShapeDtypeStruct((bn, s, BRANCH_W), BF16),
        scratch_shapes=[state, state, state],
        compiler_params=_cparams("parallel", "parallel", "arbitrary"),
        name="diff_attn_prompt",
    )(t5_flat, lam_params, gain.reshape(1, C_VDIM), pn4, k_t, pn4, t5tiles, t5tiles)


def _diff_attn_sample(p4, layer, k_cache_t, v_cache, bias_cache, bias_new, lam_init, lam_params, gain):
    _, bn, t, _ = p4.shape
    past = k_cache_t.shape[-1]
    new = lambda name: _pspec((None, t, BRANCH_W), lambda b: (b, 0, _CB[name] * LANES // BRANCH_W), layer)
    return pl.pallas_call(
        functools.partial(_diff_sample_kernel, lam_init),
        grid=(bn,),
        in_specs=[pl.BlockSpec((4, HEAD_DIM), lambda b: (0, 0)),
                  pl.BlockSpec((1, C_VDIM), lambda b: (0, 0)),
                  new("qc"),
                  pl.BlockSpec((None, None, BRANCH_W, past), lambda b: (layer, b, 0, 0)),
                  pl.BlockSpec((None, None, past * C_HEADS, C_VDIM), lambda b: (layer, b, 0, 0)),
                  new("kc"), new("vc"),
                  pl.BlockSpec((C_HEADS, t, past), lambda b: (0, 0, 0)),
                  pl.BlockSpec((C_HEADS, t, t), lambda b: (0, 0, 0))],
        out_specs=pl.BlockSpec((None, t, BRANCH_W), lambda b: (b, 0, 0)),
        out_shape=jax.ShapeDtypeStruct((bn, t, BRANCH_W), BF16),
        compiler_params=_cparams("parallel"),
        name="diff_attn_sample",
    )(lam_params, gain.reshape(1, C_VDIM), p4, k_cache_t, v_cache, p4, p4, bias_cache, bias_new)


def _merge_kernel(x_ref, oa_ref, ob_ref, oc_ref, gm_ref, wg_ref, bg_ref, wb_ref, wo_ref, o_ref):
    x = x_ref[...]
    d = x.shape[1]
    xn = _rms(x, gm_ref[...]).astype(BF16)
    h = None
    for n, o_r in enumerate((oa_ref, ob_ref, oc_ref)):
        logits = _dot(xn, wg_ref[:, n * d:(n + 1) * d]) + bg_ref[n:n + 1, :]
        t = _dot(o_r[...], wb_ref[n]) / (1.0 + jnp.exp(-logits))
        h = t if h is None else h + t
    o_ref[...] = x + _dot(h.astype(BF16), wo_ref[...])


def _merge(x, o_a, o_b, o_c, g_mix, w_gate, b_gate, w_branch, w_out, tm):
    t, d = x.shape
    row = lambda i: (i, 0)
    fixed2 = lambda i: (0, 0)
    return pl.pallas_call(
        _merge_kernel,
        grid=(t // tm,),
        in_specs=[pl.BlockSpec((tm, d), row),
                  pl.BlockSpec((tm, BRANCH_W), row),
                  pl.BlockSpec((tm, BRANCH_W), row),
                  pl.BlockSpec((tm, BRANCH_W), row),
                  pl.BlockSpec((1, d), fixed2),
                  pl.BlockSpec((d, N_BRANCH * d), fixed2),
                  pl.BlockSpec((N_BRANCH, d), fixed2),
                  pl.BlockSpec((N_BRANCH, BRANCH_W, d), lambda i: (0, 0, 0)),
                  pl.BlockSpec((d, d), fixed2)],
        out_specs=pl.BlockSpec((tm, d), row),
        out_shape=jax.ShapeDtypeStruct((t, d), F32),
        compiler_params=_cparams("parallel"),
        name="gated_merge",
    )(x, o_a, o_b, o_c, g_mix.reshape(1, d), w_gate, b_gate, w_branch, w_out)


FFN_TF = D_FF // 2
FFN_PREV = 16


def _gelu_tanh(x):
    return x * (0.5 * (1.0 + jnp.tanh(math.sqrt(2.0 / math.pi) * (x + 0.044715 * (x * x * x)))))


def _ffn_kernel(tiles_per_seq, final, x_ref, xp_ref, gn_ref, wg_ref, wu_ref, cw_ref, cb_ref, wd_ref, st_ref, gf_ref,
                o_ref, so_ref, xn_ref, xpn_ref):
    i = pl.program_id(0)
    j = pl.program_id(1)
    whole_seqs = tiles_per_seq == 0

    @pl.when(j == 0)
    def _():
        xn_ref[...] = _rms(x_ref[...], gn_ref[...]).astype(BF16)
        if not whole_seqs:
            xpn_ref[...] = _rms(xp_ref[...], gn_ref[...]).astype(BF16)

    xn = xn_ref[...]
    tm = xn.shape[0]
    tf = wg_ref.shape[1]
    g = _dot(xn, wg_ref[...])
    u = _dot(xn, wu_ref[...])
    if whole_seqs:
        ns = st_ref.shape[0]
        st = st_ref[...]
        g = g.reshape(ns, tm // ns, tf)
        u = u.reshape(ns, tm // ns, tf)
        pm2, pm1 = st[:, 0:1, :], st[:, 1:2, :]
    else:
        gp = _dot(xpn_ref[...], wg_ref[...])
        first = (i % tiles_per_seq) == 0
        st = st_ref[0]
        pm2 = jnp.where(first, st[0:1, :], gp[FFN_PREV - 2:FFN_PREV - 1, :])
        pm1 = jnp.where(first, st[1:2, :], gp[FFN_PREV - 1:FFN_PREV, :])
    ax = g.ndim - 2
    row = lax.broadcasted_iota(jnp.int32, g.shape, ax)
    g1 = jnp.where(row == 0, pm1, pltpu.roll(g, 1, ax))
    g2 = jnp.where(row == 0, pm2, jnp.where(row == 1, pm1, pltpu.roll(g, 2, ax)))
    cw = cw_ref[...]
    c = cb_ref[...] + cw[0:1, :] * g2 + cw[1:2, :] * g1 + cw[2:3, :] * g
    hid = (_gelu_tanh(c) * u).reshape(tm, tf).astype(BF16)
    contrib = _dot(hid, wd_ref[...])

    last = pl.num_programs(1) - 1
    finish = (lambda y: _rms(y, gf_ref[...])) if final else (lambda y: y)

    @pl.when(j == 0)
    def _():
        o_ref[...] = x_ref[...] + contrib

    @pl.when(jnp.logical_and(j > 0, j < last))
    def _():
        o_ref[...] += contrib

    @pl.when(jnp.logical_and(j > 0, j == last))
    def _():
        o_ref[...] = finish(o_ref[...] + contrib)

    seq_len = g.shape[ax]
    if whole_seqs:
        so_ref[...] = g[:, seq_len - 2:seq_len, :]
    else:
        so_ref[0] = g[seq_len - 2:seq_len, :]


def _conv_ffn(x, g_norm, w_gate, w_up, conv_w, conv_b, w_down, state, seq_len, tm, g_final, final):
    t, d = x.shape
    nseq = t // seq_len
    nf = D_FF // FFN_TF
    assert nf > 1
    if seq_len >= tm:
        tiles_per_seq = seq_len // tm
        st_spec = pl.BlockSpec((1, 2, FFN_TF), lambda i, j: (i // tiles_per_seq, 0, j))
    else:
        tiles_per_seq = 0
        st_spec = pl.BlockSpec((tm // seq_len, 2, FFN_TF), lambda i, j: (i, 0, j))
    prev_blocks = tm // FFN_PREV
    tail_spec = pl.BlockSpec(st_spec.block_shape, lambda i, j: (i, 0, j))
    n_tail = t // tm * st_spec.block_shape[0]
    y, tails = pl.pallas_call(
        functools.partial(_ffn_kernel, tiles_per_seq, final),
        grid=(t // tm, nf),
        in_specs=[pl.BlockSpec((tm, d), lambda i, j: (i, 0)),
                  pl.BlockSpec((FFN_PREV, d), lambda i, j: (jnp.maximum(i * prev_blocks - 1, 0), 0)),
                  pl.BlockSpec((1, d), lambda i, j: (0, 0)),
                  pl.BlockSpec((d, FFN_TF), lambda i, j: (0, j)),
                  pl.BlockSpec((d, FFN_TF), lambda i, j: (0, j)),
                  pl.BlockSpec((3, FFN_TF), lambda i, j: (0, j)),
                  pl.BlockSpec((1, FFN_TF), lambda i, j: (0, j)),
                  pl.BlockSpec((FFN_TF, d), lambda i, j: (j, 0)),
                  st_spec,
                  pl.BlockSpec((1, d), lambda i, j: (0, 0))],
        out_specs=[pl.BlockSpec((tm, d), lambda i, j: (i, 0)), tail_spec],
        out_shape=[jax.ShapeDtypeStruct((t, d), F32),
                   jax.ShapeDtypeStruct((n_tail, 2, D_FF), F32)],
        scratch_shapes=[pltpu.VMEM((tm, d), BF16), pltpu.VMEM((FFN_PREV, d), BF16)],
        compiler_params=_cparams("arbitrary", "arbitrary"),
        name="conv_ffn_prompt" if tiles_per_seq else "conv_ffn_sample",
    )(x, x, g_norm.reshape(1, d), w_gate, w_up, conv_w, conv_b.reshape(1, D_FF), w_down, state, g_final.reshape(1, d))
    return y, tails[n_tail // nseq - 1::n_tail // nseq]


PROJ_TM = 1024
PROMPT_TN, SAMPLE_TN = ROW_COLS // 2, IN_COLS // 3
ROW_TM = 512


def _lam_init(layer):
    return 0.8 - 0.6 * math.exp(-0.3 * layer)


def _cols(p4, name):
    c0 = _CB[name] * LANES
    return p4[..., c0:c0 + BRANCH_W]


def _layer_tail(x2, o_a, o_b, o_c, layer, prm, conv_state, seq_len):
    flat = lambda a: a.reshape(x2.shape[0], BRANCH_W)
    x2 = _merge(x2, flat(o_a), flat(o_b), flat(o_c), prm["norm_mix"][layer], prm["w_gate_mix"][layer],
                prm["b_gate"][layer], prm["w_branch"][layer], prm["w_out"][layer], ROW_TM)
    depth = prm["norm_ffn"].shape[0]
    return _conv_ffn(x2, prm["norm_ffn"][layer], prm["w_gate"][layer], prm["w_up"][layer],
                     prm["conv_w"][layer], prm["conv_b"][layer], prm["w_down"][layer], conv_state, seq_len, ROW_TM,
                     prm["norm_final"], layer == depth - 1)


def _trunk_prompt(x, prm, abias, t5tiles, t5_flat):
    bn, s, d = x.shape
    x2 = x.reshape(bn * s, d)
    depth = prm["w_rows"].shape[0]
    conv_states = []
    rows, kv_t = None, None
    for l in range(depth):
        rows = _norm_proj(x2, prm["norm_mix"][l], prm["w_rows"][l], PROJ_TM, PROMPT_TN, l, depth, rows,
                          heads_group=(_CBP["vc"] * LANES, C_HEADS))
        p_rows, vc_heads = rows
        kv_t = _norm_proj_t(x2, prm["norm_mix"][l], prm["w_feat_t"][l], PROJ_TM, l, depth, bn, kv_t)
        ka_t, va_t, kb_t, vb_t, kc_t = kv_t
        pn4 = p_rows.reshape(depth, bn, s, ROW_COLS)
        o_a = _band_attn_prompt(pn4, l, ka_t, va_t, abias[l])
        o_b = _sb_attn_prompt(pn4, l, kb_t, vb_t)
        o_c = _diff_attn_prompt(pn4, l, kc_t, t5tiles, _lam_init(l), t5_flat, prm["c_lambda"][l], prm["c_subln"][l])
        x2, st_conv = _layer_tail(x2, o_a, o_b, o_c, l, prm, jnp.zeros((bn, 2, D_FF), F32), s)
        conv_states.append(st_conv)
    y = x2.reshape(bn, s, d)
    heads_t = lambda a: jnp.transpose(a.reshape(depth, bn, A_HEADS, HEAD_DIM, a.shape[-1]), (0, 1, 4, 2, 3))
    keep = min(A_PAST_CHUNKS * CHUNK, s)
    states = (heads_t(ka_t[..., s - keep:]), heads_t(va_t[..., s - keep:]), heads_t(kb_t), heads_t(vb_t),
              jnp.transpose(kc_t.reshape(depth, bn, C_HEADS, 2, HEAD_DIM, s), (0, 1, 5, 2, 3, 4)),
              vc_heads.reshape(depth, bn, s, C_HEADS, C_VDIM),
              jnp.stack(conv_states, axis=0))
    return y, states


def _feature_major(cache):
    nd = cache.ndim
    t = jnp.transpose(cache, (0, 1) + tuple(range(3, nd)) + (2,))
    return t.reshape(cache.shape[0], cache.shape[1], BRANCH_W, cache.shape[2])


def _trunk_sample(x, caches, prm, abias, t5tiles):
    bn, t, d = x.shape
    x2 = x.reshape(bn * t, d)
    depth = prm["w_in"].shape[0]
    ca_k, ca_v, cb_k, cb_v, cc_k, cc_v, st_conv_in = caches
    ca_kt, ca_vt, cb_kt, cb_vt, cc_kt = (_feature_major(c) for c in (ca_k, ca_v, cb_k, cb_v, cc_k))
    keep, past = ca_k.shape[2], cc_k.shape[2]
    cc_vr = cc_v.reshape(depth, bn, past * C_HEADS, C_VDIM)
    conv_states = []
    p_all = None
    for l in range(depth):
        p_all, = _norm_proj(x2, prm["norm_mix"][l], prm["w_in"][l], PROJ_TM, SAMPLE_TN, l, depth,
                            None if p_all is None else [p_all])
        p4 = p_all.reshape(depth, bn, t, IN_COLS)
        bias_cache = jnp.concatenate([abias[l][:, 0, :t, :], abias[l][:, 1, :t, :]], axis=-1)[..., -keep:]
        o_a = _band_attn_sample(p4, l, ca_kt, ca_vt, bias_cache, abias[l][:, 2, :t, :t])
        o_b = _sb_attn_sample(p4, l, cb_kt, cb_vt)
        far = jnp.broadcast_to(t5tiles[:, 1, :1, :1], (C_HEADS, t, past - ATT_BLK))
        bias_c = jnp.concatenate([far, t5tiles[:, 1, :t, :]], axis=-1)
        o_c = _diff_attn_sample(p4, l, cc_kt, cc_vr, bias_c, t5tiles[:, 0, :t, :t],
                                _lam_init(l), prm["c_lambda"][l], prm["c_subln"][l])
        x2, st_conv = _layer_tail(x2, o_a, o_b, o_c, l, prm, st_conv_in[l], t)
        conv_states.append(st_conv)
    y = x2.reshape(bn, t, d)
    kb, vb, kc, vc = (_cols(p4, n) for n in ("kb", "vb", "kc", "vc"))
    heads = lambda a: a.reshape(a.shape[:3] + (A_HEADS, HEAD_DIM))
    heads_t = lambda a: jnp.transpose(a.reshape(depth, bn, A_HEADS, HEAD_DIM, a.shape[-1]), (0, 1, 4, 2, 3))
    sa_kt, sa_vt = _band_roll(ca_kt, ca_vt, p4)
    states = (heads_t(sa_kt), heads_t(sa_vt), heads(kb), heads(vb),
              kc.reshape(depth, bn, t, C_HEADS, 2, HEAD_DIM), vc.reshape(depth, bn, t, C_HEADS, C_VDIM),
              jnp.stack(conv_states, axis=0))
    return y, states


def kernel(x_prompt, x_sample, cache_a_k, cache_a_v, cache_b_k, cache_b_v, cache_c_k, cache_c_v,
           state_ffn_conv, norm_mix, w_in, b_gate, a_rel_bias, t5_bias, c_lambda, c_subln,
           w_branch, w_out, norm_ffn, w_up, conv_w, conv_b, w_down, norm_final):
    group = {name: w_in[..., i * BRANCH_W:(i + 1) * BRANCH_W] for i, name in enumerate(_GROUPS)}
    gate = w_in[..., len(_GROUPS) * BRANCH_W:]
    prm = dict(
        norm_mix=norm_mix,
        w_in=w_in[..., :len(_GROUPS) * BRANCH_W].astype(BF16),
        w_rows=jnp.concatenate([group[n] for n in _ROW_GROUPS], axis=-1).astype(BF16),
        w_gate_mix=gate.astype(BF16),
        w_feat_t=jnp.swapaxes(jnp.concatenate([group[n] for n in _T_GROUPS], axis=-1), 1, 2).astype(BF16),
        b_gate=b_gate, c_lambda=c_lambda, c_subln=c_subln,
        w_branch=w_branch.astype(BF16), w_out=w_out.astype(BF16), norm_ffn=norm_ffn,
        w_gate=w_up[..., :D_FF].astype(BF16), w_up=w_up[..., D_FF:].astype(BF16),
        conv_w=conv_w, conv_b=conv_b, w_down=w_down.astype(BF16), norm_final=norm_final)
    abias = [_build_abias(a_rel_bias[l]) for l in range(a_rel_bias.shape[0])]
    t5tiles = _build_t5bias(t5_bias)
    y_p, p_states = _trunk_prompt(x_prompt, prm, abias, t5tiles, t5_bias.reshape(-1))
    caches = (cache_a_k, cache_a_v, cache_b_k, cache_b_v, cache_c_k, cache_c_v, state_ffn_conv)
    y_s, s_states = _trunk_sample(x_sample, caches, prm, abias, t5tiles)
    return (y_p, y_s) + p_states + s_states
```

```python
import functools
import math

import jax
import jax.numpy as jnp
from jax import lax
from jax.experimental import pallas as pl
from jax.experimental.pallas import tpu as pltpu

F32 = jnp.float32
BF16 = jnp.bfloat16

D_MODEL = 1024
CHUNK = 64
HEAD_DIM = 64
A_HEADS = 8
A_PAST_CHUNKS = 8
A_REL_CLIP = 128
B_HEADS = 8
C_HEADS = 4
C_VDIM = 128
T5_BUCKETS = 32
D_FF = 2816
N_BRANCH = 3
BRANCH_W = 512
IN_COLS = 9 * BRANCH_W
EPS = 1e-6
NEG_INF = -1e30
SCALE = HEAD_DIM ** -0.5
LOG2E = math.log2(math.e)
SOFTMAX_QSCALE = SCALE * LOG2E

LANES = 128
_GROUPS = ("qa", "ka", "va", "qb", "kb", "vb", "qc", "kc", "vc")
_CB = {name: i * BRANCH_W // LANES for i, name in enumerate(_GROUPS)}
_ROW_GROUPS = ("qa", "qb", "qc", "vc")
_T_GROUPS = ("ka", "va", "kb", "vb", "kc")
_CBP = {name: i * BRANCH_W // LANES for i, name in enumerate(_ROW_GROUPS)}
ROW_COLS = len(_ROW_GROUPS) * BRANCH_W

ATT_BLK = 256
ATT_HEADS = 4
SB_LOG_CUT = -104.0
T5_FAR_BUCKET = 15
BAND_SUB = 16
DIFF_SUB = 16
DIFF_FAR_GROUP = 4
V7X_VMEM_BYTES = 64 * 1024 * 1024
VMEM_LIMIT = V7X_VMEM_BYTES - 8 * 1024 * 1024


def _cparams(*sem):
    return pltpu.CompilerParams(dimension_semantics=sem, vmem_limit_bytes=VMEM_LIMIT)


def _rms(x, g):
    return x * lax.rsqrt(jnp.mean(x * x, axis=-1, keepdims=True) + EPS) * g


def _dot(a, b):
    return jnp.dot(a, b, preferred_element_type=F32)


def _dot_nt(a, b):
    return lax.dot_general(a, b, (((1,), (1,)), ((), ())), preferred_element_type=F32)


def _norm_proj_kernel(n_stack, heads_cols, x_ref, g_ref, w_ref, *refs):
    outs, xn_ref = refs[n_stack:-1], refs[-1]
    j = pl.program_id(1)

    @pl.when(j == 0)
    def _():
        xn_ref[...] = _rms(x_ref[...], g_ref[...]).astype(BF16)

    res = _dot(xn_ref[...], w_ref[...])
    outs[0][...] = res
    if heads_cols is not None:
        tile, first, heads = heads_cols

        @pl.when(j == tile)
        def _():
            for h in range(heads):
                lo = first + h * LANES
                outs[1][pl.ds(h, res.shape[0], stride=heads), :] = res[:, lo:lo + LANES]


def _norm_proj(x, g, w, tm, tn, layer, depth, stacked, heads_group=None):
    t, d = x.shape
    n = w.shape[1]
    in_specs = [pl.BlockSpec((tm, d), lambda i, j: (i, 0)),
                pl.BlockSpec((1, d), lambda i, j: (0, 0)),
                pl.BlockSpec((d, tn), lambda i, j: (0, j))]
    args = [x, g.reshape(1, d), w]
    out_specs = [pl.BlockSpec((None, tm, tn), lambda i, j: (layer, i, j))]
    out_shape = [jax.ShapeDtypeStruct((depth, t, n), F32)]
    heads_cols = None
    if heads_group is not None:
        first, heads = heads_group
        heads_cols = (first // tn, first % tn, heads)
        out_specs.append(pl.BlockSpec((None, tm * heads, LANES), lambda i, j: (layer, i, 0)))
        out_shape.append(jax.ShapeDtypeStruct((depth, t * heads, LANES), F32))
    n_stack = 0 if stacked is None else len(out_shape)
    if stacked is not None:
        in_specs += [pl.BlockSpec(memory_space=pl.ANY)] * n_stack
        args += list(stacked)
    return pl.pallas_call(
        functools.partial(_norm_proj_kernel, n_stack, heads_cols),
        grid=(t // tm, n // tn),
        in_specs=in_specs,
        out_specs=out_specs,
        out_shape=out_shape,
        scratch_shapes=[pltpu.VMEM((tm, d), BF16)],
        input_output_aliases={3 + k: k for k in range(n_stack)},
        compiler_params=_cparams("arbitrary", "arbitrary"),
        name="norm_proj",
    )(*args)


def _norm_proj_t_kernel(ngroups, x_ref, g_ref, wt_ref, *refs):
    o_refs = refs[-ngroups:]
    xn = _rms(x_ref[...], g_ref[...]).astype(BF16)
    for g in range(ngroups):
        o_refs[g][...] = _dot_nt(wt_ref[g * BRANCH_W:(g + 1) * BRANCH_W, :], xn)


def _norm_proj_t(x, g, wt, tm, layer, depth, bn, stacked):
    t, d = x.shape
    s = t // bn
    ngroups = wt.shape[0] // BRANCH_W
    tiles = s // tm
    in_specs = [pl.BlockSpec((tm, d), lambda i: (i, 0)),
                pl.BlockSpec((1, d), lambda i: (0, 0)),
                pl.BlockSpec(wt.shape, lambda i: (0, 0))]
    args = [x, g.reshape(1, d), wt]
    aliases = {}
    if stacked is not None:
        in_specs += [pl.BlockSpec(memory_space=pl.ANY)] * ngroups
        args += list(stacked)
        aliases = {3 + n: n for n in range(ngroups)}
    out_spec = pl.BlockSpec((None, None, BRANCH_W, tm), lambda i: (layer, i // tiles, 0, i % tiles))
    return pl.pallas_call(
        functools.partial(_norm_proj_t_kernel, ngroups),
        grid=(t // tm,),
        in_specs=in_specs,
        out_specs=[out_spec] * ngroups,
        out_shape=[jax.ShapeDtypeStruct((depth, bn, BRANCH_W, s), F32)] * ngroups,
        input_output_aliases=aliases,
        compiler_params=_cparams("arbitrary"),
        name="norm_proj_t",
    )(*args)


def _abias_kernel(tbl_ref, o_ref):
    h = pl.program_id(0)
    sub = 8
    ql = lax.broadcasted_iota(jnp.int32, (ATT_BLK, ATT_BLK), 0)
    kl = lax.broadcasted_iota(jnp.int32, (ATT_BLK, ATT_BLK), 1)
    far = jnp.full((ATT_BLK, ATT_BLK), tbl_ref[2 * A_REL_CLIP * A_HEADS + h] * LOG2E, F32)
    o_ref[0, 0] = jnp.where((kl >> 6) >= (ql >> 6), far, NEG_INF)
    lead = ATT_BLK - sub
    b = lax.broadcasted_iota(jnp.int32, (sub, 2 * ATT_BLK), 0)
    x = lax.broadcasted_iota(jnp.int32, (sub, 2 * ATT_BLK), 1)
    for seg, off in ((1, ATT_BLK), (2, 0)):
        idx = jnp.clip(b + lead - x + off, -A_REL_CLIP, A_REL_CLIP) + A_REL_CLIP

        def body(r, acc, idx=idx):
            return jnp.where(idx == r, tbl_ref[r * A_HEADS + h] * LOG2E, acc)

        strip = lax.fori_loop(0, 2 * A_REL_CLIP + 1, body, jnp.zeros((sub, 2 * ATT_BLK), F32))
        for a in range(ATT_BLK // sub):
            o_ref[0, seg, a * sub:(a + 1) * sub, :] = strip[:, lead - a * sub:lead - a * sub + ATT_BLK]
    o_ref[0, 2] = jnp.where((kl >> 6) <= (ql >> 6), o_ref[0, 2], NEG_INF)


def _build_abias(table):
    return pl.pallas_call(
        _abias_kernel,
        grid=(A_HEADS,),
        in_specs=[pl.BlockSpec(memory_space=pltpu.SMEM)],
        out_specs=pl.BlockSpec((1, 3, ATT_BLK, ATT_BLK), lambda h: (h, 0, 0, 0)),
        out_shape=jax.ShapeDtypeStruct((A_HEADS, 3, ATT_BLK, ATT_BLK), F32),
        compiler_params=_cparams("parallel"),
        name="band_bias_tiles",
    )(table.reshape(-1))


_T5_LARGE_STEPS = (12, 16, 23, 32, 46, 64, 91)


def _t5bias_kernel(tbl_ref, o_ref):
    h = pl.program_id(0)
    rows = 64
    for seg, off in ((0, 0), (1, -ATT_BLK)):
        for rc in range(ATT_BLK // rows):
            ql = lax.broadcasted_iota(jnp.int32, (rows, ATT_BLK), 0) + rc * rows
            kl = lax.broadcasted_iota(jnp.int32, (rows, ATT_BLK), 1)
            rel = kl - ql + off
            n = jnp.abs(rel)
            large = jnp.full((rows, ATT_BLK), T5_BUCKETS // 4, jnp.int32)
            for th in _T5_LARGE_STEPS:
                large = large + jnp.where(n >= th, 1, 0)
            bucket = jnp.where(rel > 0, T5_BUCKETS // 2, 0) + jnp.where(n < T5_BUCKETS // 4, n, large)
            tile = jnp.zeros((rows, ATT_BLK), F32)
            for r in range(T5_BUCKETS):
                tile = jnp.where(bucket == r, tbl_ref[r * C_HEADS + h] * LOG2E, tile)
            if seg == 0:
                tile = jnp.where((kl >> 6) <= (ql >> 6), tile, NEG_INF)
            o_ref[0, seg, rc * rows:(rc + 1) * rows, :] = tile


def _build_t5bias(table):
    return pl.pallas_call(
        _t5bias_kernel,
        grid=(C_HEADS,),
        in_specs=[pl.BlockSpec(memory_space=pltpu.SMEM)],
        out_specs=pl.BlockSpec((1, 2, ATT_BLK, ATT_BLK), lambda h: (h, 0, 0, 0)),
        out_shape=jax.ShapeDtypeStruct((C_HEADS, 2, ATT_BLK, ATT_BLK), F32),
        compiler_params=_cparams("parallel"),
        name="t5_bias_tiles",
    )(table.reshape(-1))


def _pspec(block, index_map, layer):
    return pl.BlockSpec((None,) + block, lambda *g: (layer,) + tuple(index_map(*g)))


def _lane_tile(x, n):
    return x if n == 1 else jnp.concatenate([x] * n, axis=1)


def _stack_heads(q, width):
    group = lax.broadcasted_iota(jnp.int32, q.shape, 1) // width
    return jnp.concatenate([jnp.where(group == h, q, 0.0) for h in range(q.shape[1] // width)],
                           axis=0).astype(BF16)


def _unstack_heads(x, t, width):
    group = lax.broadcasted_iota(jnp.int32, (t, x.shape[1]), 1) // width
    out = jnp.zeros((t, x.shape[1]), x.dtype)
    for h in range(x.shape[1] // width):
        out = jnp.where(group == h, x[h * t:(h + 1) * t, :], out)
    return out


def _qk(q, k, k_t):
    return _dot(q, k) if k_t else _dot_nt(q, k)


def _pv(p, v, v_t):
    return _dot_nt(p, v) if v_t else _dot(p, v)


def _softmax_segments(qh, kb, vb, biases, k_t, v_t):
    s = [_qk(qh, k, t) + b for k, b, t in zip(kb, biases, k_t)]
    m = functools.reduce(jnp.maximum, [jnp.max(sj, axis=1, keepdims=True) for sj in s])
    p = [jnp.exp2(sj - m) for sj in s]
    l = functools.reduce(jnp.add, [jnp.sum(pj, axis=1, keepdims=True) for pj in p])
    acc = functools.reduce(jnp.add, [_pv(pj.astype(BF16), v, t) for pj, v, t in zip(p, vb, v_t)])
    return acc / l


def _band_kernel(q_ref, k_ref, v_ref, b_ref, o_ref):
    blk = ATT_BLK
    heads = b_ref.shape[0]
    n_sub = q_ref.shape[0] // blk

    def one_block(sub, carry):
        qi = pl.program_id(2) * n_sub + sub
        rows = pl.ds(pl.multiple_of(sub * blk, blk), blk)
        qs = _stack_heads(q_ref[rows, :] * SOFTMAX_QSCALE, HEAD_DIM)
        kb, vb, biases = [], [], []
        for seg, back in enumerate((2, 1, 0)):
            keys = pl.ds(pl.multiple_of(jnp.maximum(qi - back, 0) * blk, blk), blk)
            kb.append(k_ref[:, keys].astype(BF16))
            vb.append(v_ref[:, keys].astype(BF16))
            tile = b_ref[:, seg].reshape(heads * blk, blk)
            biases.append(tile if back == 0 else jnp.where(qi >= back, tile, NEG_INF))
        o = _softmax_segments(qs, kb, vb, biases, (True,) * 3, (True,) * 3)
        o_ref[rows, :] = _unstack_heads(o, blk, HEAD_DIM).astype(o_ref.dtype)
        return carry

    lax.fori_loop(0, n_sub, one_block, 0)


def _band_attn_prompt(pn4, layer, k_t, v_t, abias):
    _, bn, s, _ = pn4.shape
    width = ATT_HEADS * HEAD_DIM
    sub = math.gcd(s // ATT_BLK, BAND_SUB)
    blk = (None, sub * ATT_BLK, width)
    seq = pl.BlockSpec((None, None, width, s), lambda b, hg, g: (layer, b, hg, 0))
    return pl.pallas_call(
        _band_kernel,
        grid=(bn, A_HEADS // ATT_HEADS, s // (sub * ATT_BLK)),
        in_specs=[_pspec(blk, lambda b, hg, g: (b, g, _CBP["qa"] * LANES // width + hg), layer), seq, seq,
                  pl.BlockSpec((ATT_HEADS, 3, ATT_BLK, ATT_BLK), lambda b, hg, g: (hg, 0, 0, 0))],
        out_specs=pl.BlockSpec(blk, lambda b, hg, g: (b, g, hg)),
        out_shape=jax.ShapeDtypeStruct((bn, s, BRANCH_W), BF16),
        compiler_params=_cparams("parallel", "parallel", "arbitrary"),
        name="band_attn_prompt",
    )(pn4, k_t, v_t, abias)


def _band_sample_kernel(q_ref, kc_ref, kn_ref, vc_ref, vn_ref, bc_ref, bn_ref, o_ref):
    t = q_ref.shape[0]
    width = ATT_HEADS * HEAD_DIM
    q = q_ref[...] * SOFTMAX_QSCALE
    for g in range(q.shape[1] // width):
        cols = slice(g * width, (g + 1) * width)
        rows = slice(g * ATT_HEADS * t, (g + 1) * ATT_HEADS * t)
        qs = _stack_heads(q[:, cols], HEAD_DIM)
        kb = [kc_ref[cols, :].astype(BF16), kn_ref[:, cols].astype(BF16)]
        vb = [vc_ref[cols, :].astype(BF16), vn_ref[:, cols].astype(BF16)]
        o = _softmax_segments(qs, kb, vb, [bc_ref[rows, :], bn_ref[rows, :]], (True, False), (True, False))
        o_ref[:, cols] = _unstack_heads(o, t, HEAD_DIM).astype(o_ref.dtype)


def _band_attn_sample(p4, layer, k_cache_t, v_cache_t, bias_cache, bias_new):
    _, bn, t, _ = p4.shape
    keep = k_cache_t.shape[-1]
    new = lambda name: _pspec((None, t, BRANCH_W), lambda b: (b, 0, _CB[name] * LANES // BRANCH_W), layer)
    old = pl.BlockSpec((None, None, BRANCH_W, keep), lambda b: (layer, b, 0, 0))
    return pl.pallas_call(
        _band_sample_kernel,
        grid=(bn,),
        in_specs=[new("qa"), old, new("ka"), old, new("va"),
                  pl.BlockSpec((A_HEADS * t, keep), lambda b: (0, 0)),
                  pl.BlockSpec((A_HEADS * t, t), lambda b: (0, 0))],
        out_specs=pl.BlockSpec((None, t, BRANCH_W), lambda b: (b, 0, 0)),
        out_shape=jax.ShapeDtypeStruct((bn, t, BRANCH_W), BF16),
        compiler_params=_cparams("parallel"),
        name="band_attn_sample",
    )(p4, k_cache_t, p4, v_cache_t, p4, bias_cache.reshape(A_HEADS * t, keep), bias_new.reshape(A_HEADS * t, t))


def _band_roll_kernel(kc_ref, vc_ref, kn_ref, vn_ref, ko_ref, vo_ref):
    t = kn_ref.shape[0]
    keep = kc_ref.shape[1]
    for c_ref, n_ref, o_ref in ((kc_ref, kn_ref, ko_ref), (vc_ref, vn_ref, vo_ref)):
        rolled = pltpu.roll(c_ref[...], keep - t, 1)
        new = jnp.concatenate([n_ref[...], jnp.zeros((LANES - t, n_ref.shape[1]), F32)], axis=0)
        new_t = pltpu.roll(new.T, LANES - t, 1)
        lane = lax.broadcasted_iota(jnp.int32, new_t.shape, 1)
        last = jnp.where(lane >= LANES - t, new_t, rolled[:, keep - LANES:])
        o_ref[...] = jnp.concatenate([rolled[:, :keep - LANES], last], axis=1)


def _band_roll(k_cache_t, v_cache_t, p4):
    depth, bn, t, _ = p4.shape
    keep = k_cache_t.shape[-1]
    assert t <= LANES <= keep
    old = pl.BlockSpec((None, None, BRANCH_W, keep), lambda l, b: (l, b, 0, 0))
    new = lambda name: pl.BlockSpec((None, None, t, BRANCH_W), lambda l, b: (l, b, 0, _CB[name] * LANES // BRANCH_W))
    return pl.pallas_call(
        _band_roll_kernel,
        grid=(depth, bn),
        in_specs=[old, old, new("ka"), new("va")],
        out_specs=[old, old],
        out_shape=[jax.ShapeDtypeStruct(k_cache_t.shape, F32)] * 2,
        compiler_params=_cparams("parallel", "parallel"),
        name="band_roll",
    )(k_cache_t, v_cache_t, p4, p4)


def _strict_upper(n):
    r = lax.broadcasted_iota(jnp.int32, (n, n), 0)
    c = lax.broadcasted_iota(jnp.int32, (n, n), 1)
    return jnp.where(r > c, 1.0, 0.0).astype(BF16)


def _sb_fold(qh, kb, vb, kv_t, upper, mask, carry_ref, acc_ref, first):
    tq = qh[0].shape[0]
    tk = kb[0].shape[1] if kv_t else kb[0].shape[0]
    cw = upper.shape[0]
    top = None
    for h in range(len(qh)):
        z = _qk(qh[h], kb[0], kv_t)
        soft = jnp.log(1.0 + jnp.exp(-jnp.abs(z)))
        log_keep = -(jnp.maximum(z, 0.0) + soft)
        log_take = jnp.minimum(z, 0.0) - soft
        if mask is not None:
            log_keep = jnp.where(mask, log_keep, 0.0)
        hi = log_keep.astype(BF16)
        lo = (log_keep - hi.astype(F32)).astype(BF16)
        pieces, total = [], None
        for c0 in reversed(range(0, tk, cw)):
            piece = _dot(hi[:, c0:c0 + cw], upper) + _dot(lo[:, c0:c0 + cw], upper)
            pieces.insert(0, piece if total is None else piece + total)
            part = jnp.sum(log_keep[:, c0:c0 + cw], axis=1, keepdims=True)
            total = part if total is None else total + part
        after = pieces[0] if len(pieces) == 1 else jnp.concatenate(pieces, axis=1)
        if not first:
            after = after + _lane_tile(carry_ref[h], tk // LANES)
        w = jnp.exp(log_take + after)
        if mask is not None:
            w = jnp.where(mask, w, 0.0)
        pv = _pv(w.astype(BF16), vb[0], kv_t)
        if first:
            acc_ref[h] = pv
            carry = jnp.broadcast_to(total, (tq, LANES))
        else:
            acc_ref[h] += pv
            carry = carry_ref[h] + total
        carry_ref[h] = carry
        top = jnp.max(carry) if top is None else jnp.maximum(top, jnp.max(carry))
    return (top > SB_LOG_CUT).astype(jnp.int32)


def _sb_kernel(tk, n_past_static, own_rows, q_ref, kd_ref, vd_ref, kp_ref, vp_ref, o_ref, carry_ref, acc_ref):
    n_past = pl.program_id(2) if n_past_static is None else n_past_static
    q = q_ref[...] * SCALE
    tq = q.shape[0]
    rq = lax.broadcasted_iota(jnp.int32, (tq, tq), 0)
    ck = lax.broadcasted_iota(jnp.int32, (tq, tq), 1)
    qh = [_stack_heads(q, HEAD_DIM)]
    causal = jnp.concatenate([ck < rq] * (q.shape[1] // HEAD_DIM), axis=0)
    if own_rows:
        kd, vd = [kd_ref[...].astype(BF16)], [vd_ref[...].astype(BF16)]
        live = _sb_fold(qh, kd, vd, False, _strict_upper(tq), causal, carry_ref, acc_ref, True)
        n_left = n_past
    else:
        near = pl.multiple_of(jnp.maximum(n_past - 1, 0) * tq, tq)
        kd = [jnp.concatenate([kp_ref[:, pl.ds(near, tq)], kd_ref[...]], axis=1).astype(BF16)]
        vd = [jnp.concatenate([vp_ref[:, pl.ds(near, tq)], vd_ref[...]], axis=1).astype(BF16)]
        mask = jnp.concatenate([jnp.broadcast_to(n_past >= 1, causal.shape), causal], axis=1)
        live = _sb_fold(qh, kd, vd, True, _strict_upper(tq), mask, carry_ref, acc_ref, True)
        n_left = n_past - 1

    def cond(st):
        return jnp.logical_and(st[0] >= 0, st[1] > 0)

    def body(st):
        start = pl.multiple_of(st[0] * tk, tk)
        kp, vp = [kp_ref[:, pl.ds(start, tk)].astype(BF16)], [vp_ref[:, pl.ds(start, tk)].astype(BF16)]
        live = _sb_fold(qh, kp, vp, True, _strict_upper(tk), None, carry_ref, acc_ref, False)
        return st[0] - 1, live

    lax.while_loop(cond, body, (n_left - 1, live))
    o_ref[...] = _unstack_heads(acc_ref[0], tq, HEAD_DIM).astype(o_ref.dtype)


def _sb_attn_prompt(pn4, layer, k_t, v_t):
    _, bn, s, _ = pn4.shape
    width = ATT_HEADS * HEAD_DIM
    blk = (None, ATT_BLK, width)
    own = pl.BlockSpec((None, None, width, ATT_BLK), lambda b, hg, qi: (layer, b, hg, qi))
    older = pl.BlockSpec((None, None, width, s), lambda b, hg, qi: (layer, b, hg, 0))
    rows = ATT_HEADS * ATT_BLK
    return pl.pallas_call(
        functools.partial(_sb_kernel, ATT_BLK, None, False),
        grid=(bn, B_HEADS // ATT_HEADS, s // ATT_BLK),
        in_specs=[_pspec(blk, lambda b, hg, qi: (b, qi, _CBP["qb"] * LANES // width + hg), layer),
                  own, own, older, older],
        out_specs=pl.BlockSpec(blk, lambda b, hg, qi: (b, qi, hg)),
        out_shape=jax.ShapeDtypeStruct((bn, s, BRANCH_W), BF16),
        scratch_shapes=[pltpu.VMEM((1, rows, LANES), F32), pltpu.VMEM((1, rows, width), F32)],
        compiler_params=_cparams("parallel", "parallel", "arbitrary"),
        name="stick_attn_prompt",
    )(pn4, k_t, v_t, k_t, v_t)


def _sb_attn_sample(p4, layer, k_cache_t, v_cache_t):
    _, bn, t, _ = p4.shape
    past = k_cache_t.shape[-1]
    new = lambda name: _pspec((None, t, BRANCH_W), lambda b: (b, 0, _CB[name] * LANES // BRANCH_W), layer)
    older = pl.BlockSpec((None, None, BRANCH_W, past), lambda b: (layer, b, 0, 0))
    rows = B_HEADS * t
    return pl.pallas_call(
        functools.partial(_sb_kernel, ATT_BLK, past // ATT_BLK, True),
        grid=(bn,),
        in_specs=[new("qb"), new("kb"), new("vb"), older, older],
        out_specs=pl.BlockSpec((None, t, BRANCH_W), lambda b: (b, 0, 0)),
        out_shape=jax.ShapeDtypeStruct((bn, t, BRANCH_W), BF16),
        scratch_shapes=[pltpu.VMEM((1, rows, LANES), F32), pltpu.VMEM((1, rows, BRANCH_W), F32)],
        compiler_params=_cparams("parallel"),
        name="stick_attn_sample",
    )(p4, p4, p4, k_cache_t, v_cache_t)


def _diff_combine(o0, o1, lam_init, lam_ref, gain_ref):
    lp = lam_ref[...]
    lam = (jnp.exp(jnp.sum(lp[0:1] * lp[1:2], axis=1, keepdims=True))
           - jnp.exp(jnp.sum(lp[2:3] * lp[3:4], axis=1, keepdims=True)) + lam_init)
    return _rms(o0 - lam * o1, gain_ref[...]) * (1.0 - lam_init)


def _diff_fold(qs, k_t, v, bias, m_ref, l_ref, acc_ref, first):
    kb = k_t.astype(BF16)
    vb = v.astype(BF16)
    rows, tk = qs.shape[0], vb.shape[0]
    s = _dot(qs, kb) + bias
    m_cur = jnp.max(s, axis=1, keepdims=True)
    m_new = jnp.broadcast_to(m_cur, (rows, LANES)) if first else jnp.maximum(m_ref[...], m_cur)
    p = jnp.exp2(s - _lane_tile(m_new, tk // LANES))
    row = jnp.sum(p, axis=1, keepdims=True)
    pv = _dot(p.astype(BF16), vb)
    if first:
        l_ref[...] = jnp.broadcast_to(row, (rows, LANES))
        acc_ref[...] = pv
    else:
        alpha = jnp.exp2(m_ref[...] - m_new)
        l_ref[...] = alpha * l_ref[...] + row
        acc_ref[...] = alpha * acc_ref[...] + pv
    m_ref[...] = m_new


def _diff_prompt_kernel(lam_init, tbl_ref, lam_ref, gain_ref, q_ref, k_ref, v_ref, bd_ref, bp_ref, o_ref,
                        m_ref, l_ref, acc_ref):
    h = pl.program_id(1)
    blk = ATT_BLK
    n_sub = q_ref.shape[0] // blk
    far = tbl_ref[T5_FAR_BUCKET * C_HEADS + h] * LOG2E
    wide = DIFF_FAR_GROUP * blk

    def one_block(sub, carry):
        qi = pl.program_id(2) * n_sub + sub
        rows = pl.ds(pl.multiple_of(sub * blk, blk), blk)
        qm = _stack_heads(q_ref[rows, :] * SOFTMAX_QSCALE, HEAD_DIM)

        own = pl.multiple_of(qi * blk, blk)
        near = pl.multiple_of(jnp.maximum(qi - 1, 0) * blk, blk)
        k0 = jnp.concatenate([k_ref[:, pl.ds(near, blk)], k_ref[:, pl.ds(own, blk)]], axis=1)
        v0 = jnp.concatenate([v_ref[pl.ds(near, blk), :], v_ref[pl.ds(own, blk), :]], axis=0)
        b0 = jnp.concatenate([jnp.where(qi >= 1, bp_ref[...], NEG_INF), bd_ref[...]], axis=1)
        b0 = jnp.concatenate([b0, b0], axis=0)
        _diff_fold(qm, k0, v0, b0, m_ref, l_ref, acc_ref, True)

        n_far = jnp.maximum(qi - 1, 0)
        n_group = n_far // DIFF_FAR_GROUP

        def body(j, c):
            start = pl.multiple_of(j * wide, wide)
            _diff_fold(qm, k_ref[:, pl.ds(start, wide)], v_ref[pl.ds(start, wide), :], far,
                       m_ref, l_ref, acc_ref, False)
            return c

        lax.fori_loop(0, n_group, body, 0)
        rest = pl.multiple_of(n_group * wide, wide)
        for r in range(1, DIFF_FAR_GROUP):
            @pl.when(n_far - n_group * DIFF_FAR_GROUP == r)
            def _(r=r):
                _diff_fold(qm, k_ref[:, pl.ds(rest, r * blk)], v_ref[pl.ds(rest, r * blk), :], far,
                           m_ref, l_ref, acc_ref, False)

        o = acc_ref[...] / l_ref[...]
        o_ref[rows, :] = _diff_combine(o[:blk], o[blk:], lam_init, lam_ref, gain_ref).astype(o_ref.dtype)
        return carry

    lax.fori_loop(0, n_sub, one_block, 0)


def _diff_sample_kernel(lam_init, lam_ref, gain_ref, q_ref, kc_ref, vc_ref, kn_ref, vn_ref,
                        bc_ref, bn_ref, o_ref):
    q_all = q_ref[...] * SOFTMAX_QSCALE
    past = kc_ref.shape[1]
    k_t, v_t = (True, False), (False, False)
    for h in range(C_HEADS):
        cols = slice(h * LANES, (h + 1) * LANES)
        q = q_all[:, cols]
        kb = [kc_ref[cols, :].astype(BF16), kn_ref[:, cols].astype(BF16)]
        vb = [vc_ref[pl.ds(h, past, stride=C_HEADS), :].astype(BF16), vn_ref[:, cols].astype(BF16)]
        biases = [jnp.concatenate([b[h], b[h]], axis=0) for b in (bc_ref, bn_ref)]
        t = q.shape[0]
        o = _softmax_segments(_stack_heads(q, HEAD_DIM), kb, vb, biases, k_t, v_t)
        o_ref[:, cols] = _diff_combine(o[:t], o[t:], lam_init, lam_ref, gain_ref).astype(o_ref.dtype)


def _diff_attn_prompt(pn4, layer, k_t, t5tiles, lam_init, t5_flat, lam_params, gain):
    _, bn, s, _ = pn4.shape
    sub = math.gcd(s // ATT_BLK, DIFF_SUB)
    blk = (None, sub * ATT_BLK, LANES)
    full = (None, s, LANES)
    state = pltpu.VMEM((2 * ATT_BLK, LANES), F32)
    return pl.pallas_call(
        functools.partial(_diff_prompt_kernel, lam_init),
        grid=(bn, C_HEADS, s // (sub * ATT_BLK)),
        in_specs=[pl.BlockSpec(memory_space=pltpu.SMEM),
                  pl.BlockSpec((4, HEAD_DIM), lambda b, h, g: (0, 0)),
                  pl.BlockSpec((1, C_VDIM), lambda b, h, g: (0, 0)),
                  _pspec(blk, lambda b, h, g: (b, g, _CBP["qc"] + h), layer),
                  pl.BlockSpec((None, None, LANES, s), lambda b, h, g: (layer, b, h, 0)),
                  _pspec(full, lambda b, h, g: (b, 0, _CBP["vc"] + h), layer),
                  pl.BlockSpec((None, None, ATT_BLK, ATT_BLK), lambda b, h, g: (h, 0, 0, 0)),
                  pl.BlockSpec((None, None, ATT_BLK, ATT_BLK), lambda b, h, g: (h, 1, 0, 0))],
        out_specs=pl.BlockSpec(blk, lambda b, h, g: (b, g, h)),
        out_shape=jax.ShapeDtypeStruct((bn, s, BRANCH_W), BF16),
        scratch_shapes=[state, state, state],
        compiler_params=_cparams("parallel", "parallel", "arbitrary"),
        name="diff_attn_prompt",
    )(t5_flat, lam_params, gain.reshape(1, C_VDIM), pn4, k_t, pn4, t5tiles, t5tiles)


def _diff_attn_sample(p4, layer, k_cache_t, v_cache, bias_cache, bias_new, lam_init, lam_params, gain):
    _, bn, t, _ = p4.shape
    past = k_cache_t.shape[-1]
    new = lambda name: _pspec((None, t, BRANCH_W), lambda b: (b, 0, _CB[name] * LANES // BRANCH_W), layer)
    return pl.pallas_call(
        functools.partial(_diff_sample_kernel, lam_init),
        grid=(bn,),
        in_specs=[pl.BlockSpec((4, HEAD_DIM), lambda b: (0, 0)),
                  pl.BlockSpec((1, C_VDIM), lambda b: (0, 0)),
                  new("qc"),
                  pl.BlockSpec((None, None, BRANCH_W, past), lambda b: (layer, b, 0, 0)),
                  pl.BlockSpec((None, None, past * C_HEADS, C_VDIM), lambda b: (layer, b, 0, 0)),
                  new("kc"), new("vc"),
                  pl.BlockSpec((C_HEADS, t, past), lambda b: (0, 0, 0)),
                  pl.BlockSpec((C_HEADS, t, t), lambda b: (0, 0, 0))],
        out_specs=pl.BlockSpec((None, t, BRANCH_W), lambda b: (b, 0, 0)),
        out_shape=jax.ShapeDtypeStruct((bn, t, BRANCH_W), BF16),
        compiler_params=_cparams("parallel"),
        name="diff_attn_sample",
    )(lam_params, gain.reshape(1, C_VDIM), p4, k_cache_t, v_cache, p4, p4, bias_cache, bias_new)


def _merge_kernel(x_ref, oa_ref, ob_ref, oc_ref, gm_ref, wg_ref, bg_ref, wb_ref, wo_ref, o_ref):
    x = x_ref[...]
    d = x.shape[1]
    xn = _rms(x, gm_ref[...]).astype(BF16)
    h = None
    for n, o_r in enumerate((oa_ref, ob_ref, oc_ref)):
        logits = _dot(xn, wg_ref[:, n * d:(n + 1) * d]) + bg_ref[n:n + 1, :]
        t = _dot(o_r[...], wb_ref[n]) / (1.0 + jnp.exp(-logits))
        h = t if h is None else h + t
    o_ref[...] = x + _dot(h.astype(BF16), wo_ref[...])


def _merge(x, o_a, o_b, o_c, g_mix, w_gate, b_gate, w_branch, w_out, tm):
    t, d = x.shape
    row = lambda i: (i, 0)
    fixed2 = lambda i: (0, 0)
    return pl.pallas_call(
        _merge_kernel,
        grid=(t // tm,),
        in_specs=[pl.BlockSpec((tm, d), row),
                  pl.BlockSpec((tm, BRANCH_W), row),
                  pl.BlockSpec((tm, BRANCH_W), row),
                  pl.BlockSpec((tm, BRANCH_W), row),
                  pl.BlockSpec((1, d), fixed2),
                  pl.BlockSpec((d, N_BRANCH * d), fixed2),
                  pl.BlockSpec((N_BRANCH, d), fixed2),
                  pl.BlockSpec((N_BRANCH, BRANCH_W, d), lambda i: (0, 0, 0)),
                  pl.BlockSpec((d, d), fixed2)],
        out_specs=pl.BlockSpec((tm, d), row),
        out_shape=jax.ShapeDtypeStruct((t, d), F32),
        compiler_params=_cparams("parallel"),
        name="gated_merge",
    )(x, o_a, o_b, o_c, g_mix.reshape(1, d), w_gate, b_gate, w_branch, w_out)


FFN_TF = D_FF // 2
FFN_PREV = 16


def _gelu_tanh(x):
    return x * (0.5 * (1.0 + jnp.tanh(math.sqrt(2.0 / math.pi) * (x + 0.044715 * (x * x * x)))))


def _ffn_kernel(tiles_per_seq, final, x_ref, xp_ref, gn_ref, wg_ref, wu_ref, cw_ref, cb_ref, wd_ref, st_ref, gf_ref,
                o_ref, so_ref, xn_ref, xpn_ref):
    i = pl.program_id(0)
    j = pl.program_id(1)
    whole_seqs = tiles_per_seq == 0

    @pl.when(j == 0)
    def _():
        xn_ref[...] = _rms(x_ref[...], gn_ref[...]).astype(BF16)
        if not whole_seqs:
            xpn_ref[...] = _rms(xp_ref[...], gn_ref[...]).astype(BF16)

    xn = xn_ref[...]
    tm = xn.shape[0]
    tf = wg_ref.shape[1]
    g = _dot(xn, wg_ref[...])
    u = _dot(xn, wu_ref[...])
    if whole_seqs:
        ns = st_ref.shape[0]
        st = st_ref[...]
        g = g.reshape(ns, tm // ns, tf)
        u = u.reshape(ns, tm // ns, tf)
        pm2, pm1 = st[:, 0:1, :], st[:, 1:2, :]
    else:
        gp = _dot(xpn_ref[...], wg_ref[...])
        first = (i % tiles_per_seq) == 0
        st = st_ref[0]
        pm2 = jnp.where(first, st[0:1, :], gp[FFN_PREV - 2:FFN_PREV - 1, :])
        pm1 = jnp.where(first, st[1:2, :], gp[FFN_PREV - 1:FFN_PREV, :])
    ax = g.ndim - 2
    row = lax.broadcasted_iota(jnp.int32, g.shape, ax)
    g1 = jnp.where(row == 0, pm1, pltpu.roll(g, 1, ax))
    g2 = jnp.where(row == 0, pm2, jnp.where(row == 1, pm1, pltpu.roll(g, 2, ax)))
    cw = cw_ref[...]
    c = cb_ref[...] + cw[0:1, :] * g2 + cw[1:2, :] * g1 + cw[2:3, :] * g
    hid = (_gelu_tanh(c) * u).reshape(tm, tf).astype(BF16)
    contrib = _dot(hid, wd_ref[...])

    last = pl.num_programs(1) - 1
    finish = (lambda y: _rms(y, gf_ref[...])) if final else (lambda y: y)

    @pl.when(j == 0)
    def _():
        o_ref[...] = x_ref[...] + contrib

    @pl.when(jnp.logical_and(j > 0, j < last))
    def _():
        o_ref[...] += contrib

    @pl.when(jnp.logical_and(j > 0, j == last))
    def _():
        o_ref[...] = finish(o_ref[...] + contrib)

    seq_len = g.shape[ax]
    if whole_seqs:
        so_ref[...] = g[:, seq_len - 2:seq_len, :]
    else:
        so_ref[0] = g[seq_len - 2:seq_len, :]


def _conv_ffn(x, g_norm, w_gate, w_up, conv_w, conv_b, w_down, state, seq_len, tm, g_final, final):
    t, d = x.shape
    nseq = t // seq_len
    nf = D_FF // FFN_TF
    assert nf > 1
    if seq_len >= tm:
        tiles_per_seq = seq_len // tm
        st_spec = pl.BlockSpec((1, 2, FFN_TF), lambda i, j: (i // tiles_per_seq, 0, j))
    else:
        tiles_per_seq = 0
        st_spec = pl.BlockSpec((tm // seq_len, 2, FFN_TF), lambda i, j: (i, 0, j))
    prev_blocks = tm // FFN_PREV
    tail_spec = pl.BlockSpec(st_spec.block_shape, lambda i, j: (i, 0, j))
    n_tail = t // tm * st_spec.block_shape[0]
    y, tails = pl.pallas_call(
        functools.partial(_ffn_kernel, tiles_per_seq, final),
        grid=(t // tm, nf),
        in_specs=[pl.BlockSpec((tm, d), lambda i, j: (i, 0)),
                  pl.BlockSpec((FFN_PREV, d), lambda i, j: (jnp.maximum(i * prev_blocks - 1, 0), 0)),
                  pl.BlockSpec((1, d), lambda i, j: (0, 0)),
                  pl.BlockSpec((d, FFN_TF), lambda i, j: (0, j)),
                  pl.BlockSpec((d, FFN_TF), lambda i, j: (0, j)),
                  pl.BlockSpec((3, FFN_TF), lambda i, j: (0, j)),
                  pl.BlockSpec((1, FFN_TF), lambda i, j: (0, j)),
                  pl.BlockSpec((FFN_TF, d), lambda i, j: (j, 0)),
                  st_spec,
                  pl.BlockSpec((1, d), lambda i, j: (0, 0))],
        out_specs=[pl.BlockSpec((tm, d), lambda i, j: (i, 0)), tail_spec],
        out_shape=[jax.ShapeDtypeStruct((t, d), F32),
                   jax.ShapeDtypeStruct((n_tail, 2, D_FF), F32)],
        scratch_shapes=[pltpu.VMEM((tm, d), BF16), pltpu.VMEM((FFN_PREV, d), BF16)],
        compiler_params=_cparams("arbitrary", "arbitrary"),
        name="conv_ffn_prompt" if tiles_per_seq else "conv_ffn_sample",
    )(x, x, g_norm.reshape(1, d), w_gate, w_up, conv_w, conv_b.reshape(1, D_FF), w_down, state, g_final.reshape(1, d))
    return y, tails[n_tail // nseq - 1::n_tail // nseq]


PROJ_TM = 1024
PROMPT_TN, SAMPLE_TN = ROW_COLS // 2, IN_COLS // 3
ROW_TM = 512


def _lam_init(layer):
    return 0.8 - 0.6 * math.exp(-0.3 * layer)


def _cols(p4, name):
    c0 = _CB[name] * LANES
    return p4[..., c0:c0 + BRANCH_W]


def _layer_tail(x2, o_a, o_b, o_c, layer, prm, conv_state, seq_len):
    flat = lambda a: a.reshape(x2.shape[0], BRANCH_W)
    x2 = _merge(x2, flat(o_a), flat(o_b), flat(o_c), prm["norm_mix"][layer], prm["w_gate_mix"][layer],
                prm["b_gate"][layer], prm["w_branch"][layer], prm["w_out"][layer], ROW_TM)
    depth = prm["norm_ffn"].shape[0]
    return _conv_ffn(x2, prm["norm_ffn"][layer], prm["w_gate"][layer], prm["w_up"][layer],
                     prm["conv_w"][layer], prm["conv_b"][layer], prm["w_down"][layer], conv_state, seq_len, ROW_TM,
                     prm["norm_final"], layer == depth - 1)


def _trunk_prompt(x, prm, abias, t5tiles, t5_flat):
    bn, s, d = x.shape
    x2 = x.reshape(bn * s, d)
    depth = prm["w_rows"].shape[0]
    conv_states = []
    rows, kv_t = None, None
    for l in range(depth):
        rows = _norm_proj(x2, prm["norm_mix"][l], prm["w_rows"][l], PROJ_TM, PROMPT_TN, l, depth, rows,
                          heads_group=(_CBP["vc"] * LANES, C_HEADS))
        p_rows, vc_heads = rows
        kv_t = _norm_proj_t(x2, prm["norm_mix"][l], prm["w_feat_t"][l], PROJ_TM, l, depth, bn, kv_t)
        ka_t, va_t, kb_t, vb_t, kc_t = kv_t
        pn4 = p_rows.reshape(depth, bn, s, ROW_COLS)
        o_a = _band_attn_prompt(pn4, l, ka_t, va_t, abias[l])
        o_b = _sb_attn_prompt(pn4, l, kb_t, vb_t)
        o_c = _diff_attn_prompt(pn4, l, kc_t, t5tiles, _lam_init(l), t5_flat, prm["c_lambda"][l], prm["c_subln"][l])
        x2, st_conv = _layer_tail(x2, o_a, o_b, o_c, l, prm, jnp.zeros((bn, 2, D_FF), F32), s)
        conv_states.append(st_conv)
    y = x2.reshape(bn, s, d)
    heads_t = lambda a: jnp.transpose(a.reshape(depth, bn, A_HEADS, HEAD_DIM, a.shape[-1]), (0, 1, 4, 2, 3))
    keep = min(A_PAST_CHUNKS * CHUNK, s)
    states = (heads_t(ka_t[..., s - keep:]), heads_t(va_t[..., s - keep:]), heads_t(kb_t), heads_t(vb_t),
              jnp.transpose(kc_t.reshape(depth, bn, C_HEADS, 2, HEAD_DIM, s), (0, 1, 5, 2, 3, 4)),
              vc_heads.reshape(depth, bn, s, C_HEADS, C_VDIM),
              jnp.stack(conv_states, axis=0))
    return y, states


def _feature_major(cache):
    nd = cache.ndim
    t = jnp.transpose(cache, (0, 1) + tuple(range(3, nd)) + (2,))
    return t.reshape(cache.shape[0], cache.shape[1], BRANCH_W, cache.shape[2])


def _trunk_sample(x, caches, prm, abias, t5tiles):
    bn, t, d = x.shape
    x2 = x.reshape(bn * t, d)
    depth = prm["w_in"].shape[0]
    ca_k, ca_v, cb_k, cb_v, cc_k, cc_v, st_conv_in = caches
    ca_kt, ca_vt, cb_kt, cb_vt, cc_kt = (_feature_major(c) for c in (ca_k, ca_v, cb_k, cb_v, cc_k))
    keep, past = ca_k.shape[2], cc_k.shape[2]
    cc_vr = cc_v.reshape(depth, bn, past * C_HEADS, C_VDIM)
    conv_states = []
    p_all = None
    for l in range(depth):
        p_all, = _norm_proj(x2, prm["norm_mix"][l], prm["w_in"][l], PROJ_TM, SAMPLE_TN, l, depth,
                            None if p_all is None else [p_all])
        p4 = p_all.reshape(depth, bn, t, IN_COLS)
        bias_cache = jnp.concatenate([abias[l][:, 0, :t, :], abias[l][:, 1, :t, :]], axis=-1)[..., -keep:]
        o_a = _band_attn_sample(p4, l, ca_kt, ca_vt, bias_cache, abias[l][:, 2, :t, :t])
        o_b = _sb_attn_sample(p4, l, cb_kt, cb_vt)
        far = jnp.broadcast_to(t5tiles[:, 1, :1, :1], (C_HEADS, t, past - ATT_BLK))
        bias_c = jnp.concatenate([far, t5tiles[:, 1, :t, :]], axis=-1)
        o_c = _diff_attn_sample(p4, l, cc_kt, cc_vr, bias_c, t5tiles[:, 0, :t, :t],
                                _lam_init(l), prm["c_lambda"][l], prm["c_subln"][l])
        x2, st_conv = _layer_tail(x2, o_a, o_b, o_c, l, prm, st_conv_in[l], t)
        conv_states.append(st_conv)
    y = x2.reshape(bn, t, d)
    kb, vb, kc, vc = (_cols(p4, n) for n in ("kb", "vb", "kc", "vc"))
    heads = lambda a: a.reshape(a.shape[:3] + (A_HEADS, HEAD_DIM))
    heads_t = lambda a: jnp.transpose(a.reshape(depth, bn, A_HEADS, HEAD_DIM, a.shape[-1]), (0, 1, 4, 2, 3))
    sa_kt, sa_vt = _band_roll(ca_kt, ca_vt, p4)
    states = (heads_t(sa_kt), heads_t(sa_vt), heads(kb), heads(vb),
              kc.reshape(depth, bn, t, C_HEADS, 2, HEAD_DIM), vc.reshape(depth, bn, t, C_HEADS, C_VDIM),
              jnp.stack(conv_states, axis=0))
    return y, states


def kernel(x_prompt, x_sample, cache_a_k, cache_a_v, cache_b_k, cache_b_v, cache_c_k, cache_c_v,
           state_ffn_conv, norm_mix, w_in, b_gate, a_rel_bias, t5_bias, c_lambda, c_subln,
           w_branch, w_out, norm_ffn, w_up, conv_w, conv_b, w_down, norm_final):
    group = {name: w_in[..., i * BRANCH_W:(i + 1) * BRANCH_W] for i, name in enumerate(_GROUPS)}
    gate = w_in[..., len(_GROUPS) * BRANCH_W:]
    prm = dict(
        norm_mix=norm_mix,
        w_in=w_in[..., :len(_GROUPS) * BRANCH_W].astype(BF16),
        w_rows=jnp.concatenate([group[n] for n in _ROW_GROUPS], axis=-1).astype(BF16),
        w_gate_mix=gate.astype(BF16),
        w_feat_t=jnp.swapaxes(jnp.concatenate([group[n] for n in _T_GROUPS], axis=-1), 1, 2).astype(BF16),
        b_gate=b_gate, c_lambda=c_lambda, c_subln=c_subln,
        w_branch=w_branch.astype(BF16), w_out=w_out.astype(BF16), norm_ffn=norm_ffn,
        w_gate=w_up[..., :D_FF].astype(BF16), w_up=w_up[..., D_FF:].astype(BF16),
        conv_w=conv_w, conv_b=conv_b, w_down=w_down.astype(BF16), norm_final=norm_final)
    abias = [_build_abias(a_rel_bias[l]) for l in range(a_rel_bias.shape[0])]
    t5tiles = _build_t5bias(t5_bias)
    y_p, p_states = _trunk_prompt(x_prompt, prm, abias, t5tiles, t5_bias.reshape(-1))
    caches = (cache_a_k, cache_a_v, cache_b_k, cache_b_v, cache_c_k, cache_c_v, state_ffn_conv)
    y_s, s_states = _trunk_sample(x_sample, caches, prm, abias, t5tiles)
    return (y_p, y_s) + p_states + s_states
```

```python
import functools
import math

import jax
import jax.numpy as jnp
from jax import lax
from jax.experimental import pallas as pl
from jax.experimental.pallas import tpu as pltpu

F32 = jnp.float32
BF16 = jnp.bfloat16

D_MODEL = 1024
CHUNK = 64
HEAD_DIM = 64
A_HEADS = 8
A_PAST_CHUNKS = 8
A_REL_CLIP = 128
B_HEADS = 8
C_HEADS = 4
C_VDIM = 128
T5_BUCKETS = 32
D_FF = 2816
N_BRANCH = 3
BRANCH_W = 512
IN_COLS = 9 * BRANCH_W
EPS = 1e-6
NEG_INF = -1e30
SCALE = HEAD_DIM ** -0.5
LOG2E = math.log2(math.e)
SOFTMAX_QSCALE = SCALE * LOG2E

LANES = 128
_GROUPS = ("qa", "ka", "va", "qb", "kb", "vb", "qc", "kc", "vc")
_CB = {name: i * BRANCH_W // LANES for i, name in enumerate(_GROUPS)}
_ROW_GROUPS = ("qa", "qb", "qc", "vc")
_T_GROUPS = ("ka", "va", "kb", "vb", "kc")
_CBP = {name: i * BRANCH_W // LANES for i, name in enumerate(_ROW_GROUPS)}
ROW_COLS = len(_ROW_GROUPS) * BRANCH_W

ATT_BLK = 256
ATT_HEADS = 4
SB_LOG_CUT = -104.0
T5_FAR_BUCKET = 15
STICK_SUB = 16
BAND_SUB = 16
DIFF_SUB = 16
DIFF_FAR_GROUP = 4
V7X_VMEM_BYTES = 64 * 1024 * 1024
VMEM_LIMIT = V7X_VMEM_BYTES - 8 * 1024 * 1024


def _cparams(*sem):
    return pltpu.CompilerParams(dimension_semantics=sem, vmem_limit_bytes=VMEM_LIMIT)


def _rms(x, g):
    return x * lax.rsqrt(jnp.mean(x * x, axis=-1, keepdims=True) + EPS) * g


def _dot(a, b):
    return jnp.dot(a, b, preferred_element_type=F32)


def _dot_nt(a, b):
    return lax.dot_general(a, b, (((1,), (1,)), ((), ())), preferred_element_type=F32)


def _norm_proj_kernel(n_stack, heads_cols, x_ref, g_ref, w_ref, *refs):
    outs, xn_ref = refs[n_stack:-1], refs[-1]
    j = pl.program_id(1)

    @pl.when(j == 0)
    def _():
        xn_ref[...] = _rms(x_ref[...], g_ref[...]).astype(BF16)

    res = _dot(xn_ref[...], w_ref[...])
    outs[0][...] = res
    if heads_cols is not None:
        tile, first, heads = heads_cols

        @pl.when(j == tile)
        def _():
            for h in range(heads):
                lo = first + h * LANES
                outs[1][pl.ds(h, res.shape[0], stride=heads), :] = res[:, lo:lo + LANES]


def _norm_proj(x, g, w, tm, tn, layer, depth, stacked, heads_group=None):
    t, d = x.shape
    n = w.shape[1]
    in_specs = [pl.BlockSpec((tm, d), lambda i, j: (i, 0)),
                pl.BlockSpec((1, d), lambda i, j: (0, 0)),
                pl.BlockSpec((d, tn), lambda i, j: (0, j))]
    args = [x, g.reshape(1, d), w]
    out_specs = [pl.BlockSpec((None, tm, tn), lambda i, j: (layer, i, j))]
    out_shape = [jax.ShapeDtypeStruct((depth, t, n), F32)]
    heads_cols = None
    if heads_group is not None:
        first, heads = heads_group
        heads_cols = (first // tn, first % tn, heads)
        out_specs.append(pl.BlockSpec((None, tm * heads, LANES), lambda i, j: (layer, i, 0)))
        out_shape.append(jax.ShapeDtypeStruct((depth, t * heads, LANES), F32))
    n_stack = 0 if stacked is None else len(out_shape)
    if stacked is not None:
        in_specs += [pl.BlockSpec(memory_space=pl.ANY)] * n_stack
        args += list(stacked)
    return pl.pallas_call(
        functools.partial(_norm_proj_kernel, n_stack, heads_cols),
        grid=(t // tm, n // tn),
        in_specs=in_specs,
        out_specs=out_specs,
        out_shape=out_shape,
        scratch_shapes=[pltpu.VMEM((tm, d), BF16)],
        input_output_aliases={3 + k: k for k in range(n_stack)},
        compiler_params=_cparams("arbitrary", "arbitrary"),
        name="norm_proj",
    )(*args)


def _norm_proj_t_kernel(ngroups, x_ref, g_ref, wt_ref, *refs):
    o_refs = refs[-ngroups:]
    xn = _rms(x_ref[...], g_ref[...]).astype(BF16)
    for g in range(ngroups):
        o_refs[g][...] = _dot_nt(wt_ref[g * BRANCH_W:(g + 1) * BRANCH_W, :], xn)


def _norm_proj_t(x, g, wt, tm, layer, depth, bn, stacked):
    t, d = x.shape
    s = t // bn
    ngroups = wt.shape[0] // BRANCH_W
    tiles = s // tm
    in_specs = [pl.BlockSpec((tm, d), lambda i: (i, 0)),
                pl.BlockSpec((1, d), lambda i: (0, 0)),
                pl.BlockSpec(wt.shape, lambda i: (0, 0))]
    args = [x, g.reshape(1, d), wt]
    aliases = {}
    if stacked is not None:
        in_specs += [pl.BlockSpec(memory_space=pl.ANY)] * ngroups
        args += list(stacked)
        aliases = {3 + n: n for n in range(ngroups)}
    out_spec = pl.BlockSpec((None, None, BRANCH_W, tm), lambda i: (layer, i // tiles, 0, i % tiles))
    return pl.pallas_call(
        functools.partial(_norm_proj_t_kernel, ngroups),
        grid=(t // tm,),
        in_specs=in_specs,
        out_specs=[out_spec] * ngroups,
        out_shape=[jax.ShapeDtypeStruct((depth, bn, BRANCH_W, s), F32)] * ngroups,
        input_output_aliases=aliases,
        compiler_params=_cparams("arbitrary"),
        name="norm_proj_t",
    )(*args)


def _abias_kernel(tbl_ref, o_ref):
    h = pl.program_id(0)
    sub = 8
    ql = lax.broadcasted_iota(jnp.int32, (ATT_BLK, ATT_BLK), 0)
    kl = lax.broadcasted_iota(jnp.int32, (ATT_BLK, ATT_BLK), 1)
    far = jnp.full((ATT_BLK, ATT_BLK), tbl_ref[2 * A_REL_CLIP * A_HEADS + h] * LOG2E, F32)
    o_ref[0, 0] = jnp.where((kl >> 6) >= (ql >> 6), far, NEG_INF)
    lead = ATT_BLK - sub
    b = lax.broadcasted_iota(jnp.int32, (sub, 2 * ATT_BLK), 0)
    x = lax.broadcasted_iota(jnp.int32, (sub, 2 * ATT_BLK), 1)
    for seg, off in ((1, ATT_BLK), (2, 0)):
        idx = jnp.clip(b + lead - x + off, -A_REL_CLIP, A_REL_CLIP) + A_REL_CLIP

        def body(r, acc, idx=idx):
            return jnp.where(idx == r, tbl_ref[r * A_HEADS + h] * LOG2E, acc)

        strip = lax.fori_loop(0, 2 * A_REL_CLIP + 1, body, jnp.zeros((sub, 2 * ATT_BLK), F32))
        for a in range(ATT_BLK // sub):
            o_ref[0, seg, a * sub:(a + 1) * sub, :] = strip[:, lead - a * sub:lead - a * sub + ATT_BLK]
    o_ref[0, 2] = jnp.where((kl >> 6) <= (ql >> 6), o_ref[0, 2], NEG_INF)


def _build_abias(table):
    return pl.pallas_call(
        _abias_kernel,
        grid=(A_HEADS,),
        in_specs=[pl.BlockSpec(memory_space=pltpu.SMEM)],
        out_specs=pl.BlockSpec((1, 3, ATT_BLK, ATT_BLK), lambda h: (h, 0, 0, 0)),
        out_shape=jax.ShapeDtypeStruct((A_HEADS, 3, ATT_BLK, ATT_BLK), F32),
        compiler_params=_cparams("parallel"),
        name="band_bias_tiles",
    )(table.reshape(-1))


_T5_LARGE_STEPS = (12, 16, 23, 32, 46, 64, 91)


def _t5bias_kernel(tbl_ref, o_ref):
    h = pl.program_id(0)
    rows = 64
    for seg, off in ((0, 0), (1, -ATT_BLK)):
        for rc in range(ATT_BLK // rows):
            ql = lax.broadcasted_iota(jnp.int32, (rows, ATT_BLK), 0) + rc * rows
            kl = lax.broadcasted_iota(jnp.int32, (rows, ATT_BLK), 1)
            rel = kl - ql + off
            n = jnp.abs(rel)
            large = jnp.full((rows, ATT_BLK), T5_BUCKETS // 4, jnp.int32)
            for th in _T5_LARGE_STEPS:
                large = large + jnp.where(n >= th, 1, 0)
            bucket = jnp.where(rel > 0, T5_BUCKETS // 2, 0) + jnp.where(n < T5_BUCKETS // 4, n, large)
            tile = jnp.zeros((rows, ATT_BLK), F32)
            for r in range(T5_BUCKETS):
                tile = jnp.where(bucket == r, tbl_ref[r * C_HEADS + h] * LOG2E, tile)
            if seg == 0:
                tile = jnp.where((kl >> 6) <= (ql >> 6), tile, NEG_INF)
            o_ref[0, seg, rc * rows:(rc + 1) * rows, :] = tile


def _build_t5bias(table):
    return pl.pallas_call(
        _t5bias_kernel,
        grid=(C_HEADS,),
        in_specs=[pl.BlockSpec(memory_space=pltpu.SMEM)],
        out_specs=pl.BlockSpec((1, 2, ATT_BLK, ATT_BLK), lambda h: (h, 0, 0, 0)),
        out_shape=jax.ShapeDtypeStruct((C_HEADS, 2, ATT_BLK, ATT_BLK), F32),
        compiler_params=_cparams("parallel"),
        name="t5_bias_tiles",
    )(table.reshape(-1))


def _pspec(block, index_map, layer):
    return pl.BlockSpec((None,) + block, lambda *g: (layer,) + tuple(index_map(*g)))


def _lane_tile(x, n):
    return x if n == 1 else jnp.concatenate([x] * n, axis=1)


def _stack_heads(q, width):
    group = lax.broadcasted_iota(jnp.int32, q.shape, 1) // width
    return jnp.concatenate([jnp.where(group == h, q, 0.0) for h in range(q.shape[1] // width)],
                           axis=0).astype(BF16)


def _unstack_heads(x, t, width):
    group = lax.broadcasted_iota(jnp.int32, (t, x.shape[1]), 1) // width
    out = jnp.zeros((t, x.shape[1]), x.dtype)
    for h in range(x.shape[1] // width):
        out = jnp.where(group == h, x[h * t:(h + 1) * t, :], out)
    return out


def _qk(q, k, k_t):
    return _dot(q, k) if k_t else _dot_nt(q, k)


def _pv(p, v, v_t):
    return _dot_nt(p, v) if v_t else _dot(p, v)


def _softmax_segments(qh, kb, vb, biases, k_t, v_t):
    s = [_qk(qh, k, t) + b for k, b, t in zip(kb, biases, k_t)]
    m = functools.reduce(jnp.maximum, [jnp.max(sj, axis=1, keepdims=True) for sj in s])
    p = [jnp.exp2(sj - m) for sj in s]
    l = functools.reduce(jnp.add, [jnp.sum(pj, axis=1, keepdims=True) for pj in p])
    acc = functools.reduce(jnp.add, [_pv(pj.astype(BF16), v, t) for pj, v, t in zip(p, vb, v_t)])
    return acc / l


def _band_kernel(q_ref, k_ref, v_ref, b_ref, o_ref):
    blk = ATT_BLK
    heads = b_ref.shape[0]
    n_sub = q_ref.shape[0] // blk

    def one_block(sub, carry):
        qi = pl.program_id(2) * n_sub + sub
        rows = pl.ds(pl.multiple_of(sub * blk, blk), blk)
        qs = _stack_heads(q_ref[rows, :] * SOFTMAX_QSCALE, HEAD_DIM)
        kb, vb, biases = [], [], []
        for seg, back in enumerate((2, 1, 0)):
            keys = pl.ds(pl.multiple_of(jnp.maximum(qi - back, 0) * blk, blk), blk)
            kb.append(k_ref[:, keys].astype(BF16))
            vb.append(v_ref[:, keys].astype(BF16))
            tile = b_ref[:, seg].reshape(heads * blk, blk)
            biases.append(tile if back == 0 else jnp.where(qi >= back, tile, NEG_INF))
        o = _softmax_segments(qs, kb, vb, biases, (True,) * 3, (True,) * 3)
        o_ref[rows, :] = _unstack_heads(o, blk, HEAD_DIM).astype(o_ref.dtype)
        return carry

    lax.fori_loop(0, n_sub, one_block, 0)


def _band_attn_prompt(pn4, layer, k_t, v_t, abias):
    _, bn, s, _ = pn4.shape
    width = ATT_HEADS * HEAD_DIM
    sub = math.gcd(s // ATT_BLK, BAND_SUB)
    blk = (None, sub * ATT_BLK, width)
    seq = pl.BlockSpec((None, None, width, s), lambda b, hg, g: (layer, b, hg, 0))
    return pl.pallas_call(
        _band_kernel,
        grid=(bn, A_HEADS // ATT_HEADS, s // (sub * ATT_BLK)),
        in_specs=[_pspec(blk, lambda b, hg, g: (b, g, _CBP["qa"] * LANES // width + hg), layer), seq, seq,
                  pl.BlockSpec((ATT_HEADS, 3, ATT_BLK, ATT_BLK), lambda b, hg, g: (hg, 0, 0, 0))],
        out_specs=pl.BlockSpec(blk, lambda b, hg, g: (b, g, hg)),
        out_shape=jax.ShapeDtypeStruct((bn, s, BRANCH_W), BF16),
        compiler_params=_cparams("parallel", "parallel", "arbitrary"),
        name="band_attn_prompt",
    )(pn4, k_t, v_t, abias)


def _band_sample_kernel(q_ref, kc_ref, kn_ref, vc_ref, vn_ref, bc_ref, bn_ref, o_ref):
    t = q_ref.shape[0]
    width = ATT_HEADS * HEAD_DIM
    q = q_ref[...] * SOFTMAX_QSCALE
    for g in range(q.shape[1] // width):
        cols = slice(g * width, (g + 1) * width)
        rows = slice(g * ATT_HEADS * t, (g + 1) * ATT_HEADS * t)
        qs = _stack_heads(q[:, cols], HEAD_DIM)
        kb = [kc_ref[cols, :].astype(BF16), kn_ref[:, cols].astype(BF16)]
        vb = [vc_ref[cols, :].astype(BF16), vn_ref[:, cols].astype(BF16)]
        o = _softmax_segments(qs, kb, vb, [bc_ref[rows, :], bn_ref[rows, :]], (True, False), (True, False))
        o_ref[:, cols] = _unstack_heads(o, t, HEAD_DIM).astype(o_ref.dtype)


def _band_attn_sample(p4, layer, k_cache_t, v_cache_t, bias_cache, bias_new):
    _, bn, t, _ = p4.shape
    keep = k_cache_t.shape[-1]
    new = lambda name: _pspec((None, t, BRANCH_W), lambda b: (b, 0, _CB[name] * LANES // BRANCH_W), layer)
    old = pl.BlockSpec((None, None, BRANCH_W, keep), lambda b: (layer, b, 0, 0))
    return pl.pallas_call(
        _band_sample_kernel,
        grid=(bn,),
        in_specs=[new("qa"), old, new("ka"), old, new("va"),
                  pl.BlockSpec((A_HEADS * t, keep), lambda b: (0, 0)),
                  pl.BlockSpec((A_HEADS * t, t), lambda b: (0, 0))],
        out_specs=pl.BlockSpec((None, t, BRANCH_W), lambda b: (b, 0, 0)),
        out_shape=jax.ShapeDtypeStruct((bn, t, BRANCH_W), BF16),
        compiler_params=_cparams("parallel"),
        name="band_attn_sample",
    )(p4, k_cache_t, p4, v_cache_t, p4, bias_cache.reshape(A_HEADS * t, keep), bias_new.reshape(A_HEADS * t, t))


def _band_roll_kernel(kc_ref, vc_ref, kn_ref, vn_ref, ko_ref, vo_ref):
    t = kn_ref.shape[0]
    keep = kc_ref.shape[1]
    for c_ref, n_ref, o_ref in ((kc_ref, kn_ref, ko_ref), (vc_ref, vn_ref, vo_ref)):
        rolled = pltpu.roll(c_ref[...], keep - t, 1)
        new = jnp.concatenate([n_ref[...], jnp.zeros((LANES - t, n_ref.shape[1]), F32)], axis=0)
        new_t = pltpu.roll(new.T, LANES - t, 1)
        lane = lax.broadcasted_iota(jnp.int32, new_t.shape, 1)
        last = jnp.where(lane >= LANES - t, new_t, rolled[:, keep - LANES:])
        o_ref[...] = jnp.concatenate([rolled[:, :keep - LANES], last], axis=1)


def _band_roll(k_cache_t, v_cache_t, p4):
    depth, bn, t, _ = p4.shape
    keep = k_cache_t.shape[-1]
    assert t <= LANES <= keep
    old = pl.BlockSpec((None, None, BRANCH_W, keep), lambda l, b: (l, b, 0, 0))
    new = lambda name: pl.BlockSpec((None, None, t, BRANCH_W), lambda l, b: (l, b, 0, _CB[name] * LANES // BRANCH_W))
    return pl.pallas_call(
        _band_roll_kernel,
        grid=(depth, bn),
        in_specs=[old, old, new("ka"), new("va")],
        out_specs=[old, old],
        out_shape=[jax.ShapeDtypeStruct(k_cache_t.shape, F32)] * 2,
        compiler_params=_cparams("parallel", "parallel"),
        name="band_roll",
    )(k_cache_t, v_cache_t, p4, p4)


def _strict_upper(n):
    r = lax.broadcasted_iota(jnp.int32, (n, n), 0)
    c = lax.broadcasted_iota(jnp.int32, (n, n), 1)
    return jnp.where(r > c, 1.0, 0.0).astype(BF16)


def _sb_fold(qh, kb, vb, kv_t, upper, mask, carry_ref, acc_ref, first):
    tq = qh[0].shape[0]
    tk = kb[0].shape[1] if kv_t else kb[0].shape[0]
    cw = upper.shape[0]
    top = None
    for h in range(len(qh)):
        z = _qk(qh[h], kb[0], kv_t)
        soft = jnp.log(1.0 + jnp.exp(-jnp.abs(z)))
        log_keep = -(jnp.maximum(z, 0.0) + soft)
        log_take = jnp.minimum(z, 0.0) - soft
        if mask is not None:
            log_keep = jnp.where(mask, log_keep, 0.0)
        hi = log_keep.astype(BF16)
        lo = (log_keep - hi.astype(F32)).astype(BF16)
        pieces, total = [], None
        for c0 in reversed(range(0, tk, cw)):
            piece = _dot(hi[:, c0:c0 + cw], upper) + _dot(lo[:, c0:c0 + cw], upper)
            pieces.insert(0, piece if total is None else piece + total)
            part = jnp.sum(log_keep[:, c0:c0 + cw], axis=1, keepdims=True)
            total = part if total is None else total + part
        after = pieces[0] if len(pieces) == 1 else jnp.concatenate(pieces, axis=1)
        if not first:
            after = after + _lane_tile(carry_ref[h], tk // LANES)
        w = jnp.exp(log_take + after)
        if mask is not None:
            w = jnp.where(mask, w, 0.0)
        pv = _pv(w.astype(BF16), vb[0], kv_t)
        if first:
            acc_ref[h] = pv
            carry = jnp.broadcast_to(total, (tq, LANES))
        else:
            acc_ref[h] += pv
            carry = carry_ref[h] + total
        carry_ref[h] = carry
        top = jnp.max(carry) if top is None else jnp.maximum(top, jnp.max(carry))
    return (top > SB_LOG_CUT).astype(jnp.int32)


def _sb_kernel(tk, n_past_static, own_rows, q_ref, *refs):
    if own_rows:
        kd_ref, vd_ref, kp_ref, vp_ref, o_ref, carry_ref, acc_ref = refs
        tq = q_ref.shape[0]
    else:
        kp_ref, vp_ref, o_ref, carry_ref, acc_ref = refs
        tq = ATT_BLK
    n_sub = q_ref.shape[0] // tq
    rq = lax.broadcasted_iota(jnp.int32, (tq, tq), 0)
    ck = lax.broadcasted_iota(jnp.int32, (tq, tq), 1)
    causal = jnp.concatenate([ck < rq] * (q_ref.shape[1] // HEAD_DIM), axis=0)

    def one_block(sub, carry_unused):
        n_past = pl.program_id(2) * n_sub + sub if n_past_static is None else n_past_static
        rows = pl.ds(pl.multiple_of(sub * tq, tq), tq)
        qh = [_stack_heads(q_ref[rows, :] * SCALE, HEAD_DIM)]
        if own_rows:
            kd, vd = [kd_ref[...].astype(BF16)], [vd_ref[...].astype(BF16)]
            live = _sb_fold(qh, kd, vd, False, _strict_upper(tq), causal, carry_ref, acc_ref, True)
            n_left = n_past
        else:
            own = pl.ds(pl.multiple_of(n_past * tq, tq), tq)
            near = pl.ds(pl.multiple_of(jnp.maximum(n_past - 1, 0) * tq, tq), tq)
            kd = [jnp.concatenate([kp_ref[:, near], kp_ref[:, own]], axis=1).astype(BF16)]
            vd = [jnp.concatenate([vp_ref[:, near], vp_ref[:, own]], axis=1).astype(BF16)]
            mask = jnp.concatenate([jnp.broadcast_to(n_past >= 1, causal.shape), causal], axis=1)
            live = _sb_fold(qh, kd, vd, True, _strict_upper(tq), mask, carry_ref, acc_ref, True)
            n_left = n_past - 1

        def cond(st):
            return jnp.logical_and(st[0] >= 0, st[1] > 0)

        def body(st):
            start = pl.multiple_of(st[0] * tk, tk)
            kp, vp = [kp_ref[:, pl.ds(start, tk)].astype(BF16)], [vp_ref[:, pl.ds(start, tk)].astype(BF16)]
            live = _sb_fold(qh, kp, vp, True, _strict_upper(tk), None, carry_ref, acc_ref, False)
            return st[0] - 1, live

        lax.while_loop(cond, body, (n_left - 1, live))
        o_ref[rows, :] = _unstack_heads(acc_ref[0], tq, HEAD_DIM).astype(o_ref.dtype)
        return carry_unused

    lax.fori_loop(0, n_sub, one_block, 0)


def _sb_attn_prompt(pn4, layer, k_t, v_t):
    _, bn, s, _ = pn4.shape
    width = ATT_HEADS * HEAD_DIM
    sub = math.gcd(s // ATT_BLK, STICK_SUB)
    blk = (None, sub * ATT_BLK, width)
    seq = pl.BlockSpec((None, None, width, s), lambda b, hg, g: (layer, b, hg, 0))
    rows = ATT_HEADS * ATT_BLK
    return pl.pallas_call(
        functools.partial(_sb_kernel, ATT_BLK, None, False),
        grid=(bn, B_HEADS // ATT_HEADS, s // (sub * ATT_BLK)),
        in_specs=[_pspec(blk, lambda b, hg, g: (b, g, _CBP["qb"] * LANES // width + hg), layer), seq, seq],
        out_specs=pl.BlockSpec(blk, lambda b, hg, g: (b, g, hg)),
        out_shape=jax.ShapeDtypeStruct((bn, s, BRANCH_W), BF16),
        scratch_shapes=[pltpu.VMEM((1, rows, LANES), F32), pltpu.VMEM((1, rows, width), F32)],
        compiler_params=_cparams("parallel", "parallel", "arbitrary"),
        name="stick_attn_prompt",
    )(pn4, k_t, v_t)


def _sb_attn_sample(p4, layer, k_cache_t, v_cache_t):
    _, bn, t, _ = p4.shape
    past = k_cache_t.shape[-1]
    new = lambda name: _pspec((None, t, BRANCH_W), lambda b: (b, 0, _CB[name] * LANES // BRANCH_W), layer)
    older = pl.BlockSpec((None, None, BRANCH_W, past), lambda b: (layer, b, 0, 0))
    rows = B_HEADS * t
    return pl.pallas_call(
        functools.partial(_sb_kernel, ATT_BLK, past // ATT_BLK, True),
        grid=(bn,),
        in_specs=[new("qb"), new("kb"), new("vb"), older, older],
        out_specs=pl.BlockSpec((None, t, BRANCH_W), lambda b: (b, 0, 0)),
        out_shape=jax.ShapeDtypeStruct((bn, t, BRANCH_W), BF16),
        scratch_shapes=[pltpu.VMEM((1, rows, LANES), F32), pltpu.VMEM((1, rows, BRANCH_W), F32)],
        compiler_params=_cparams("parallel"),
        name="stick_attn_sample",
    )(p4, p4, p4, k_cache_t, v_cache_t)


def _diff_combine(o0, o1, lam_init, lam_ref, gain_ref):
    lp = lam_ref[...]
    lam = (jnp.exp(jnp.sum(lp[0:1] * lp[1:2], axis=1, keepdims=True))
           - jnp.exp(jnp.sum(lp[2:3] * lp[3:4], axis=1, keepdims=True)) + lam_init)
    return _rms(o0 - lam * o1, gain_ref[...]) * (1.0 - lam_init)


def _diff_fold(qs, k_t, v, bias, m_ref, l_ref, acc_ref, first):
    kb = k_t.astype(BF16)
    vb = v.astype(BF16)
    rows, tk = qs.shape[0], vb.shape[0]
    s = _dot(qs, kb) + bias
    m_cur = jnp.max(s, axis=1, keepdims=True)
    m_new = jnp.broadcast_to(m_cur, (rows, LANES)) if first else jnp.maximum(m_ref[...], m_cur)
    p = jnp.exp2(s - _lane_tile(m_new, tk // LANES))
    row = jnp.sum(p, axis=1, keepdims=True)
    pv = _dot(p.astype(BF16), vb)
    if first:
        l_ref[...] = jnp.broadcast_to(row, (rows, LANES))
        acc_ref[...] = pv
    else:
        alpha = jnp.exp2(m_ref[...] - m_new)
        l_ref[...] = alpha * l_ref[...] + row
        acc_ref[...] = alpha * acc_ref[...] + pv
    m_ref[...] = m_new


def _diff_prompt_kernel(lam_init, tbl_ref, lam_ref, gain_ref, q_ref, k_ref, v_ref, bd_ref, bp_ref, o_ref,
                        m_ref, l_ref, acc_ref):
    h = pl.program_id(1)
    blk = ATT_BLK
    n_sub = q_ref.shape[0] // blk
    far = tbl_ref[T5_FAR_BUCKET * C_HEADS + h] * LOG2E
    wide = DIFF_FAR_GROUP * blk

    def one_block(sub, carry):
        qi = pl.program_id(2) * n_sub + sub
        rows = pl.ds(pl.multiple_of(sub * blk, blk), blk)
        qm = _stack_heads(q_ref[rows, :] * SOFTMAX_QSCALE, HEAD_DIM)

        own = pl.multiple_of(qi * blk, blk)
        near = pl.multiple_of(jnp.maximum(qi - 1, 0) * blk, blk)
        k0 = jnp.concatenate([k_ref[:, pl.ds(near, blk)], k_ref[:, pl.ds(own, blk)]], axis=1)
        v0 = jnp.concatenate([v_ref[pl.ds(near, blk), :], v_ref[pl.ds(own, blk), :]], axis=0)
        b0 = jnp.concatenate([jnp.where(qi >= 1, bp_ref[...], NEG_INF), bd_ref[...]], axis=1)
        b0 = jnp.concatenate([b0, b0], axis=0)
        _diff_fold(qm, k0, v0, b0, m_ref, l_ref, acc_ref, True)

        n_far = jnp.maximum(qi - 1, 0)
        n_group = n_far // DIFF_FAR_GROUP

        def body(j, c):
            start = pl.multiple_of(j * wide, wide)
            _diff_fold(qm, k_ref[:, pl.ds(start, wide)], v_ref[pl.ds(start, wide), :], far,
                       m_ref, l_ref, acc_ref, False)
            return c

        lax.fori_loop(0, n_group, body, 0)
        rest = pl.multiple_of(n_group * wide, wide)
        for r in range(1, DIFF_FAR_GROUP):
            @pl.when(n_far - n_group * DIFF_FAR_GROUP == r)
            def _(r=r):
                _diff_fold(qm, k_ref[:, pl.ds(rest, r * blk)], v_ref[pl.ds(rest, r * blk), :], far,
                           m_ref, l_ref, acc_ref, False)

        o = acc_ref[...] / l_ref[...]
        o_ref[rows, :] = _diff_combine(o[:blk], o[blk:], lam_init, lam_ref, gain_ref).astype(o_ref.dtype)
        return carry

    lax.fori_loop(0, n_sub, one_block, 0)


def _diff_sample_kernel(lam_init, lam_ref, gain_ref, q_ref, kc_ref, vc_ref, kn_ref, vn_ref,
                        bc_ref, bn_ref, o_ref):
    q_all = q_ref[...] * SOFTMAX_QSCALE
    past = kc_ref.shape[1]
    k_t, v_t = (True, False), (False, False)
    for h in range(C_HEADS):
        cols = slice(h * LANES, (h + 1) * LANES)
        q = q_all[:, cols]
        kb = [kc_ref[cols, :].astype(BF16), kn_ref[:, cols].astype(BF16)]
        vb = [vc_ref[pl.ds(h, past, stride=C_HEADS), :].astype(BF16), vn_ref[:, cols].astype(BF16)]
        biases = [jnp.concatenate([b[h], b[h]], axis=0) for b in (bc_ref, bn_ref)]
        t = q.shape[0]
        o = _softmax_segments(_stack_heads(q, HEAD_DIM), kb, vb, biases, k_t, v_t)
        o_ref[:, cols] = _diff_combine(o[:t], o[t:], lam_init, lam_ref, gain_ref).astype(o_ref.dtype)


def _diff_attn_prompt(pn4, layer, k_t, t5tiles, lam_init, t5_flat, lam_params, gain):
    _, bn, s, _ = pn4.shape
    sub = math.gcd(s // ATT_BLK, DIFF_SUB)
    blk = (None, sub * ATT_BLK, LANES)
    full = (None, s, LANES)
    state = pltpu.VMEM((2 * ATT_BLK, LANES), F32)
    return pl.pallas_call(
        functools.partial(_diff_prompt_kernel, lam_init),
        grid=(bn, C_HEADS, s // (sub * ATT_BLK)),
        in_specs=[pl.BlockSpec(memory_space=pltpu.SMEM),
                  pl.BlockSpec((4, HEAD_DIM), lambda b, h, g: (0, 0)),
                  pl.BlockSpec((1, C_VDIM), lambda b, h, g: (0, 0)),
                  _pspec(blk, lambda b, h, g: (b, g, _CBP["qc"] + h), layer),
                  pl.BlockSpec((None, None, LANES, s), lambda b, h, g: (layer, b, h, 0)),
                  _pspec(full, lambda b, h, g: (b, 0, _CBP["vc"] + h), layer),
                  pl.BlockSpec((None, None, ATT_BLK, ATT_BLK), lambda b, h, g: (h, 0, 0, 0)),
                  pl.BlockSpec((None, None, ATT_BLK, ATT_BLK), lambda b, h, g: (h, 1, 0, 0))],
        out_specs=pl.BlockSpec(blk, lambda b, h, g: (b, g, h)),
        out_shape=jax.ShapeDtypeStruct((bn, s, BRANCH_W), BF16),
        scratch_shapes=[state, state, state],
        compiler_params=_cparams("parallel", "parallel", "arbitrary"),
        name="diff_attn_prompt",
    )(t5_flat, lam_params, gain.reshape(1, C_VDIM), pn4, k_t, pn4, t5tiles, t5tiles)


def _diff_attn_sample(p4, layer, k_cache_t, v_cache, bias_cache, bias_new, lam_init, lam_params, gain):
    _, bn, t, _ = p4.shape
    past = k_cache_t.shape[-1]
    new = lambda name: _pspec((None, t, BRANCH_W), lambda b: (b, 0, _CB[name] * LANES // BRANCH_W), layer)
    return pl.pallas_call(
        functools.partial(_diff_sample_kernel, lam_init),
        grid=(bn,),
        in_specs=[pl.BlockSpec((4, HEAD_DIM), lambda b: (0, 0)),
                  pl.BlockSpec((1, C_VDIM), lambda b: (0, 0)),
                  new("qc"),
                  pl.BlockSpec((None, None, BRANCH_W, past), lambda b: (layer, b, 0, 0)),
                  pl.BlockSpec((None, None, past * C_HEADS, C_VDIM), lambda b: (layer, b, 0, 0)),
                  new("kc"), new("vc"),
                  pl.BlockSpec((C_HEADS, t, past), lambda b: (0, 0, 0)),
                  pl.BlockSpec((C_HEADS, t, t), lambda b: (0, 0, 0))],
        out_specs=pl.BlockSpec((None, t, BRANCH_W), lambda b: (b, 0, 0)),
        out_shape=jax.ShapeDtypeStruct((bn, t, BRANCH_W), BF16),
        compiler_params=_cparams("parallel"),
        name="diff_attn_sample",
    )(lam_params, gain.reshape(1, C_VDIM), p4, k_cache_t, v_cache, p4, p4, bias_cache, bias_new)


def _merge_kernel(x_ref, oa_ref, ob_ref, oc_ref, gm_ref, wg_ref, bg_ref, wb_ref, wo_ref, o_ref):
    x = x_ref[...]
    d = x.shape[1]
    xn = _rms(x, gm_ref[...]).astype(BF16)
    h = None
    for n, o_r in enumerate((oa_ref, ob_ref, oc_ref)):
        logits = _dot(xn, wg_ref[:, n * d:(n + 1) * d]) + bg_ref[n:n + 1, :]
        t = _dot(o_r[...], wb_ref[n]) / (1.0 + jnp.exp(-logits))
        h = t if h is None else h + t
    o_ref[...] = x + _dot(h.astype(BF16), wo_ref[...])


def _merge(x, o_a, o_b, o_c, g_mix, w_gate, b_gate, w_branch, w_out, tm):
    t, d = x.shape
    row = lambda i: (i, 0)
    fixed2 = lambda i: (0, 0)
    return pl.pallas_call(
        _merge_kernel,
        grid=(t // tm,),
        in_specs=[pl.BlockSpec((tm, d), row),
                  pl.BlockSpec((tm, BRANCH_W), row),
                  pl.BlockSpec((tm, BRANCH_W), row),
                  pl.BlockSpec((tm, BRANCH_W), row),
                  pl.BlockSpec((1, d), fixed2),
                  pl.BlockSpec((d, N_BRANCH * d), fixed2),
                  pl.BlockSpec((N_BRANCH, d), fixed2),
                  pl.BlockSpec((N_BRANCH, BRANCH_W, d), lambda i: (0, 0, 0)),
                  pl.BlockSpec((d, d), fixed2)],
        out_specs=pl.BlockSpec((tm, d), row),
        out_shape=jax.ShapeDtypeStruct((t, d), F32),
        compiler_params=_cparams("parallel"),
        name="gated_merge",
    )(x, o_a, o_b, o_c, g_mix.reshape(1, d), w_gate, b_gate, w_branch, w_out)


FFN_TF = D_FF // 2
FFN_PREV = 16


def _gelu_tanh(x):
    return x * (0.5 * (1.0 + jnp.tanh(math.sqrt(2.0 / math.pi) * (x + 0.044715 * (x * x * x)))))


def _ffn_kernel(tiles_per_seq, final, x_ref, xp_ref, gn_ref, wg_ref, wu_ref, cw_ref, cb_ref, wd_ref, st_ref, gf_ref,
                o_ref, so_ref, xn_ref, xpn_ref):
    i = pl.program_id(0)
    j = pl.program_id(1)
    whole_seqs = tiles_per_seq == 0

    @pl.when(j == 0)
    def _():
        xn_ref[...] = _rms(x_ref[...], gn_ref[...]).astype(BF16)
        if not whole_seqs:
            xpn_ref[...] = _rms(xp_ref[...], gn_ref[...]).astype(BF16)

    xn = xn_ref[...]
    tm = xn.shape[0]
    tf = wg_ref.shape[1]
    g = _dot(xn, wg_ref[...])
    u = _dot(xn, wu_ref[...])
    if whole_seqs:
        ns = st_ref.shape[0]
        st = st_ref[...]
        g = g.reshape(ns, tm // ns, tf)
        u = u.reshape(ns, tm // ns, tf)
        pm2, pm1 = st[:, 0:1, :], st[:, 1:2, :]
    else:
        gp = _dot(xpn_ref[...], wg_ref[...])
        first = (i % tiles_per_seq) == 0
        st = st_ref[0]
        pm2 = jnp.where(first, st[0:1, :], gp[FFN_PREV - 2:FFN_PREV - 1, :])
        pm1 = jnp.where(first, st[1:2, :], gp[FFN_PREV - 1:FFN_PREV, :])
    ax = g.ndim - 2
    row = lax.broadcasted_iota(jnp.int32, g.shape, ax)
    g1 = jnp.where(row == 0, pm1, pltpu.roll(g, 1, ax))
    g2 = jnp.where(row == 0, pm2, jnp.where(row == 1, pm1, pltpu.roll(g, 2, ax)))
    cw = cw_ref[...]
    c = cb_ref[...] + cw[0:1, :] * g2 + cw[1:2, :] * g1 + cw[2:3, :] * g
    hid = (_gelu_tanh(c) * u).reshape(tm, tf).astype(BF16)
    contrib = _dot(hid, wd_ref[...])

    last = pl.num_programs(1) - 1
    finish = (lambda y: _rms(y, gf_ref[...])) if final else (lambda y: y)

    @pl.when(j == 0)
    def _():
        o_ref[...] = x_ref[...] + contrib

    @pl.when(jnp.logical_and(j > 0, j < last))
    def _():
        o_ref[...] += contrib

    @pl.when(jnp.logical_and(j > 0, j == last))
    def _():
        o_ref[...] = finish(o_ref[...] + contrib)

    seq_len = g.shape[ax]
    if whole_seqs:
        so_ref[...] = g[:, seq_len - 2:seq_len, :]
    else:
        so_ref[0] = g[seq_len - 2:seq_len, :]


def _conv_ffn(x, g_norm, w_gate, w_up, conv_w, conv_b, w_down, state, seq_len, tm, g_final, final):
    t, d = x.shape
    nseq = t // seq_len
    nf = D_FF // FFN_TF
    assert nf > 1
    if seq_len >= tm:
        tiles_per_seq = seq_len // tm
        st_spec = pl.BlockSpec((1, 2, FFN_TF), lambda i, j: (i // tiles_per_seq, 0, j))
    else:
        tiles_per_seq = 0
        st_spec = pl.BlockSpec((tm // seq_len, 2, FFN_TF), lambda i, j: (i, 0, j))
    prev_blocks = tm // FFN_PREV
    tail_spec = pl.BlockSpec(st_spec.block_shape, lambda i, j: (i, 0, j))
    n_tail = t // tm * st_spec.block_shape[0]
    y, tails = pl.pallas_call(
        functools.partial(_ffn_kernel, tiles_per_seq, final),
        grid=(t // tm, nf),
        in_specs=[pl.BlockSpec((tm, d), lambda i, j: (i, 0)),
                  pl.BlockSpec((FFN_PREV, d), lambda i, j: (jnp.maximum(i * prev_blocks - 1, 0), 0)),
                  pl.BlockSpec((1, d), lambda i, j: (0, 0)),
                  pl.BlockSpec((d, FFN_TF), lambda i, j: (0, j)),
                  pl.BlockSpec((d, FFN_TF), lambda i, j: (0, j)),
                  pl.BlockSpec((3, FFN_TF), lambda i, j: (0, j)),
                  pl.BlockSpec((1, FFN_TF), lambda i, j: (0, j)),
                  pl.BlockSpec((FFN_TF, d), lambda i, j: (j, 0)),
                  st_spec,
                  pl.BlockSpec((1, d), lambda i, j: (0, 0))],
        out_specs=[pl.BlockSpec((tm, d), lambda i, j: (i, 0)), tail_spec],
        out_shape=[jax.ShapeDtypeStruct((t, d), F32),
                   jax.ShapeDtypeStruct((n_tail, 2, D_FF), F32)],
        scratch_shapes=[pltpu.VMEM((tm, d), BF16), pltpu.VMEM((FFN_PREV, d), BF16)],
        compiler_params=_cparams("arbitrary", "arbitrary"),
        name="conv_ffn_prompt" if tiles_per_seq else "conv_ffn_sample",
    )(x, x, g_norm.reshape(1, d), w_gate, w_up, conv_w, conv_b.reshape(1, D_FF), w_down, state, g_final.reshape(1, d))
    return y, tails[n_tail // nseq - 1::n_tail // nseq]


PROJ_TM = 1024
PROMPT_TN, SAMPLE_TN = ROW_COLS // 2, IN_COLS // 3
ROW_TM = 512


def _lam_init(layer):
    return 0.8 - 0.6 * math.exp(-0.3 * layer)


def _cols(p4, name):
    c0 = _CB[name] * LANES
    return p4[..., c0:c0 + BRANCH_W]


def _layer_tail(x2, o_a, o_b, o_c, layer, prm, conv_state, seq_len):
    flat = lambda a: a.reshape(x2.shape[0], BRANCH_W)
    x2 = _merge(x2, flat(o_a), flat(o_b), flat(o_c), prm["norm_mix"][layer], prm["w_gate_mix"][layer],
                prm["b_gate"][layer], prm["w_branch"][layer], prm["w_out"][layer], ROW_TM)
    depth = prm["norm_ffn"].shape[0]
    return _conv_ffn(x2, prm["norm_ffn"][layer], prm["w_gate"][layer], prm["w_up"][layer],
                     prm["conv_w"][layer], prm["conv_b"][layer], prm["w_down"][layer], conv_state, seq_len, ROW_TM,
                     prm["norm_final"], layer == depth - 1)


def _trunk_prompt(x, prm, abias, t5tiles, t5_flat):
    bn, s, d = x.shape
    x2 = x.reshape(bn * s, d)
    depth = prm["w_rows"].shape[0]
    conv_states = []
    rows, kv_t = None, None
    for l in range(depth):
        rows = _norm_proj(x2, prm["norm_mix"][l], prm["w_rows"][l], PROJ_TM, PROMPT_TN, l, depth, rows,
                          heads_group=(_CBP["vc"] * LANES, C_HEADS))
        p_rows, vc_heads = rows
        kv_t = _norm_proj_t(x2, prm["norm_mix"][l], prm["w_feat_t"][l], PROJ_TM, l, depth, bn, kv_t)
        ka_t, va_t, kb_t, vb_t, kc_t = kv_t
        pn4 = p_rows.reshape(depth, bn, s, ROW_COLS)
        o_a = _band_attn_prompt(pn4, l, ka_t, va_t, abias[l])
        o_b = _sb_attn_prompt(pn4, l, kb_t, vb_t)
        o_c = _diff_attn_prompt(pn4, l, kc_t, t5tiles, _lam_init(l), t5_flat, prm["c_lambda"][l], prm["c_subln"][l])
        x2, st_conv = _layer_tail(x2, o_a, o_b, o_c, l, prm, jnp.zeros((bn, 2, D_FF), F32), s)
        conv_states.append(st_conv)
    y = x2.reshape(bn, s, d)
    heads_t = lambda a: jnp.transpose(a.reshape(depth, bn, A_HEADS, HEAD_DIM, a.shape[-1]), (0, 1, 4, 2, 3))
    keep = min(A_PAST_CHUNKS * CHUNK, s)
    states = (heads_t(ka_t[..., s - keep:]), heads_t(va_t[..., s - keep:]), heads_t(kb_t), heads_t(vb_t),
              jnp.transpose(kc_t.reshape(depth, bn, C_HEADS, 2, HEAD_DIM, s), (0, 1, 5, 2, 3, 4)),
              vc_heads.reshape(depth, bn, s, C_HEADS, C_VDIM),
              jnp.stack(conv_states, axis=0))
    return y, states


def _feature_major(cache):
    nd = cache.ndim
    t = jnp.transpose(cache, (0, 1) + tuple(range(3, nd)) + (2,))
    return t.reshape(cache.shape[0], cache.shape[1], BRANCH_W, cache.shape[2])


def _trunk_sample(x, caches, prm, abias, t5tiles):
    bn, t, d = x.shape
    x2 = x.reshape(bn * t, d)
    depth = prm["w_in"].shape[0]
    ca_k, ca_v, cb_k, cb_v, cc_k, cc_v, st_conv_in = caches
    ca_kt, ca_vt, cb_kt, cb_vt, cc_kt = (_feature_major(c) for c in (ca_k, ca_v, cb_k, cb_v, cc_k))
    keep, past = ca_k.shape[2], cc_k.shape[2]
    cc_vr = cc_v.reshape(depth, bn, past * C_HEADS, C_VDIM)
    conv_states = []
    p_all = None
    for l in range(depth):
        p_all, = _norm_proj(x2, prm["norm_mix"][l], prm["w_in"][l], PROJ_TM, SAMPLE_TN, l, depth,
                            None if p_all is None else [p_all])
        p4 = p_all.reshape(depth, bn, t, IN_COLS)
        bias_cache = jnp.concatenate([abias[l][:, 0, :t, :], abias[l][:, 1, :t, :]], axis=-1)[..., -keep:]
        o_a = _band_attn_sample(p4, l, ca_kt, ca_vt, bias_cache, abias[l][:, 2, :t, :t])
        o_b = _sb_attn_sample(p4, l, cb_kt, cb_vt)
        far = jnp.broadcast_to(t5tiles[:, 1, :1, :1], (C_HEADS, t, past - ATT_BLK))
        bias_c = jnp.concatenate([far, t5tiles[:, 1, :t, :]], axis=-1)
        o_c = _diff_attn_sample(p4, l, cc_kt, cc_vr, bias_c, t5tiles[:, 0, :t, :t],
                                _lam_init(l), prm["c_lambda"][l], prm["c_subln"][l])
        x2, st_conv = _layer_tail(x2, o_a, o_b, o_c, l, prm, st_conv_in[l], t)
        conv_states.append(st_conv)
    y = x2.reshape(bn, t, d)
    kb, vb, kc, vc = (_cols(p4, n) for n in ("kb", "vb", "kc", "vc"))
    heads = lambda a: a.reshape(a.shape[:3] + (A_HEADS, HEAD_DIM))
    heads_t = lambda a: jnp.transpose(a.reshape(depth, bn, A_HEADS, HEAD_DIM, a.shape[-1]), (0, 1, 4, 2, 3))
    sa_kt, sa_vt = _band_roll(ca_kt, ca_vt, p4)
    states = (heads_t(sa_kt), heads_t(sa_vt), heads(kb), heads(vb),
              kc.reshape(depth, bn, t, C_HEADS, 2, HEAD_DIM), vc.reshape(depth, bn, t, C_HEADS, C_VDIM),
              jnp.stack(conv_states, axis=0))
    return y, states


def kernel(x_prompt, x_sample, cache_a_k, cache_a_v, cache_b_k, cache_b_v, cache_c_k, cache_c_v,
           state_ffn_conv, norm_mix, w_in, b_gate, a_rel_bias, t5_bias, c_lambda, c_subln,
           w_branch, w_out, norm_ffn, w_up, conv_w, conv_b, w_down, norm_final):
    group = {name: w_in[..., i * BRANCH_W:(i + 1) * BRANCH_W] for i, name in enumerate(_GROUPS)}
    gate = w_in[..., len(_GROUPS) * BRANCH_W:]
    prm = dict(
        norm_mix=norm_mix,
        w_in=w_in[..., :len(_GROUPS) * BRANCH_W].astype(BF16),
        w_rows=jnp.concatenate([group[n] for n in _ROW_GROUPS], axis=-1).astype(BF16),
        w_gate_mix=gate.astype(BF16),
        w_feat_t=jnp.swapaxes(jnp.concatenate([group[n] for n in _T_GROUPS], axis=-1), 1, 2).astype(BF16),
        b_gate=b_gate, c_lambda=c_lambda, c_subln=c_subln,
        w_branch=w_branch.astype(BF16), w_out=w_out.astype(BF16), norm_ffn=norm_ffn,
        w_gate=w_up[..., :D_FF].astype(BF16), w_up=w_up[..., D_FF:].astype(BF16),
        conv_w=conv_w, conv_b=conv_b, w_down=w_down.astype(BF16), norm_final=norm_final)
    abias = [_build_abias(a_rel_bias[l]) for l in range(a_rel_bias.shape[0])]
    t5tiles = _build_t5bias(t5_bias)
    y_p, p_states = _trunk_prompt(x_prompt, prm, abias, t5tiles, t5_bias.reshape(-1))
    caches = (cache_a_k, cache_a_v, cache_b_k, cache_b_v, cache_c_k, cache_c_v, state_ffn_conv)
    y_s, s_states = _trunk_sample(x_sample, caches, prm, abias, t5tiles)
    return (y_p, y_s) + p_states + s_states
```

```python
import functools
import math

import jax
import jax.numpy as jnp
from jax import lax
from jax.experimental import pallas as pl
from jax.experimental.pallas import tpu as pltpu

F32 = jnp.float32
BF16 = jnp.bfloat16

D_MODEL = 1024
CHUNK = 64
HEAD_DIM = 64
A_HEADS = 8
A_PAST_CHUNKS = 8
A_REL_CLIP = 128
B_HEADS = 8
C_HEADS = 4
C_VDIM = 128
T5_BUCKETS = 32
D_FF = 2816
N_BRANCH = 3
BRANCH_W = 512
IN_COLS = 9 * BRANCH_W
EPS = 1e-6
NEG_INF = -1e30
SCALE = HEAD_DIM ** -0.5
LOG2E = math.log2(math.e)
SOFTMAX_QSCALE = SCALE * LOG2E

LANES = 128
_GROUPS = ("qa", "ka", "va", "qb", "kb", "vb", "qc", "kc", "vc")
_CB = {name: i * BRANCH_W // LANES for i, name in enumerate(_GROUPS)}
_ROW_GROUPS = ("qa", "qb", "qc", "vc")
_T_GROUPS = ("ka", "va", "kb", "vb", "kc")
_CBP = {name: i * BRANCH_W // LANES for i, name in enumerate(_ROW_GROUPS)}
ROW_COLS = len(_ROW_GROUPS) * BRANCH_W

ATT_BLK = 256
ATT_HEADS = 4
SB_LOG_CUT = -104.0
T5_FAR_BUCKET = 15
STICK_SUB = 16
BAND_SUB = 16
DIFF_SUB = 16
DIFF_FAR_GROUP = 4
V7X_VMEM_BYTES = 64 * 1024 * 1024
VMEM_LIMIT = V7X_VMEM_BYTES - 8 * 1024 * 1024


def _cparams(*sem):
    return pltpu.CompilerParams(dimension_semantics=sem, vmem_limit_bytes=VMEM_LIMIT)


def _rms(x, g):
    return x * lax.rsqrt(jnp.mean(x * x, axis=-1, keepdims=True) + EPS) * g


def _dot(a, b):
    return jnp.dot(a, b, preferred_element_type=F32)


def _dot_nt(a, b):
    return lax.dot_general(a, b, (((1,), (1,)), ((), ())), preferred_element_type=F32)


def _norm_proj_kernel(n_stack, heads_cols, x_ref, g_ref, w_ref, *refs):
    outs, xn_ref = refs[n_stack:-1], refs[-1]
    j = pl.program_id(1)

    @pl.when(j == 0)
    def _():
        xn_ref[...] = _rms(x_ref[...], g_ref[...]).astype(BF16)

    res = _dot(xn_ref[...], w_ref[...])
    outs[0][...] = res
    if heads_cols is not None:
        tile, first, heads = heads_cols

        @pl.when(j == tile)
        def _():
            for h in range(heads):
                lo = first + h * LANES
                outs[1][pl.ds(h, res.shape[0], stride=heads), :] = res[:, lo:lo + LANES]


def _norm_proj(x, g, w, tm, tn, layer, depth, stacked, heads_group=None):
    t, d = x.shape
    n = w.shape[1]
    in_specs = [pl.BlockSpec((tm, d), lambda i, j: (i, 0)),
                pl.BlockSpec((1, d), lambda i, j: (0, 0)),
                pl.BlockSpec((d, tn), lambda i, j: (0, j))]
    args = [x, g.reshape(1, d), w]
    out_specs = [pl.BlockSpec((None, tm, tn), lambda i, j: (layer, i, j))]
    out_shape = [jax.ShapeDtypeStruct((depth, t, n), F32)]
    heads_cols = None
    if heads_group is not None:
        first, heads = heads_group
        heads_cols = (first // tn, first % tn, heads)
        out_specs.append(pl.BlockSpec((None, tm * heads, LANES), lambda i, j: (layer, i, 0)))
        out_shape.append(jax.ShapeDtypeStruct((depth, t * heads, LANES), F32))
    n_stack = 0 if stacked is None else len(out_shape)
    if stacked is not None:
        in_specs += [pl.BlockSpec(memory_space=pl.ANY)] * n_stack
        args += list(stacked)
    return pl.pallas_call(
        functools.partial(_norm_proj_kernel, n_stack, heads_cols),
        grid=(t // tm, n // tn),
        in_specs=in_specs,
        out_specs=out_specs,
        out_shape=out_shape,
        scratch_shapes=[pltpu.VMEM((tm, d), BF16)],
        input_output_aliases={3 + k: k for k in range(n_stack)},
        compiler_params=_cparams("arbitrary", "arbitrary"),
        name="norm_proj",
    )(*args)


def _norm_proj_t_kernel(ngroups, x_ref, g_ref, wt_ref, *refs):
    o_refs = refs[-ngroups:]
    xn = _rms(x_ref[...], g_ref[...]).astype(BF16)
    for g in range(ngroups):
        o_refs[g][...] = _dot_nt(wt_ref[g * BRANCH_W:(g + 1) * BRANCH_W, :], xn)


def _norm_proj_t(x, g, wt, tm, layer, depth, bn, stacked):
    t, d = x.shape
    s = t // bn
    ngroups = wt.shape[0] // BRANCH_W
    tiles = s // tm
    in_specs = [pl.BlockSpec((tm, d), lambda i: (i, 0)),
                pl.BlockSpec((1, d), lambda i: (0, 0)),
                pl.BlockSpec(wt.shape, lambda i: (0, 0))]
    args = [x, g.reshape(1, d), wt]
    aliases = {}
    if stacked is not None:
        in_specs += [pl.BlockSpec(memory_space=pl.ANY)] * ngroups
        args += list(stacked)
        aliases = {3 + n: n for n in range(ngroups)}
    out_spec = pl.BlockSpec((None, None, BRANCH_W, tm), lambda i: (layer, i // tiles, 0, i % tiles))
    return pl.pallas_call(
        functools.partial(_norm_proj_t_kernel, ngroups),
        grid=(t // tm,),
        in_specs=in_specs,
        out_specs=[out_spec] * ngroups,
        out_shape=[jax.ShapeDtypeStruct((depth, bn, BRANCH_W, s), F32)] * ngroups,
        input_output_aliases=aliases,
        compiler_params=_cparams("arbitrary"),
        name="norm_proj_t",
    )(*args)


def _abias_kernel(tbl_ref, o_ref):
    h = pl.program_id(0)
    sub = 8
    ql = lax.broadcasted_iota(jnp.int32, (ATT_BLK, ATT_BLK), 0)
    kl = lax.broadcasted_iota(jnp.int32, (ATT_BLK, ATT_BLK), 1)
    far = jnp.full((ATT_BLK, ATT_BLK), tbl_ref[2 * A_REL_CLIP * A_HEADS + h] * LOG2E, F32)
    o_ref[0, 0] = jnp.where((kl >> 6) >= (ql >> 6), far, NEG_INF)
    lead = ATT_BLK - sub
    b = lax.broadcasted_iota(jnp.int32, (sub, 2 * ATT_BLK), 0)
    x = lax.broadcasted_iota(jnp.int32, (sub, 2 * ATT_BLK), 1)
    for seg, off in ((1, ATT_BLK), (2, 0)):
        idx = jnp.clip(b + lead - x + off, -A_REL_CLIP, A_REL_CLIP) + A_REL_CLIP

        def body(r, acc, idx=idx):
            return jnp.where(idx == r, tbl_ref[r * A_HEADS + h] * LOG2E, acc)

        strip = lax.fori_loop(0, 2 * A_REL_CLIP + 1, body, jnp.zeros((sub, 2 * ATT_BLK), F32))
        for a in range(ATT_BLK // sub):
            o_ref[0, seg, a * sub:(a + 1) * sub, :] = strip[:, lead - a * sub:lead - a * sub + ATT_BLK]
    o_ref[0, 2] = jnp.where((kl >> 6) <= (ql >> 6), o_ref[0, 2], NEG_INF)


def _build_abias(table):
    return pl.pallas_call(
        _abias_kernel,
        grid=(A_HEADS,),
        in_specs=[pl.BlockSpec(memory_space=pltpu.SMEM)],
        out_specs=pl.BlockSpec((1, 3, ATT_BLK, ATT_BLK), lambda h: (h, 0, 0, 0)),
        out_shape=jax.ShapeDtypeStruct((A_HEADS, 3, ATT_BLK, ATT_BLK), F32),
        compiler_params=_cparams("parallel"),
        name="band_bias_tiles",
    )(table.reshape(-1))


_T5_LARGE_STEPS = (12, 16, 23, 32, 46, 64, 91)


def _t5bias_kernel(tbl_ref, o_ref):
    h = pl.program_id(0)
    rows = 64
    for seg, off in ((0, 0), (1, -ATT_BLK)):
        for rc in range(ATT_BLK // rows):
            ql = lax.broadcasted_iota(jnp.int32, (rows, ATT_BLK), 0) + rc * rows
            kl = lax.broadcasted_iota(jnp.int32, (rows, ATT_BLK), 1)
            rel = kl - ql + off
            n = jnp.abs(rel)
            large = jnp.full((rows, ATT_BLK), T5_BUCKETS // 4, jnp.int32)
            for th in _T5_LARGE_STEPS:
                large = large + jnp.where(n >= th, 1, 0)
            bucket = jnp.where(rel > 0, T5_BUCKETS // 2, 0) + jnp.where(n < T5_BUCKETS // 4, n, large)
            tile = jnp.zeros((rows, ATT_BLK), F32)
            for r in range(T5_BUCKETS):
                tile = jnp.where(bucket == r, tbl_ref[r * C_HEADS + h] * LOG2E, tile)
            if seg == 0:
                tile = jnp.where((kl >> 6) <= (ql >> 6), tile, NEG_INF)
            o_ref[0, seg, rc * rows:(rc + 1) * rows, :] = tile


def _build_t5bias(table):
    return pl.pallas_call(
        _t5bias_kernel,
        grid=(C_HEADS,),
        in_specs=[pl.BlockSpec(memory_space=pltpu.SMEM)],
        out_specs=pl.BlockSpec((1, 2, ATT_BLK, ATT_BLK), lambda h: (h, 0, 0, 0)),
        out_shape=jax.ShapeDtypeStruct((C_HEADS, 2, ATT_BLK, ATT_BLK), F32),
        compiler_params=_cparams("parallel"),
        name="t5_bias_tiles",
    )(table.reshape(-1))


def _pspec(block, index_map, layer):
    return pl.BlockSpec((None,) + block, lambda *g: (layer,) + tuple(index_map(*g)))


def _lane_tile(x, n):
    return x if n == 1 else jnp.concatenate([x] * n, axis=1)


def _stack_heads(q, width):
    group = lax.broadcasted_iota(jnp.int32, q.shape, 1) // width
    return jnp.concatenate([jnp.where(group == h, q, 0.0) for h in range(q.shape[1] // width)],
                           axis=0).astype(BF16)


def _unstack_heads(x, t, width):
    group = lax.broadcasted_iota(jnp.int32, (t, x.shape[1]), 1) // width
    out = jnp.zeros((t, x.shape[1]), x.dtype)
    for h in range(x.shape[1] // width):
        out = jnp.where(group == h, x[h * t:(h + 1) * t, :], out)
    return out


def _qk(q, k, k_t):
    return _dot(q, k) if k_t else _dot_nt(q, k)


def _pv(p, v, v_t):
    return _dot_nt(p, v) if v_t else _dot(p, v)


def _softmax_segments(qh, kb, vb, biases, k_t, v_t):
    s = [_qk(qh, k, t) + b for k, b, t in zip(kb, biases, k_t)]
    m = functools.reduce(jnp.maximum, [jnp.max(sj, axis=1, keepdims=True) for sj in s])
    p = [jnp.exp2(sj - m) for sj in s]
    l = functools.reduce(jnp.add, [jnp.sum(pj, axis=1, keepdims=True) for pj in p])
    acc = functools.reduce(jnp.add, [_pv(pj.astype(BF16), v, t) for pj, v, t in zip(p, vb, v_t)])
    return acc / l


def _band_kernel(q_ref, k_ref, v_ref, b_ref, o_ref):
    blk = ATT_BLK
    heads = b_ref.shape[0]
    n_sub = q_ref.shape[0] // blk

    def one_block(sub, carry):
        qi = pl.program_id(2) * n_sub + sub
        rows = pl.ds(pl.multiple_of(sub * blk, blk), blk)
        qs = _stack_heads(q_ref[rows, :] * SOFTMAX_QSCALE, HEAD_DIM)
        kb, vb, biases = [], [], []
        for seg, back in enumerate((2, 1, 0)):
            keys = pl.ds(pl.multiple_of(jnp.maximum(qi - back, 0) * blk, blk), blk)
            kb.append(k_ref[:, keys].astype(BF16))
            vb.append(v_ref[:, keys].astype(BF16))
            tile = b_ref[:, seg].reshape(heads * blk, blk)
            biases.append(tile if back == 0 else jnp.where(qi >= back, tile, NEG_INF))
        o = _softmax_segments(qs, kb, vb, biases, (True,) * 3, (True,) * 3)
        o_ref[rows, :] = _unstack_heads(o, blk, HEAD_DIM).astype(o_ref.dtype)
        return carry

    lax.fori_loop(0, n_sub, one_block, 0)


def _band_attn_prompt(pn4, layer, k_t, v_t, abias):
    _, bn, s, _ = pn4.shape
    width = ATT_HEADS * HEAD_DIM
    sub = math.gcd(s // ATT_BLK, BAND_SUB)
    blk = (None, sub * ATT_BLK, width)
    seq = pl.BlockSpec((None, None, width, s), lambda b, hg, g: (layer, b, hg, 0))
    return pl.pallas_call(
        _band_kernel,
        grid=(bn, A_HEADS // ATT_HEADS, s // (sub * ATT_BLK)),
        in_specs=[_pspec(blk, lambda b, hg, g: (b, g, _CBP["qa"] * LANES // width + hg), layer), seq, seq,
                  pl.BlockSpec((ATT_HEADS, 3, ATT_BLK, ATT_BLK), lambda b, hg, g: (hg, 0, 0, 0))],
        out_specs=pl.BlockSpec(blk, lambda b, hg, g: (b, g, hg)),
        out_shape=jax.ShapeDtypeStruct((bn, s, BRANCH_W), BF16),
        compiler_params=_cparams("parallel", "parallel", "arbitrary"),
        name="band_attn_prompt",
    )(pn4, k_t, v_t, abias)


def _band_sample_kernel(q_ref, kc_ref, kn_ref, vc_ref, vn_ref, bc_ref, bn_ref, o_ref):
    t = q_ref.shape[0]
    width = ATT_HEADS * HEAD_DIM
    q = q_ref[...] * SOFTMAX_QSCALE
    for g in range(q.shape[1] // width):
        cols = slice(g * width, (g + 1) * width)
        rows = slice(g * ATT_HEADS * t, (g + 1) * ATT_HEADS * t)
        qs = _stack_heads(q[:, cols], HEAD_DIM)
        kb = [kc_ref[cols, :].astype(BF16), kn_ref[:, cols].astype(BF16)]
        vb = [vc_ref[cols, :].astype(BF16), vn_ref[:, cols].astype(BF16)]
        o = _softmax_segments(qs, kb, vb, [bc_ref[rows, :], bn_ref[rows, :]], (True, False), (True, False))
        o_ref[:, cols] = _unstack_heads(o, t, HEAD_DIM).astype(o_ref.dtype)


def _band_attn_sample(p4, layer, k_cache_t, v_cache_t, bias_cache, bias_new):
    _, bn, t, _ = p4.shape
    keep = k_cache_t.shape[-1]
    new = lambda name: _pspec((None, t, BRANCH_W), lambda b: (b, 0, _CB[name] * LANES // BRANCH_W), layer)
    old = pl.BlockSpec((None, None, BRANCH_W, keep), lambda b: (layer, b, 0, 0))
    return pl.pallas_call(
        _band_sample_kernel,
        grid=(bn,),
        in_specs=[new("qa"), old, new("ka"), old, new("va"),
                  pl.BlockSpec((A_HEADS * t, keep), lambda b: (0, 0)),
                  pl.BlockSpec((A_HEADS * t, t), lambda b: (0, 0))],
        out_specs=pl.BlockSpec((None, t, BRANCH_W), lambda b: (b, 0, 0)),
        out_shape=jax.ShapeDtypeStruct((bn, t, BRANCH_W), BF16),
        compiler_params=_cparams("parallel"),
        name="band_attn_sample",
    )(p4, k_cache_t, p4, v_cache_t, p4, bias_cache.reshape(A_HEADS * t, keep), bias_new.reshape(A_HEADS * t, t))


def _band_roll_kernel(kc_ref, vc_ref, kn_ref, vn_ref, ko_ref, vo_ref):
    t = kn_ref.shape[0]
    keep = kc_ref.shape[1]
    for c_ref, n_ref, o_ref in ((kc_ref, kn_ref, ko_ref), (vc_ref, vn_ref, vo_ref)):
        rolled = pltpu.roll(c_ref[...], keep - t, 1)
        new = jnp.concatenate([n_ref[...], jnp.zeros((LANES - t, n_ref.shape[1]), F32)], axis=0)
        new_t = pltpu.roll(new.T, LANES - t, 1)
        lane = lax.broadcasted_iota(jnp.int32, new_t.shape, 1)
        last = jnp.where(lane >= LANES - t, new_t, rolled[:, keep - LANES:])
        o_ref[...] = jnp.concatenate([rolled[:, :keep - LANES], last], axis=1)


def _band_roll(k_cache_t, v_cache_t, p4):
    depth, bn, t, _ = p4.shape
    keep = k_cache_t.shape[-1]
    assert t <= LANES <= keep
    old = pl.BlockSpec((None, None, BRANCH_W, keep), lambda l, b: (l, b, 0, 0))
    new = lambda name: pl.BlockSpec((None, None, t, BRANCH_W), lambda l, b: (l, b, 0, _CB[name] * LANES // BRANCH_W))
    return pl.pallas_call(
        _band_roll_kernel,
        grid=(depth, bn),
        in_specs=[old, old, new("ka"), new("va")],
        out_specs=[old, old],
        out_shape=[jax.ShapeDtypeStruct(k_cache_t.shape, F32)] * 2,
        compiler_params=_cparams("parallel", "parallel"),
        name="band_roll",
    )(k_cache_t, v_cache_t, p4, p4)


def _strict_upper(n):
    r = lax.broadcasted_iota(jnp.int32, (n, n), 0)
    c = lax.broadcasted_iota(jnp.int32, (n, n), 1)
    return jnp.where(r > c, 1.0, 0.0).astype(BF16)


def _sb_fold(qh, kb, vb, kv_t, upper, mask, carry_ref, acc_ref, first):
    tq = qh[0].shape[0]
    tk = kb[0].shape[1] if kv_t else kb[0].shape[0]
    cw = upper.shape[0]
    top = None
    for h in range(len(qh)):
        z = _qk(qh[h], kb[0], kv_t)
        soft = jnp.log(1.0 + jnp.exp(-jnp.abs(z)))
        log_keep = -(jnp.maximum(z, 0.0) + soft)
        log_take = jnp.minimum(z, 0.0) - soft
        if mask is not None:
            log_keep = jnp.where(mask, log_keep, 0.0)
        hi = log_keep.astype(BF16)
        lo = (log_keep - hi.astype(F32)).astype(BF16)
        pieces, total = [], None
        for c0 in reversed(range(0, tk, cw)):
            piece = _dot(hi[:, c0:c0 + cw], upper) + _dot(lo[:, c0:c0 + cw], upper)
            pieces.insert(0, piece if total is None else piece + total)
            part = jnp.sum(log_keep[:, c0:c0 + cw], axis=1, keepdims=True)
            total = part if total is None else total + part
        after = pieces[0] if len(pieces) == 1 else jnp.concatenate(pieces, axis=1)
        if not first:
            after = after + _lane_tile(carry_ref[h], tk // LANES)
        w = jnp.exp(log_take + after)
        if mask is not None:
            w = jnp.where(mask, w, 0.0)
        pv = _pv(w.astype(BF16), vb[0], kv_t)
        if first:
            acc_ref[h] = pv
            carry = jnp.broadcast_to(total, (tq, LANES))
        else:
            acc_ref[h] += pv
            carry = carry_ref[h] + total
        carry_ref[h] = carry
        top = jnp.max(carry) if top is None else jnp.maximum(top, jnp.max(carry))
    return (top > SB_LOG_CUT).astype(jnp.int32)


def _sb_kernel(tk, n_past_static, own_rows, q_ref, *refs):
    if own_rows:
        kd_ref, vd_ref, kp_ref, vp_ref, o_ref, carry_ref, acc_ref = refs
        tq = q_ref.shape[0]
    else:
        kp_ref, vp_ref, o_ref, carry_ref, acc_ref = refs
        tq = ATT_BLK
    n_sub = q_ref.shape[0] // tq
    rq = lax.broadcasted_iota(jnp.int32, (tq, tq), 0)
    ck = lax.broadcasted_iota(jnp.int32, (tq, tq), 1)
    causal = jnp.concatenate([ck < rq] * (q_ref.shape[1] // HEAD_DIM), axis=0)

    def one_block(sub, carry_unused):
        n_past = pl.program_id(2) * n_sub + sub if n_past_static is None else n_past_static
        rows = pl.ds(pl.multiple_of(sub * tq, tq), tq)
        qh = [_stack_heads(q_ref[rows, :] * SCALE, HEAD_DIM)]
        if own_rows:
            kd, vd = [kd_ref[...].astype(BF16)], [vd_ref[...].astype(BF16)]
            live = _sb_fold(qh, kd, vd, False, _strict_upper(tq), causal, carry_ref, acc_ref, True)
            n_left = n_past
        else:
            own = pl.ds(pl.multiple_of(n_past * tq, tq), tq)
            near = pl.ds(pl.multiple_of(jnp.maximum(n_past - 1, 0) * tq, tq), tq)
            kd = [jnp.concatenate([kp_ref[:, near], kp_ref[:, own]], axis=1).astype(BF16)]
            vd = [jnp.concatenate([vp_ref[:, near], vp_ref[:, own]], axis=1).astype(BF16)]
            mask = jnp.concatenate([jnp.broadcast_to(n_past >= 1, causal.shape), causal], axis=1)
            live = _sb_fold(qh, kd, vd, True, _strict_upper(tq), mask, carry_ref, acc_ref, True)
            n_left = n_past - 1

        def cond(st):
            return jnp.logical_and(st[0] >= 0, st[1] > 0)

        def body(st):
            start = pl.multiple_of(st[0] * tk, tk)
            kp, vp = [kp_ref[:, pl.ds(start, tk)].astype(BF16)], [vp_ref[:, pl.ds(start, tk)].astype(BF16)]
            live = _sb_fold(qh, kp, vp, True, _strict_upper(tk), None, carry_ref, acc_ref, False)
            return st[0] - 1, live

        lax.while_loop(cond, body, (n_left - 1, live))
        o_ref[rows, :] = _unstack_heads(acc_ref[0], tq, HEAD_DIM).astype(o_ref.dtype)
        return carry_unused

    lax.fori_loop(0, n_sub, one_block, 0)


def _sb_attn_prompt(pn4, layer, k_t, v_t):
    _, bn, s, _ = pn4.shape
    width = ATT_HEADS * HEAD_DIM
    sub = math.gcd(s // ATT_BLK, STICK_SUB)
    blk = (None, sub * ATT_BLK, width)
    seq = pl.BlockSpec((None, None, width, s), lambda b, hg, g: (layer, b, hg, 0))
    rows = ATT_HEADS * ATT_BLK
    return pl.pallas_call(
        functools.partial(_sb_kernel, ATT_BLK, None, False),
        grid=(bn, B_HEADS // ATT_HEADS, s // (sub * ATT_BLK)),
        in_specs=[_pspec(blk, lambda b, hg, g: (b, g, _CBP["qb"] * LANES // width + hg), layer), seq, seq],
        out_specs=pl.BlockSpec(blk, lambda b, hg, g: (b, g, hg)),
        out_shape=jax.ShapeDtypeStruct((bn, s, BRANCH_W), BF16),
        scratch_shapes=[pltpu.VMEM((1, rows, LANES), F32), pltpu.VMEM((1, rows, width), F32)],
        compiler_params=_cparams("parallel", "parallel", "arbitrary"),
        name="stick_attn_prompt",
    )(pn4, k_t, v_t)


def _sb_attn_sample(p4, layer, k_cache_t, v_cache_t):
    _, bn, t, _ = p4.shape
    past = k_cache_t.shape[-1]
    new = lambda name: _pspec((None, t, BRANCH_W), lambda b: (b, 0, _CB[name] * LANES // BRANCH_W), layer)
    older = pl.BlockSpec((None, None, BRANCH_W, past), lambda b: (layer, b, 0, 0))
    rows = B_HEADS * t
    return pl.pallas_call(
        functools.partial(_sb_kernel, ATT_BLK, past // ATT_BLK, True),
        grid=(bn,),
        in_specs=[new("qb"), new("kb"), new("vb"), older, older],
        out_specs=pl.BlockSpec((None, t, BRANCH_W), lambda b: (b, 0, 0)),
        out_shape=jax.ShapeDtypeStruct((bn, t, BRANCH_W), BF16),
        scratch_shapes=[pltpu.VMEM((1, rows, LANES), F32), pltpu.VMEM((1, rows, BRANCH_W), F32)],
        compiler_params=_cparams("parallel"),
        name="stick_attn_sample",
    )(p4, p4, p4, k_cache_t, v_cache_t)


def _diff_combine(o0, o1, lam_init, lam_ref, gain_ref):
    lp = lam_ref[...]
    lam = (jnp.exp(jnp.sum(lp[0:1] * lp[1:2], axis=1, keepdims=True))
           - jnp.exp(jnp.sum(lp[2:3] * lp[3:4], axis=1, keepdims=True)) + lam_init)
    return _rms(o0 - lam * o1, gain_ref[...]) * (1.0 - lam_init)


def _diff_fold(qs, k_t, v, bias, m_ref, l_ref, acc_ref, first):
    kb = k_t.astype(BF16)
    vb = v.astype(BF16)
    rows, tk = qs.shape[0], vb.shape[0]
    s = _dot(qs, kb) + bias
    m_cur = jnp.max(s, axis=1, keepdims=True)
    m_new = jnp.broadcast_to(m_cur, (rows, LANES)) if first else jnp.maximum(m_ref[...], m_cur)
    p = jnp.exp2(s - _lane_tile(m_new, tk // LANES))
    row = jnp.sum(p, axis=1, keepdims=True)
    pv = _dot(p.astype(BF16), vb)
    if first:
        l_ref[...] = jnp.broadcast_to(row, (rows, LANES))
        acc_ref[...] = pv
    else:
        alpha = jnp.exp2(m_ref[...] - m_new)
        l_ref[...] = alpha * l_ref[...] + row
        acc_ref[...] = alpha * acc_ref[...] + pv
    m_ref[...] = m_new


def _diff_prompt_kernel(lam_init, tbl_ref, lam_ref, gain_ref, q_ref, k_ref, v_ref, bd_ref, bp_ref, o_ref,
                        m_ref, l_ref, acc_ref):
    h = pl.program_id(1)
    blk = ATT_BLK
    n_sub = q_ref.shape[0] // blk
    far = tbl_ref[T5_FAR_BUCKET * C_HEADS + h] * LOG2E
    wide = DIFF_FAR_GROUP * blk

    def one_block(sub, carry):
        qi = pl.program_id(2) * n_sub + sub
        rows = pl.ds(pl.multiple_of(sub * blk, blk), blk)
        qm = _stack_heads(q_ref[rows, :] * SOFTMAX_QSCALE, HEAD_DIM)

        own = pl.multiple_of(qi * blk, blk)
        near = pl.multiple_of(jnp.maximum(qi - 1, 0) * blk, blk)
        k0 = jnp.concatenate([k_ref[:, pl.ds(near, blk)], k_ref[:, pl.ds(own, blk)]], axis=1)
        v0 = jnp.concatenate([v_ref[pl.ds(near, blk), :], v_ref[pl.ds(own, blk), :]], axis=0)
        b0 = jnp.concatenate([jnp.where(qi >= 1, bp_ref[...], NEG_INF), bd_ref[...]], axis=1)
        b0 = jnp.concatenate([b0, b0], axis=0)
        _diff_fold(qm, k0, v0, b0, m_ref, l_ref, acc_ref, True)

        n_far = jnp.maximum(qi - 1, 0)
        n_group = n_far // DIFF_FAR_GROUP

        def body(j, c):
            start = pl.multiple_of(j * wide, wide)
            _diff_fold(qm, k_ref[:, pl.ds(start, wide)], v_ref[pl.ds(start, wide), :], far,
                       m_ref, l_ref, acc_ref, False)
            return c

        lax.fori_loop(0, n_group, body, 0)
        rest = pl.multiple_of(n_group * wide, wide)
        for r in range(1, DIFF_FAR_GROUP):
            @pl.when(n_far - n_group * DIFF_FAR_GROUP == r)
            def _(r=r):
                _diff_fold(qm, k_ref[:, pl.ds(rest, r * blk)], v_ref[pl.ds(rest, r * blk), :], far,
                           m_ref, l_ref, acc_ref, False)

        o = acc_ref[...] / l_ref[...]
        o_ref[rows, :] = _diff_combine(o[:blk], o[blk:], lam_init, lam_ref, gain_ref).astype(o_ref.dtype)
        return carry

    lax.fori_loop(0, n_sub, one_block, 0)


def _diff_sample_kernel(lam_init, lam_ref, gain_ref, q_ref, kc_ref, vc_ref, kn_ref, vn_ref,
                        bc_ref, bn_ref, o_ref):
    q_all = q_ref[...] * SOFTMAX_QSCALE
    past = kc_ref.shape[1]
    k_t, v_t = (True, False), (False, False)
    for h in range(C_HEADS):
        cols = slice(h * LANES, (h + 1) * LANES)
        q = q_all[:, cols]
        kb = [kc_ref[cols, :].astype(BF16), kn_ref[:, cols].astype(BF16)]
        vb = [vc_ref[pl.ds(h, past, stride=C_HEADS), :].astype(BF16), vn_ref[:, cols].astype(BF16)]
        biases = [jnp.concatenate([b[h], b[h]], axis=0) for b in (bc_ref, bn_ref)]
        t = q.shape[0]
        o = _softmax_segments(_stack_heads(q, HEAD_DIM), kb, vb, biases, k_t, v_t)
        o_ref[:, cols] = _diff_combine(o[:t], o[t:], lam_init, lam_ref, gain_ref).astype(o_ref.dtype)


def _diff_attn_prompt(pn4, layer, k_t, t5tiles, lam_init, t5_flat, lam_params, gain):
    _, bn, s, _ = pn4.shape
    sub = math.gcd(s // ATT_BLK, DIFF_SUB)
    blk = (None, sub * ATT_BLK, LANES)
    full = (None, s, LANES)
    state = pltpu.VMEM((2 * ATT_BLK, LANES), F32)
    return pl.pallas_call(
        functools.partial(_diff_prompt_kernel, lam_init),
        grid=(bn, C_HEADS, s // (sub * ATT_BLK)),
        in_specs=[pl.BlockSpec(memory_space=pltpu.SMEM),
                  pl.BlockSpec((4, HEAD_DIM), lambda b, h, g: (0, 0)),
                  pl.BlockSpec((1, C_VDIM), lambda b, h, g: (0, 0)),
                  _pspec(blk, lambda b, h, g: (b, g, _CBP["qc"] + h), layer),
                  pl.BlockSpec((None, None, LANES, s), lambda b, h, g: (layer, b, h, 0)),
                  _pspec(full, lambda b, h, g: (b, 0, _CBP["vc"] + h), layer),
                  pl.BlockSpec((None, None, ATT_BLK, ATT_BLK), lambda b, h, g: (h, 0, 0, 0)),
                  pl.BlockSpec((None, None, ATT_BLK, ATT_BLK), lambda b, h, g: (h, 1, 0, 0))],
        out_specs=pl.BlockSpec(blk, lambda b, h, g: (b, g, h)),
        out_shape=jax.ShapeDtypeStruct((bn, s, BRANCH_W), BF16),
        scratch_shapes=[state, state, state],
        compiler_params=_cparams("parallel", "parallel", "arbitrary"),
        name="diff_attn_prompt",
    )(t5_flat, lam_params, gain.reshape(1, C_VDIM), pn4, k_t, pn4, t5tiles, t5tiles)


def _diff_attn_sample(p4, layer, k_cache_t, v_cache, bias_cache, bias_new, lam_init, lam_params, gain):
    _, bn, t, _ = p4.shape
    past = k_cache_t.shape[-1]
    new = lambda name: _pspec((None, t, BRANCH_W), lambda b: (b, 0, _CB[name] * LANES // BRANCH_W), layer)
    return pl.pallas_call(
        functools.partial(_diff_sample_kernel, lam_init),
        grid=(bn,),
        in_specs=[pl.BlockSpec((4, HEAD_DIM), lambda b: (0, 0)),
                  pl.BlockSpec((1, C_VDIM), lambda b: (0, 0)),
                  new("qc"),
                  pl.BlockSpec((None, None, BRANCH_W, past), lambda b: (layer, b, 0, 0)),
                  pl.BlockSpec((None, None, past * C_HEADS, C_VDIM), lambda b: (layer, b, 0, 0)),
                  new("kc"), new("vc"),
                  pl.BlockSpec((C_HEADS, t, past), lambda b: (0, 0, 0)),
                  pl.BlockSpec((C_HEADS, t, t), lambda b: (0, 0, 0))],
        out_specs=pl.BlockSpec((None, t, BRANCH_W), lambda b: (b, 0, 0)),
        out_shape=jax.ShapeDtypeStruct((bn, t, BRANCH_W), BF16),
        compiler_params=_cparams("parallel"),
        name="diff_attn_sample",
    )(lam_params, gain.reshape(1, C_VDIM), p4, k_cache_t, v_cache, p4, p4, bias_cache, bias_new)


def _merge_kernel(x_ref, oa_ref, ob_ref, oc_ref, gm_ref, wg_ref, bg_ref, wb_ref, wo_ref, o_ref):
    x = x_ref[...]
    d = x.shape[1]
    xn = _rms(x, gm_ref[...]).astype(BF16)
    h = None
    for n, o_r in enumerate((oa_ref, ob_ref, oc_ref)):
        logits = _dot(xn, wg_ref[:, n * d:(n + 1) * d]) + bg_ref[n:n + 1, :]
        t = _dot(o_r[...], wb_ref[n]) / (1.0 + jnp.exp(-logits))
        h = t if h is None else h + t
    o_ref[...] = x + _dot(h.astype(BF16), wo_ref[...])


def _merge(x, o_a, o_b, o_c, g_mix, w_gate, b_gate, w_branch, w_out, tm):
    t, d = x.shape
    row = lambda i: (i, 0)
    fixed2 = lambda i: (0, 0)
    return pl.pallas_call(
        _merge_kernel,
        grid=(t // tm,),
        in_specs=[pl.BlockSpec((tm, d), row),
                  pl.BlockSpec((tm, BRANCH_W), row),
                  pl.BlockSpec((tm, BRANCH_W), row),
                  pl.BlockSpec((tm, BRANCH_W), row),
                  pl.BlockSpec((1, d), fixed2),
                  pl.BlockSpec((d, N_BRANCH * d), fixed2),
                  pl.BlockSpec((N_BRANCH, d), fixed2),
                  pl.BlockSpec((N_BRANCH, BRANCH_W, d), lambda i: (0, 0, 0)),
                  pl.BlockSpec((d, d), fixed2)],
        out_specs=pl.BlockSpec((tm, d), row),
        out_shape=jax.ShapeDtypeStruct((t, d), F32),
        compiler_params=_cparams("parallel"),
        name="gated_merge",
    )(x, o_a, o_b, o_c, g_mix.reshape(1, d), w_gate, b_gate, w_branch, w_out)


FFN_TF = D_FF // 2
FFN_PREV = 16


def _gelu_tanh(x):
    return x * (0.5 * (1.0 + jnp.tanh(math.sqrt(2.0 / math.pi) * (x + 0.044715 * (x * x * x)))))


def _ffn_kernel(tiles_per_seq, final, x_ref, xp_ref, gn_ref, wg_ref, wu_ref, cw_ref, cb_ref, wd_ref, st_ref, gf_ref,
                o_ref, so_ref, xn_ref, xpn_ref):
    i = pl.program_id(0)
    j = pl.program_id(1)
    whole_seqs = tiles_per_seq == 0

    @pl.when(j == 0)
    def _():
        xn_ref[...] = _rms(x_ref[...], gn_ref[...]).astype(BF16)
        if not whole_seqs:
            xpn_ref[...] = _rms(xp_ref[...], gn_ref[...]).astype(BF16)

    xn = xn_ref[...]
    tm = xn.shape[0]
    tf = wg_ref.shape[1]
    g = _dot(xn, wg_ref[...])
    u = _dot(xn, wu_ref[...])
    if whole_seqs:
        ns = st_ref.shape[0]
        st = st_ref[...]
        g = g.reshape(ns, tm // ns, tf)
        u = u.reshape(ns, tm // ns, tf)
        pm2, pm1 = st[:, 0:1, :], st[:, 1:2, :]
    else:
        gp = _dot(xpn_ref[...], wg_ref[...])
        first = (i % tiles_per_seq) == 0
        st = st_ref[0]
        pm2 = jnp.where(first, st[0:1, :], gp[FFN_PREV - 2:FFN_PREV - 1, :])
        pm1 = jnp.where(first, st[1:2, :], gp[FFN_PREV - 1:FFN_PREV, :])
    ax = g.ndim - 2
    row = lax.broadcasted_iota(jnp.int32, g.shape, ax)
    g1 = jnp.where(row == 0, pm1, pltpu.roll(g, 1, ax))
    g2 = jnp.where(row == 0, pm2, jnp.where(row == 1, pm1, pltpu.roll(g, 2, ax)))
    cw = cw_ref[...]
    c = cb_ref[...] + cw[0:1, :] * g2 + cw[1:2, :] * g1 + cw[2:3, :] * g
    hid = (_gelu_tanh(c) * u).reshape(tm, tf).astype(BF16)
    contrib = _dot(hid, wd_ref[...])

    last = pl.num_programs(1) - 1
    finish = (lambda y: _rms(y, gf_ref[...])) if final else (lambda y: y)

    @pl.when(j == 0)
    def _():
        o_ref[...] = x_ref[...] + contrib

    @pl.when(jnp.logical_and(j > 0, j < last))
    def _():
        o_ref[...] += contrib

    @pl.when(jnp.logical_and(j > 0, j == last))
    def _():
        o_ref[...] = finish(o_ref[...] + contrib)

    seq_len = g.shape[ax]
    if whole_seqs:
        so_ref[...] = g[:, seq_len - 2:seq_len, :]
    else:
        so_ref[0] = g[seq_len - 2:seq_len, :]


def _conv_ffn(x, g_norm, w_gate, w_up, conv_w, conv_b, w_down, state, seq_len, tm, g_final, final):
    t, d = x.shape
    nseq = t // seq_len
    nf = D_FF // FFN_TF
    assert nf > 1
    if seq_len >= tm:
        tiles_per_seq = seq_len // tm
        st_spec = pl.BlockSpec((1, 2, FFN_TF), lambda i, j: (i // tiles_per_seq, 0, j))
    else:
        tiles_per_seq = 0
        st_spec = pl.BlockSpec((tm // seq_len, 2, FFN_TF), lambda i, j: (i, 0, j))
    prev_blocks = tm // FFN_PREV
    tail_spec = pl.BlockSpec(st_spec.block_shape, lambda i, j: (i, 0, j))
    n_tail = t // tm * st_spec.block_shape[0]
    y, tails = pl.pallas_call(
        functools.partial(_ffn_kernel, tiles_per_seq, final),
        grid=(t // tm, nf),
        in_specs=[pl.BlockSpec((tm, d), lambda i, j: (i, 0)),
                  pl.BlockSpec((FFN_PREV, d), lambda i, j: (jnp.maximum(i * prev_blocks - 1, 0), 0)),
                  pl.BlockSpec((1, d), lambda i, j: (0, 0)),
                  pl.BlockSpec((d, FFN_TF), lambda i, j: (0, j)),
                  pl.BlockSpec((d, FFN_TF), lambda i, j: (0, j)),
                  pl.BlockSpec((3, FFN_TF), lambda i, j: (0, j)),
                  pl.BlockSpec((1, FFN_TF), lambda i, j: (0, j)),
                  pl.BlockSpec((FFN_TF, d), lambda i, j: (j, 0)),
                  st_spec,
                  pl.BlockSpec((1, d), lambda i, j: (0, 0))],
        out_specs=[pl.BlockSpec((tm, d), lambda i, j: (i, 0)), tail_spec],
        out_shape=[jax.ShapeDtypeStruct((t, d), F32),
                   jax.ShapeDtypeStruct((n_tail, 2, D_FF), F32)],
        scratch_shapes=[pltpu.VMEM((tm, d), BF16), pltpu.VMEM((FFN_PREV, d), BF16)],
        compiler_params=_cparams("arbitrary", "arbitrary"),
        name="conv_ffn_prompt" if tiles_per_seq else "conv_ffn_sample",
    )(x, x, g_norm.reshape(1, d), w_gate, w_up, conv_w, conv_b.reshape(1, D_FF), w_down, state, g_final.reshape(1, d))
    return y, tails[n_tail // nseq - 1::n_tail // nseq]


PROJ_TM = 1024
PROMPT_TN, SAMPLE_TN = ROW_COLS, IN_COLS // 3
ROW_TM = 512


def _lam_init(layer):
    return 0.8 - 0.6 * math.exp(-0.3 * layer)


def _cols(p4, name):
    c0 = _CB[name] * LANES
    return p4[..., c0:c0 + BRANCH_W]


def _layer_tail(x2, o_a, o_b, o_c, layer, prm, conv_state, seq_len):
    flat = lambda a: a.reshape(x2.shape[0], BRANCH_W)
    x2 = _merge(x2, flat(o_a), flat(o_b), flat(o_c), prm["norm_mix"][layer], prm["w_gate_mix"][layer],
                prm["b_gate"][layer], prm["w_branch"][layer], prm["w_out"][layer], ROW_TM)
    depth = prm["norm_ffn"].shape[0]
    return _conv_ffn(x2, prm["norm_ffn"][layer], prm["w_gate"][layer], prm["w_up"][layer],
                     prm["conv_w"][layer], prm["conv_b"][layer], prm["w_down"][layer], conv_state, seq_len, ROW_TM,
                     prm["norm_final"], layer == depth - 1)


def _trunk_prompt(x, prm, abias, t5tiles, t5_flat):
    bn, s, d = x.shape
    x2 = x.reshape(bn * s, d)
    depth = prm["w_rows"].shape[0]
    conv_states = []
    rows, kv_t = None, None
    for l in range(depth):
        rows = _norm_proj(x2, prm["norm_mix"][l], prm["w_rows"][l], PROJ_TM, PROMPT_TN, l, depth, rows,
                          heads_group=(_CBP["vc"] * LANES, C_HEADS))
        p_rows, vc_heads = rows
        kv_t = _norm_proj_t(x2, prm["norm_mix"][l], prm["w_feat_t"][l], PROJ_TM, l, depth, bn, kv_t)
        ka_t, va_t, kb_t, vb_t, kc_t = kv_t
        pn4 = p_rows.reshape(depth, bn, s, ROW_COLS)
        o_a = _band_attn_prompt(pn4, l, ka_t, va_t, abias[l])
        o_b = _sb_attn_prompt(pn4, l, kb_t, vb_t)
        o_c = _diff_attn_prompt(pn4, l, kc_t, t5tiles, _lam_init(l), t5_flat, prm["c_lambda"][l], prm["c_subln"][l])
        x2, st_conv = _layer_tail(x2, o_a, o_b, o_c, l, prm, jnp.zeros((bn, 2, D_FF), F32), s)
        conv_states.append(st_conv)
    y = x2.reshape(bn, s, d)
    heads_t = lambda a: jnp.transpose(a.reshape(depth, bn, A_HEADS, HEAD_DIM, a.shape[-1]), (0, 1, 4, 2, 3))
    keep = min(A_PAST_CHUNKS * CHUNK, s)
    states = (heads_t(ka_t[..., s - keep:]), heads_t(va_t[..., s - keep:]), heads_t(kb_t), heads_t(vb_t),
              jnp.transpose(kc_t.reshape(depth, bn, C_HEADS, 2, HEAD_DIM, s), (0, 1, 5, 2, 3, 4)),
              vc_heads.reshape(depth, bn, s, C_HEADS, C_VDIM),
              jnp.stack(conv_states, axis=0))
    return y, states


def _feature_major(cache):
    nd = cache.ndim
    t = jnp.transpose(cache, (0, 1) + tuple(range(3, nd)) + (2,))
    return t.reshape(cache.shape[0], cache.shape[1], BRANCH_W, cache.shape[2])


def _trunk_sample(x, caches, prm, abias, t5tiles):
    bn, t, d = x.shape
    x2 = x.reshape(bn * t, d)
    depth = prm["w_in"].shape[0]
    ca_k, ca_v, cb_k, cb_v, cc_k, cc_v, st_conv_in = caches
    ca_kt, ca_vt, cb_kt, cb_vt, cc_kt = (_feature_major(c) for c in (ca_k, ca_v, cb_k, cb_v, cc_k))
    keep, past = ca_k.shape[2], cc_k.shape[2]
    cc_vr = cc_v.reshape(depth, bn, past * C_HEADS, C_VDIM)
    conv_states = []
    p_all = None
    for l in range(depth):
        p_all, = _norm_proj(x2, prm["norm_mix"][l], prm["w_in"][l], PROJ_TM, SAMPLE_TN, l, depth,
                            None if p_all is None else [p_all])
        p4 = p_all.reshape(depth, bn, t, IN_COLS)
        bias_cache = jnp.concatenate([abias[l][:, 0, :t, :], abias[l][:, 1, :t, :]], axis=-1)[..., -keep:]
        o_a = _band_attn_sample(p4, l, ca_kt, ca_vt, bias_cache, abias[l][:, 2, :t, :t])
        o_b = _sb_attn_sample(p4, l, cb_kt, cb_vt)
        far = jnp.broadcast_to(t5tiles[:, 1, :1, :1], (C_HEADS, t, past - ATT_BLK))
        bias_c = jnp.concatenate([far, t5tiles[:, 1, :t, :]], axis=-1)
        o_c = _diff_attn_sample(p4, l, cc_kt, cc_vr, bias_c, t5tiles[:, 0, :t, :t],
                                _lam_init(l), prm["c_lambda"][l], prm["c_subln"][l])
        x2, st_conv = _layer_tail(x2, o_a, o_b, o_c, l, prm, st_conv_in[l], t)
        conv_states.append(st_conv)
    y = x2.reshape(bn, t, d)
    kb, vb, kc, vc = (_cols(p4, n) for n in ("kb", "vb", "kc", "vc"))
    heads = lambda a: a.reshape(a.shape[:3] + (A_HEADS, HEAD_DIM))
    heads_t = lambda a: jnp.transpose(a.reshape(depth, bn, A_HEADS, HEAD_DIM, a.shape[-1]), (0, 1, 4, 2, 3))
    sa_kt, sa_vt = _band_roll(ca_kt, ca_vt, p4)
    states = (heads_t(sa_kt), heads_t(sa_vt), heads(kb), heads(vb),
              kc.reshape(depth, bn, t, C_HEADS, 2, HEAD_DIM), vc.reshape(depth, bn, t, C_HEADS, C_VDIM),
              jnp.stack(conv_states, axis=0))
    return y, states


def kernel(x_prompt, x_sample, cache_a_k, cache_a_v, cache_b_k, cache_b_v, cache_c_k, cache_c_v,
           state_ffn_conv, norm_mix, w_in, b_gate, a_rel_bias, t5_bias, c_lambda, c_subln,
           w_branch, w_out, norm_ffn, w_up, conv_w, conv_b, w_down, norm_final):
    group = {name: w_in[..., i * BRANCH_W:(i + 1) * BRANCH_W] for i, name in enumerate(_GROUPS)}
    gate = w_in[..., len(_GROUPS) * BRANCH_W:]
    prm = dict(
        norm_mix=norm_mix,
        w_in=w_in[..., :len(_GROUPS) * BRANCH_W].astype(BF16),
        w_rows=jnp.concatenate([group[n] for n in _ROW_GROUPS], axis=-1).astype(BF16),
        w_gate_mix=gate.astype(BF16),
        w_feat_t=jnp.swapaxes(jnp.concatenate([group[n] for n in _T_GROUPS], axis=-1), 1, 2).astype(BF16),
        b_gate=b_gate, c_lambda=c_lambda, c_subln=c_subln,
        w_branch=w_branch.astype(BF16), w_out=w_out.astype(BF16), norm_ffn=norm_ffn,
        w_gate=w_up[..., :D_FF].astype(BF16), w_up=w_up[..., D_FF:].astype(BF16),
        conv_w=conv_w, conv_b=conv_b, w_down=w_down.astype(BF16), norm_final=norm_final)
    abias = [_build_abias(a_rel_bias[l]) for l in range(a_rel_bias.shape[0])]
    t5tiles = _build_t5bias(t5_bias)
    y_p, p_states = _trunk_prompt(x_prompt, prm, abias, t5tiles, t5_bias.reshape(-1))
    caches = (cache_a_k, cache_a_v, cache_b_k, cache_b_v, cache_c_k, cache_c_v, state_ffn_conv)
    y_s, s_states = _trunk_sample(x_sample, caches, prm, abias, t5tiles)
    return (y_p, y_s) + p_states + s_states
```

```python
import functools
import math

import jax
import jax.numpy as jnp
from jax import lax
from jax.experimental import pallas as pl
from jax.experimental.pallas import tpu as pltpu

F32 = jnp.float32
BF16 = jnp.bfloat16

D_MODEL = 1024
CHUNK = 64
HEAD_DIM = 64
A_HEADS = 8
A_PAST_CHUNKS = 8
A_REL_CLIP = 128
B_HEADS = 8
C_HEADS = 4
C_VDIM = 128
T5_BUCKETS = 32
D_FF = 2816
N_BRANCH = 3
BRANCH_W = 512
IN_COLS = 9 * BRANCH_W
EPS = 1e-6
NEG_INF = -1e30
SCALE = HEAD_DIM ** -0.5
LOG2E = math.log2(math.e)
SOFTMAX_QSCALE = SCALE * LOG2E

LANES = 128
_GROUPS = ("qa", "ka", "va", "qb", "kb", "vb", "qc", "kc", "vc")
_CB = {name: i * BRANCH_W // LANES for i, name in enumerate(_GROUPS)}
_ROW_GROUPS = ("qa", "qb", "qc", "vc")
_T_GROUPS = ("ka", "va", "kb", "vb", "kc")
_CBP = {name: i * BRANCH_W // LANES for i, name in enumerate(_ROW_GROUPS)}
ROW_COLS = len(_ROW_GROUPS) * BRANCH_W

ATT_BLK = 256
ATT_HEADS = 4
SB_LOG_CUT = -104.0
T5_FAR_BUCKET = 15
STICK_SUB = 16
BAND_SUB = 16
DIFF_SUB = 16
DIFF_FAR_GROUP = 4
V7X_VMEM_BYTES = 64 * 1024 * 1024
VMEM_LIMIT = V7X_VMEM_BYTES - 8 * 1024 * 1024


def _cparams(*sem):
    return pltpu.CompilerParams(dimension_semantics=sem, vmem_limit_bytes=VMEM_LIMIT)


def _small_cparams(*sem):
    return pltpu.CompilerParams(dimension_semantics=sem)


def _rms(x, g):
    return x * lax.rsqrt(jnp.mean(x * x, axis=-1, keepdims=True) + EPS) * g


def _dot(a, b):
    return jnp.dot(a, b, preferred_element_type=F32)


def _dot_nt(a, b):
    return lax.dot_general(a, b, (((1,), (1,)), ((), ())), preferred_element_type=F32)


def _norm_proj_kernel(n_stack, heads_cols, x_ref, g_ref, w_ref, *refs):
    outs, xn_ref = refs[n_stack:-1], refs[-1]
    j = pl.program_id(1)

    @pl.when(j == 0)
    def _():
        xn_ref[...] = _rms(x_ref[...], g_ref[...]).astype(BF16)

    res = _dot(xn_ref[...], w_ref[...])
    outs[0][...] = res
    if heads_cols is not None:
        tile, first, heads = heads_cols

        @pl.when(j == tile)
        def _():
            for h in range(heads):
                lo = first + h * LANES
                outs[1][pl.ds(h, res.shape[0], stride=heads), :] = res[:, lo:lo + LANES]


def _norm_proj(x, g, w, tm, tn, layer, depth, stacked, heads_group=None):
    t, d = x.shape
    n = w.shape[1]
    in_specs = [pl.BlockSpec((tm, d), lambda i, j: (i, 0)),
                pl.BlockSpec((1, d), lambda i, j: (0, 0)),
                pl.BlockSpec((d, tn), lambda i, j: (0, j))]
    args = [x, g.reshape(1, d), w]
    out_specs = [pl.BlockSpec((None, tm, tn), lambda i, j: (layer, i, j))]
    out_shape = [jax.ShapeDtypeStruct((depth, t, n), F32)]
    heads_cols = None
    if heads_group is not None:
        first, heads = heads_group
        heads_cols = (first // tn, first % tn, heads)
        out_specs.append(pl.BlockSpec((None, tm * heads, LANES), lambda i, j: (layer, i, 0)))
        out_shape.append(jax.ShapeDtypeStruct((depth, t * heads, LANES), F32))
    n_stack = 0 if stacked is None else len(out_shape)
    if stacked is not None:
        in_specs += [pl.BlockSpec(memory_space=pl.ANY)] * n_stack
        args += list(stacked)
    return pl.pallas_call(
        functools.partial(_norm_proj_kernel, n_stack, heads_cols),
        grid=(t // tm, n // tn),
        in_specs=in_specs,
        out_specs=out_specs,
        out_shape=out_shape,
        scratch_shapes=[pltpu.VMEM((tm, d), BF16)],
        input_output_aliases={3 + k: k for k in range(n_stack)},
        compiler_params=_cparams("arbitrary", "arbitrary"),
        name="norm_proj",
    )(*args)


def _norm_proj_t_kernel(ngroups, x_ref, g_ref, wt_ref, *refs):
    o_refs = refs[-ngroups:]
    xn = _rms(x_ref[...], g_ref[...]).astype(BF16)
    for g in range(ngroups):
        o_refs[g][...] = _dot_nt(wt_ref[g * BRANCH_W:(g + 1) * BRANCH_W, :], xn)


def _norm_proj_t(x, g, wt, tm, layer, depth, bn, stacked):
    t, d = x.shape
    s = t // bn
    ngroups = wt.shape[0] // BRANCH_W
    tiles = s // tm
    in_specs = [pl.BlockSpec((tm, d), lambda i: (i, 0)),
                pl.BlockSpec((1, d), lambda i: (0, 0)),
                pl.BlockSpec(wt.shape, lambda i: (0, 0))]
    args = [x, g.reshape(1, d), wt]
    aliases = {}
    if stacked is not None:
        in_specs += [pl.BlockSpec(memory_space=pl.ANY)] * ngroups
        args += list(stacked)
        aliases = {3 + n: n for n in range(ngroups)}
    out_spec = pl.BlockSpec((None, None, BRANCH_W, tm), lambda i: (layer, i // tiles, 0, i % tiles))
    return pl.pallas_call(
        functools.partial(_norm_proj_t_kernel, ngroups),
        grid=(t // tm,),
        in_specs=in_specs,
        out_specs=[out_spec] * ngroups,
        out_shape=[jax.ShapeDtypeStruct((depth, bn, BRANCH_W, s), F32)] * ngroups,
        input_output_aliases=aliases,
        compiler_params=_cparams("arbitrary"),
        name="norm_proj_t",
    )(*args)


def _abias_kernel(tbl_ref, o_ref):
    h = pl.program_id(0)
    sub = 8
    ql = lax.broadcasted_iota(jnp.int32, (ATT_BLK, ATT_BLK), 0)
    kl = lax.broadcasted_iota(jnp.int32, (ATT_BLK, ATT_BLK), 1)
    far = jnp.full((ATT_BLK, ATT_BLK), tbl_ref[2 * A_REL_CLIP * A_HEADS + h] * LOG2E, F32)
    o_ref[0, 0] = jnp.where((kl >> 6) >= (ql >> 6), far, NEG_INF)
    lead = ATT_BLK - sub
    b = lax.broadcasted_iota(jnp.int32, (sub, 2 * ATT_BLK), 0)
    x = lax.broadcasted_iota(jnp.int32, (sub, 2 * ATT_BLK), 1)
    for seg, off in ((1, ATT_BLK), (2, 0)):
        idx = jnp.clip(b + lead - x + off, -A_REL_CLIP, A_REL_CLIP) + A_REL_CLIP

        def body(r, acc, idx=idx):
            return jnp.where(idx == r, tbl_ref[r * A_HEADS + h] * LOG2E, acc)

        strip = lax.fori_loop(0, 2 * A_REL_CLIP + 1, body, jnp.zeros((sub, 2 * ATT_BLK), F32))
        for a in range(ATT_BLK // sub):
            o_ref[0, seg, a * sub:(a + 1) * sub, :] = strip[:, lead - a * sub:lead - a * sub + ATT_BLK]
    o_ref[0, 2] = jnp.where((kl >> 6) <= (ql >> 6), o_ref[0, 2], NEG_INF)


def _build_abias(table):
    return pl.pallas_call(
        _abias_kernel,
        grid=(A_HEADS,),
        in_specs=[pl.BlockSpec(memory_space=pltpu.SMEM)],
        out_specs=pl.BlockSpec((1, 3, ATT_BLK, ATT_BLK), lambda h: (h, 0, 0, 0)),
        out_shape=jax.ShapeDtypeStruct((A_HEADS, 3, ATT_BLK, ATT_BLK), F32),
        compiler_params=_small_cparams("parallel"),
        name="band_bias_tiles",
    )(table.reshape(-1))


_T5_LARGE_STEPS = (12, 16, 23, 32, 46, 64, 91)


def _t5bias_kernel(tbl_ref, o_ref):
    h = pl.program_id(0)
    rows = 64
    for seg, off in ((0, 0), (1, -ATT_BLK)):
        for rc in range(ATT_BLK // rows):
            ql = lax.broadcasted_iota(jnp.int32, (rows, ATT_BLK), 0) + rc * rows
            kl = lax.broadcasted_iota(jnp.int32, (rows, ATT_BLK), 1)
            rel = kl - ql + off
            n = jnp.abs(rel)
            large = jnp.full((rows, ATT_BLK), T5_BUCKETS // 4, jnp.int32)
            for th in _T5_LARGE_STEPS:
                large = large + jnp.where(n >= th, 1, 0)
            bucket = jnp.where(rel > 0, T5_BUCKETS // 2, 0) + jnp.where(n < T5_BUCKETS // 4, n, large)
            tile = jnp.zeros((rows, ATT_BLK), F32)
            for r in range(T5_BUCKETS):
                tile = jnp.where(bucket == r, tbl_ref[r * C_HEADS + h] * LOG2E, tile)
            if seg == 0:
                tile = jnp.where((kl >> 6) <= (ql >> 6), tile, NEG_INF)
            o_ref[0, seg, rc * rows:(rc + 1) * rows, :] = tile


def _build_t5bias(table):
    return pl.pallas_call(
        _t5bias_kernel,
        grid=(C_HEADS,),
        in_specs=[pl.BlockSpec(memory_space=pltpu.SMEM)],
        out_specs=pl.BlockSpec((1, 2, ATT_BLK, ATT_BLK), lambda h: (h, 0, 0, 0)),
        out_shape=jax.ShapeDtypeStruct((C_HEADS, 2, ATT_BLK, ATT_BLK), F32),
        compiler_params=_small_cparams("parallel"),
        name="t5_bias_tiles",
    )(table.reshape(-1))


def _pspec(block, index_map, layer):
    return pl.BlockSpec((None,) + block, lambda *g: (layer,) + tuple(index_map(*g)))


def _lane_tile(x, n):
    return x if n == 1 else jnp.concatenate([x] * n, axis=1)


def _stack_heads(q, width):
    group = lax.broadcasted_iota(jnp.int32, q.shape, 1) // width
    return jnp.concatenate([jnp.where(group == h, q, 0.0) for h in range(q.shape[1] // width)],
                           axis=0).astype(BF16)


def _unstack_heads(x, t, width):
    group = lax.broadcasted_iota(jnp.int32, (t, x.shape[1]), 1) // width
    out = jnp.zeros((t, x.shape[1]), x.dtype)
    for h in range(x.shape[1] // width):
        out = jnp.where(group == h, x[h * t:(h + 1) * t, :], out)
    return out


def _qk(q, k, k_t):
    return _dot(q, k) if k_t else _dot_nt(q, k)


def _pv(p, v, v_t):
    return _dot_nt(p, v) if v_t else _dot(p, v)


def _softmax_segments(qh, kb, vb, biases, k_t, v_t):
    s = [_qk(qh, k, t) + b for k, b, t in zip(kb, biases, k_t)]
    m = functools.reduce(jnp.maximum, [jnp.max(sj, axis=1, keepdims=True) for sj in s])
    p = [jnp.exp2(sj - m) for sj in s]
    l = functools.reduce(jnp.add, [jnp.sum(pj, axis=1, keepdims=True) for pj in p])
    acc = functools.reduce(jnp.add, [_pv(pj.astype(BF16), v, t) for pj, v, t in zip(p, vb, v_t)])
    return acc / l


def _band_kernel(q_ref, k_ref, v_ref, b_ref, o_ref):
    blk = ATT_BLK
    heads = b_ref.shape[0]
    n_sub = q_ref.shape[0] // blk

    def one_block(sub, carry):
        qi = pl.program_id(2) * n_sub + sub
        rows = pl.ds(pl.multiple_of(sub * blk, blk), blk)
        qs = _stack_heads(q_ref[rows, :] * SOFTMAX_QSCALE, HEAD_DIM)
        kb, vb, biases = [], [], []
        for seg, back in enumerate((2, 1, 0)):
            keys = pl.ds(pl.multiple_of(jnp.maximum(qi - back, 0) * blk, blk), blk)
            kb.append(k_ref[:, keys].astype(BF16))
            vb.append(v_ref[:, keys].astype(BF16))
            tile = b_ref[:, seg].reshape(heads * blk, blk)
            biases.append(tile if back == 0 else jnp.where(qi >= back, tile, NEG_INF))
        o = _softmax_segments(qs, kb, vb, biases, (True,) * 3, (True,) * 3)
        o_ref[rows, :] = _unstack_heads(o, blk, HEAD_DIM).astype(o_ref.dtype)
        return carry

    lax.fori_loop(0, n_sub, one_block, 0)


def _band_attn_prompt(pn4, layer, k_t, v_t, abias):
    _, bn, s, _ = pn4.shape
    width = ATT_HEADS * HEAD_DIM
    sub = math.gcd(s // ATT_BLK, BAND_SUB)
    blk = (None, sub * ATT_BLK, width)
    seq = pl.BlockSpec((None, None, width, s), lambda b, hg, g: (layer, b, hg, 0))
    return pl.pallas_call(
        _band_kernel,
        grid=(bn, A_HEADS // ATT_HEADS, s // (sub * ATT_BLK)),
        in_specs=[_pspec(blk, lambda b, hg, g: (b, g, _CBP["qa"] * LANES // width + hg), layer), seq, seq,
                  pl.BlockSpec((ATT_HEADS, 3, ATT_BLK, ATT_BLK), lambda b, hg, g: (hg, 0, 0, 0))],
        out_specs=pl.BlockSpec(blk, lambda b, hg, g: (b, g, hg)),
        out_shape=jax.ShapeDtypeStruct((bn, s, BRANCH_W), BF16),
        compiler_params=_cparams("parallel", "parallel", "arbitrary"),
        name="band_attn_prompt",
    )(pn4, k_t, v_t, abias)


def _band_sample_kernel(q_ref, kc_ref, kn_ref, vc_ref, vn_ref, bc_ref, bn_ref, o_ref):
    t = q_ref.shape[0]
    width = ATT_HEADS * HEAD_DIM
    q = q_ref[...] * SOFTMAX_QSCALE
    for g in range(q.shape[1] // width):
        cols = slice(g * width, (g + 1) * width)
        rows = slice(g * ATT_HEADS * t, (g + 1) * ATT_HEADS * t)
        qs = _stack_heads(q[:, cols], HEAD_DIM)
        kb = [kc_ref[cols, :].astype(BF16), kn_ref[:, cols].astype(BF16)]
        vb = [vc_ref[cols, :].astype(BF16), vn_ref[:, cols].astype(BF16)]
        o = _softmax_segments(qs, kb, vb, [bc_ref[rows, :], bn_ref[rows, :]], (True, False), (True, False))
        o_ref[:, cols] = _unstack_heads(o, t, HEAD_DIM).astype(o_ref.dtype)


def _band_attn_sample(p4, layer, k_cache_t, v_cache_t, bias_cache, bias_new):
    _, bn, t, _ = p4.shape
    keep = k_cache_t.shape[-1]
    new = lambda name: _pspec((None, t, BRANCH_W), lambda b: (b, 0, _CB[name] * LANES // BRANCH_W), layer)
    old = pl.BlockSpec((None, None, BRANCH_W, keep), lambda b: (layer, b, 0, 0))
    return pl.pallas_call(
        _band_sample_kernel,
        grid=(bn,),
        in_specs=[new("qa"), old, new("ka"), old, new("va"),
                  pl.BlockSpec((A_HEADS * t, keep), lambda b: (0, 0)),
                  pl.BlockSpec((A_HEADS * t, t), lambda b: (0, 0))],
        out_specs=pl.BlockSpec((None, t, BRANCH_W), lambda b: (b, 0, 0)),
        out_shape=jax.ShapeDtypeStruct((bn, t, BRANCH_W), BF16),
        compiler_params=_small_cparams("parallel"),
        name="band_attn_sample",
    )(p4, k_cache_t, p4, v_cache_t, p4, bias_cache.reshape(A_HEADS * t, keep), bias_new.reshape(A_HEADS * t, t))


def _band_roll_kernel(kc_ref, vc_ref, kn_ref, vn_ref, ko_ref, vo_ref):
    t = kn_ref.shape[0]
    keep = kc_ref.shape[1]
    for c_ref, n_ref, o_ref in ((kc_ref, kn_ref, ko_ref), (vc_ref, vn_ref, vo_ref)):
        rolled = pltpu.roll(c_ref[...], keep - t, 1)
        new = jnp.concatenate([n_ref[...], jnp.zeros((LANES - t, n_ref.shape[1]), F32)], axis=0)
        new_t = pltpu.roll(new.T, LANES - t, 1)
        lane = lax.broadcasted_iota(jnp.int32, new_t.shape, 1)
        last = jnp.where(lane >= LANES - t, new_t, rolled[:, keep - LANES:])
        o_ref[...] = jnp.concatenate([rolled[:, :keep - LANES], last], axis=1)


def _band_roll(k_cache_t, v_cache_t, p4):
    depth, bn, t, _ = p4.shape
    keep = k_cache_t.shape[-1]
    assert t <= LANES <= keep
    old = pl.BlockSpec((None, None, BRANCH_W, keep), lambda l, b: (l, b, 0, 0))
    new = lambda name: pl.BlockSpec((None, None, t, BRANCH_W), lambda l, b: (l, b, 0, _CB[name] * LANES // BRANCH_W))
    return pl.pallas_call(
        _band_roll_kernel,
        grid=(depth, bn),
        in_specs=[old, old, new("ka"), new("va")],
        out_specs=[old, old],
        out_shape=[jax.ShapeDtypeStruct(k_cache_t.shape, F32)] * 2,
        compiler_params=_small_cparams("parallel", "parallel"),
        name="band_roll",
    )(k_cache_t, v_cache_t, p4, p4)


def _strict_upper(n):
    r = lax.broadcasted_iota(jnp.int32, (n, n), 0)
    c = lax.broadcasted_iota(jnp.int32, (n, n), 1)
    return jnp.where(r > c, 1.0, 0.0).astype(BF16)


def _sb_fold(qh, kb, vb, kv_t, upper, mask, carry_ref, acc_ref, first):
    tq = qh[0].shape[0]
    tk = kb[0].shape[1] if kv_t else kb[0].shape[0]
    cw = upper.shape[0]
    top = None
    for h in range(len(qh)):
        z = _qk(qh[h], kb[0], kv_t)
        soft = jnp.log(1.0 + jnp.exp(-jnp.abs(z)))
        log_keep = -(jnp.maximum(z, 0.0) + soft)
        log_take = jnp.minimum(z, 0.0) - soft
        if mask is not None:
            log_keep = jnp.where(mask, log_keep, 0.0)
        hi = log_keep.astype(BF16)
        lo = (log_keep - hi.astype(F32)).astype(BF16)
        pieces, total = [], None
        for c0 in reversed(range(0, tk, cw)):
            piece = _dot(hi[:, c0:c0 + cw], upper) + _dot(lo[:, c0:c0 + cw], upper)
            pieces.insert(0, piece if total is None else piece + total)
            part = jnp.sum(log_keep[:, c0:c0 + cw], axis=1, keepdims=True)
            total = part if total is None else total + part
        after = pieces[0] if len(pieces) == 1 else jnp.concatenate(pieces, axis=1)
        if not first:
            after = after + _lane_tile(carry_ref[h], tk // LANES)
        w = jnp.exp(log_take + after)
        if mask is not None:
            w = jnp.where(mask, w, 0.0)
        pv = _pv(w.astype(BF16), vb[0], kv_t)
        if first:
            acc_ref[h] = pv
            carry = jnp.broadcast_to(total, (tq, LANES))
        else:
            acc_ref[h] += pv
            carry = carry_ref[h] + total
        carry_ref[h] = carry
        top = jnp.max(carry) if top is None else jnp.maximum(top, jnp.max(carry))
    return (top > SB_LOG_CUT).astype(jnp.int32)


def _sb_kernel(tk, n_past_static, own_rows, q_ref, *refs):
    if own_rows:
        kd_ref, vd_ref, kp_ref, vp_ref, o_ref, carry_ref, acc_ref = refs
        tq = q_ref.shape[0]
    else:
        kp_ref, vp_ref, o_ref, carry_ref, acc_ref = refs
        tq = ATT_BLK
    n_sub = q_ref.shape[0] // tq
    rq = lax.broadcasted_iota(jnp.int32, (tq, tq), 0)
    ck = lax.broadcasted_iota(jnp.int32, (tq, tq), 1)
    causal = jnp.concatenate([ck < rq] * (q_ref.shape[1] // HEAD_DIM), axis=0)

    def one_block(sub, carry_unused):
        n_past = pl.program_id(2) * n_sub + sub if n_past_static is None else n_past_static
        rows = pl.ds(pl.multiple_of(sub * tq, tq), tq)
        qh = [_stack_heads(q_ref[rows, :] * SCALE, HEAD_DIM)]
        if own_rows:
            kd, vd = [kd_ref[...].astype(BF16)], [vd_ref[...].astype(BF16)]
            live = _sb_fold(qh, kd, vd, False, _strict_upper(tq), causal, carry_ref, acc_ref, True)
            n_left = n_past
        else:
            own = pl.ds(pl.multiple_of(n_past * tq, tq), tq)
            near = pl.ds(pl.multiple_of(jnp.maximum(n_past - 1, 0) * tq, tq), tq)
            kd = [jnp.concatenate([kp_ref[:, near], kp_ref[:, own]], axis=1).astype(BF16)]
            vd = [jnp.concatenate([vp_ref[:, near], vp_ref[:, own]], axis=1).astype(BF16)]
            mask = jnp.concatenate([jnp.broadcast_to(n_past >= 1, causal.shape), causal], axis=1)
            live = _sb_fold(qh, kd, vd, True, _strict_upper(tq), mask, carry_ref, acc_ref, True)
            n_left = n_past - 1

        def cond(st):
            return jnp.logical_and(st[0] >= 0, st[1] > 0)

        def body(st):
            start = pl.multiple_of(st[0] * tk, tk)
            kp, vp = [kp_ref[:, pl.ds(start, tk)].astype(BF16)], [vp_ref[:, pl.ds(start, tk)].astype(BF16)]
            live = _sb_fold(qh, kp, vp, True, _strict_upper(tk), None, carry_ref, acc_ref, False)
            return st[0] - 1, live

        lax.while_loop(cond, body, (n_left - 1, live))
        o_ref[rows, :] = _unstack_heads(acc_ref[0], tq, HEAD_DIM).astype(o_ref.dtype)
        return carry_unused

    lax.fori_loop(0, n_sub, one_block, 0)


def _sb_attn_prompt(pn4, layer, k_t, v_t):
    _, bn, s, _ = pn4.shape
    width = ATT_HEADS * HEAD_DIM
    sub = math.gcd(s // ATT_BLK, STICK_SUB)
    blk = (None, sub * ATT_BLK, width)
    seq = pl.BlockSpec((None, None, width, s), lambda b, hg, g: (layer, b, hg, 0))
    rows = ATT_HEADS * ATT_BLK
    return pl.pallas_call(
        functools.partial(_sb_kernel, ATT_BLK, None, False),
        grid=(bn, B_HEADS // ATT_HEADS, s // (sub * ATT_BLK)),
        in_specs=[_pspec(blk, lambda b, hg, g: (b, g, _CBP["qb"] * LANES // width + hg), layer), seq, seq],
        out_specs=pl.BlockSpec(blk, lambda b, hg, g: (b, g, hg)),
        out_shape=jax.ShapeDtypeStruct((bn, s, BRANCH_W), BF16),
        scratch_shapes=[pltpu.VMEM((1, rows, LANES), F32), pltpu.VMEM((1, rows, width), F32)],
        compiler_params=_cparams("parallel", "parallel", "arbitrary"),
        name="stick_attn_prompt",
    )(pn4, k_t, v_t)


def _sb_attn_sample(p4, layer, k_cache_t, v_cache_t):
    _, bn, t, _ = p4.shape
    past = k_cache_t.shape[-1]
    new = lambda name: _pspec((None, t, BRANCH_W), lambda b: (b, 0, _CB[name] * LANES // BRANCH_W), layer)
    older = pl.BlockSpec((None, None, BRANCH_W, past), lambda b: (layer, b, 0, 0))
    rows = B_HEADS * t
    return pl.pallas_call(
        functools.partial(_sb_kernel, ATT_BLK, past // ATT_BLK, True),
        grid=(bn,),
        in_specs=[new("qb"), new("kb"), new("vb"), older, older],
        out_specs=pl.BlockSpec((None, t, BRANCH_W), lambda b: (b, 0, 0)),
        out_shape=jax.ShapeDtypeStruct((bn, t, BRANCH_W), BF16),
        scratch_shapes=[pltpu.VMEM((1, rows, LANES), F32), pltpu.VMEM((1, rows, BRANCH_W), F32)],
        compiler_params=_small_cparams("parallel"),
        name="stick_attn_sample",
    )(p4, p4, p4, k_cache_t, v_cache_t)


def _diff_combine(o0, o1, lam_init, lam_ref, gain_ref):
    lp = lam_ref[...]
    lam = (jnp.exp(jnp.sum(lp[0:1] * lp[1:2], axis=1, keepdims=True))
           - jnp.exp(jnp.sum(lp[2:3] * lp[3:4], axis=1, keepdims=True)) + lam_init)
    return _rms(o0 - lam * o1, gain_ref[...]) * (1.0 - lam_init)


def _diff_fold(qs, k_t, v, bias, m_ref, l_ref, acc_ref, first):
    kb = k_t.astype(BF16)
    vb = v.astype(BF16)
    rows, tk = qs.shape[0], vb.shape[0]
    s = _dot(qs, kb) + bias
    m_cur = jnp.max(s, axis=1, keepdims=True)
    m_new = jnp.broadcast_to(m_cur, (rows, LANES)) if first else jnp.maximum(m_ref[...], m_cur)
    p = jnp.exp2(s - _lane_tile(m_new, tk // LANES))
    row = jnp.sum(p, axis=1, keepdims=True)
    pv = _dot(p.astype(BF16), vb)
    if first:
        l_ref[...] = jnp.broadcast_to(row, (rows, LANES))
        acc_ref[...] = pv
    else:
        alpha = jnp.exp2(m_ref[...] - m_new)
        l_ref[...] = alpha * l_ref[...] + row
        acc_ref[...] = alpha * acc_ref[...] + pv
    m_ref[...] = m_new


def _diff_prompt_kernel(lam_init, tbl_ref, lam_ref, gain_ref, q_ref, k_ref, v_ref, bd_ref, bp_ref, o_ref,
                        m_ref, l_ref, acc_ref):
    h = pl.program_id(1)
    blk = ATT_BLK
    n_sub = q_ref.shape[0] // blk
    far = tbl_ref[T5_FAR_BUCKET * C_HEADS + h] * LOG2E
    wide = DIFF_FAR_GROUP * blk

    def one_block(sub, carry):
        qi = pl.program_id(2) * n_sub + sub
        rows = pl.ds(pl.multiple_of(sub * blk, blk), blk)
        qm = _stack_heads(q_ref[rows, :] * SOFTMAX_QSCALE, HEAD_DIM)

        own = pl.multiple_of(qi * blk, blk)
        near = pl.multiple_of(jnp.maximum(qi - 1, 0) * blk, blk)
        k0 = jnp.concatenate([k_ref[:, pl.ds(near, blk)], k_ref[:, pl.ds(own, blk)]], axis=1)
        v0 = jnp.concatenate([v_ref[pl.ds(near, blk), :], v_ref[pl.ds(own, blk), :]], axis=0)
        b0 = jnp.concatenate([jnp.where(qi >= 1, bp_ref[...], NEG_INF), bd_ref[...]], axis=1)
        b0 = jnp.concatenate([b0, b0], axis=0)
        _diff_fold(qm, k0, v0, b0, m_ref, l_ref, acc_ref, True)

        n_far = jnp.maximum(qi - 1, 0)
        n_group = n_far // DIFF_FAR_GROUP

        def body(j, c):
            start = pl.multiple_of(j * wide, wide)
            _diff_fold(qm, k_ref[:, pl.ds(start, wide)], v_ref[pl.ds(start, wide), :], far,
                       m_ref, l_ref, acc_ref, False)
            return c

        lax.fori_loop(0, n_group, body, 0)
        rest = pl.multiple_of(n_group * wide, wide)
        for r in range(1, DIFF_FAR_GROUP):
            @pl.when(n_far - n_group * DIFF_FAR_GROUP == r)
            def _(r=r):
                _diff_fold(qm, k_ref[:, pl.ds(rest, r * blk)], v_ref[pl.ds(rest, r * blk), :], far,
                           m_ref, l_ref, acc_ref, False)

        o = acc_ref[...] / l_ref[...]
        o_ref[rows, :] = _diff_combine(o[:blk], o[blk:], lam_init, lam_ref, gain_ref).astype(o_ref.dtype)
        return carry

    lax.fori_loop(0, n_sub, one_block, 0)


def _diff_sample_kernel(lam_init, lam_ref, gain_ref, q_ref, kc_ref, vc_ref, kn_ref, vn_ref,
                        bc_ref, bn_ref, o_ref):
    q_all = q_ref[...] * SOFTMAX_QSCALE
    past = kc_ref.shape[1]
    k_t, v_t = (True, False), (False, False)
    for h in range(C_HEADS):
        cols = slice(h * LANES, (h + 1) * LANES)
        q = q_all[:, cols]
        kb = [kc_ref[cols, :].astype(BF16), kn_ref[:, cols].astype(BF16)]
        vb = [vc_ref[pl.ds(h, past, stride=C_HEADS), :].astype(BF16), vn_ref[:, cols].astype(BF16)]
        biases = [jnp.concatenate([b[h], b[h]], axis=0) for b in (bc_ref, bn_ref)]
        t = q.shape[0]
        o = _softmax_segments(_stack_heads(q, HEAD_DIM), kb, vb, biases, k_t, v_t)
        o_ref[:, cols] = _diff_combine(o[:t], o[t:], lam_init, lam_ref, gain_ref).astype(o_ref.dtype)


def _diff_attn_prompt(pn4, layer, k_t, t5tiles, lam_init, t5_flat, lam_params, gain):
    _, bn, s, _ = pn4.shape
    sub = math.gcd(s // ATT_BLK, DIFF_SUB)
    blk = (None, sub * ATT_BLK, LANES)
    full = (None, s, LANES)
    state = pltpu.VMEM((2 * ATT_BLK, LANES), F32)
    return pl.pallas_call(
        functools.partial(_diff_prompt_kernel, lam_init),
        grid=(bn, C_HEADS, s // (sub * ATT_BLK)),
        in_specs=[pl.BlockSpec(memory_space=pltpu.SMEM),
                  pl.BlockSpec((4, HEAD_DIM), lambda b, h, g: (0, 0)),
                  pl.BlockSpec((1, C_VDIM), lambda b, h, g: (0, 0)),
                  _pspec(blk, lambda b, h, g: (b, g, _CBP["qc"] + h), layer),
                  pl.BlockSpec((None, None, LANES, s), lambda b, h, g: (layer, b, h, 0)),
                  _pspec(full, lambda b, h, g: (b, 0, _CBP["vc"] + h), layer),
                  pl.BlockSpec((None, None, ATT_BLK, ATT_BLK), lambda b, h, g: (h, 0, 0, 0)),
                  pl.BlockSpec((None, None, ATT_BLK, ATT_BLK), lambda b, h, g: (h, 1, 0, 0))],
        out_specs=pl.BlockSpec(blk, lambda b, h, g: (b, g, h)),
        out_shape=jax.ShapeDtypeStruct((bn, s, BRANCH_W), BF16),
        scratch_shapes=[state, state, state],
        compiler_params=_cparams("parallel", "parallel", "arbitrary"),
        name="diff_attn_prompt",
    )(t5_flat, lam_params, gain.reshape(1, C_VDIM), pn4, k_t, pn4, t5tiles, t5tiles)


def _diff_attn_sample(p4, layer, k_cache_t, v_cache, bias_cache, bias_new, lam_init, lam_params, gain):
    _, bn, t, _ = p4.shape
    past = k_cache_t.shape[-1]
    new = lambda name: _pspec((None, t, BRANCH_W), lambda b: (b, 0, _CB[name] * LANES // BRANCH_W), layer)
    return pl.pallas_call(
        functools.partial(_diff_sample_kernel, lam_init),
        grid=(bn,),
        in_specs=[pl.BlockSpec((4, HEAD_DIM), lambda b: (0, 0)),
                  pl.BlockSpec((1, C_VDIM), lambda b: (0, 0)),
                  new("qc"),
                  pl.BlockSpec((None, None, BRANCH_W, past), lambda b: (layer, b, 0, 0)),
                  pl.BlockSpec((None, None, past * C_HEADS, C_VDIM), lambda b: (layer, b, 0, 0)),
                  new("kc"), new("vc"),
                  pl.BlockSpec((C_HEADS, t, past), lambda b: (0, 0, 0)),
                  pl.BlockSpec((C_HEADS, t, t), lambda b: (0, 0, 0))],
        out_specs=pl.BlockSpec((None, t, BRANCH_W), lambda b: (b, 0, 0)),
        out_shape=jax.ShapeDtypeStruct((bn, t, BRANCH_W), BF16),
        compiler_params=_small_cparams("parallel"),
        name="diff_attn_sample",
    )(lam_params, gain.reshape(1, C_VDIM), p4, k_cache_t, v_cache, p4, p4, bias_cache, bias_new)


def _merge_kernel(x_ref, oa_ref, ob_ref, oc_ref, gm_ref, wg_ref, bg_ref, wb_ref, wo_ref, o_ref):
    x = x_ref[...]
    d = x.shape[1]
    xn = _rms(x, gm_ref[...]).astype(BF16)
    h = None
    for n, o_r in enumerate((oa_ref, ob_ref, oc_ref)):
        logits = _dot(xn, wg_ref[:, n * d:(n + 1) * d]) + bg_ref[n:n + 1, :]
        t = _dot(o_r[...], wb_ref[n]) / (1.0 + jnp.exp(-logits))
        h = t if h is None else h + t
    o_ref[...] = x + _dot(h.astype(BF16), wo_ref[...])


def _merge(x, o_a, o_b, o_c, g_mix, w_gate, b_gate, w_branch, w_out, tm):
    t, d = x.shape
    row = lambda i: (i, 0)
    fixed2 = lambda i: (0, 0)
    return pl.pallas_call(
        _merge_kernel,
        grid=(t // tm,),
        in_specs=[pl.BlockSpec((tm, d), row),
                  pl.BlockSpec((tm, BRANCH_W), row),
                  pl.BlockSpec((tm, BRANCH_W), row),
                  pl.BlockSpec((tm, BRANCH_W), row),
                  pl.BlockSpec((1, d), fixed2),
                  pl.BlockSpec((d, N_BRANCH * d), fixed2),
                  pl.BlockSpec((N_BRANCH, d), fixed2),
                  pl.BlockSpec((N_BRANCH, BRANCH_W, d), lambda i: (0, 0, 0)),
                  pl.BlockSpec((d, d), fixed2)],
        out_specs=pl.BlockSpec((tm, d), row),
        out_shape=jax.ShapeDtypeStruct((t, d), F32),
        compiler_params=_cparams("parallel"),
        name="gated_merge",
    )(x, o_a, o_b, o_c, g_mix.reshape(1, d), w_gate, b_gate, w_branch, w_out)


FFN_TF = D_FF // 2
FFN_PREV = 16


def _gelu_tanh(x):
    return x * (0.5 * (1.0 + jnp.tanh(math.sqrt(2.0 / math.pi) * (x + 0.044715 * (x * x * x)))))


def _ffn_kernel(tiles_per_seq, final, x_ref, xp_ref, gn_ref, wg_ref, wu_ref, cw_ref, cb_ref, wd_ref, st_ref, gf_ref,
                o_ref, so_ref, xn_ref, xpn_ref):
    i = pl.program_id(0)
    j = pl.program_id(1)
    whole_seqs = tiles_per_seq == 0

    @pl.when(j == 0)
    def _():
        xn_ref[...] = _rms(x_ref[...], gn_ref[...]).astype(BF16)
        if not whole_seqs:
            xpn_ref[...] = _rms(xp_ref[...], gn_ref[...]).astype(BF16)

    xn = xn_ref[...]
    tm = xn.shape[0]
    tf = wg_ref.shape[1]
    g = _dot(xn, wg_ref[...])
    u = _dot(xn, wu_ref[...])
    if whole_seqs:
        ns = st_ref.shape[0]
        st = st_ref[...]
        g = g.reshape(ns, tm // ns, tf)
        u = u.reshape(ns, tm // ns, tf)
        pm2, pm1 = st[:, 0:1, :], st[:, 1:2, :]
    else:
        gp = _dot(xpn_ref[...], wg_ref[...])
        first = (i % tiles_per_seq) == 0
        st = st_ref[0]
        pm2 = jnp.where(first, st[0:1, :], gp[FFN_PREV - 2:FFN_PREV - 1, :])
        pm1 = jnp.where(first, st[1:2, :], gp[FFN_PREV - 1:FFN_PREV, :])
    ax = g.ndim - 2
    row = lax.broadcasted_iota(jnp.int32, g.shape, ax)
    g1 = jnp.where(row == 0, pm1, pltpu.roll(g, 1, ax))
    g2 = jnp.where(row == 0, pm2, jnp.where(row == 1, pm1, pltpu.roll(g, 2, ax)))
    cw = cw_ref[...]
    c = cb_ref[...] + cw[0:1, :] * g2 + cw[1:2, :] * g1 + cw[2:3, :] * g
    hid = (_gelu_tanh(c) * u).reshape(tm, tf).astype(BF16)
    contrib = _dot(hid, wd_ref[...])

    last = pl.num_programs(1) - 1
    finish = (lambda y: _rms(y, gf_ref[...])) if final else (lambda y: y)

    @pl.when(j == 0)
    def _():
        o_ref[...] = x_ref[...] + contrib

    @pl.when(jnp.logical_and(j > 0, j < last))
    def _():
        o_ref[...] += contrib

    @pl.when(jnp.logical_and(j > 0, j == last))
    def _():
        o_ref[...] = finish(o_ref[...] + contrib)

    seq_len = g.shape[ax]
    if whole_seqs:
        so_ref[...] = g[:, seq_len - 2:seq_len, :]
    else:
        so_ref[0] = g[seq_len - 2:seq_len, :]


def _conv_ffn(x, g_norm, w_gate, w_up, conv_w, conv_b, w_down, state, seq_len, tm, g_final, final):
    t, d = x.shape
    nseq = t // seq_len
    nf = D_FF // FFN_TF
    assert nf > 1
    if seq_len >= tm:
        tiles_per_seq = seq_len // tm
        st_spec = pl.BlockSpec((1, 2, FFN_TF), lambda i, j: (i // tiles_per_seq, 0, j))
    else:
        tiles_per_seq = 0
        st_spec = pl.BlockSpec((tm // seq_len, 2, FFN_TF), lambda i, j: (i, 0, j))
    prev_blocks = tm // FFN_PREV
    tail_spec = pl.BlockSpec(st_spec.block_shape, lambda i, j: (i, 0, j))
    n_tail = t // tm * st_spec.block_shape[0]
    y, tails = pl.pallas_call(
        functools.partial(_ffn_kernel, tiles_per_seq, final),
        grid=(t // tm, nf),
        in_specs=[pl.BlockSpec((tm, d), lambda i, j: (i, 0)),
                  pl.BlockSpec((FFN_PREV, d), lambda i, j: (jnp.maximum(i * prev_blocks - 1, 0), 0)),
                  pl.BlockSpec((1, d), lambda i, j: (0, 0)),
                  pl.BlockSpec((d, FFN_TF), lambda i, j: (0, j)),
                  pl.BlockSpec((d, FFN_TF), lambda i, j: (0, j)),
                  pl.BlockSpec((3, FFN_TF), lambda i, j: (0, j)),
                  pl.BlockSpec((1, FFN_TF), lambda i, j: (0, j)),
                  pl.BlockSpec((FFN_TF, d), lambda i, j: (j, 0)),
                  st_spec,
                  pl.BlockSpec((1, d), lambda i, j: (0, 0))],
        out_specs=[pl.BlockSpec((tm, d), lambda i, j: (i, 0)), tail_spec],
        out_shape=[jax.ShapeDtypeStruct((t, d), F32),
                   jax.ShapeDtypeStruct((n_tail, 2, D_FF), F32)],
        scratch_shapes=[pltpu.VMEM((tm, d), BF16), pltpu.VMEM((FFN_PREV, d), BF16)],
        compiler_params=_cparams("arbitrary", "arbitrary"),
        name="conv_ffn_prompt" if tiles_per_seq else "conv_ffn_sample",
    )(x, x, g_norm.reshape(1, d), w_gate, w_up, conv_w, conv_b.reshape(1, D_FF), w_down, state, g_final.reshape(1, d))
    return y, tails[n_tail // nseq - 1::n_tail // nseq]


PROJ_TM = 1024
PROMPT_TN, SAMPLE_TN = ROW_COLS, IN_COLS // 3
ROW_TM = 512


def _lam_init(layer):
    return 0.8 - 0.6 * math.exp(-0.3 * layer)


def _cols(p4, name):
    c0 = _CB[name] * LANES
    return p4[..., c0:c0 + BRANCH_W]


def _layer_tail(x2, o_a, o_b, o_c, layer, prm, conv_state, seq_len):
    flat = lambda a: a.reshape(x2.shape[0], BRANCH_W)
    x2 = _merge(x2, flat(o_a), flat(o_b), flat(o_c), prm["norm_mix"][layer], prm["w_gate_mix"][layer],
                prm["b_gate"][layer], prm["w_branch"][layer], prm["w_out"][layer], ROW_TM)
    depth = prm["norm_ffn"].shape[0]
    return _conv_ffn(x2, prm["norm_ffn"][layer], prm["w_gate"][layer], prm["w_up"][layer],
                     prm["conv_w"][layer], prm["conv_b"][layer], prm["w_down"][layer], conv_state, seq_len, ROW_TM,
                     prm["norm_final"], layer == depth - 1)


def _trunk_prompt(x, prm, abias, t5tiles, t5_flat):
    bn, s, d = x.shape
    x2 = x.reshape(bn * s, d)
    depth = prm["w_rows"].shape[0]
    conv_states = []
    rows, kv_t = None, None
    for l in range(depth):
        rows = _norm_proj(x2, prm["norm_mix"][l], prm["w_rows"][l], PROJ_TM, PROMPT_TN, l, depth, rows,
                          heads_group=(_CBP["vc"] * LANES, C_HEADS))
        p_rows, vc_heads = rows
        kv_t = _norm_proj_t(x2, prm["norm_mix"][l], prm["w_feat_t"][l], PROJ_TM, l, depth, bn, kv_t)
        ka_t, va_t, kb_t, vb_t, kc_t = kv_t
        pn4 = p_rows.reshape(depth, bn, s, ROW_COLS)
        o_a = _band_attn_prompt(pn4, l, ka_t, va_t, abias[l])
        o_b = _sb_attn_prompt(pn4, l, kb_t, vb_t)
        o_c = _diff_attn_prompt(pn4, l, kc_t, t5tiles, _lam_init(l), t5_flat, prm["c_lambda"][l], prm["c_subln"][l])
        x2, st_conv = _layer_tail(x2, o_a, o_b, o_c, l, prm, jnp.zeros((bn, 2, D_FF), F32), s)
        conv_states.append(st_conv)
    y = x2.reshape(bn, s, d)
    heads_t = lambda a: jnp.transpose(a.reshape(depth, bn, A_HEADS, HEAD_DIM, a.shape[-1]), (0, 1, 4, 2, 3))
    keep = min(A_PAST_CHUNKS * CHUNK, s)
    states = (heads_t(ka_t[..., s - keep:]), heads_t(va_t[..., s - keep:]), heads_t(kb_t), heads_t(vb_t),
              jnp.transpose(kc_t.reshape(depth, bn, C_HEADS, 2, HEAD_DIM, s), (0, 1, 5, 2, 3, 4)),
              vc_heads.reshape(depth, bn, s, C_HEADS, C_VDIM),
              jnp.stack(conv_states, axis=0))
    return y, states


def _feature_major(cache):
    nd = cache.ndim
    t = jnp.transpose(cache, (0, 1) + tuple(range(3, nd)) + (2,))
    return t.reshape(cache.shape[0], cache.shape[1], BRANCH_W, cache.shape[2])


def _trunk_sample(x, caches, prm, abias, t5tiles):
    bn, t, d = x.shape
    x2 = x.reshape(bn * t, d)
    depth = prm["w_in"].shape[0]
    ca_k, ca_v, cb_k, cb_v, cc_k, cc_v, st_conv_in = caches
    ca_kt, ca_vt, cb_kt, cb_vt, cc_kt = (_feature_major(c) for c in (ca_k, ca_v, cb_k, cb_v, cc_k))
    keep, past = ca_k.shape[2], cc_k.shape[2]
    cc_vr = cc_v.reshape(depth, bn, past * C_HEADS, C_VDIM)
    conv_states = []
    p_all = None
    for l in range(depth):
        p_all, = _norm_proj(x2, prm["norm_mix"][l], prm["w_in"][l], PROJ_TM, SAMPLE_TN, l, depth,
                            None if p_all is None else [p_all])
        p4 = p_all.reshape(depth, bn, t, IN_COLS)
        bias_cache = jnp.concatenate([abias[l][:, 0, :t, :], abias[l][:, 1, :t, :]], axis=-1)[..., -keep:]
        o_a = _band_attn_sample(p4, l, ca_kt, ca_vt, bias_cache, abias[l][:, 2, :t, :t])
        o_b = _sb_attn_sample(p4, l, cb_kt, cb_vt)
        far = jnp.broadcast_to(t5tiles[:, 1, :1, :1], (C_HEADS, t, past - ATT_BLK))
        bias_c = jnp.concatenate([far, t5tiles[:, 1, :t, :]], axis=-1)
        o_c = _diff_attn_sample(p4, l, cc_kt, cc_vr, bias_c, t5tiles[:, 0, :t, :t],
                                _lam_init(l), prm["c_lambda"][l], prm["c_subln"][l])
        x2, st_conv = _layer_tail(x2, o_a, o_b, o_c, l, prm, st_conv_in[l], t)
        conv_states.append(st_conv)
    y = x2.reshape(bn, t, d)
    kb, vb, kc, vc = (_cols(p4, n) for n in ("kb", "vb", "kc", "vc"))
    heads = lambda a: a.reshape(a.shape[:3] + (A_HEADS, HEAD_DIM))
    heads_t = lambda a: jnp.transpose(a.reshape(depth, bn, A_HEADS, HEAD_DIM, a.shape[-1]), (0, 1, 4, 2, 3))
    sa_kt, sa_vt = _band_roll(ca_kt, ca_vt, p4)
    states = (heads_t(sa_kt), heads_t(sa_vt), heads(kb), heads(vb),
              kc.reshape(depth, bn, t, C_HEADS, 2, HEAD_DIM), vc.reshape(depth, bn, t, C_HEADS, C_VDIM),
              jnp.stack(conv_states, axis=0))
    return y, states


def kernel(x_prompt, x_sample, cache_a_k, cache_a_v, cache_b_k, cache_b_v, cache_c_k, cache_c_v,
           state_ffn_conv, norm_mix, w_in, b_gate, a_rel_bias, t5_bias, c_lambda, c_subln,
           w_branch, w_out, norm_ffn, w_up, conv_w, conv_b, w_down, norm_final):
    group = {name: w_in[..., i * BRANCH_W:(i + 1) * BRANCH_W] for i, name in enumerate(_GROUPS)}
    gate = w_in[..., len(_GROUPS) * BRANCH_W:]
    prm = dict(
        norm_mix=norm_mix,
        w_in=w_in[..., :len(_GROUPS) * BRANCH_W].astype(BF16),
        w_rows=jnp.concatenate([group[n] for n in _ROW_GROUPS], axis=-1).astype(BF16),
        w_gate_mix=gate.astype(BF16),
        w_feat_t=jnp.swapaxes(jnp.concatenate([group[n] for n in _T_GROUPS], axis=-1), 1, 2).astype(BF16),
        b_gate=b_gate, c_lambda=c_lambda, c_subln=c_subln,
        w_branch=w_branch.astype(BF16), w_out=w_out.astype(BF16), norm_ffn=norm_ffn,
        w_gate=w_up[..., :D_FF].astype(BF16), w_up=w_up[..., D_FF:].astype(BF16),
        conv_w=conv_w, conv_b=conv_b, w_down=w_down.astype(BF16), norm_final=norm_final)
    abias = [_build_abias(a_rel_bias[l]) for l in range(a_rel_bias.shape[0])]
    t5tiles = _build_t5bias(t5_bias)
    y_p, p_states = _trunk_prompt(x_prompt, prm, abias, t5tiles, t5_bias.reshape(-1))
    caches = (cache_a_k, cache_a_v, cache_b_k, cache_b_v, cache_c_k, cache_c_v, state_ffn_conv)
    y_s, s_states = _trunk_sample(x_sample, caches, prm, abias, t5tiles)
    return (y_p, y_s) + p_states + s_states
```
